```python
import math
import jax
import jax.numpy as jnp
from jax import lax
import numpy as np

D_MODEL = 1024
BATCH = 2
SEQ = 16384
DEPTH = 1
DEC_BATCH = 32
DEC_SEQ = 16
PAST_LEN = 4096

CHUNK = 64
Q_BLOCK = 128
HEAD_DIM = 64
H_A = 4
H_B = 8
A_QK = H_A * 2 * HEAD_DIM
A_V = H_A * 2 * HEAD_DIM
B_W = H_B * HEAD_DIM
MIX_WIDTH = A_V + B_W
PROJ_WIDTH = 2 * A_QK + A_V + 3 * B_W + H_B
SPLIT_POINTS = (A_QK, 2 * A_QK, 2 * A_QK + A_V, 2 * A_QK + A_V + B_W,
                2 * A_QK + A_V + 2 * B_W, 2 * A_QK + A_V + 3 * B_W)
N_BUCKETS = 32
MAX_DISTANCE = 128
FORGET_BIAS_INIT = 3.0
N_GROUPS = 4
EXPERTS_PER_GROUP = 8
N_EXPERTS = N_GROUPS * EXPERTS_PER_GROUP
TOP_K = 2
EXPERT_FF = 512
MOE_BLOCK = 512
RMS_EPS = 1e-6
NEG_INF = -1e30

kernel_name = "hymba_diff_fox_hmoe_stream_step"


def _rms(x, g):
    xf = x.astype(jnp.float32)
    y = xf * lax.rsqrt(jnp.mean(xf * xf, axis=-1, keepdims=True) + RMS_EPS)
    return (y * g.astype(jnp.float32)).astype(x.dtype)


def _t5_bucket(rel):
    nb = N_BUCKETS // 2
    max_exact = nb // 2
    base = jnp.where(rel > 0, nb, 0)
    n = jnp.abs(rel)
    large = max_exact + (jnp.log(jnp.maximum(n, max_exact).astype(jnp.float32) / max_exact)
                         / math.log(MAX_DISTANCE / max_exact) * (nb - max_exact)).astype(jnp.int32)
    large = jnp.minimum(large, nb - 1)
    return base + jnp.where(n < max_exact, n, large)


def _sweep_queries(fn, q_pos, *q_args):
    t = q_pos.shape[0]
    if t <= Q_BLOCK:
        return fn(q_pos, *q_args)
    nb = t // Q_BLOCK

    def split(a):
        return jnp.moveaxis(a.reshape((a.shape[0], nb, Q_BLOCK) + a.shape[2:]), 1, 0)

    out = lax.map(lambda args: fn(*args),
                  (q_pos.reshape(nb, Q_BLOCK),) + tuple(split(a) for a in q_args))
    out = jnp.moveaxis(out, 0, 1)
    return out.reshape((out.shape[0], t) + out.shape[3:])


def _diff_attention(qa, ka, va, q_pos, k_pos, rel_bias, lam):
    scale = HEAD_DIM ** -0.5
    k_chunk = k_pos // CHUNK

    def block(qp, qb):
        s = jnp.einsum("bqhmd,bkhmd->bhmqk", qb, ka).astype(jnp.float32) * scale
        bias = rel_bias[_t5_bucket(k_pos[None, :] - qp[:, None])].astype(jnp.float32)
        s = s + jnp.transpose(bias, (2, 0, 1))[None, :, None]
        visible = k_chunk[None, :] <= (qp // CHUNK)[:, None]
        p = jax.nn.softmax(jnp.where(visible, s, NEG_INF), axis=-1)
        a = p[:, :, 0] - lam * p[:, :, 1]
        return jnp.einsum("bhqk,bkhe->bqhe", a, va)

    return _sweep_queries(block, q_pos, qa)


def _forgetting_attention(qb, kb, vb, c_q, c_k, q_pos, k_pos):
    scale = HEAD_DIM ** -0.5
    ck = jnp.transpose(c_k, (0, 2, 1))

    def block(qp, q, cq):
        s = jnp.einsum("bqhd,bkhd->bhqk", q, kb).astype(jnp.float32) * scale
        s = s + jnp.transpose(cq, (0, 2, 1))[..., None] - ck[:, :, None, :]
        visible = k_pos[None, :] <= qp[:, None]
        p = jax.nn.softmax(jnp.where(visible, s, NEG_INF), axis=-1)
        return jnp.einsum("bhqk,bkhd->bqhd", p, vb)

    return _sweep_queries(block, q_pos, qb, c_q)


def _token_mixers(n, pos, w_in, b_forget, g_qk_a, g_qk_b, lambda_qk, g_out_a, g_out_b,
                  w_out, rel_bias, lam_init, past):
    b, t, _ = n.shape
    proj = jnp.einsum("btd,dp->btp", n, w_in)
    qa, ka, va, qb, kb, vb, fl = jnp.split(proj, SPLIT_POINTS, axis=-1)
    qa = _rms(qa.reshape(b, t, H_A, 2, HEAD_DIM), g_qk_a[0])
    ka = _rms(ka.reshape(b, t, H_A, 2, HEAD_DIM), g_qk_a[1])
    va = va.reshape(b, t, H_A, 2 * HEAD_DIM)
    qb = _rms(qb.reshape(b, t, H_B, HEAD_DIM), g_qk_b[0])
    kb = _rms(kb.reshape(b, t, H_B, HEAD_DIM), g_qk_b[1])
    vb = vb.reshape(b, t, H_B, HEAD_DIM)
    logf = jax.nn.log_sigmoid((fl + b_forget).astype(jnp.float32))

    if past is None:
        ka_all, va_all, kb_all, vb_all, logf_all = ka, va, kb, vb, logf
        k_pos = pos
    else:
        pa_k, pa_v, pb_k, pb_v, pb_logf = past
        ka_all = jnp.concatenate([pa_k.astype(ka.dtype), ka], axis=1)
        va_all = jnp.concatenate([pa_v.astype(va.dtype), va], axis=1)
        kb_all = jnp.concatenate([pb_k.astype(kb.dtype), kb], axis=1)
        vb_all = jnp.concatenate([pb_v.astype(vb.dtype), vb], axis=1)
        logf_all = jnp.concatenate([pb_logf.astype(jnp.float32), logf], axis=1)
        k_pos = jnp.arange(pa_k.shape[1] + t, dtype=jnp.int32)

    c_all = jnp.cumsum(logf_all, axis=1)
    c_q = c_all[:, -t:]

    lq = lambda_qk.astype(jnp.float32)
    lam = jnp.exp(jnp.sum(lq[0] * lq[1])) - jnp.exp(jnp.sum(lq[2] * lq[3])) + lam_init

    o_a = _diff_attention(qa, ka_all, va_all, pos, k_pos, rel_bias, lam)
    o_a = _rms(o_a, g_out_a) * (1.0 - lam_init)
    o_b = _forgetting_attention(qb, kb_all, vb_all, c_q, c_all, pos, k_pos)
    o_b = _rms(o_b, g_out_b)
    o = jnp.concatenate([o_a.reshape(b, t, A_V), o_b.reshape(b, t, B_W)], axis=-1).astype(n.dtype)
    y = jnp.einsum("btm,md->btd", o, w_out)
    return y, (ka, va, kb, vb, logf.astype(n.dtype))


def _hier_moe(x, w_rg, b_rg, w_re, b_re, w_gate, w_up, w_down):
    b, t, d = x.shape
    n_tok = b * t
    xt = x.reshape(n_tok, d)
    lg = jnp.einsum("nd,dg->ng", xt, w_rg).astype(jnp.float32) + b_rg.astype(jnp.float32)
    pg = jax.nn.softmax(lg, axis=-1)
    grp = jnp.argmax(lg, axis=-1)
    p_grp = jnp.take_along_axis(pg, grp[:, None], axis=-1)
    le = (jnp.einsum("nd,de->ne", xt, w_re).astype(jnp.float32)
          + b_re.astype(jnp.float32)).reshape(n_tok, N_GROUPS, EXPERTS_PER_GROUP)
    le_sel = jnp.take_along_axis(le, grp[:, None, None], axis=1)[:, 0]
    top_v, top_i = lax.top_k(le_sel, TOP_K)
    w_top = jax.nn.softmax(top_v, axis=-1) * p_grp
    expert = grp[:, None] * EXPERTS_PER_GROUP + top_i
    gates = jnp.sum(jax.nn.one_hot(expert, N_EXPERTS, dtype=jnp.float32) * w_top[..., None], axis=1)

    pad = (-n_tok) % MOE_BLOCK
    xt_p = jnp.pad(xt, ((0, pad), (0, 0)))
    g_p = jnp.pad(gates, ((0, pad), (0, 0)))
    nb = (n_tok + pad) // MOE_BLOCK

    def block(args):
        xb, gb = args
        h = jax.nn.silu(jnp.einsum("td,edf->tef", xb, w_gate)) * jnp.einsum("td,edf->tef", xb, w_up)
        return jnp.einsum("tef,efd->td", h * gb[..., None], w_down)

    out = lax.map(block, (xt_p.reshape(nb, MOE_BLOCK, d), g_p.reshape(nb, MOE_BLOCK, N_EXPERTS)))
    return out.reshape(-1, d)[:n_tok].reshape(b, t, d).astype(x.dtype)


def _layer(x, pos, past, lam_init, g_norm_mix, w_in, b_forget, g_qk_a, g_qk_b, lambda_qk,
           g_out_a, g_out_b, w_out, rel_bias, g_norm_ffn, w_rg, b_rg, w_re, b_re,
           w_gate, w_up, w_down):
    h, rows = _token_mixers(_rms(x, g_norm_mix), pos, w_in, b_forget, g_qk_a, g_qk_b, lambda_qk,
                            g_out_a, g_out_b, w_out, rel_bias, lam_init, past)
    x = x + h
    x = x + _hier_moe(_rms(x, g_norm_ffn), w_rg, b_rg, w_re, b_re, w_gate, w_up, w_down)
    return x, rows


def setup_inputs(seed: int = 0) -> dict:
    key = jax.random.key(seed)
    ks = jax.random.split(key, 32)
    f32 = jnp.float32

    def nrm(k, shape, scale=1.0):
        return jax.random.normal(k, shape, f32) * scale

    def gain(k, shape):
        return 1.0 + 0.02 * jax.random.normal(k, shape, f32)

    return {
        "x_prompt": nrm(ks[0], (BATCH, SEQ, D_MODEL)),
        "x_sample": nrm(ks[1], (DEC_BATCH, DEC_SEQ, D_MODEL)),
        "cache_a_k": nrm(ks[2], (DEPTH, DEC_BATCH, PAST_LEN, H_A, 2, HEAD_DIM)),
        "cache_a_v": nrm(ks[3], (DEPTH, DEC_BATCH, PAST_LEN, H_A, 2 * HEAD_DIM)),
        "cache_b_k": nrm(ks[4], (DEPTH, DEC_BATCH, PAST_LEN, H_B, HEAD_DIM)),
        "cache_b_v": nrm(ks[5], (DEPTH, DEC_BATCH, PAST_LEN, H_B, HEAD_DIM)),
        "cache_b_logf": jax.nn.log_sigmoid(FORGET_BIAS_INIT + nrm(ks[6], (DEPTH, DEC_BATCH, PAST_LEN, H_B))),
        "g_norm_mix": gain(ks[7], (DEPTH, D_MODEL)),
        "w_in": nrm(ks[8], (DEPTH, D_MODEL, PROJ_WIDTH), D_MODEL ** -0.5),
        "b_forget": FORGET_BIAS_INIT + nrm(ks[9], (DEPTH, H_B), 0.1),
        "g_qk_a": gain(ks[10], (DEPTH, 2, HEAD_DIM)),
        "g_qk_b": gain(ks[11], (DEPTH, 2, HEAD_DIM)),
        "lambda_qk": nrm(ks[12], (DEPTH, 4, HEAD_DIM), 0.1),
        "g_out_a": gain(ks[13], (DEPTH, H_A, 2 * HEAD_DIM)),
        "g_out_b": gain(ks[14], (DEPTH, H_B, HEAD_DIM)),
        "w_out": nrm(ks[15], (DEPTH, MIX_WIDTH, D_MODEL), MIX_WIDTH ** -0.5),
        "rel_bias": nrm(ks[16], (N_BUCKETS, H_A), 0.5),
        "g_norm_ffn": gain(ks[17], (DEPTH, D_MODEL)),
        "w_router_group": nrm(ks[18], (DEPTH, D_MODEL, N_GROUPS), D_MODEL ** -0.5),
        "b_router_group": nrm(ks[19], (DEPTH, N_GROUPS), 0.01),
        "w_router_expert": nrm(ks[20], (DEPTH, D_MODEL, N_EXPERTS), D_MODEL ** -0.5),
        "b_router_expert": nrm(ks[21], (DEPTH, N_EXPERTS), 0.01),
        "w_exp_gate": nrm(ks[22], (DEPTH, N_EXPERTS, D_MODEL, EXPERT_FF), D_MODEL ** -0.5),
        "w_exp_up": nrm(ks[23], (DEPTH, N_EXPERTS, D_MODEL, EXPERT_FF), D_MODEL ** -0.5),
        "w_exp_down": nrm(ks[24], (DEPTH, N_EXPERTS, EXPERT_FF, D_MODEL), EXPERT_FF ** -0.5),
    }


def reference(x_prompt, x_sample, cache_a_k, cache_a_v, cache_b_k, cache_b_v, cache_b_logf,
              g_norm_mix, w_in, b_forget, g_qk_a, g_qk_b, lambda_qk, g_out_a, g_out_b, w_out,
              rel_bias, g_norm_ffn, w_router_group, b_router_group, w_router_expert,
              b_router_expert, w_exp_gate, w_exp_up, w_exp_down):
    past_len = cache_a_k.shape[2]
    pos_p = jnp.arange(x_prompt.shape[1], dtype=jnp.int32)
    pos_s = past_len + jnp.arange(x_sample.shape[1], dtype=jnp.int32)
    yp, ys = x_prompt, x_sample
    rows_p, rows_s = [], []
    for l in range(DEPTH):
        lam_init = 0.8 - 0.6 * math.exp(-0.3 * l)
        lw = (g_norm_mix[l], w_in[l], b_forget[l], g_qk_a[l], g_qk_b[l], lambda_qk[l],
              g_out_a[l], g_out_b[l], w_out[l], rel_bias, g_norm_ffn[l], w_router_group[l],
              b_router_group[l], w_router_expert[l], b_router_expert[l], w_exp_gate[l],
              w_exp_up[l], w_exp_down[l])
        yp, r_p = _layer(yp, pos_p, None, lam_init, *lw)
        past = (cache_a_k[l], cache_a_v[l], cache_b_k[l], cache_b_v[l], cache_b_logf[l])
        ys, r_s = _layer(ys, pos_s, past, lam_init, *lw)
        rows_p.append(r_p)
        rows_s.append(r_s)
    a_k_p, a_v_p, b_k_p, b_v_p, b_logf_p = [jnp.stack(t) for t in zip(*rows_p)]
    a_k_s, a_v_s, b_k_s, b_v_s, b_logf_s = [jnp.stack(t) for t in zip(*rows_s)]
    return (yp, ys, a_k_p, a_v_p, b_k_p, b_v_p, b_logf_p, a_k_s, a_v_s, b_k_s, b_v_s, b_logf_s)
```

```python
import functools
import math

import jax
import jax.numpy as jnp
from jax import lax
from jax.experimental import pallas as pl
from jax.experimental.pallas import tpu as pltpu

F32 = jnp.float32
BF16 = jnp.bfloat16

LANES = 128
VMEM_LIMIT_BYTES = 56 * 1024 * 1024

HEAD_DIM = 64
H_A = 4
H_B = 8
GROUP_W = 512
MAIN_W = 6 * GROUP_W
CHUNK = 64
N_BUCKETS = 32
MAX_DISTANCE = 128
N_GROUPS = 4
EXPERTS_PER_GROUP = 8
N_EXPERTS = N_GROUPS * EXPERTS_PER_GROUP
ROUTER_LANE0 = N_GROUPS
RMS_EPS = 1e-6
NEG = -1e30
LOG2E = 1.4426950408889634
QK_SCALE = HEAD_DIM ** -0.5
LAM_INIT = 0.8 - 0.6 * math.exp(-0.3 * 0)
N_CPARTS = 3

TOKEN_TILE = 512
ATTN_BLOCK = 512
CACHE_CHUNK = 1024
MOE_TILE = 1024


def _cparams(sem):
    return pltpu.CompilerParams(dimension_semantics=sem, vmem_limit_bytes=VMEM_LIMIT_BYTES)


def _const_spec(shape):
    nd = len(shape)
    return pl.BlockSpec(shape, lambda *_: (0,) * nd)


def _split3(x):
    p1 = x.astype(BF16).astype(F32)
    r1 = x - p1
    p2 = r1.astype(BF16).astype(F32)
    p3 = (r1 - p2).astype(BF16).astype(F32)
    return p1, p2, p3


def _lane_groups(parts, lane):
    return jnp.where(lane < 8, parts[0], jnp.where(lane < 16, parts[1], parts[2]))


def _cumsum_rows(tri, x, carry):
    c = carry
    for part in _split3(x):
        c = c + jnp.dot(tri, part.astype(BF16), preferred_element_type=F32)
    return c


def _log_sigmoid(x):
    return jnp.minimum(x, 0.0) - jnp.log(1.0 + jnp.exp(-jnp.abs(x)))


def _group_rms(raw, gain_row, gn):
    ms = jnp.dot((raw * raw).astype(BF16), gn, preferred_element_type=F32)
    return raw * lax.rsqrt(ms + RMS_EPS) * gain_row


def _online_update(state, s, v):
    m, l, acc = state
    m_new = jnp.maximum(m, jnp.max(s, axis=-1, keepdims=True))
    p = jnp.exp2(s - m_new)
    alpha = jnp.exp2(m - m_new)
    l_new = alpha * l + jnp.sum(p, axis=-1, keepdims=True)
    acc_new = alpha * acc + jnp.dot(p.astype(BF16), v, preferred_element_type=F32)
    return m_new, l_new, acc_new


def _qk(q, k):
    return lax.dot_general(q, k, (((1,), (1,)), ((), ())), preferred_element_type=F32)


def _proj_body(*refs, with_aug, bt):
    (x_ref, g1_ref, wm_ref, wf_ref, bf_ref, gains_ref, gn_ref, tri_ref, place_ref) = refs[:9]
    ka_ref, va_ref, kb_ref, vb_ref, logf_ref = refs[9:14]
    x = x_ref[0]
    ms = jnp.mean(x * x, axis=-1, keepdims=True)
    xn = (x * lax.rsqrt(ms + RMS_EPS)) * g1_ref[...]
    xb = xn.astype(BF16)
    proj = jnp.dot(xb, wm_ref[...], preferred_element_type=F32)
    gains = gains_ref[...]
    gn = gn_ref[...]
    w = GROUP_W
    qa = _group_rms(proj[:, 0:w], gains[0:1], gn)
    ka = _group_rms(proj[:, w:2 * w], gains[1:2], gn)
    va = proj[:, 2 * w:3 * w]
    qb = _group_rms(proj[:, 3 * w:4 * w], gains[2:3], gn)
    kb = _group_rms(proj[:, 4 * w:5 * w], gains[3:4], gn)
    vb = proj[:, 5 * w:6 * w]
    ka_ref[0] = ka
    va_ref[0] = va
    kb_ref[0] = kb
    vb_ref[0] = vb
    fl = jnp.dot(xb, wf_ref[...], preferred_element_type=F32) + bf_ref[...]
    logf = _log_sigmoid(fl)
    logf_ref[0] = logf[:, 0:H_B]
    qscale = QK_SCALE * LOG2E
    if not with_aug:
        qa_ref, qb_ref = refs[14:16]
        qa_ref[0] = (qa * qscale).astype(BF16)
        qb_ref[0] = (qb * qscale).astype(BF16)
        return
    qa_bf, ka_bf, va_bf, qb_aug, kb_aug, vb_bf, carry_ref = refs[14:21]
    for h in range(GROUP_W // LANES):
        sl = slice(h * LANES, (h + 1) * LANES)
        qa_bf[0, h] = (qa[:, sl] * qscale).astype(BF16)
        ka_bf[0, h] = ka[:, sl].astype(BF16)
        va_bf[0, h] = va[:, sl].astype(BF16)
        vb_bf[0, h] = vb[:, sl].astype(BF16)

    @pl.when(pl.program_id(1) == 0)
    def _():
        carry_ref[...] = jnp.zeros_like(carry_ref)

    c = _cumsum_rows(tri_ref[...], logf * LOG2E, carry_ref[0:1, :])
    carry_ref[0:1, :] = c[bt - 1:bt, :]
    lane = lax.broadcasted_iota(jnp.int32, (bt, LANES), 1)
    cparts = _lane_groups(_split3(c), lane).astype(BF16)
    extras = jnp.dot(cparts, place_ref[...], preferred_element_type=F32)
    ones_q = jnp.where((lane >= HEAD_DIM + N_CPARTS) & (lane < HEAD_DIM + 2 * N_CPARTS), 1.0, 0.0)
    ones_k = jnp.where((lane >= HEAD_DIM) & (lane < HEAD_DIM + N_CPARTS), 1.0, 0.0)
    for h in range(H_B):
        sl = slice((h // 2) * LANES, (h // 2 + 1) * LANES)
        qp = qb[:, sl] * qscale
        kp = kb[:, sl]
        if h % 2:
            qp = pltpu.roll(qp, HEAD_DIM, 1)
            kp = pltpu.roll(kp, HEAD_DIM, 1)
        eq = extras[:, h * LANES:(h + 1) * LANES] + ones_q
        ek = extras[:, (H_B + h) * LANES:(H_B + h + 1) * LANES] + ones_k
        qb_aug[0, h] = jnp.where(lane < HEAD_DIM, qp, eq).astype(BF16)
        kb_aug[0, h] = jnp.where(lane < HEAD_DIM, kp, ek).astype(BF16)


def _blockdiag_mean(n, group):
    r = jnp.arange(n)
    return jnp.where((r[:, None] // group) == (r[None, :] // group), 1.0 / group, 0.0).astype(BF16)


def _tri(n):
    r = jnp.arange(n)
    return (r[None, :] <= r[:, None]).astype(BF16)


def _place_matrix():
    rows = jnp.arange(LANES)[:, None]
    cols = jnp.arange(2 * H_B * LANES)[None, :]
    p, h = rows // 8, rows % 8
    valid = rows < 8 * N_CPARTS
    qcol = h * LANES + HEAD_DIM + p
    kcol = (H_B + h) * LANES + HEAD_DIM + N_CPARTS + p
    m = jnp.where(valid & (cols == qcol), 1.0, 0.0) - jnp.where(valid & (cols == kcol), 1.0, 0.0)
    return m.astype(BF16)


def _projection(x, g_norm, w_in, b_forget, g_qk_a, g_qk_b, *, with_aug):
    b, t, d = x.shape
    bt = min(TOKEN_TILE, t)
    nt = t // bt
    wm = w_in[:, :MAIN_W].astype(BF16)
    wf_cols = w_in[:, MAIN_W:MAIN_W + H_B]
    wf = jnp.concatenate([wf_cols] * N_CPARTS + [jnp.zeros((d, LANES - H_B * N_CPARTS), F32)], axis=1).astype(BF16)
    bfv = jnp.concatenate([b_forget] * N_CPARTS + [jnp.zeros((LANES - H_B * N_CPARTS,), F32)])[None, :]
    gains = jnp.stack([jnp.tile(g_qk_a[0], 2 * H_A), jnp.tile(g_qk_a[1], 2 * H_A),
                       jnp.tile(g_qk_b[0], H_B), jnp.tile(g_qk_b[1], H_B)])
    gn = _blockdiag_mean(GROUP_W, HEAD_DIM)
    tri = _tri(bt)
    place = _place_matrix()
    in_specs = [
        pl.BlockSpec((1, bt, d), lambda i, j: (i, j, 0)),
        _const_spec((1, d)),
        pl.BlockSpec((d, MAIN_W), lambda i, j: (0, 0), pipeline_mode=pl.Buffered(1)),
        _const_spec((d, LANES)), _const_spec((1, LANES)), _const_spec((4, GROUP_W)),
        _const_spec((GROUP_W, GROUP_W)), _const_spec((bt, bt)), _const_spec((LANES, 2 * H_B * LANES)),
    ]
    row_spec = pl.BlockSpec((1, bt, GROUP_W), lambda i, j: (i, j, 0))
    out_shape = [jax.ShapeDtypeStruct((b, t, GROUP_W), F32)] * 4 + [jax.ShapeDtypeStruct((b, t, H_B), F32)]
    out_specs = [row_spec] * 4 + [pl.BlockSpec((1, bt, H_B), lambda i, j: (i, j, 0))]
    scratch = []
    if with_aug:
        pairs = GROUP_W // LANES
        hm = lambda n: pl.BlockSpec((1, n, bt, LANES), lambda i, j: (i, 0, j, 0))
        out_shape += [jax.ShapeDtypeStruct((b, pairs, t, LANES), BF16)] * 3
        out_specs += [hm(pairs)] * 3
        out_shape += [jax.ShapeDtypeStruct((b, H_B, t, LANES), BF16)] * 2
        out_specs += [hm(H_B)] * 2
        out_shape += [jax.ShapeDtypeStruct((b, pairs, t, LANES), BF16)]
        out_specs += [hm(pairs)]
        scratch = [pltpu.VMEM((8, LANES), F32)]
    else:
        out_shape += [jax.ShapeDtypeStruct((b, t, GROUP_W), BF16)] * 2
        out_specs += [row_spec] * 2
    return pl.pallas_call(
        functools.partial(_proj_body, with_aug=with_aug, bt=bt),
        grid=(b, nt), in_specs=in_specs, out_specs=out_specs, out_shape=out_shape, scratch_shapes=scratch,
        compiler_params=_cparams(("arbitrary", "arbitrary")),
        name="proj_aug" if with_aug else "proj_plain",
    )(x, g_norm[None, :], wm, wf, bfv, gains, gn, tri, place)


def _t5_bucket(rel):
    nb = N_BUCKETS // 2
    max_exact = nb // 2
    base = jnp.where(rel > 0, nb, 0)
    n = jnp.abs(rel)
    large = max_exact + (jnp.log(jnp.maximum(n, max_exact).astype(jnp.float32) / max_exact)
                         / math.log(MAX_DISTANCE / max_exact) * (nb - max_exact)).astype(jnp.int32)
    large = jnp.minimum(large, nb - 1)
    return base + jnp.where(n < max_exact, n, large)


def _bucket_map(q_pos, k_pos):
    bkt = _t5_bucket(k_pos[None, :] - q_pos[:, None])
    visible = (k_pos[None, :] // CHUNK) <= (q_pos[:, None] // CHUNK)
    return jnp.where(visible, bkt, -1).astype(jnp.int32)


def _bias_body(rb_ref, bkt_ref, o_ref):
    h = pl.program_id(0)
    bkt = bkt_ref[0]
    far = rb_ref[N_BUCKETS // 2 - 1, h]
    acc = jnp.zeros(bkt.shape, F32)
    for b in range(N_BUCKETS):
        acc = jnp.where(bkt == b, rb_ref[b, h] - far, acc)
    o_ref[0, 0] = jnp.where(bkt < 0, NEG, acc * LOG2E)


def _bias_tiles(rel_bias, bkt):
    n, r, c = bkt.shape
    return pl.pallas_call(
        _bias_body,
        grid=(H_A, n),
        in_specs=[pl.BlockSpec(memory_space=pltpu.SMEM), pl.BlockSpec((1, r, c), lambda h, i: (i, 0, 0))],
        out_specs=pl.BlockSpec((1, 1, r, c), lambda h, i: (h, i, 0, 0)),
        out_shape=jax.ShapeDtypeStruct((H_A, n, r, c), F32),
        compiler_params=_cparams(("arbitrary", "arbitrary")),
        name="bias_tiles",
    )(rel_bias, bkt)


def _lam(lq):
    a = jnp.sum(lq[0:1, :] * lq[1:2, :], axis=-1, keepdims=True)
    b = jnp.sum(lq[2:3, :] * lq[3:4, :], axis=-1, keepdims=True)
    return jnp.exp(a) - jnp.exp(b) + LAM_INIT


def _attn_init(bq):
    return (jnp.full((bq, 1), NEG, F32), jnp.zeros((bq, 1), F32), jnp.zeros((bq, LANES), F32))


def _fox_body(q_ref, k_ref, v_ref, gout_ref, gn_ref, o_ref, *, blk):
    i = pl.program_id(2)
    row = lax.broadcasted_iota(jnp.int32, (blk, blk), 0)
    col = lax.broadcasted_iota(jnp.int32, (blk, blk), 1)
    outs = []
    for hh in range(2):
        q = q_ref[0, hh]

        def step(j, state, masked, hh=hh, q=q):
            off = pl.multiple_of(j * blk, blk)
            s = _qk(q, k_ref[0, hh, pl.ds(off, blk), :])
            if masked:
                s = jnp.where(col <= row, s, NEG)
            return _online_update(state, s, v_ref[0, 0, pl.ds(off, blk), :])

        state = lax.fori_loop(0, i, lambda j, st: step(j, st, False), _attn_init(blk))
        _, l, acc = step(i, state, True)
        outs.append(acc / l)
    lane = lax.broadcasted_iota(jnp.int32, (blk, LANES), 1)
    o = jnp.where(lane < HEAD_DIM, outs[0], outs[1])
    o_ref[0] = (_group_rms(o, gout_ref[0], gn_ref[...])).astype(BF16)


def _fox_attention(qb_aug, kb_aug, vb_bf, g_out_b):
    b, _, t, _ = qb_aug.shape
    blk = min(ATTN_BLOCK, t)
    pairs = H_B // 2
    gout = g_out_b.reshape(pairs, 1, LANES)
    return pl.pallas_call(
        functools.partial(_fox_body, blk=blk),
        grid=(b, pairs, t // blk),
        in_specs=[
            pl.BlockSpec((1, 2, blk, LANES), lambda bi, p, i: (bi, p, i, 0)),
            pl.BlockSpec((1, 2, t, LANES), lambda bi, p, i: (bi, p, 0, 0)),
            pl.BlockSpec((1, 1, t, LANES), lambda bi, p, i: (bi, p, 0, 0)),
            pl.BlockSpec((1, 1, LANES), lambda bi, p, i: (p, 0, 0)),
            _const_spec((LANES, LANES)),
        ],
        out_specs=pl.BlockSpec((1, blk, LANES), lambda bi, p, i: (bi, i, p)),
        out_shape=jax.ShapeDtypeStruct((b, t, H_B * HEAD_DIM), BF16),
        compiler_params=_cparams(("arbitrary", "arbitrary", "arbitrary")),
        name="fox_attention",
    )(qb_aug, kb_aug, vb_bf, gout, _blockdiag_mean(LANES, HEAD_DIM))


def _diff_finish(states, lam, gout):
    (_, l1, a1), (_, l2, a2) = states
    o = a1 / l1 - lam * (a2 / l2)
    ms = jnp.mean(o * o, axis=-1, keepdims=True)
    return (o * lax.rsqrt(ms + RMS_EPS)) * gout * (1.0 - LAM_INIT)


def _diff_body(lam_ref, q_ref, k_ref, v_ref, bias_ref, gout_ref, o_ref, *, blk):
    i = pl.program_id(2)
    lane = lax.broadcasted_iota(jnp.int32, (blk, LANES), 1)
    qfull = q_ref[0, 0]
    zero = jnp.zeros_like(qfull)
    qs = (jnp.where(lane < HEAD_DIM, qfull, zero), jnp.where(lane >= HEAD_DIM, qfull, zero))

    def step(j, states, bias):
        off = pl.multiple_of(j * blk, blk)
        k = k_ref[0, 0, pl.ds(off, blk), :]
        v = v_ref[0, 0, pl.ds(off, blk), :]
        new = []
        for mi in range(2):
            s = _qk(qs[mi], k)
            if bias is not None:
                s = s + bias
            new.append(_online_update(states[mi], s, v))
        return tuple(new)

    n_far = jnp.maximum(i - 1, 0)
    states = lax.fori_loop(0, n_far, lambda j, st: step(j, st, None), (_attn_init(blk), _attn_init(blk)))
    states = lax.fori_loop(n_far, i + 1, lambda j, st: step(j, st, bias_ref[0, i - j]), states)
    o_ref[0] = _diff_finish(states, _lam(lam_ref[...]), gout_ref[0]).astype(BF16)


def _diff_attention(qa_bf, ka_bf, va_bf, bias, lambda_qk, g_out_a):
    b, _, t, _ = qa_bf.shape
    blk = min(ATTN_BLOCK, t)
    gout = g_out_a.reshape(H_A, 1, LANES)
    res = lambda bi, h, i: (bi, h, 0, 0)
    return pl.pallas_call(
        functools.partial(_diff_body, blk=blk),
        grid=(b, H_A, t // blk),
        in_specs=[
            _const_spec((4, HEAD_DIM)),
            pl.BlockSpec((1, 1, blk, LANES), lambda bi, h, i: (bi, h, i, 0)),
            pl.BlockSpec((1, 1, t, LANES), res),
            pl.BlockSpec((1, 1, t, LANES), res),
            pl.BlockSpec((1, 2, blk, blk), lambda bi, h, i: (h, 0, 0, 0)),
            pl.BlockSpec((1, 1, LANES), lambda bi, h, i: (h, 0, 0)),
        ],
        out_specs=pl.BlockSpec((1, blk, LANES), lambda bi, h, i: (bi, i, h)),
        out_shape=jax.ShapeDtypeStruct((b, t, H_A * 2 * HEAD_DIM), BF16),
        compiler_params=_cparams(("arbitrary", "arbitrary", "arbitrary")),
        name="diff_attention",
    )(lambda_qk, qa_bf, ka_bf, va_bf, bias, gout)


def _dec_load(m_scr, l_scr, acc_scr, idx):
    return m_scr[idx], l_scr[idx], acc_scr[idx]


def _dec_store(m_scr, l_scr, acc_scr, idx, state):
    m_scr[idx], l_scr[idx], acc_scr[idx] = state


def _dec_init(m_scr, l_scr, acc_scr):
    m_scr[...] = jnp.full(m_scr.shape, NEG, F32)
    l_scr[...] = jnp.zeros(l_scr.shape, F32)
    acc_scr[...] = jnp.zeros(acc_scr.shape, F32)


def _diff_dec_body(lam_ref, q_ref, kn_ref, vn_ref, kc_ref, vc_ref, bc_ref, bn_ref, gout_ref, o_ref,
                   m_scr, l_scr, acc_scr, *, nq):
    kc = pl.program_id(1)
    lane = lax.broadcasted_iota(jnp.int32, (nq, LANES), 1)

    @pl.when(kc == 0)
    def _():
        _dec_init(m_scr, l_scr, acc_scr)

    def sweep(k_all, v_all, bias_of):
        for h in range(H_A):
            sl = slice(h * LANES, (h + 1) * LANES)
            qfull = q_ref[0][:, sl]
            zero = jnp.zeros_like(qfull)
            k = k_all[:, sl].astype(BF16)
            v = v_all[:, sl].astype(BF16)
            for mi in range(2):
                qm = jnp.where((lane < HEAD_DIM) if mi == 0 else (lane >= HEAD_DIM), qfull, zero)
                s = _qk(qm, k) + bias_of(h)
                idx = 2 * h + mi
                _dec_store(m_scr, l_scr, acc_scr, idx,
                           _online_update(_dec_load(m_scr, l_scr, acc_scr, idx), s, v))

    sweep(kc_ref[0], vc_ref[0], lambda h: bc_ref[h, 0])

    @pl.when(kc == pl.num_programs(1) - 1)
    def _():
        sweep(kn_ref[0], vn_ref[0], lambda h: bn_ref[h, 0])
        lam = _lam(lam_ref[...])
        for h in range(H_A):
            states = (_dec_load(m_scr, l_scr, acc_scr, 2 * h), _dec_load(m_scr, l_scr, acc_scr, 2 * h + 1))
            o_ref[0, :, h * LANES:(h + 1) * LANES] = _diff_finish(states, lam, gout_ref[h]).astype(BF16)


def _diff_decode(qa, ka_new, va_new, cache_k, cache_v, bias_c, bias_n, lambda_qk, g_out_a):
    b, nq, w = qa.shape
    past = cache_k.shape[1]
    ck = min(CACHE_CHUNK, past)
    n_kc = past // ck
    new_spec = pl.BlockSpec((1, nq, w), lambda bi, c: (bi, 0, 0))
    cache_spec = pl.BlockSpec((1, ck, w), lambda bi, c: (bi, c, 0))
    return pl.pallas_call(
        functools.partial(_diff_dec_body, nq=nq),
        grid=(b, n_kc),
        in_specs=[
            _const_spec((4, HEAD_DIM)), new_spec, new_spec, new_spec, cache_spec, cache_spec,
            pl.BlockSpec((H_A, 1, nq, ck), lambda bi, c: (0, c, 0, 0)),
            _const_spec((H_A, 1, nq, nq)),
            _const_spec((H_A, 1, LANES)),
        ],
        out_specs=new_spec,
        out_shape=jax.ShapeDtypeStruct((b, nq, w), BF16),
        scratch_shapes=[pltpu.VMEM((2 * H_A, nq, 1), F32), pltpu.VMEM((2 * H_A, nq, 1), F32),
                        pltpu.VMEM((2 * H_A, nq, LANES), F32)],
        compiler_params=_cparams(("arbitrary", "arbitrary")),
        name="diff_decode",
    )(lambda_qk, qa, ka_new, va_new, cache_k, cache_v, bias_c, bias_n, g_out_a.reshape(H_A, 1, LANES))


def _fox_dec_body(q_ref, kn_ref, vn_ref, lfn_ref, kc_ref, vc_ref, lfc_ref, tri_ref, place_ref, gout_ref, gn_ref,
                  o_ref, ck_scr, ckn_scr, qc_scr, m_scr, l_scr, acc_scr, *, nq, past, ck_rows, sub):
    kc = pl.program_id(1)
    lane_q = lax.broadcasted_iota(jnp.int32, (nq, LANES), 1)
    n_c = 8 * N_CPARTS

    def key_aug(c, lane):
        body = _lane_groups(_split3(c), lane)
        tail = jnp.where(lane < n_c + N_CPARTS, 1.0, 0.0)
        return jnp.where(lane < n_c, body, tail).astype(BF16)

    @pl.when(kc == 0)
    def _():
        _dec_init(m_scr, l_scr, acc_scr)
        tri = tri_ref[...]
        lane_s = lax.broadcasted_iota(jnp.int32, (sub, LANES), 1)
        carry = jnp.zeros((1, LANES), F32)
        for blk in range(past // sub):
            c = _cumsum_rows(tri, lfc_ref[0, blk * sub:(blk + 1) * sub, :] * LOG2E, carry)
            carry = c[sub - 1:sub, :]
            ck_scr[blk * sub:(blk + 1) * sub, :] = key_aug(c, lane_s)
        r = lax.broadcasted_iota(jnp.int32, (nq, nq), 0)
        cc = lax.broadcasted_iota(jnp.int32, (nq, nq), 1)
        tri_n = jnp.where(cc <= r, 1.0, 0.0).astype(BF16)
        cn = _cumsum_rows(tri_n, lfn_ref[0] * LOG2E, carry)
        ckn_scr[...] = key_aug(cn, lane_q)
        extras = jnp.dot(_lane_groups(_split3(cn), lane_q).astype(BF16), place_ref[...],
                         preferred_element_type=F32)
        for h in range(H_B):
            neg_sel = jnp.where((lane_q < n_c) & ((lane_q & 7) == h), -1.0, 0.0)
            qc_scr[h] = (extras[:, h * LANES:(h + 1) * LANES] + neg_sel).astype(BF16)

    def sweep(k_all, v_all, ck_all, masked):
        row = lax.broadcasted_iota(jnp.int32, (nq, nq), 0)
        col = lax.broadcasted_iota(jnp.int32, (nq, nq), 1)
        for h in range(H_B):
            sl = slice((h // 2) * LANES, (h // 2 + 1) * LANES)
            qfull = q_ref[0][:, sl]
            qm = jnp.where((lane_q < HEAD_DIM) if h % 2 == 0 else (lane_q >= HEAD_DIM), qfull, jnp.zeros_like(qfull))
            s = _qk(qm, k_all[:, sl].astype(BF16)) + _qk(qc_scr[h], ck_all)
            if masked:
                s = jnp.where(col <= row, s, NEG)
            _dec_store(m_scr, l_scr, acc_scr, h,
                       _online_update(_dec_load(m_scr, l_scr, acc_scr, h), s, v_all[:, sl].astype(BF16)))

    off = pl.multiple_of(kc * ck_rows, ck_rows)
    sweep(kc_ref[0], vc_ref[0], ck_scr[pl.ds(off, ck_rows), :], False)

    @pl.when(kc == pl.num_programs(1) - 1)
    def _():
        sweep(kn_ref[0], vn_ref[0], ckn_scr[...], True)
        for p in range(H_B // 2):
            _, l0, a0 = _dec_load(m_scr, l_scr, acc_scr, 2 * p)
            _, l1, a1 = _dec_load(m_scr, l_scr, acc_scr, 2 * p + 1)
            o = jnp.where(lane_q < HEAD_DIM, a0 / l0, a1 / l1)
            o_ref[0, :, p * LANES:(p + 1) * LANES] = _group_rms(o, gout_ref[p], gn_ref[...]).astype(BF16)


def _place_matrix_dec():
    rows = jnp.arange(LANES)[:, None]
    cols = jnp.arange(H_B * LANES)[None, :]
    p, h = rows // 8, rows % 8
    return jnp.where((rows < 8 * N_CPARTS) & (cols == h * LANES + 8 * N_CPARTS + p), 1.0, 0.0).astype(BF16)


def _rep_lanes(logf):
    pad = jnp.zeros(logf.shape[:-1] + (LANES - H_B * N_CPARTS,), logf.dtype)
    return jnp.concatenate([logf] * N_CPARTS + [pad], axis=-1)


def _fox_decode(qb, kb_new, vb_new, logf_new, cache_k, cache_v, cache_logf, g_out_b):
    b, nq, w = qb.shape
    past = cache_k.shape[1]
    ck = min(CACHE_CHUNK, past)
    n_kc = past // ck
    sub = min(512, past)
    new_spec = pl.BlockSpec((1, nq, w), lambda bi, c: (bi, 0, 0))
    cache_spec = pl.BlockSpec((1, ck, w), lambda bi, c: (bi, c, 0))
    return pl.pallas_call(
        functools.partial(_fox_dec_body, nq=nq, past=past, ck_rows=ck, sub=sub),
        grid=(b, n_kc),
        in_specs=[
            new_spec, new_spec, new_spec,
            pl.BlockSpec((1, nq, LANES), lambda bi, c: (bi, 0, 0)),
            cache_spec, cache_spec,
            pl.BlockSpec((1, past, LANES), lambda bi, c: (bi, 0, 0)),
            _const_spec((sub, sub)), _const_spec((LANES, H_B * LANES)),
            _const_spec((H_B // 2, 1, LANES)), _const_spec((LANES, LANES)),
        ],
        out_specs=new_spec,
        out_shape=jax.ShapeDtypeStruct((b, nq, w), BF16),
        scratch_shapes=[pltpu.VMEM((past, LANES), BF16), pltpu.VMEM((nq, LANES), BF16),
                        pltpu.VMEM((H_B, nq, LANES), BF16),
                        pltpu.VMEM((H_B, nq, 1), F32), pltpu.VMEM((H_B, nq, 1), F32),
                        pltpu.VMEM((H_B, nq, LANES), F32)],
        compiler_params=_cparams(("arbitrary", "arbitrary")),
        name="fox_decode",
    )(qb, kb_new, vb_new, _rep_lanes(logf_new), cache_k, cache_v, _rep_lanes(cache_logf), _tri(sub),
      _place_matrix_dec(), g_out_b.reshape(H_B // 2, 1, LANES), _blockdiag_mean(LANES, HEAD_DIM))


def _route(logits):
    lane_i = lax.broadcasted_iota(jnp.int32, logits.shape, 1)
    lane = lane_i.astype(F32)
    big = float(LANES)
    lg = jnp.where(lane_i < N_GROUPS, logits, NEG)
    mx = jnp.max(lg, axis=-1, keepdims=True)
    grp = jnp.min(jnp.where(lg == mx, lane, big), axis=-1, keepdims=True)
    p_grp = 1.0 / jnp.sum(jnp.exp(lg - mx), axis=-1, keepdims=True)
    e = lane_i - ROUTER_LANE0
    e_grp = lax.shift_right_arithmetic(e, 3).astype(F32)
    sel = (e >= 0) & (e < N_EXPERTS) & (e_grp == grp)
    v = jnp.where(sel, logits, NEG)
    v1 = jnp.max(v, axis=-1, keepdims=True)
    i1 = jnp.min(jnp.where(sel & (v == v1), lane, big), axis=-1, keepdims=True)
    sel2 = sel & (lane != i1)
    vv = jnp.where(sel2, logits, NEG)
    v2 = jnp.max(vv, axis=-1, keepdims=True)
    i2 = jnp.min(jnp.where(sel2 & (vv == v2), lane, big), axis=-1, keepdims=True)
    e2 = jnp.exp(v2 - v1)
    w1 = p_grp / (1.0 + e2)
    w2 = p_grp * e2 / (1.0 + e2)
    return jnp.where(lane == i1, w1, 0.0) + jnp.where(lane == i2, w2, 0.0)


def _mix_body(oa_ref, ob_ref, x_ref, wa_ref, wb_ref, g2_ref, wr1_ref, wr2_ref, br_ref, x1_ref, xn_ref, gates_ref):
    y = (jnp.dot(oa_ref[...], wa_ref[...], preferred_element_type=F32)
         + jnp.dot(ob_ref[...], wb_ref[...], preferred_element_type=F32))
    x1 = x_ref[...] + y
    x1_ref[...] = x1
    ms = jnp.mean(x1 * x1, axis=-1, keepdims=True)
    xn = (x1 * lax.rsqrt(ms + RMS_EPS)) * g2_ref[...]
    xn_ref[...] = xn.astype(BF16)
    h1 = xn.astype(BF16)
    h2 = (xn - h1.astype(F32)).astype(BF16)
    logits = (jnp.dot(h1, wr1_ref[...], preferred_element_type=F32)
              + jnp.dot(h1, wr2_ref[...], preferred_element_type=F32)
              + jnp.dot(h2, wr1_ref[...], preferred_element_type=F32)) + br_ref[...]
    gates_ref[...] = _route(logits)


def _mix_and_route(o_a, o_b, x, w_out, g_norm, w_rg, b_rg, w_re, b_re):
    n, d = x.shape
    bt = min(TOKEN_TILE, n)
    wa = w_out[:GROUP_W].astype(BF16)
    wb = w_out[GROUP_W:].astype(BF16)
    n_r = N_GROUPS + N_EXPERTS
    wr = jnp.concatenate([w_rg, w_re, jnp.zeros((d, LANES - n_r), F32)], axis=1)
    wr1 = wr.astype(BF16)
    wr2 = (wr - wr1.astype(F32)).astype(BF16)
    br = jnp.concatenate([b_rg, b_re, jnp.zeros((LANES - n_r,), F32)])[None, :]
    row = lambda width: pl.BlockSpec((bt, width), lambda i: (i, 0))
    return pl.pallas_call(
        _mix_body,
        grid=(n // bt,),
        in_specs=[row(GROUP_W), row(GROUP_W), row(d), _const_spec((GROUP_W, d)), _const_spec((GROUP_W, d)),
                  _const_spec((1, d)), _const_spec((d, LANES)), _const_spec((d, LANES)), _const_spec((1, LANES))],
        out_specs=[row(d), row(d), row(LANES)],
        out_shape=[jax.ShapeDtypeStruct((n, d), F32), jax.ShapeDtypeStruct((n, d), BF16),
                   jax.ShapeDtypeStruct((n, LANES), F32)],
        compiler_params=_cparams(("arbitrary",)),
        name="mix_route",
    )(o_a, o_b, x, wa, wb, g_norm[None, :], wr1, wr2, br)


def _expert_body(xn_ref, x1_ref, gates_ref, wg_ref, wu_ref, wd_ref, o_ref):
    e = pl.program_id(1)

    @pl.when(e == 0)
    def _():
        o_ref[...] = x1_ref[...]

    gates = gates_ref[...]
    lane = lax.broadcasted_iota(jnp.int32, gates.shape, 1)
    gate = jnp.sum(jnp.where(lane == e + ROUTER_LANE0, gates, 0.0), axis=-1, keepdims=True)
    xn = xn_ref[...]
    g = jnp.dot(xn, wg_ref[0], preferred_element_type=F32)
    u = jnp.dot(xn, wu_ref[0], preferred_element_type=F32)
    h = (g * jax.nn.sigmoid(g)) * u * gate
    o_ref[...] += jnp.dot(h.astype(BF16), wd_ref[0], preferred_element_type=F32)


def _experts(xn, x1, gates, w_gate, w_up, w_down):
    n, d = x1.shape
    ff = w_gate.shape[-1]
    bt = min(MOE_TILE, n)
    row = lambda width: pl.BlockSpec((bt, width), lambda i, e: (i, 0))
    return pl.pallas_call(
        _expert_body,
        grid=(n // bt, N_EXPERTS),
        in_specs=[row(d), row(d), row(LANES),
                  pl.BlockSpec((1, d, ff), lambda i, e: (e, 0, 0)),
                  pl.BlockSpec((1, d, ff), lambda i, e: (e, 0, 0)),
                  pl.BlockSpec((1, ff, d), lambda i, e: (e, 0, 0))],
        out_specs=row(d),
        out_shape=jax.ShapeDtypeStruct((n, d), F32),
        compiler_params=_cparams(("arbitrary", "arbitrary")),
        name="experts",
    )(xn, x1, gates, w_gate, w_up, w_down)


def _ffn(o_a, o_b, x, w_out, g_norm_ffn, w_rg, b_rg, w_re, b_re, wg, wu, wd):
    b, t, d = x.shape
    n = b * t
    x1, xn, gates = _mix_and_route(o_a.reshape(n, -1), o_b.reshape(n, -1), x.reshape(n, d), w_out, g_norm_ffn,
                                   w_rg, b_rg, w_re, b_re)
    return _experts(xn, x1, gates, wg, wu, wd).reshape(b, t, d)


def kernel(x_prompt, x_sample, cache_a_k, cache_a_v, cache_b_k, cache_b_v, cache_b_logf, g_norm_mix, w_in, b_forget, g_qk_a, g_qk_b, lambda_qk, g_out_a, g_out_b, w_out, rel_bias, g_norm_ffn, w_router_group, b_router_group, w_router_expert, b_router_expert, w_exp_gate, w_exp_up, w_exp_down):
    depth = w_in.shape[0]
    assert depth == 1, "single-layer step only"
    bp, tp, d = x_prompt.shape
    bs, ts, _ = x_sample.shape
    past = cache_a_k.shape[2]
    w_in0, w_out0 = w_in[0], w_out[0]
    wg, wu, wd = w_exp_gate[0].astype(BF16), w_exp_up[0].astype(BF16), w_exp_down[0].astype(BF16)
    ffn_w = (w_out0, g_norm_ffn[0], w_router_group[0], b_router_group[0], w_router_expert[0], b_router_expert[0],
             wg, wu, wd)

    (ka_p, va_p, kb_p, vb_p, logf_p, qa_bf, ka_bf, va_bf, qb_aug, kb_aug, vb_bf) = _projection(
        x_prompt, g_norm_mix[0], w_in0, b_forget[0], g_qk_a[0], g_qk_b[0], with_aug=True)
    blk = min(ATTN_BLOCK, tp)
    pos = jnp.arange(blk, dtype=jnp.int32)
    bkt_p = jnp.stack([_bucket_map(pos + blk, pos + blk), _bucket_map(pos + blk, pos)])
    bias_p = _bias_tiles(rel_bias, bkt_p)
    o_a = _diff_attention(qa_bf, ka_bf, va_bf, bias_p, lambda_qk[0], g_out_a[0])
    o_b = _fox_attention(qb_aug, kb_aug, vb_bf, g_out_b[0])
    y_p = _ffn(o_a, o_b, x_prompt, *ffn_w)

    xs = x_sample.reshape(1, bs * ts, d)
    (ka_s, va_s, kb_s, vb_s, logf_s, qa_s, qb_s) = _projection(
        xs, g_norm_mix[0], w_in0, b_forget[0], g_qk_a[0], g_qk_b[0], with_aug=False)
    per_stream = lambda a: a.reshape(bs, ts, a.shape[-1])
    ka_s, va_s, kb_s, vb_s, logf_s, qa_s, qb_s = map(per_stream, (ka_s, va_s, kb_s, vb_s, logf_s, qa_s, qb_s))
    ck = min(CACHE_CHUNK, past)
    q_pos = past + jnp.arange(ts, dtype=jnp.int32)
    bkt_c = _bucket_map(q_pos, jnp.arange(past, dtype=jnp.int32)).reshape(ts, past // ck, ck).transpose(1, 0, 2)
    bias_c = _bias_tiles(rel_bias, bkt_c)
    bias_n = _bias_tiles(rel_bias, _bucket_map(q_pos, q_pos)[None])
    w_a = H_A * 2 * HEAD_DIM
    o_a_s = _diff_decode(qa_s, ka_s, va_s, cache_a_k[0].reshape(bs, past, w_a), cache_a_v[0].reshape(bs, past, w_a),
                         bias_c, bias_n, lambda_qk[0], g_out_a[0])
    w_b = H_B * HEAD_DIM
    o_b_s = _fox_decode(qb_s, kb_s, vb_s, logf_s, cache_b_k[0].reshape(bs, past, w_b),
                        cache_b_v[0].reshape(bs, past, w_b), cache_b_logf[0], g_out_b[0])
    y_s = _ffn(o_a_s, o_b_s, x_sample, *ffn_w)

    def rows(ka, va, kb, vb, logf, b, t):
        return (ka.reshape(1, b, t, H_A, 2, HEAD_DIM), va.reshape(1, b, t, H_A, 2 * HEAD_DIM),
                kb.reshape(1, b, t, H_B, HEAD_DIM), vb.reshape(1, b, t, H_B, HEAD_DIM), logf.reshape(1, b, t, H_B))

    return (y_p, y_s) + rows(ka_p, va_p, kb_p, vb_p, logf_p, bp, tp) + rows(ka_s, va_s, kb_s, vb_s, logf_s, bs, ts)
```

```python
import functools
import math

import jax
import jax.numpy as jnp
from jax import lax
from jax.experimental import pallas as pl
from jax.experimental.pallas import tpu as pltpu

F32 = jnp.float32
BF16 = jnp.bfloat16

LANES = 128
VMEM_LIMIT_BYTES = 56 * 1024 * 1024

HEAD_DIM = 64
H_A = 4
H_B = 8
GROUP_W = 512
MAIN_W = 6 * GROUP_W
CHUNK = 64
N_BUCKETS = 32
MAX_DISTANCE = 128
N_GROUPS = 4
EXPERTS_PER_GROUP = 8
N_EXPERTS = N_GROUPS * EXPERTS_PER_GROUP
ROUTER_LANE0 = N_GROUPS
RMS_EPS = 1e-6
NEG = -1e30
LOG2E = 1.4426950408889634
QK_SCALE = HEAD_DIM ** -0.5
LAM_INIT = 0.8 - 0.6 * math.exp(-0.3 * 0)
N_CPARTS = 3

ONES_ROWS = 16
VA_ROWS = 2 * HEAD_DIM + ONES_ROWS
VB_ROWS = HEAD_DIM + ONES_ROWS

TOKEN_TILE = 512
ATTN_BLOCK = 512
KV_BLOCK = 256
CACHE_CHUNK = 1024
MOE_TILE = 1024


def _cparams(sem):
    return pltpu.CompilerParams(dimension_semantics=sem, vmem_limit_bytes=VMEM_LIMIT_BYTES)


def _const_spec(shape):
    nd = len(shape)
    return pl.BlockSpec(shape, lambda *_: (0,) * nd)


def _split3(x):
    p1 = x.astype(BF16).astype(F32)
    r1 = x - p1
    p2 = r1.astype(BF16).astype(F32)
    p3 = (r1 - p2).astype(BF16).astype(F32)
    return p1, p2, p3


def _lane_groups(parts, lane):
    return jnp.where(lane < 8, parts[0], jnp.where(lane < 16, parts[1], parts[2]))


def _cumsum_rows(tri, x, carry):
    c = carry
    for part in _split3(x):
        c = c + jnp.dot(tri, part.astype(BF16), preferred_element_type=F32)
    return c


def _log_sigmoid(x):
    return jnp.minimum(x, 0.0) - jnp.log(1.0 + jnp.exp(-jnp.abs(x)))


def _group_rms(raw, gain_row, gn):
    ms = jnp.dot((raw * raw).astype(BF16), gn, preferred_element_type=F32)
    return raw * lax.rsqrt(ms + RMS_EPS) * gain_row


def _online_update(state, s, v):
    m, l, acc = state
    m_new = jnp.maximum(m, jnp.max(s, axis=-1, keepdims=True))
    p = jnp.exp2(s - m_new)
    alpha = jnp.exp2(m - m_new)
    l_new = alpha * l + jnp.sum(p, axis=-1, keepdims=True)
    acc_new = alpha * acc + jnp.dot(p.astype(BF16), v, preferred_element_type=F32)
    return m_new, l_new, acc_new


def _qk(q, k):
    return lax.dot_general(q, k, (((1,), (1,)), ((), ())), preferred_element_type=F32)


def _proj_body(*refs, with_aug, bt, bk):
    (x_ref, g1_ref, wm_ref, wf_ref, bf_ref, gains_ref, gn_ref, tri_ref, place_ref) = refs[:9]
    ka_ref, va_ref, kb_ref, vb_ref, logf_ref = refs[9:14]
    x = x_ref[0]
    ms = jnp.mean(x * x, axis=-1, keepdims=True)
    xn = (x * lax.rsqrt(ms + RMS_EPS)) * g1_ref[...]
    xb = xn.astype(BF16)
    proj = jnp.dot(xb, wm_ref[...], preferred_element_type=F32)
    gains = gains_ref[...]
    gn = gn_ref[...]
    w = GROUP_W
    qa = _group_rms(proj[:, 0:w], gains[0:1], gn)
    ka = _group_rms(proj[:, w:2 * w], gains[1:2], gn)
    va = proj[:, 2 * w:3 * w]
    qb = _group_rms(proj[:, 3 * w:4 * w], gains[2:3], gn)
    kb = _group_rms(proj[:, 4 * w:5 * w], gains[3:4], gn)
    vb = proj[:, 5 * w:6 * w]
    ka_ref[0] = ka
    va_ref[0] = va
    kb_ref[0] = kb
    vb_ref[0] = vb
    fl = jnp.dot(xb, wf_ref[...], preferred_element_type=F32) + bf_ref[...]
    logf = _log_sigmoid(fl)
    logf_ref[0] = logf[:, 0:H_B]
    qscale = QK_SCALE * LOG2E
    if not with_aug:
        qa_ref, qb_ref = refs[14:16]
        qa_ref[0] = (qa * qscale).astype(BF16)
        qb_ref[0] = (qb * qscale).astype(BF16)
        return
    qa_t, ka_bf, va_t, qb_t, kb_aug, vb_t, carry_ref = refs[14:21]
    n_chunk = bt // bk
    row = lax.broadcasted_iota(jnp.int32, (LANES, bt), 0)
    ones_tail = jnp.where(lax.broadcasted_iota(jnp.int32, (ONES_ROWS, bt), 0) == 0, 1.0, 0.0)

    def put_chunks(ref, idx, vt):
        vt = vt.astype(BF16)
        for c in range(n_chunk):
            ref[0, idx, c] = vt[:, c * bk:(c + 1) * bk]

    for h in range(H_A):
        sl = slice(h * LANES, (h + 1) * LANES)
        q_t = (qa[:, sl] * qscale).T
        qa_t[0, 2 * h] = jnp.where(row < HEAD_DIM, q_t, 0.0).astype(BF16)
        qa_t[0, 2 * h + 1] = jnp.where(row >= HEAD_DIM, q_t, 0.0).astype(BF16)
        ka_bf[0, h] = ka[:, sl].astype(BF16)
        put_chunks(va_t, h, jnp.concatenate([va[:, sl].T, ones_tail], axis=0))
        vb_pair_t = vb[:, sl].T
        for hh in range(2):
            put_chunks(vb_t, 2 * h + hh,
                       jnp.concatenate([vb_pair_t[hh * HEAD_DIM:(hh + 1) * HEAD_DIM], ones_tail], axis=0))

    @pl.when(pl.program_id(1) == 0)
    def _():
        carry_ref[...] = jnp.zeros_like(carry_ref)

    c = _cumsum_rows(tri_ref[...], logf * LOG2E, carry_ref[0:1, :])
    carry_ref[0:1, :] = c[bt - 1:bt, :]
    lane = lax.broadcasted_iota(jnp.int32, (bt, LANES), 1)
    cparts = _lane_groups(_split3(c), lane).astype(BF16)
    extras = jnp.dot(cparts, place_ref[...], preferred_element_type=F32)
    ones_q = jnp.where((lane >= HEAD_DIM + N_CPARTS) & (lane < HEAD_DIM + 2 * N_CPARTS), 1.0, 0.0)
    ones_k = jnp.where((lane >= HEAD_DIM) & (lane < HEAD_DIM + N_CPARTS), 1.0, 0.0)
    for h in range(H_B):
        sl = slice((h // 2) * LANES, (h // 2 + 1) * LANES)
        qp = qb[:, sl] * qscale
        kp = kb[:, sl]
        if h % 2:
            qp = pltpu.roll(qp, HEAD_DIM, 1)
            kp = pltpu.roll(kp, HEAD_DIM, 1)
        eq = extras[:, h * LANES:(h + 1) * LANES] + ones_q
        ek = extras[:, (H_B + h) * LANES:(H_B + h + 1) * LANES] + ones_k
        qb_t[0, h] = jnp.where(lane < HEAD_DIM, qp, eq).T.astype(BF16)
        kb_aug[0, h] = jnp.where(lane < HEAD_DIM, kp, ek).astype(BF16)


def _blockdiag_mean(n, group):
    r = jnp.arange(n)
    return jnp.where((r[:, None] // group) == (r[None, :] // group), 1.0 / group, 0.0).astype(BF16)


def _tri(n):
    r = jnp.arange(n)
    return (r[None, :] <= r[:, None]).astype(BF16)


def _place_matrix():
    rows = jnp.arange(LANES)[:, None]
    cols = jnp.arange(2 * H_B * LANES)[None, :]
    p, h = rows // 8, rows % 8
    valid = rows < 8 * N_CPARTS
    qcol = h * LANES + HEAD_DIM + p
    kcol = (H_B + h) * LANES + HEAD_DIM + N_CPARTS + p
    m = jnp.where(valid & (cols == qcol), 1.0, 0.0) - jnp.where(valid & (cols == kcol), 1.0, 0.0)
    return m.astype(BF16)


def _projection(x, g_norm, w_in, b_forget, g_qk_a, g_qk_b, *, with_aug):
    b, t, d = x.shape
    bt = min(TOKEN_TILE, t)
    nt = t // bt
    wm = w_in[:, :MAIN_W].astype(BF16)
    wf_cols = w_in[:, MAIN_W:MAIN_W + H_B]
    wf = jnp.concatenate([wf_cols] * N_CPARTS + [jnp.zeros((d, LANES - H_B * N_CPARTS), F32)], axis=1).astype(BF16)
    bfv = jnp.concatenate([b_forget] * N_CPARTS + [jnp.zeros((LANES - H_B * N_CPARTS,), F32)])[None, :]
    gains = jnp.stack([jnp.tile(g_qk_a[0], 2 * H_A), jnp.tile(g_qk_a[1], 2 * H_A),
                       jnp.tile(g_qk_b[0], H_B), jnp.tile(g_qk_b[1], H_B)])
    gn = _blockdiag_mean(GROUP_W, HEAD_DIM)
    tri = _tri(bt)
    place = _place_matrix()
    in_specs = [
        pl.BlockSpec((1, bt, d), lambda i, j: (i, j, 0)),
        _const_spec((1, d)),
        pl.BlockSpec((d, MAIN_W), lambda i, j: (0, 0), pipeline_mode=pl.Buffered(1)),
        _const_spec((d, LANES)), _const_spec((1, LANES)), _const_spec((4, GROUP_W)),
        _const_spec((GROUP_W, GROUP_W)), _const_spec((bt, bt)), _const_spec((LANES, 2 * H_B * LANES)),
    ]
    row_spec = pl.BlockSpec((1, bt, GROUP_W), lambda i, j: (i, j, 0))
    out_shape = [jax.ShapeDtypeStruct((b, t, GROUP_W), F32)] * 4 + [jax.ShapeDtypeStruct((b, t, H_B), F32)]
    out_specs = [row_spec] * 4 + [pl.BlockSpec((1, bt, H_B), lambda i, j: (i, j, 0))]
    scratch = []
    bk = min(KV_BLOCK, bt)
    if with_aug:
        def add(shape, block, index_map):
            out_shape.append(jax.ShapeDtypeStruct(shape, BF16))
            out_specs.append(pl.BlockSpec(block, index_map))

        rows_major = lambda i, j: (i, 0, j, 0)
        time_minor = lambda i, j: (i, 0, 0, j)
        chunked = lambda i, j: (i, 0, j, 0, 0)
        add((b, 2 * H_A, LANES, t), (1, 2 * H_A, LANES, bt), time_minor)
        add((b, H_A, t, LANES), (1, H_A, bt, LANES), rows_major)
        add((b, H_A, t // bk, VA_ROWS, bk), (1, H_A, bt // bk, VA_ROWS, bk), chunked)
        add((b, H_B, LANES, t), (1, H_B, LANES, bt), time_minor)
        add((b, H_B, t, LANES), (1, H_B, bt, LANES), rows_major)
        add((b, H_B, t // bk, VB_ROWS, bk), (1, H_B, bt // bk, VB_ROWS, bk), chunked)
        scratch = [pltpu.VMEM((8, LANES), F32)]
    else:
        out_shape += [jax.ShapeDtypeStruct((b, t, GROUP_W), BF16)] * 2
        out_specs += [row_spec] * 2
    return pl.pallas_call(
        functools.partial(_proj_body, with_aug=with_aug, bt=bt, bk=bk),
        grid=(b, nt), in_specs=in_specs, out_specs=out_specs, out_shape=out_shape, scratch_shapes=scratch,
        compiler_params=_cparams(("arbitrary", "arbitrary")),
        name="proj_aug" if with_aug else "proj_plain",
    )(x, g_norm[None, :], wm, wf, bfv, gains, gn, tri, place)


def _t5_bucket(rel):
    nb = N_BUCKETS // 2
    max_exact = nb // 2
    base = jnp.where(rel > 0, nb, 0)
    n = jnp.abs(rel)
    large = max_exact + (jnp.log(jnp.maximum(n, max_exact).astype(jnp.float32) / max_exact)
                         / math.log(MAX_DISTANCE / max_exact) * (nb - max_exact)).astype(jnp.int32)
    large = jnp.minimum(large, nb - 1)
    return base + jnp.where(n < max_exact, n, large)


def _bucket_map(q_pos, k_pos):
    bkt = _t5_bucket(k_pos[None, :] - q_pos[:, None])
    visible = (k_pos[None, :] // CHUNK) <= (q_pos[:, None] // CHUNK)
    return jnp.where(visible, bkt, -1).astype(jnp.int32)


def _bias_body(rb_ref, bkt_ref, o_ref):
    h = pl.program_id(0)
    bkt = bkt_ref[0]
    far = rb_ref[N_BUCKETS // 2 - 1, h]
    acc = jnp.zeros(bkt.shape, F32)
    for b in range(N_BUCKETS):
        acc = jnp.where(bkt == b, rb_ref[b, h] - far, acc)
    o_ref[0, 0] = jnp.where(bkt < 0, NEG, acc * LOG2E)


def _bias_tiles(rel_bias, bkt):
    n, r, c = bkt.shape
    return pl.pallas_call(
        _bias_body,
        grid=(H_A, n),
        in_specs=[pl.BlockSpec(memory_space=pltpu.SMEM), pl.BlockSpec((1, r, c), lambda h, i: (i, 0, 0))],
        out_specs=pl.BlockSpec((1, 1, r, c), lambda h, i: (h, i, 0, 0)),
        out_shape=jax.ShapeDtypeStruct((H_A, n, r, c), F32),
        compiler_params=_cparams(("arbitrary", "arbitrary")),
        name="bias_tiles",
    )(rel_bias, bkt)


def _lam(lq):
    a = jnp.sum(lq[0:1, :] * lq[1:2, :], axis=-1, keepdims=True)
    b = jnp.sum(lq[2:3, :] * lq[3:4, :], axis=-1, keepdims=True)
    return jnp.exp(a) - jnp.exp(b) + LAM_INIT


def _attn_init_t(rows, bq):
    return (jnp.full((1, bq), NEG, F32), jnp.zeros((rows, bq), F32))


def _online_update_t(state, st, vt):
    m, acc = state
    m_new = jnp.maximum(m, jnp.max(st, axis=0, keepdims=True))
    p = jnp.exp2(st - m_new)
    alpha = jnp.exp2(m - m_new)
    return m_new, alpha * acc + jnp.dot(vt, p.astype(BF16), preferred_element_type=F32)


def _normalized_t(state, rows):
    acc = state[1]
    return acc[0:rows] / acc[rows:rows + 1]


def _pipelined_sweep(i, n_sub, n_near, slots, score_fn, value_fn, modify, rows, blk):
    assert n_sub % 2 == 0 and n_near in (1, 2)

    def run_block(jb, states, near, last):
        states = list(states)
        for s in range(n_sub):
            j = jb * n_sub + s
            cur, nxt = slots[s % 2], slots[(s + 1) % 2]
            for c in range(2):
                if not (last and s == n_sub - 1):
                    nxt[c] = score_fn(c, j + 1)
                st = cur[c]
                if near is not None:
                    st = modify(st, near, s)
                states[c] = _online_update_t(states[c], st, value_fn(c, j))
        return tuple(states)

    for c in range(2):
        slots[0][c] = score_fn(c, 0)
    states = (_attn_init_t(rows, blk), _attn_init_t(rows, blk))
    n_far = jnp.maximum(i + 1 - n_near, 0)
    states = lax.fori_loop(0, n_far, lambda jb, st: run_block(jb, st, None, False), states)
    if n_near == 2:
        states = lax.fori_loop(n_far, i, lambda jb, st: run_block(jb, st, 1, False), states)
    return run_block(i, states, 0, True)


def _fox_body(qt_ref, k_ref, vt_ref, gout_ref, gn_ref, o_ref, s0_scr, s1_scr, *, blk, bk):
    i = pl.program_id(2)
    krow = lax.broadcasted_iota(jnp.int32, (bk, blk), 0)
    qcol = lax.broadcasted_iota(jnp.int32, (bk, blk), 1)
    qts = (qt_ref[0, 0], qt_ref[0, 1])

    def score_fn(hh, j):
        off = pl.multiple_of(j * bk, bk)
        return jnp.dot(k_ref[0, hh, pl.ds(off, bk), :], qts[hh], preferred_element_type=F32)

    def causal(st, near, s):
        return jnp.where(krow + s * bk <= qcol, st, NEG)

    states = _pipelined_sweep(i, blk // bk, 1, (s0_scr, s1_scr), score_fn, lambda hh, j: vt_ref[0, hh, j],
                              causal, VB_ROWS, blk)
    o_t = jnp.concatenate([_normalized_t(states[0], HEAD_DIM), _normalized_t(states[1], HEAD_DIM)], axis=0)
    o_ref[0] = (_group_rms(o_t.T, gout_ref[0], gn_ref[...])).astype(BF16)


def _fox_attention(qb_t, kb_aug, vb_t, g_out_b):
    b, _, _, t = qb_t.shape
    blk = min(ATTN_BLOCK, t)
    bk = vb_t.shape[-1]
    pairs = H_B // 2
    gout = g_out_b.reshape(pairs, 1, LANES)
    return pl.pallas_call(
        functools.partial(_fox_body, blk=blk, bk=bk),
        grid=(b, pairs, t // blk),
        in_specs=[
            pl.BlockSpec((1, 2, LANES, blk), lambda bi, p, i: (bi, p, 0, i)),
            pl.BlockSpec((1, 2, t, LANES), lambda bi, p, i: (bi, p, 0, 0)),
            pl.BlockSpec((1, 2, t // bk, VB_ROWS, bk), lambda bi, p, i: (bi, p, 0, 0, 0)),
            pl.BlockSpec((1, 1, LANES), lambda bi, p, i: (p, 0, 0)),
            _const_spec((LANES, LANES)),
        ],
        out_specs=pl.BlockSpec((1, blk, LANES), lambda bi, p, i: (bi, i, p)),
        out_shape=jax.ShapeDtypeStruct((b, t, H_B * HEAD_DIM), BF16),
        scratch_shapes=[pltpu.VMEM((2, bk, blk), F32), pltpu.VMEM((2, bk, blk), F32)],
        compiler_params=_cparams(("arbitrary", "arbitrary", "arbitrary")),
        name="fox_attention",
    )(qb_t, kb_aug, vb_t, gout, _blockdiag_mean(LANES, HEAD_DIM))


def _diff_finish(states, lam, gout):
    (_, l1, a1), (_, l2, a2) = states
    o = a1 / l1 - lam * (a2 / l2)
    ms = jnp.mean(o * o, axis=-1, keepdims=True)
    return (o * lax.rsqrt(ms + RMS_EPS)) * gout * (1.0 - LAM_INIT)


def _diff_body(lam_ref, qt_ref, k_ref, vt_ref, bias_ref, gout_ref, o_ref, s0_scr, s1_scr, *, blk, bk):
    i = pl.program_id(2)
    n_sub = blk // bk
    qts = (qt_ref[0, 0], qt_ref[0, 1])

    def score_fn(mi, j):
        off = pl.multiple_of(j * bk, bk)
        return jnp.dot(k_ref[0, 0, pl.ds(off, bk), :], qts[mi], preferred_element_type=F32)

    def add_bias(st, near, s):
        return st + bias_ref[0, (1 - near) * n_sub + s]

    states = _pipelined_sweep(i, n_sub, 2, (s0_scr, s1_scr), score_fn, lambda mi, j: vt_ref[0, 0, j],
                              add_bias, VA_ROWS, blk)
    lam = _lam(lam_ref[...])
    o = (_normalized_t(states[0], 2 * HEAD_DIM) - lam * _normalized_t(states[1], 2 * HEAD_DIM)).T
    ms = jnp.mean(o * o, axis=-1, keepdims=True)
    o_ref[0] = ((o * lax.rsqrt(ms + RMS_EPS)) * gout_ref[0] * (1.0 - LAM_INIT)).astype(BF16)


def _diff_attention(qa_t, ka_bf, va_t, bias, lambda_qk, g_out_a):
    b, _, _, t = qa_t.shape
    blk = min(ATTN_BLOCK, t)
    bk = va_t.shape[-1]
    gout = g_out_a.reshape(H_A, 1, LANES)
    return pl.pallas_call(
        functools.partial(_diff_body, blk=blk, bk=bk),
        grid=(b, H_A, t // blk),
        in_specs=[
            _const_spec((4, HEAD_DIM)),
            pl.BlockSpec((1, 2, LANES, blk), lambda bi, h, i: (bi, h, 0, i)),
            pl.BlockSpec((1, 1, t, LANES), lambda bi, h, i: (bi, h, 0, 0)),
            pl.BlockSpec((1, 1, t // bk, VA_ROWS, bk), lambda bi, h, i: (bi, h, 0, 0, 0)),
            pl.BlockSpec((1, 2 * (blk // bk), bk, blk), lambda bi, h, i: (h, 0, 0, 0)),
            pl.BlockSpec((1, 1, LANES), lambda bi, h, i: (h, 0, 0)),
        ],
        out_specs=pl.BlockSpec((1, blk, LANES), lambda bi, h, i: (bi, i, h)),
        out_shape=jax.ShapeDtypeStruct((b, t, H_A * 2 * HEAD_DIM), BF16),
        scratch_shapes=[pltpu.VMEM((2, bk, blk), F32), pltpu.VMEM((2, bk, blk), F32)],
        compiler_params=_cparams(("arbitrary", "arbitrary", "arbitrary")),
        name="diff_attention",
    )(lambda_qk, qa_t, ka_bf, va_t, bias, gout)


def _dec_load(m_scr, l_scr, acc_scr, idx):
    return m_scr[idx], l_scr[idx], acc_scr[idx]


def _dec_store(m_scr, l_scr, acc_scr, idx, state):
    m_scr[idx], l_scr[idx], acc_scr[idx] = state


def _dec_init(m_scr, l_scr, acc_scr):
    m_scr[...] = jnp.full(m_scr.shape, NEG, F32)
    l_scr[...] = jnp.zeros(l_scr.shape, F32)
    acc_scr[...] = jnp.zeros(acc_scr.shape, F32)


def _diff_dec_body(lam_ref, q_ref, kn_ref, vn_ref, kc_ref, vc_ref, bc_ref, bn_ref, gout_ref, o_ref,
                   m_scr, l_scr, acc_scr, *, nq):
    kc = pl.program_id(1)
    lane = lax.broadcasted_iota(jnp.int32, (nq, LANES), 1)

    @pl.when(kc == 0)
    def _():
        _dec_init(m_scr, l_scr, acc_scr)

    def sweep(k_all, v_all, bias_of):
        for h in range(H_A):
            sl = slice(h * LANES, (h + 1) * LANES)
            qfull = q_ref[0][:, sl]
            zero = jnp.zeros_like(qfull)
            k = k_all[:, sl].astype(BF16)
            v = v_all[:, sl].astype(BF16)
            for mi in range(2):
                qm = jnp.where((lane < HEAD_DIM) if mi == 0 else (lane >= HEAD_DIM), qfull, zero)
                s = _qk(qm, k) + bias_of(h)
                idx = 2 * h + mi
                _dec_store(m_scr, l_scr, acc_scr, idx,
                           _online_update(_dec_load(m_scr, l_scr, acc_scr, idx), s, v))

    sweep(kc_ref[0], vc_ref[0], lambda h: bc_ref[h, 0])

    @pl.when(kc == pl.num_programs(1) - 1)
    def _():
        sweep(kn_ref[0], vn_ref[0], lambda h: bn_ref[h, 0])
        lam = _lam(lam_ref[...])
        for h in range(H_A):
            states = (_dec_load(m_scr, l_scr, acc_scr, 2 * h), _dec_load(m_scr, l_scr, acc_scr, 2 * h + 1))
            o_ref[0, :, h * LANES:(h + 1) * LANES] = _diff_finish(states, lam, gout_ref[h]).astype(BF16)


def _diff_decode(qa, ka_new, va_new, cache_k, cache_v, bias_c, bias_n, lambda_qk, g_out_a):
    b, nq, w = qa.shape
    past = cache_k.shape[1]
    ck = min(CACHE_CHUNK, past)
    n_kc = past // ck
    new_spec = pl.BlockSpec((1, nq, w), lambda bi, c: (bi, 0, 0))
    cache_spec = pl.BlockSpec((1, ck, w), lambda bi, c: (bi, c, 0))
    return pl.pallas_call(
        functools.partial(_diff_dec_body, nq=nq),
        grid=(b, n_kc),
        in_specs=[
            _const_spec((4, HEAD_DIM)), new_spec, new_spec, new_spec, cache_spec, cache_spec,
            pl.BlockSpec((H_A, 1, nq, ck), lambda bi, c: (0, c, 0, 0)),
            _const_spec((H_A, 1, nq, nq)),
            _const_spec((H_A, 1, LANES)),
        ],
        out_specs=new_spec,
        out_shape=jax.ShapeDtypeStruct((b, nq, w), BF16),
        scratch_shapes=[pltpu.VMEM((2 * H_A, nq, 1), F32), pltpu.VMEM((2 * H_A, nq, 1), F32),
                        pltpu.VMEM((2 * H_A, nq, LANES), F32)],
        compiler_params=_cparams(("arbitrary", "arbitrary")),
        name="diff_decode",
    )(lambda_qk, qa, ka_new, va_new, cache_k, cache_v, bias_c, bias_n, g_out_a.reshape(H_A, 1, LANES))


def _fox_dec_body(q_ref, kn_ref, vn_ref, lfn_ref, kc_ref, vc_ref, lfc_ref, tri_ref, place_ref, gout_ref, gn_ref,
                  o_ref, ck_scr, ckn_scr, qc_scr, m_scr, l_scr, acc_scr, *, nq, past, ck_rows, sub):
    kc = pl.program_id(1)
    lane_q = lax.broadcasted_iota(jnp.int32, (nq, LANES), 1)
    n_c = 8 * N_CPARTS

    def key_aug(c, lane):
        body = _lane_groups(_split3(c), lane)
        tail = jnp.where(lane < n_c + N_CPARTS, 1.0, 0.0)
        return jnp.where(lane < n_c, body, tail).astype(BF16)

    @pl.when(kc == 0)
    def _():
        _dec_init(m_scr, l_scr, acc_scr)
        tri = tri_ref[...]
        lane_s = lax.broadcasted_iota(jnp.int32, (sub, LANES), 1)
        carry = jnp.zeros((1, LANES), F32)
        for blk in range(past // sub):
            c = _cumsum_rows(tri, lfc_ref[0, blk * sub:(blk + 1) * sub, :] * LOG2E, carry)
            carry = c[sub - 1:sub, :]
            ck_scr[blk * sub:(blk + 1) * sub, :] = key_aug(c, lane_s)
        r = lax.broadcasted_iota(jnp.int32, (nq, nq), 0)
        cc = lax.broadcasted_iota(jnp.int32, (nq, nq), 1)
        tri_n = jnp.where(cc <= r, 1.0, 0.0).astype(BF16)
        cn = _cumsum_rows(tri_n, lfn_ref[0] * LOG2E, carry)
        ckn_scr[...] = key_aug(cn, lane_q)
        extras = jnp.dot(_lane_groups(_split3(cn), lane_q).astype(BF16), place_ref[...],
                         preferred_element_type=F32)
        for h in range(H_B):
            neg_sel = jnp.where((lane_q < n_c) & ((lane_q & 7) == h), -1.0, 0.0)
            qc_scr[h] = (extras[:, h * LANES:(h + 1) * LANES] + neg_sel).astype(BF16)

    def sweep(k_all, v_all, ck_all, masked):
        row = lax.broadcasted_iota(jnp.int32, (nq, nq), 0)
        col = lax.broadcasted_iota(jnp.int32, (nq, nq), 1)
        for h in range(H_B):
            sl = slice((h // 2) * LANES, (h // 2 + 1) * LANES)
            qfull = q_ref[0][:, sl]
            qm = jnp.where((lane_q < HEAD_DIM) if h % 2 == 0 else (lane_q >= HEAD_DIM), qfull, jnp.zeros_like(qfull))
            s = _qk(qm, k_all[:, sl].astype(BF16)) + _qk(qc_scr[h], ck_all)
            if masked:
                s = jnp.where(col <= row, s, NEG)
            _dec_store(m_scr, l_scr, acc_scr, h,
                       _online_update(_dec_load(m_scr, l_scr, acc_scr, h), s, v_all[:, sl].astype(BF16)))

    off = pl.multiple_of(kc * ck_rows, ck_rows)
    sweep(kc_ref[0], vc_ref[0], ck_scr[pl.ds(off, ck_rows), :], False)

    @pl.when(kc == pl.num_programs(1) - 1)
    def _():
        sweep(kn_ref[0], vn_ref[0], ckn_scr[...], True)
        for p in range(H_B // 2):
            _, l0, a0 = _dec_load(m_scr, l_scr, acc_scr, 2 * p)
            _, l1, a1 = _dec_load(m_scr, l_scr, acc_scr, 2 * p + 1)
            o = jnp.where(lane_q < HEAD_DIM, a0 / l0, a1 / l1)
            o_ref[0, :, p * LANES:(p + 1) * LANES] = _group_rms(o, gout_ref[p], gn_ref[...]).astype(BF16)


def _place_matrix_dec():
    rows = jnp.arange(LANES)[:, None]
    cols = jnp.arange(H_B * LANES)[None, :]
    p, h = rows // 8, rows % 8
    return jnp.where((rows < 8 * N_CPARTS) & (cols == h * LANES + 8 * N_CPARTS + p), 1.0, 0.0).astype(BF16)


def _rep_lanes(logf):
    pad = jnp.zeros(logf.shape[:-1] + (LANES - H_B * N_CPARTS,), logf.dtype)
    return jnp.concatenate([logf] * N_CPARTS + [pad], axis=-1)


def _fox_decode(qb, kb_new, vb_new, logf_new, cache_k, cache_v, cache_logf, g_out_b):
    b, nq, w = qb.shape
    past = cache_k.shape[1]
    ck = min(CACHE_CHUNK, past)
    n_kc = past // ck
    sub = min(512, past)
    new_spec = pl.BlockSpec((1, nq, w), lambda bi, c: (bi, 0, 0))
    cache_spec = pl.BlockSpec((1, ck, w), lambda bi, c: (bi, c, 0))
    return pl.pallas_call(
        functools.partial(_fox_dec_body, nq=nq, past=past, ck_rows=ck, sub=sub),
        grid=(b, n_kc),
        in_specs=[
            new_spec, new_spec, new_spec,
            pl.BlockSpec((1, nq, LANES), lambda bi, c: (bi, 0, 0)),
            cache_spec, cache_spec,
            pl.BlockSpec((1, past, LANES), lambda bi, c: (bi, 0, 0)),
            _const_spec((sub, sub)), _const_spec((LANES, H_B * LANES)),
            _const_spec((H_B // 2, 1, LANES)), _const_spec((LANES, LANES)),
        ],
        out_specs=new_spec,
        out_shape=jax.ShapeDtypeStruct((b, nq, w), BF16),
        scratch_shapes=[pltpu.VMEM((past, LANES), BF16), pltpu.VMEM((nq, LANES), BF16),
                        pltpu.VMEM((H_B, nq, LANES), BF16),
                        pltpu.VMEM((H_B, nq, 1), F32), pltpu.VMEM((H_B, nq, 1), F32),
                        pltpu.VMEM((H_B, nq, LANES), F32)],
        compiler_params=_cparams(("arbitrary", "arbitrary")),
        name="fox_decode",
    )(qb, kb_new, vb_new, _rep_lanes(logf_new), cache_k, cache_v, _rep_lanes(cache_logf), _tri(sub),
      _place_matrix_dec(), g_out_b.reshape(H_B // 2, 1, LANES), _blockdiag_mean(LANES, HEAD_DIM))


def _route(logits):
    lane_i = lax.broadcasted_iota(jnp.int32, logits.shape, 1)
    lane = lane_i.astype(F32)
    big = float(LANES)
    lg = jnp.where(lane_i < N_GROUPS, logits, NEG)
    mx = jnp.max(lg, axis=-1, keepdims=True)
    grp = jnp.min(jnp.where(lg == mx, lane, big), axis=-1, keepdims=True)
    p_grp = 1.0 / jnp.sum(jnp.exp(lg - mx), axis=-1, keepdims=True)
    e = lane_i - ROUTER_LANE0
    e_grp = lax.shift_right_arithmetic(e, 3).astype(F32)
    sel = (e >= 0) & (e < N_EXPERTS) & (e_grp == grp)
    v = jnp.where(sel, logits, NEG)
    v1 = jnp.max(v, axis=-1, keepdims=True)
    i1 = jnp.min(jnp.where(sel & (v == v1), lane, big), axis=-1, keepdims=True)
    sel2 = sel & (lane != i1)
    vv = jnp.where(sel2, logits, NEG)
    v2 = jnp.max(vv, axis=-1, keepdims=True)
    i2 = jnp.min(jnp.where(sel2 & (vv == v2), lane, big), axis=-1, keepdims=True)
    e2 = jnp.exp(v2 - v1)
    w1 = p_grp / (1.0 + e2)
    w2 = p_grp * e2 / (1.0 + e2)
    return jnp.where(lane == i1, w1, 0.0) + jnp.where(lane == i2, w2, 0.0)


def _mix_body(oa_ref, ob_ref, x_ref, wa_ref, wb_ref, g2_ref, wr1_ref, wr2_ref, br_ref, x1_ref, xn_ref, gates_ref):
    y = (jnp.dot(oa_ref[...], wa_ref[...], preferred_element_type=F32)
         + jnp.dot(ob_ref[...], wb_ref[...], preferred_element_type=F32))
    x1 = x_ref[...] + y
    x1_ref[...] = x1
    ms = jnp.mean(x1 * x1, axis=-1, keepdims=True)
    xn = (x1 * lax.rsqrt(ms + RMS_EPS)) * g2_ref[...]
    xn_ref[...] = xn.astype(BF16)
    h1 = xn.astype(BF16)
    h2 = (xn - h1.astype(F32)).astype(BF16)
    logits = (jnp.dot(h1, wr1_ref[...], preferred_element_type=F32)
              + jnp.dot(h1, wr2_ref[...], preferred_element_type=F32)
              + jnp.dot(h2, wr1_ref[...], preferred_element_type=F32)) + br_ref[...]
    gates_ref[...] = _route(logits)


def _mix_and_route(o_a, o_b, x, w_out, g_norm, w_rg, b_rg, w_re, b_re):
    n, d = x.shape
    bt = min(TOKEN_TILE, n)
    wa = w_out[:GROUP_W].astype(BF16)
    wb = w_out[GROUP_W:].astype(BF16)
    n_r = N_GROUPS + N_EXPERTS
    wr = jnp.concatenate([w_rg, w_re, jnp.zeros((d, LANES - n_r), F32)], axis=1)
    wr1 = wr.astype(BF16)
    wr2 = (wr - wr1.astype(F32)).astype(BF16)
    br = jnp.concatenate([b_rg, b_re, jnp.zeros((LANES - n_r,), F32)])[None, :]
    row = lambda width: pl.BlockSpec((bt, width), lambda i: (i, 0))
    return pl.pallas_call(
        _mix_body,
        grid=(n // bt,),
        in_specs=[row(GROUP_W), row(GROUP_W), row(d), _const_spec((GROUP_W, d)), _const_spec((GROUP_W, d)),
                  _const_spec((1, d)), _const_spec((d, LANES)), _const_spec((d, LANES)), _const_spec((1, LANES))],
        out_specs=[row(d), row(d), row(LANES)],
        out_shape=[jax.ShapeDtypeStruct((n, d), F32), jax.ShapeDtypeStruct((n, d), BF16),
                   jax.ShapeDtypeStruct((n, LANES), F32)],
        compiler_params=_cparams(("arbitrary",)),
        name="mix_route",
    )(o_a, o_b, x, wa, wb, g_norm[None, :], wr1, wr2, br)


def _expert_body(xn_ref, x1_ref, gates_ref, wg_ref, wu_ref, wd_ref, o_ref):
    e = pl.program_id(1)

    @pl.when(e == 0)
    def _():
        o_ref[...] = x1_ref[...]

    gates = gates_ref[...]
    lane = lax.broadcasted_iota(jnp.int32, gates.shape, 1)
    gate = jnp.sum(jnp.where(lane == e + ROUTER_LANE0, gates, 0.0), axis=-1, keepdims=True)
    xn = xn_ref[...]
    g = jnp.dot(xn, wg_ref[0], preferred_element_type=F32)
    u = jnp.dot(xn, wu_ref[0], preferred_element_type=F32)
    h = (g * jax.nn.sigmoid(g)) * u * gate
    o_ref[...] += jnp.dot(h.astype(BF16), wd_ref[0], preferred_element_type=F32)


def _experts(xn, x1, gates, w_gate, w_up, w_down):
    n, d = x1.shape
    ff = w_gate.shape[-1]
    bt = min(MOE_TILE, n)
    row = lambda width: pl.BlockSpec((bt, width), lambda i, e: (i, 0))
    return pl.pallas_call(
        _expert_body,
        grid=(n // bt, N_EXPERTS),
        in_specs=[row(d), row(d), row(LANES),
                  pl.BlockSpec((1, d, ff), lambda i, e: (e, 0, 0)),
                  pl.BlockSpec((1, d, ff), lambda i, e: (e, 0, 0)),
                  pl.BlockSpec((1, ff, d), lambda i, e: (e, 0, 0))],
        out_specs=row(d),
        out_shape=jax.ShapeDtypeStruct((n, d), F32),
        compiler_params=_cparams(("arbitrary", "arbitrary")),
        name="experts",
    )(xn, x1, gates, w_gate, w_up, w_down)


def _ffn(o_a, o_b, x, w_out, g_norm_ffn, w_rg, b_rg, w_re, b_re, wg, wu, wd):
    b, t, d = x.shape
    n = b * t
    x1, xn, gates = _mix_and_route(o_a.reshape(n, -1), o_b.reshape(n, -1), x.reshape(n, d), w_out, g_norm_ffn,
                                   w_rg, b_rg, w_re, b_re)
    return _experts(xn, x1, gates, wg, wu, wd).reshape(b, t, d)


def kernel(x_prompt, x_sample, cache_a_k, cache_a_v, cache_b_k, cache_b_v, cache_b_logf, g_norm_mix, w_in, b_forget, g_qk_a, g_qk_b, lambda_qk, g_out_a, g_out_b, w_out, rel_bias, g_norm_ffn, w_router_group, b_router_group, w_router_expert, b_router_expert, w_exp_gate, w_exp_up, w_exp_down):
    depth = w_in.shape[0]
    assert depth == 1, "single-layer step only"
    bp, tp, d = x_prompt.shape
    bs, ts, _ = x_sample.shape
    past = cache_a_k.shape[2]
    w_in0, w_out0 = w_in[0], w_out[0]
    wg, wu, wd = w_exp_gate[0].astype(BF16), w_exp_up[0].astype(BF16), w_exp_down[0].astype(BF16)
    ffn_w = (w_out0, g_norm_ffn[0], w_router_group[0], b_router_group[0], w_router_expert[0], b_router_expert[0],
             wg, wu, wd)

    (ka_p, va_p, kb_p, vb_p, logf_p, qa_t, ka_bf, va_t, qb_t, kb_aug, vb_t) = _projection(
        x_prompt, g_norm_mix[0], w_in0, b_forget[0], g_qk_a[0], g_qk_b[0], with_aug=True)
    blk = min(ATTN_BLOCK, tp)
    bk = va_t.shape[-1]
    assert blk % CHUNK == 0 and blk >= MAX_DISTANCE and blk % bk == 0
    q_pos = blk + jnp.arange(blk, dtype=jnp.int32)
    bkt_p = jnp.stack([_bucket_map(q_pos, s * bk + jnp.arange(bk, dtype=jnp.int32)).T
                       for s in range(2 * blk // bk)])
    bias_p = _bias_tiles(rel_bias, bkt_p)
    o_a = _diff_attention(qa_t, ka_bf, va_t, bias_p, lambda_qk[0], g_out_a[0])
    o_b = _fox_attention(qb_t, kb_aug, vb_t, g_out_b[0])
    y_p = _ffn(o_a, o_b, x_prompt, *ffn_w)

    xs = x_sample.reshape(1, bs * ts, d)
    (ka_s, va_s, kb_s, vb_s, logf_s, qa_s, qb_s) = _projection(
        xs, g_norm_mix[0], w_in0, b_forget[0], g_qk_a[0], g_qk_b[0], with_aug=False)
    per_stream = lambda a: a.reshape(bs, ts, a.shape[-1])
    ka_s, va_s, kb_s, vb_s, logf_s, qa_s, qb_s = map(per_stream, (ka_s, va_s, kb_s, vb_s, logf_s, qa_s, qb_s))
    ck = min(CACHE_CHUNK, past)
    q_pos = past + jnp.arange(ts, dtype=jnp.int32)
    bkt_c = _bucket_map(q_pos, jnp.arange(past, dtype=jnp.int32)).reshape(ts, past // ck, ck).transpose(1, 0, 2)
    bias_c = _bias_tiles(rel_bias, bkt_c)
    bias_n = _bias_tiles(rel_bias, _bucket_map(q_pos, q_pos)[None])
    w_a = H_A * 2 * HEAD_DIM
    o_a_s = _diff_decode(qa_s, ka_s, va_s, cache_a_k[0].reshape(bs, past, w_a), cache_a_v[0].reshape(bs, past, w_a),
                         bias_c, bias_n, lambda_qk[0], g_out_a[0])
    w_b = H_B * HEAD_DIM
    o_b_s = _fox_decode(qb_s, kb_s, vb_s, logf_s, cache_b_k[0].reshape(bs, past, w_b),
                        cache_b_v[0].reshape(bs, past, w_b), cache_b_logf[0], g_out_b[0])
    y_s = _ffn(o_a_s, o_b_s, x_sample, *ffn_w)

    def rows(ka, va, kb, vb, logf, b, t):
        return (ka.reshape(1, b, t, H_A, 2, HEAD_DIM), va.reshape(1, b, t, H_A, 2 * HEAD_DIM),
                kb.reshape(1, b, t, H_B, HEAD_DIM), vb.reshape(1, b, t, H_B, HEAD_DIM), logf.reshape(1, b, t, H_B))

    return (y_p, y_s) + rows(ka_p, va_p, kb_p, vb_p, logf_p, bp, tp) + rows(ka_s, va_s, kb_s, vb_s, logf_s, bs, ts)
```

```python
import functools
import math

import jax
import jax.numpy as jnp
from jax import lax
from jax.experimental import pallas as pl
from jax.experimental.pallas import tpu as pltpu

F32 = jnp.float32
BF16 = jnp.bfloat16

LANES = 128
VMEM_LIMIT_BYTES = 56 * 1024 * 1024

HEAD_DIM = 64
H_A = 4
H_B = 8
GROUP_W = 512
MAIN_W = 6 * GROUP_W
CHUNK = 64
N_BUCKETS = 32
MAX_DISTANCE = 128
N_GROUPS = 4
EXPERTS_PER_GROUP = 8
N_EXPERTS = N_GROUPS * EXPERTS_PER_GROUP
ROUTER_LANE0 = N_GROUPS
RMS_EPS = 1e-6
NEG = -1e30
LOG2E = 1.4426950408889634
QK_SCALE = HEAD_DIM ** -0.5
LAM_INIT = 0.8 - 0.6 * math.exp(-0.3 * 0)
N_CPARTS = 3

ONES_ROWS = 16
VA_ROWS = 2 * HEAD_DIM + ONES_ROWS
VB_ROWS = HEAD_DIM + ONES_ROWS

TOKEN_TILE = 512
ATTN_BLOCK = 512
KV_BLOCK = 256
CACHE_CHUNK = 1024
MOE_TILE = 1024
MOE_ROW_TILE = 256
ROUTED_MIN_ASSIGNMENTS = 4 * N_EXPERTS * MOE_ROW_TILE


def _cparams(sem):
    return pltpu.CompilerParams(dimension_semantics=sem, vmem_limit_bytes=VMEM_LIMIT_BYTES)


def _const_spec(shape):
    nd = len(shape)
    return pl.BlockSpec(shape, lambda *_: (0,) * nd)


def _split3(x):
    p1 = x.astype(BF16).astype(F32)
    r1 = x - p1
    p2 = r1.astype(BF16).astype(F32)
    p3 = (r1 - p2).astype(BF16).astype(F32)
    return p1, p2, p3


def _lane_groups(parts, lane):
    return jnp.where(lane < 8, parts[0], jnp.where(lane < 16, parts[1], parts[2]))


def _cumsum_rows(tri, x, carry):
    c = carry
    for part in _split3(x):
        c = c + jnp.dot(tri, part.astype(BF16), preferred_element_type=F32)
    return c


def _log_sigmoid(x):
    return jnp.minimum(x, 0.0) - jnp.log(1.0 + jnp.exp(-jnp.abs(x)))


def _group_rms(raw, gain_row, gn):
    ms = jnp.dot((raw * raw).astype(BF16), gn, preferred_element_type=F32)
    return raw * lax.rsqrt(ms + RMS_EPS) * gain_row


def _online_update(state, s, v):
    m, l, acc = state
    m_new = jnp.maximum(m, jnp.max(s, axis=-1, keepdims=True))
    p = jnp.exp2(s - m_new)
    alpha = jnp.exp2(m - m_new)
    l_new = alpha * l + jnp.sum(p, axis=-1, keepdims=True)
    acc_new = alpha * acc + jnp.dot(p.astype(BF16), v, preferred_element_type=F32)
    return m_new, l_new, acc_new


def _qk(q, k):
    return lax.dot_general(q, k, (((1,), (1,)), ((), ())), preferred_element_type=F32)


def _proj_body(*refs, with_aug, bt, bk):
    (x_ref, g1_ref, wm_ref, wf_ref, bf_ref, gains_ref, gn_ref, tri_ref, place_ref) = refs[:9]
    ka_ref, va_ref, kb_ref, vb_ref, logf_ref = refs[9:14]
    x = x_ref[0]
    ms = jnp.mean(x * x, axis=-1, keepdims=True)
    xn = (x * lax.rsqrt(ms + RMS_EPS)) * g1_ref[...]
    xb = xn.astype(BF16)
    proj = jnp.dot(xb, wm_ref[...], preferred_element_type=F32)
    gains = gains_ref[...]
    gn = gn_ref[...]
    w = GROUP_W
    qa = _group_rms(proj[:, 0:w], gains[0:1], gn)
    ka = _group_rms(proj[:, w:2 * w], gains[1:2], gn)
    va = proj[:, 2 * w:3 * w]
    qb = _group_rms(proj[:, 3 * w:4 * w], gains[2:3], gn)
    kb = _group_rms(proj[:, 4 * w:5 * w], gains[3:4], gn)
    vb = proj[:, 5 * w:6 * w]
    ka_ref[0] = ka
    va_ref[0] = va
    kb_ref[0] = kb
    vb_ref[0] = vb
    fl = jnp.dot(xb, wf_ref[...], preferred_element_type=F32) + bf_ref[...]
    logf = _log_sigmoid(fl)
    logf_ref[0] = logf[:, 0:H_B]
    qscale = QK_SCALE * LOG2E
    if not with_aug:
        qa_ref, qb_ref = refs[14:16]
        qa_ref[0] = (qa * qscale).astype(BF16)
        qb_ref[0] = (qb * qscale).astype(BF16)
        return
    qa_t, ka_bf, va_t, qb_t, kb_aug, vb_t, carry_ref = refs[14:21]
    n_chunk = bt // bk
    row = lax.broadcasted_iota(jnp.int32, (LANES, bt), 0)
    ones_tail = jnp.where(lax.broadcasted_iota(jnp.int32, (ONES_ROWS, bt), 0) == 0, 1.0, 0.0)

    def put_chunks(ref, idx, vt):
        vt = vt.astype(BF16)
        for c in range(n_chunk):
            ref[0, idx, c] = vt[:, c * bk:(c + 1) * bk]

    for h in range(H_A):
        sl = slice(h * LANES, (h + 1) * LANES)
        q_t = (qa[:, sl] * qscale).T
        qa_t[0, 2 * h] = jnp.where(row < HEAD_DIM, q_t, 0.0).astype(BF16)
        qa_t[0, 2 * h + 1] = jnp.where(row >= HEAD_DIM, q_t, 0.0).astype(BF16)
        ka_bf[0, h] = ka[:, sl].astype(BF16)
        put_chunks(va_t, h, jnp.concatenate([va[:, sl].T, ones_tail], axis=0))
        vb_pair_t = vb[:, sl].T
        for hh in range(2):
            put_chunks(vb_t, 2 * h + hh,
                       jnp.concatenate([vb_pair_t[hh * HEAD_DIM:(hh + 1) * HEAD_DIM], ones_tail], axis=0))

    @pl.when(pl.program_id(1) == 0)
    def _():
        carry_ref[...] = jnp.zeros_like(carry_ref)

    c = _cumsum_rows(tri_ref[...], logf * LOG2E, carry_ref[0:1, :])
    carry_ref[0:1, :] = c[bt - 1:bt, :]
    lane = lax.broadcasted_iota(jnp.int32, (bt, LANES), 1)
    cparts = _lane_groups(_split3(c), lane).astype(BF16)
    extras = jnp.dot(cparts, place_ref[...], preferred_element_type=F32)
    ones_q = jnp.where((lane >= HEAD_DIM + N_CPARTS) & (lane < HEAD_DIM + 2 * N_CPARTS), 1.0, 0.0)
    ones_k = jnp.where((lane >= HEAD_DIM) & (lane < HEAD_DIM + N_CPARTS), 1.0, 0.0)
    for h in range(H_B):
        sl = slice((h // 2) * LANES, (h // 2 + 1) * LANES)
        qp = qb[:, sl] * qscale
        kp = kb[:, sl]
        if h % 2:
            qp = pltpu.roll(qp, HEAD_DIM, 1)
            kp = pltpu.roll(kp, HEAD_DIM, 1)
        eq = extras[:, h * LANES:(h + 1) * LANES] + ones_q
        ek = extras[:, (H_B + h) * LANES:(H_B + h + 1) * LANES] + ones_k
        qb_t[0, h] = jnp.where(lane < HEAD_DIM, qp, eq).T.astype(BF16)
        kb_aug[0, h] = jnp.where(lane < HEAD_DIM, kp, ek).astype(BF16)


def _blockdiag_mean(n, group):
    r = jnp.arange(n)
    return jnp.where((r[:, None] // group) == (r[None, :] // group), 1.0 / group, 0.0).astype(BF16)


def _tri(n):
    r = jnp.arange(n)
    return (r[None, :] <= r[:, None]).astype(BF16)


def _place_matrix():
    rows = jnp.arange(LANES)[:, None]
    cols = jnp.arange(2 * H_B * LANES)[None, :]
    p, h = rows // 8, rows % 8
    valid = rows < 8 * N_CPARTS
    qcol = h * LANES + HEAD_DIM + p
    kcol = (H_B + h) * LANES + HEAD_DIM + N_CPARTS + p
    m = jnp.where(valid & (cols == qcol), 1.0, 0.0) - jnp.where(valid & (cols == kcol), 1.0, 0.0)
    return m.astype(BF16)


def _projection(x, g_norm, w_in, b_forget, g_qk_a, g_qk_b, *, with_aug):
    b, t, d = x.shape
    bt = min(TOKEN_TILE, t)
    nt = t // bt
    wm = w_in[:, :MAIN_W].astype(BF16)
    wf_cols = w_in[:, MAIN_W:MAIN_W + H_B]
    wf = jnp.concatenate([wf_cols] * N_CPARTS + [jnp.zeros((d, LANES - H_B * N_CPARTS), F32)], axis=1).astype(BF16)
    bfv = jnp.concatenate([b_forget] * N_CPARTS + [jnp.zeros((LANES - H_B * N_CPARTS,), F32)])[None, :]
    gains = jnp.stack([jnp.tile(g_qk_a[0], 2 * H_A), jnp.tile(g_qk_a[1], 2 * H_A),
                       jnp.tile(g_qk_b[0], H_B), jnp.tile(g_qk_b[1], H_B)])
    gn = _blockdiag_mean(GROUP_W, HEAD_DIM)
    tri = _tri(bt)
    place = _place_matrix()
    in_specs = [
        pl.BlockSpec((1, bt, d), lambda i, j: (i, j, 0)),
        _const_spec((1, d)),
        pl.BlockSpec((d, MAIN_W), lambda i, j: (0, 0), pipeline_mode=pl.Buffered(1)),
        _const_spec((d, LANES)), _const_spec((1, LANES)), _const_spec((4, GROUP_W)),
        _const_spec((GROUP_W, GROUP_W)), _const_spec((bt, bt)), _const_spec((LANES, 2 * H_B * LANES)),
    ]
    row_spec = pl.BlockSpec((1, bt, GROUP_W), lambda i, j: (i, j, 0))
    out_shape = [jax.ShapeDtypeStruct((b, t, GROUP_W), F32)] * 4 + [jax.ShapeDtypeStruct((b, t, H_B), F32)]
    out_specs = [row_spec] * 4 + [pl.BlockSpec((1, bt, H_B), lambda i, j: (i, j, 0))]
    scratch = []
    bk = min(KV_BLOCK, bt)
    if with_aug:
        def add(shape, block, index_map):
            out_shape.append(jax.ShapeDtypeStruct(shape, BF16))
            out_specs.append(pl.BlockSpec(block, index_map))

        rows_major = lambda i, j: (i, 0, j, 0)
        time_minor = lambda i, j: (i, 0, 0, j)
        chunked = lambda i, j: (i, 0, j, 0, 0)
        add((b, 2 * H_A, LANES, t), (1, 2 * H_A, LANES, bt), time_minor)
        add((b, H_A, t, LANES), (1, H_A, bt, LANES), rows_major)
        add((b, H_A, t // bk, VA_ROWS, bk), (1, H_A, bt // bk, VA_ROWS, bk), chunked)
        add((b, H_B, LANES, t), (1, H_B, LANES, bt), time_minor)
        add((b, H_B, t, LANES), (1, H_B, bt, LANES), rows_major)
        add((b, H_B, t // bk, VB_ROWS, bk), (1, H_B, bt // bk, VB_ROWS, bk), chunked)
        scratch = [pltpu.VMEM((8, LANES), F32)]
    else:
        out_shape += [jax.ShapeDtypeStruct((b, t, GROUP_W), BF16)] * 2
        out_specs += [row_spec] * 2
    return pl.pallas_call(
        functools.partial(_proj_body, with_aug=with_aug, bt=bt, bk=bk),
        grid=(b, nt), in_specs=in_specs, out_specs=out_specs, out_shape=out_shape, scratch_shapes=scratch,
        compiler_params=_cparams(("arbitrary", "arbitrary")),
        name="proj_aug" if with_aug else "proj_plain",
    )(x, g_norm[None, :], wm, wf, bfv, gains, gn, tri, place)


def _t5_bucket(rel):
    nb = N_BUCKETS // 2
    max_exact = nb // 2
    base = jnp.where(rel > 0, nb, 0)
    n = jnp.abs(rel)
    large = max_exact + (jnp.log(jnp.maximum(n, max_exact).astype(jnp.float32) / max_exact)
                         / math.log(MAX_DISTANCE / max_exact) * (nb - max_exact)).astype(jnp.int32)
    large = jnp.minimum(large, nb - 1)
    return base + jnp.where(n < max_exact, n, large)


def _bucket_map(q_pos, k_pos):
    bkt = _t5_bucket(k_pos[None, :] - q_pos[:, None])
    visible = (k_pos[None, :] // CHUNK) <= (q_pos[:, None] // CHUNK)
    return jnp.where(visible, bkt, -1).astype(jnp.int32)


def _bias_body(rb_ref, bkt_ref, o_ref):
    h = pl.program_id(0)
    bkt = bkt_ref[0]
    far = rb_ref[N_BUCKETS // 2 - 1, h]
    acc = jnp.zeros(bkt.shape, F32)
    for b in range(N_BUCKETS):
        acc = jnp.where(bkt == b, rb_ref[b, h] - far, acc)
    o_ref[0, 0] = jnp.where(bkt < 0, NEG, acc * LOG2E)


def _bias_tiles(rel_bias, bkt):
    n, r, c = bkt.shape
    return pl.pallas_call(
        _bias_body,
        grid=(H_A, n),
        in_specs=[pl.BlockSpec(memory_space=pltpu.SMEM), pl.BlockSpec((1, r, c), lambda h, i: (i, 0, 0))],
        out_specs=pl.BlockSpec((1, 1, r, c), lambda h, i: (h, i, 0, 0)),
        out_shape=jax.ShapeDtypeStruct((H_A, n, r, c), F32),
        compiler_params=_cparams(("arbitrary", "arbitrary")),
        name="bias_tiles",
    )(rel_bias, bkt)


def _lam(lq):
    a = jnp.sum(lq[0:1, :] * lq[1:2, :], axis=-1, keepdims=True)
    b = jnp.sum(lq[2:3, :] * lq[3:4, :], axis=-1, keepdims=True)
    return jnp.exp(a) - jnp.exp(b) + LAM_INIT


def _attn_init_t(rows, bq):
    return (jnp.full((1, bq), NEG, F32), jnp.zeros((rows, bq), F32))


def _online_update_t(state, st, vt):
    m, acc = state
    m_new = jnp.maximum(m, jnp.max(st, axis=0, keepdims=True))
    p = jnp.exp2(st - m_new)
    alpha = jnp.exp2(m - m_new)
    return m_new, alpha * acc + jnp.dot(vt, p.astype(BF16), preferred_element_type=F32)


def _normalized_t(state, rows):
    acc = state[1]
    return acc[0:rows] / acc[rows:rows + 1]


def _pipelined_sweep(i, n_sub, n_near, slots, score_fn, value_fn, modify, rows, blk):
    assert n_sub % 2 == 0 and n_near in (1, 2)

    def run_block(jb, states, near, last):
        states = list(states)
        for s in range(n_sub):
            j = jb * n_sub + s
            cur, nxt = slots[s % 2], slots[(s + 1) % 2]
            for c in range(2):
                if not (last and s == n_sub - 1):
                    nxt[c] = score_fn(c, j + 1)
                st = cur[c]
                if near is not None:
                    st = modify(st, near, s)
                states[c] = _online_update_t(states[c], st, value_fn(c, j))
        return tuple(states)

    for c in range(2):
        slots[0][c] = score_fn(c, 0)
    states = (_attn_init_t(rows, blk), _attn_init_t(rows, blk))
    n_far = jnp.maximum(i + 1 - n_near, 0)
    states = lax.fori_loop(0, n_far, lambda jb, st: run_block(jb, st, None, False), states)
    if n_near == 2:
        states = lax.fori_loop(n_far, i, lambda jb, st: run_block(jb, st, 1, False), states)
    return run_block(i, states, 0, True)


def _fox_body(qt_ref, k_ref, vt_ref, gout_ref, gn_ref, o_ref, s0_scr, s1_scr, *, blk, bk):
    i = pl.program_id(2)
    krow = lax.broadcasted_iota(jnp.int32, (bk, blk), 0)
    qcol = lax.broadcasted_iota(jnp.int32, (bk, blk), 1)
    qts = (qt_ref[0, 0], qt_ref[0, 1])

    def score_fn(hh, j):
        off = pl.multiple_of(j * bk, bk)
        return jnp.dot(k_ref[0, hh, pl.ds(off, bk), :], qts[hh], preferred_element_type=F32)

    def causal(st, near, s):
        return jnp.where(krow + s * bk <= qcol, st, NEG)

    states = _pipelined_sweep(i, blk // bk, 1, (s0_scr, s1_scr), score_fn, lambda hh, j: vt_ref[0, hh, j],
                              causal, VB_ROWS, blk)
    o_t = jnp.concatenate([_normalized_t(states[0], HEAD_DIM), _normalized_t(states[1], HEAD_DIM)], axis=0)
    o_ref[0] = (_group_rms(o_t.T, gout_ref[0], gn_ref[...])).astype(BF16)


def _fox_attention(qb_t, kb_aug, vb_t, g_out_b):
    b, _, _, t = qb_t.shape
    blk = min(ATTN_BLOCK, t)
    bk = vb_t.shape[-1]
    pairs = H_B // 2
    gout = g_out_b.reshape(pairs, 1, LANES)
    return pl.pallas_call(
        functools.partial(_fox_body, blk=blk, bk=bk),
        grid=(b, pairs, t // blk),
        in_specs=[
            pl.BlockSpec((1, 2, LANES, blk), lambda bi, p, i: (bi, p, 0, i)),
            pl.BlockSpec((1, 2, t, LANES), lambda bi, p, i: (bi, p, 0, 0)),
            pl.BlockSpec((1, 2, t // bk, VB_ROWS, bk), lambda bi, p, i: (bi, p, 0, 0, 0)),
            pl.BlockSpec((1, 1, LANES), lambda bi, p, i: (p, 0, 0)),
            _const_spec((LANES, LANES)),
        ],
        out_specs=pl.BlockSpec((1, blk, LANES), lambda bi, p, i: (bi, i, p)),
        out_shape=jax.ShapeDtypeStruct((b, t, H_B * HEAD_DIM), BF16),
        scratch_shapes=[pltpu.VMEM((2, bk, blk), F32), pltpu.VMEM((2, bk, blk), F32)],
        compiler_params=_cparams(("arbitrary", "arbitrary", "arbitrary")),
        name="fox_attention",
    )(qb_t, kb_aug, vb_t, gout, _blockdiag_mean(LANES, HEAD_DIM))


def _diff_finish(states, lam, gout):
    (_, l1, a1), (_, l2, a2) = states
    o = a1 / l1 - lam * (a2 / l2)
    ms = jnp.mean(o * o, axis=-1, keepdims=True)
    return (o * lax.rsqrt(ms + RMS_EPS)) * gout * (1.0 - LAM_INIT)


def _diff_body(lam_ref, qt_ref, k_ref, vt_ref, bias_ref, gout_ref, o_ref, s0_scr, s1_scr, *, blk, bk):
    i = pl.program_id(2)
    n_sub = blk // bk
    qts = (qt_ref[0, 0], qt_ref[0, 1])

    def score_fn(mi, j):
        off = pl.multiple_of(j * bk, bk)
        return jnp.dot(k_ref[0, 0, pl.ds(off, bk), :], qts[mi], preferred_element_type=F32)

    def add_bias(st, near, s):
        return st + bias_ref[0, (1 - near) * n_sub + s]

    states = _pipelined_sweep(i, n_sub, 2, (s0_scr, s1_scr), score_fn, lambda mi, j: vt_ref[0, 0, j],
                              add_bias, VA_ROWS, blk)
    lam = _lam(lam_ref[...])
    o = (_normalized_t(states[0], 2 * HEAD_DIM) - lam * _normalized_t(states[1], 2 * HEAD_DIM)).T
    ms = jnp.mean(o * o, axis=-1, keepdims=True)
    o_ref[0] = ((o * lax.rsqrt(ms + RMS_EPS)) * gout_ref[0] * (1.0 - LAM_INIT)).astype(BF16)


def _diff_attention(qa_t, ka_bf, va_t, bias, lambda_qk, g_out_a):
    b, _, _, t = qa_t.shape
    blk = min(ATTN_BLOCK, t)
    bk = va_t.shape[-1]
    gout = g_out_a.reshape(H_A, 1, LANES)
    return pl.pallas_call(
        functools.partial(_diff_body, blk=blk, bk=bk),
        grid=(b, H_A, t // blk),
        in_specs=[
            _const_spec((4, HEAD_DIM)),
            pl.BlockSpec((1, 2, LANES, blk), lambda bi, h, i: (bi, h, 0, i)),
            pl.BlockSpec((1, 1, t, LANES), lambda bi, h, i: (bi, h, 0, 0)),
            pl.BlockSpec((1, 1, t // bk, VA_ROWS, bk), lambda bi, h, i: (bi, h, 0, 0, 0)),
            pl.BlockSpec((1, 2 * (blk // bk), bk, blk), lambda bi, h, i: (h, 0, 0, 0)),
            pl.BlockSpec((1, 1, LANES), lambda bi, h, i: (h, 0, 0)),
        ],
        out_specs=pl.BlockSpec((1, blk, LANES), lambda bi, h, i: (bi, i, h)),
        out_shape=jax.ShapeDtypeStruct((b, t, H_A * 2 * HEAD_DIM), BF16),
        scratch_shapes=[pltpu.VMEM((2, bk, blk), F32), pltpu.VMEM((2, bk, blk), F32)],
        compiler_params=_cparams(("arbitrary", "arbitrary", "arbitrary")),
        name="diff_attention",
    )(lambda_qk, qa_t, ka_bf, va_t, bias, gout)


def _dec_load(m_scr, l_scr, acc_scr, idx):
    return m_scr[idx], l_scr[idx], acc_scr[idx]


def _dec_store(m_scr, l_scr, acc_scr, idx, state):
    m_scr[idx], l_scr[idx], acc_scr[idx] = state


def _dec_init(m_scr, l_scr, acc_scr):
    m_scr[...] = jnp.full(m_scr.shape, NEG, F32)
    l_scr[...] = jnp.zeros(l_scr.shape, F32)
    acc_scr[...] = jnp.zeros(acc_scr.shape, F32)


def _diff_dec_body(lam_ref, q_ref, kn_ref, vn_ref, kc_ref, vc_ref, bc_ref, bn_ref, gout_ref, o_ref,
                   m_scr, l_scr, acc_scr, *, nq):
    kc = pl.program_id(1)
    lane = lax.broadcasted_iota(jnp.int32, (nq, LANES), 1)

    @pl.when(kc == 0)
    def _():
        _dec_init(m_scr, l_scr, acc_scr)

    def sweep(k_all, v_all, bias_of):
        for h in range(H_A):
            sl = slice(h * LANES, (h + 1) * LANES)
            qfull = q_ref[0][:, sl]
            zero = jnp.zeros_like(qfull)
            k = k_all[:, sl].astype(BF16)
            v = v_all[:, sl].astype(BF16)
            for mi in range(2):
                qm = jnp.where((lane < HEAD_DIM) if mi == 0 else (lane >= HEAD_DIM), qfull, zero)
                s = _qk(qm, k) + bias_of(h)
                idx = 2 * h + mi
                _dec_store(m_scr, l_scr, acc_scr, idx,
                           _online_update(_dec_load(m_scr, l_scr, acc_scr, idx), s, v))

    sweep(kc_ref[0], vc_ref[0], lambda h: bc_ref[h, 0])

    @pl.when(kc == pl.num_programs(1) - 1)
    def _():
        sweep(kn_ref[0], vn_ref[0], lambda h: bn_ref[h, 0])
        lam = _lam(lam_ref[...])
        for h in range(H_A):
            states = (_dec_load(m_scr, l_scr, acc_scr, 2 * h), _dec_load(m_scr, l_scr, acc_scr, 2 * h + 1))
            o_ref[0, :, h * LANES:(h + 1) * LANES] = _diff_finish(states, lam, gout_ref[h]).astype(BF16)


def _diff_decode(qa, ka_new, va_new, cache_k, cache_v, bias_c, bias_n, lambda_qk, g_out_a):
    b, nq, w = qa.shape
    past = cache_k.shape[1]
    ck = min(CACHE_CHUNK, past)
    n_kc = past // ck
    new_spec = pl.BlockSpec((1, nq, w), lambda bi, c: (bi, 0, 0))
    cache_spec = pl.BlockSpec((1, ck, w), lambda bi, c: (bi, c, 0))
    return pl.pallas_call(
        functools.partial(_diff_dec_body, nq=nq),
        grid=(b, n_kc),
        in_specs=[
            _const_spec((4, HEAD_DIM)), new_spec, new_spec, new_spec, cache_spec, cache_spec,
            pl.BlockSpec((H_A, 1, nq, ck), lambda bi, c: (0, c, 0, 0)),
            _const_spec((H_A, 1, nq, nq)),
            _const_spec((H_A, 1, LANES)),
        ],
        out_specs=new_spec,
        out_shape=jax.ShapeDtypeStruct((b, nq, w), BF16),
        scratch_shapes=[pltpu.VMEM((2 * H_A, nq, 1), F32), pltpu.VMEM((2 * H_A, nq, 1), F32),
                        pltpu.VMEM((2 * H_A, nq, LANES), F32)],
        compiler_params=_cparams(("arbitrary", "arbitrary")),
        name="diff_decode",
    )(lambda_qk, qa, ka_new, va_new, cache_k, cache_v, bias_c, bias_n, g_out_a.reshape(H_A, 1, LANES))


def _fox_dec_body(q_ref, kn_ref, vn_ref, lfn_ref, kc_ref, vc_ref, lfc_ref, tri_ref, place_ref, gout_ref, gn_ref,
                  o_ref, ck_scr, ckn_scr, qc_scr, m_scr, l_scr, acc_scr, *, nq, past, ck_rows, sub):
    kc = pl.program_id(1)
    lane_q = lax.broadcasted_iota(jnp.int32, (nq, LANES), 1)
    n_c = 8 * N_CPARTS

    def key_aug(c, lane):
        body = _lane_groups(_split3(c), lane)
        tail = jnp.where(lane < n_c + N_CPARTS, 1.0, 0.0)
        return jnp.where(lane < n_c, body, tail).astype(BF16)

    @pl.when(kc == 0)
    def _():
        _dec_init(m_scr, l_scr, acc_scr)
        tri = tri_ref[...]
        lane_s = lax.broadcasted_iota(jnp.int32, (sub, LANES), 1)
        carry = jnp.zeros((1, LANES), F32)
        for blk in range(past // sub):
            c = _cumsum_rows(tri, lfc_ref[0, blk * sub:(blk + 1) * sub, :] * LOG2E, carry)
            carry = c[sub - 1:sub, :]
            ck_scr[blk * sub:(blk + 1) * sub, :] = key_aug(c, lane_s)
        r = lax.broadcasted_iota(jnp.int32, (nq, nq), 0)
        cc = lax.broadcasted_iota(jnp.int32, (nq, nq), 1)
        tri_n = jnp.where(cc <= r, 1.0, 0.0).astype(BF16)
        cn = _cumsum_rows(tri_n, lfn_ref[0] * LOG2E, carry)
        ckn_scr[...] = key_aug(cn, lane_q)
        extras = jnp.dot(_lane_groups(_split3(cn), lane_q).astype(BF16), place_ref[...],
                         preferred_element_type=F32)
        for h in range(H_B):
            neg_sel = jnp.where((lane_q < n_c) & ((lane_q & 7) == h), -1.0, 0.0)
            qc_scr[h] = (extras[:, h * LANES:(h + 1) * LANES] + neg_sel).astype(BF16)

    def sweep(k_all, v_all, ck_all, masked):
        row = lax.broadcasted_iota(jnp.int32, (nq, nq), 0)
        col = lax.broadcasted_iota(jnp.int32, (nq, nq), 1)
        for h in range(H_B):
            sl = slice((h // 2) * LANES, (h // 2 + 1) * LANES)
            qfull = q_ref[0][:, sl]
            qm = jnp.where((lane_q < HEAD_DIM) if h % 2 == 0 else (lane_q >= HEAD_DIM), qfull, jnp.zeros_like(qfull))
            s = _qk(qm, k_all[:, sl].astype(BF16)) + _qk(qc_scr[h], ck_all)
            if masked:
                s = jnp.where(col <= row, s, NEG)
            _dec_store(m_scr, l_scr, acc_scr, h,
                       _online_update(_dec_load(m_scr, l_scr, acc_scr, h), s, v_all[:, sl].astype(BF16)))

    off = pl.multiple_of(kc * ck_rows, ck_rows)
    sweep(kc_ref[0], vc_ref[0], ck_scr[pl.ds(off, ck_rows), :], False)

    @pl.when(kc == pl.num_programs(1) - 1)
    def _():
        sweep(kn_ref[0], vn_ref[0], ckn_scr[...], True)
        for p in range(H_B // 2):
            _, l0, a0 = _dec_load(m_scr, l_scr, acc_scr, 2 * p)
            _, l1, a1 = _dec_load(m_scr, l_scr, acc_scr, 2 * p + 1)
            o = jnp.where(lane_q < HEAD_DIM, a0 / l0, a1 / l1)
            o_ref[0, :, p * LANES:(p + 1) * LANES] = _group_rms(o, gout_ref[p], gn_ref[...]).astype(BF16)


def _place_matrix_dec():
    rows = jnp.arange(LANES)[:, None]
    cols = jnp.arange(H_B * LANES)[None, :]
    p, h = rows // 8, rows % 8
    return jnp.where((rows < 8 * N_CPARTS) & (cols == h * LANES + 8 * N_CPARTS + p), 1.0, 0.0).astype(BF16)


def _rep_lanes(logf):
    pad = jnp.zeros(logf.shape[:-1] + (LANES - H_B * N_CPARTS,), logf.dtype)
    return jnp.concatenate([logf] * N_CPARTS + [pad], axis=-1)


def _fox_decode(qb, kb_new, vb_new, logf_new, cache_k, cache_v, cache_logf, g_out_b):
    b, nq, w = qb.shape
    past = cache_k.shape[1]
    ck = min(CACHE_CHUNK, past)
    n_kc = past // ck
    sub = min(512, past)
    new_spec = pl.BlockSpec((1, nq, w), lambda bi, c: (bi, 0, 0))
    cache_spec = pl.BlockSpec((1, ck, w), lambda bi, c: (bi, c, 0))
    return pl.pallas_call(
        functools.partial(_fox_dec_body, nq=nq, past=past, ck_rows=ck, sub=sub),
        grid=(b, n_kc),
        in_specs=[
            new_spec, new_spec, new_spec,
            pl.BlockSpec((1, nq, LANES), lambda bi, c: (bi, 0, 0)),
            cache_spec, cache_spec,
            pl.BlockSpec((1, past, LANES), lambda bi, c: (bi, 0, 0)),
            _const_spec((sub, sub)), _const_spec((LANES, H_B * LANES)),
            _const_spec((H_B // 2, 1, LANES)), _const_spec((LANES, LANES)),
        ],
        out_specs=new_spec,
        out_shape=jax.ShapeDtypeStruct((b, nq, w), BF16),
        scratch_shapes=[pltpu.VMEM((past, LANES), BF16), pltpu.VMEM((nq, LANES), BF16),
                        pltpu.VMEM((H_B, nq, LANES), BF16),
                        pltpu.VMEM((H_B, nq, 1), F32), pltpu.VMEM((H_B, nq, 1), F32),
                        pltpu.VMEM((H_B, nq, LANES), F32)],
        compiler_params=_cparams(("arbitrary", "arbitrary")),
        name="fox_decode",
    )(qb, kb_new, vb_new, _rep_lanes(logf_new), cache_k, cache_v, _rep_lanes(cache_logf), _tri(sub),
      _place_matrix_dec(), g_out_b.reshape(H_B // 2, 1, LANES), _blockdiag_mean(LANES, HEAD_DIM))


def _route(logits):
    lane_i = lax.broadcasted_iota(jnp.int32, logits.shape, 1)
    lane = lane_i.astype(F32)
    big = float(LANES)
    lg = jnp.where(lane_i < N_GROUPS, logits, NEG)
    mx = jnp.max(lg, axis=-1, keepdims=True)
    grp = jnp.min(jnp.where(lg == mx, lane, big), axis=-1, keepdims=True)
    p_grp = 1.0 / jnp.sum(jnp.exp(lg - mx), axis=-1, keepdims=True)
    e = lane_i - ROUTER_LANE0
    e_grp = lax.shift_right_arithmetic(e, 3).astype(F32)
    sel = (e >= 0) & (e < N_EXPERTS) & (e_grp == grp)
    v = jnp.where(sel, logits, NEG)
    v1 = jnp.max(v, axis=-1, keepdims=True)
    i1 = jnp.min(jnp.where(sel & (v == v1), lane, big), axis=-1, keepdims=True)
    sel2 = sel & (lane != i1)
    vv = jnp.where(sel2, logits, NEG)
    v2 = jnp.max(vv, axis=-1, keepdims=True)
    i2 = jnp.min(jnp.where(sel2 & (vv == v2), lane, big), axis=-1, keepdims=True)
    e2 = jnp.exp(v2 - v1)
    w1 = p_grp / (1.0 + e2)
    w2 = p_grp * e2 / (1.0 + e2)
    gates = jnp.where(lane == i1, w1, 0.0) + jnp.where(lane == i2, w2, 0.0)
    return gates, (i1, i2, w1, w2)


R_E1, R_E2, R_RANK1, R_RANK2, R_W1, R_W2 = range(6)


def _mix_body(oa_ref, ob_ref, x_ref, wa_ref, wb_ref, g2_ref, wr1_ref, wr2_ref, br_ref, tri_ref,
              x1_ref, xn_ref, gates_ref, route_ref, counts_ref):
    y = (jnp.dot(oa_ref[...], wa_ref[...], preferred_element_type=F32)
         + jnp.dot(ob_ref[...], wb_ref[...], preferred_element_type=F32))
    x1 = x_ref[...] + y
    x1_ref[...] = x1
    ms = jnp.mean(x1 * x1, axis=-1, keepdims=True)
    xn = (x1 * lax.rsqrt(ms + RMS_EPS)) * g2_ref[...]
    xn_ref[...] = xn
    h1 = xn.astype(BF16)
    h2 = (xn - h1.astype(F32)).astype(BF16)
    logits = (jnp.dot(h1, wr1_ref[...], preferred_element_type=F32)
              + jnp.dot(h1, wr2_ref[...], preferred_element_type=F32)
              + jnp.dot(h2, wr1_ref[...], preferred_element_type=F32)) + br_ref[...]
    gates, (i1, i2, w1, w2) = _route(logits)
    gates_ref[...] = gates

    @pl.when(pl.program_id(0) == 0)
    def _():
        counts_ref[...] = jnp.zeros_like(counts_ref)

    lane_i = lax.broadcasted_iota(jnp.int32, gates.shape, 1)
    lane = lane_i.astype(F32)
    oh1 = jnp.where(lane == i1, 1.0, 0.0)
    oh2 = jnp.where(lane == i2, 1.0, 0.0)
    comb = oh1 + oh2
    running = counts_ref[0:1, :]
    before = jnp.dot(tri_ref[...], comb.astype(BF16), preferred_element_type=F32) + running
    rank1 = jnp.sum(before * oh1, axis=-1, keepdims=True)
    rank2 = jnp.sum(before * oh2, axis=-1, keepdims=True)
    counts_ref[0:1, :] = running + jnp.sum(comb, axis=0, keepdims=True)
    rec = jnp.zeros_like(gates)
    for idx, val in ((R_E1, i1 - ROUTER_LANE0), (R_E2, i2 - ROUTER_LANE0), (R_RANK1, rank1), (R_RANK2, rank2),
                     (R_W1, w1), (R_W2, w2)):
        rec = jnp.where(lane_i == idx, val, rec)
    route_ref[...] = rec


def _mix_and_route(o_a, o_b, x, w_out, g_norm, w_rg, b_rg, w_re, b_re):
    n, d = x.shape
    bt = min(TOKEN_TILE, n)
    r = jnp.arange(bt)
    tri_strict = (r[None, :] < r[:, None]).astype(BF16)
    wa = w_out[:GROUP_W].astype(BF16)
    wb = w_out[GROUP_W:].astype(BF16)
    n_r = N_GROUPS + N_EXPERTS
    wr = jnp.concatenate([w_rg, w_re, jnp.zeros((d, LANES - n_r), F32)], axis=1)
    wr1 = wr.astype(BF16)
    wr2 = (wr - wr1.astype(F32)).astype(BF16)
    br = jnp.concatenate([b_rg, b_re, jnp.zeros((LANES - n_r,), F32)])[None, :]
    row = lambda width: pl.BlockSpec((bt, width), lambda i: (i, 0))
    return pl.pallas_call(
        _mix_body,
        grid=(n // bt,),
        in_specs=[row(GROUP_W), row(GROUP_W), row(d), _const_spec((GROUP_W, d)), _const_spec((GROUP_W, d)),
                  _const_spec((1, d)), _const_spec((d, LANES)), _const_spec((d, LANES)), _const_spec((1, LANES)),
                  _const_spec((bt, bt))],
        out_specs=[row(d), row(d), row(LANES), row(LANES), _const_spec((8, LANES))],
        out_shape=[jax.ShapeDtypeStruct((n, d), F32), jax.ShapeDtypeStruct((n, d), F32),
                   jax.ShapeDtypeStruct((n, LANES), F32), jax.ShapeDtypeStruct((n, LANES), F32),
                   jax.ShapeDtypeStruct((8, LANES), F32)],
        compiler_params=_cparams(("arbitrary",)),
        name="mix_route",
    )(o_a, o_b, x, wa, wb, g_norm[None, :], wr1, wr2, br, tri_strict)


def _swiglu(xn, wg, wu, wd, gate=None):
    x = xn.astype(BF16)
    g = jnp.dot(x, wg, preferred_element_type=F32)
    u = jnp.dot(x, wu, preferred_element_type=F32)
    h = (g * jax.nn.sigmoid(g)) * u
    if gate is not None:
        h = h * gate
    return jnp.dot(h.astype(BF16), wd, preferred_element_type=F32)


def _expert_body(xn_ref, x1_ref, gates_ref, wg_ref, wu_ref, wd_ref, o_ref):
    e = pl.program_id(1)

    @pl.when(e == 0)
    def _():
        o_ref[...] = x1_ref[...]

    gates = gates_ref[...]
    lane = lax.broadcasted_iota(jnp.int32, gates.shape, 1)
    gate = jnp.sum(jnp.where(lane == e + ROUTER_LANE0, gates, 0.0), axis=-1, keepdims=True)
    o_ref[...] += _swiglu(xn_ref[...], wg_ref[0], wu_ref[0], wd_ref[0], gate)


def _experts(xn, x1, gates, w_gate, w_up, w_down):
    n, d = x1.shape
    ff = w_gate.shape[-1]
    bt = min(MOE_TILE, n)
    row = lambda width: pl.BlockSpec((bt, width), lambda i, e: (i, 0))
    return pl.pallas_call(
        _expert_body,
        grid=(n // bt, N_EXPERTS),
        in_specs=[row(d), row(d), row(LANES),
                  pl.BlockSpec((1, d, ff), lambda i, e: (e, 0, 0)),
                  pl.BlockSpec((1, d, ff), lambda i, e: (e, 0, 0)),
                  pl.BlockSpec((1, ff, d), lambda i, e: (e, 0, 0))],
        out_specs=row(d),
        out_shape=jax.ShapeDtypeStruct((n, d), F32),
        compiler_params=_cparams(("arbitrary", "arbitrary")),
        name="experts",
    )(xn, x1, gates, w_gate, w_up, w_down)


def _row_copies(n_rows, make_copy):
    def issue(r, carry):
        for s in range(2):
            make_copy(r, s).start()
        return carry

    lax.fori_loop(0, n_rows, issue, 0, unroll=8)

    def drain(r, carry):
        for s in range(2):
            make_copy(r, s).wait()
        return carry

    lax.fori_loop(0, n_rows, drain, 0, unroll=8)


def _dispatch_body(pos_ref, x_ref, init_ref, xs_ref, sem):
    del init_ref
    _row_copies(x_ref.shape[0], lambda r, s: pltpu.make_async_copy(
        x_ref.at[pl.ds(r, 1)], xs_ref.at[pl.ds(pos_ref[0, 0, 2 * r + s], 1)], sem))


def _grouped_body(te_ref, nu_ref, xs_ref, wg_ref, wu_ref, wd_ref, ys_ref):
    del te_ref
    used = pl.program_id(0) < nu_ref[0]

    @pl.when(used)
    def _():
        ys_ref[...] = _swiglu(xs_ref[...], wg_ref[0], wu_ref[0], wd_ref[0])

    @pl.when(jnp.logical_not(used))
    def _():
        ys_ref[...] = jnp.zeros_like(ys_ref)


def _combine_body(pos_ref, route_ref, x1_ref, ys_ref, o_ref, buf_scr, sem):
    _row_copies(x1_ref.shape[0], lambda r, s: pltpu.make_async_copy(
        ys_ref.at[pl.ds(pos_ref[0, 0, 2 * r + s], 1)], buf_scr.at[s, pl.ds(r, 1)], sem))
    rec = route_ref[...]
    o_ref[...] = (x1_ref[...] + rec[:, R_W1:R_W1 + 1] * buf_scr[0] + rec[:, R_W2:R_W2 + 1] * buf_scr[1])


def _routed_experts(xn, x1, route, counts, w_gate, w_up, w_down):
    n, d = x1.shape
    ff = w_gate.shape[-1]
    bt = min(TOKEN_TILE, n)
    nt = n // bt
    tm = MOE_ROW_TILE
    n_tiles = (2 * n) // tm + N_EXPERTS
    cnt = counts[0, ROUTER_LANE0:ROUTER_LANE0 + N_EXPERTS].astype(jnp.int32)
    tiles = (cnt + tm - 1) // tm
    tile_end = jnp.cumsum(tiles)
    base_row = (tile_end - tiles) * tm
    n_used = tile_end[-1:]
    tile_expert = jnp.minimum(jnp.sum(jnp.arange(n_tiles)[:, None] >= tile_end[None, :], axis=1), N_EXPERTS - 1)
    rec = route[:, :R_RANK2 + 1].astype(jnp.int32)
    pos = jnp.stack([base_row[rec[:, R_E1]] + rec[:, R_RANK1], base_row[rec[:, R_E2]] + rec[:, R_RANK2]], axis=1)
    pos = pos.reshape(nt, 1, 2 * bt)

    pos_spec = pl.BlockSpec((1, 1, 2 * bt), lambda i: (i, 0, 0), memory_space=pltpu.SMEM)
    row = lambda width: pl.BlockSpec((bt, width), lambda i: (i, 0))
    any_spec = pl.BlockSpec(memory_space=pl.ANY)
    xs = pl.pallas_call(
        _dispatch_body,
        grid=(nt,),
        in_specs=[pos_spec, row(d), any_spec],
        out_specs=any_spec,
        out_shape=jax.ShapeDtypeStruct((n_tiles * tm, d), F32),
        scratch_shapes=[pltpu.SemaphoreType.DMA(())],
        input_output_aliases={2: 0},
        compiler_params=_cparams(("arbitrary",)),
        name="moe_dispatch",
    )(pos, xn, jnp.zeros((n_tiles * tm, d), F32))

    w_spec = lambda shape: pl.BlockSpec(shape, lambda t, te, nu: (te[t], 0, 0))
    ys = pl.pallas_call(
        _grouped_body,
        grid_spec=pltpu.PrefetchScalarGridSpec(
            num_scalar_prefetch=2, grid=(n_tiles,),
            in_specs=[pl.BlockSpec((tm, d), lambda t, te, nu: (jnp.minimum(t, nu[0] - 1), 0)),
                      w_spec((1, d, ff)), w_spec((1, d, ff)), w_spec((1, ff, d))],
            out_specs=pl.BlockSpec((tm, d), lambda t, te, nu: (t, 0))),
        out_shape=jax.ShapeDtypeStruct((n_tiles * tm, d), F32),
        compiler_params=_cparams(("arbitrary",)),
        name="moe_experts",
    )(tile_expert.astype(jnp.int32), n_used.astype(jnp.int32), xs, w_gate, w_up, w_down)

    return pl.pallas_call(
        _combine_body,
        grid=(nt,),
        in_specs=[pos_spec, row(LANES), row(d), any_spec],
        out_specs=row(d),
        out_shape=jax.ShapeDtypeStruct((n, d), F32),
        scratch_shapes=[pltpu.VMEM((2, bt, d), F32), pltpu.SemaphoreType.DMA(())],
        compiler_params=_cparams(("arbitrary",)),
        name="moe_combine",
    )(pos, route, x1, ys)


def _ffn(o_a, o_b, x, w_out, g_norm_ffn, w_rg, b_rg, w_re, b_re, wg, wu, wd):
    b, t, d = x.shape
    n = b * t
    x1, xn, gates, route, counts = _mix_and_route(o_a.reshape(n, -1), o_b.reshape(n, -1), x.reshape(n, d), w_out,
                                                  g_norm_ffn, w_rg, b_rg, w_re, b_re)
    if 2 * n >= ROUTED_MIN_ASSIGNMENTS:
        y = _routed_experts(xn, x1, route, counts, wg, wu, wd)
    else:
        y = _experts(xn, x1, gates, wg, wu, wd)
    return y.reshape(b, t, d)


def kernel(x_prompt, x_sample, cache_a_k, cache_a_v, cache_b_k, cache_b_v, cache_b_logf, g_norm_mix, w_in, b_forget, g_qk_a, g_qk_b, lambda_qk, g_out_a, g_out_b, w_out, rel_bias, g_norm_ffn, w_router_group, b_router_group, w_router_expert, b_router_expert, w_exp_gate, w_exp_up, w_exp_down):
    depth = w_in.shape[0]
    assert depth == 1, "single-layer step only"
    bp, tp, d = x_prompt.shape
    bs, ts, _ = x_sample.shape
    past = cache_a_k.shape[2]
    w_in0, w_out0 = w_in[0], w_out[0]
    wg, wu, wd = w_exp_gate[0].astype(BF16), w_exp_up[0].astype(BF16), w_exp_down[0].astype(BF16)
    ffn_w = (w_out0, g_norm_ffn[0], w_router_group[0], b_router_group[0], w_router_expert[0], b_router_expert[0],
             wg, wu, wd)

    (ka_p, va_p, kb_p, vb_p, logf_p, qa_t, ka_bf, va_t, qb_t, kb_aug, vb_t) = _projection(
        x_prompt, g_norm_mix[0], w_in0, b_forget[0], g_qk_a[0], g_qk_b[0], with_aug=True)
    blk = min(ATTN_BLOCK, tp)
    bk = va_t.shape[-1]
    assert blk % CHUNK == 0 and blk >= MAX_DISTANCE and blk % bk == 0
    q_pos = blk + jnp.arange(blk, dtype=jnp.int32)
    bkt_p = jnp.stack([_bucket_map(q_pos, s * bk + jnp.arange(bk, dtype=jnp.int32)).T
                       for s in range(2 * blk // bk)])
    bias_p = _bias_tiles(rel_bias, bkt_p)
    o_a = _diff_attention(qa_t, ka_bf, va_t, bias_p, lambda_qk[0], g_out_a[0])
    o_b = _fox_attention(qb_t, kb_aug, vb_t, g_out_b[0])
    y_p = _ffn(o_a, o_b, x_prompt, *ffn_w)

    xs = x_sample.reshape(1, bs * ts, d)
    (ka_s, va_s, kb_s, vb_s, logf_s, qa_s, qb_s) = _projection(
        xs, g_norm_mix[0], w_in0, b_forget[0], g_qk_a[0], g_qk_b[0], with_aug=False)
    per_stream = lambda a: a.reshape(bs, ts, a.shape[-1])
    ka_s, va_s, kb_s, vb_s, logf_s, qa_s, qb_s = map(per_stream, (ka_s, va_s, kb_s, vb_s, logf_s, qa_s, qb_s))
    ck = min(CACHE_CHUNK, past)
    q_pos = past + jnp.arange(ts, dtype=jnp.int32)
    bkt_c = _bucket_map(q_pos, jnp.arange(past, dtype=jnp.int32)).reshape(ts, past // ck, ck).transpose(1, 0, 2)
    bias_c = _bias_tiles(rel_bias, bkt_c)
    bias_n = _bias_tiles(rel_bias, _bucket_map(q_pos, q_pos)[None])
    w_a = H_A * 2 * HEAD_DIM
    o_a_s = _diff_decode(qa_s, ka_s, va_s, cache_a_k[0].reshape(bs, past, w_a), cache_a_v[0].reshape(bs, past, w_a),
                         bias_c, bias_n, lambda_qk[0], g_out_a[0])
    w_b = H_B * HEAD_DIM
    o_b_s = _fox_decode(qb_s, kb_s, vb_s, logf_s, cache_b_k[0].reshape(bs, past, w_b),
                        cache_b_v[0].reshape(bs, past, w_b), cache_b_logf[0], g_out_b[0])
    y_s = _ffn(o_a_s, o_b_s, x_sample, *ffn_w)

    def rows(ka, va, kb, vb, logf, b, t):
        return (ka.reshape(1, b, t, H_A, 2, HEAD_DIM), va.reshape(1, b, t, H_A, 2 * HEAD_DIM),
                kb.reshape(1, b, t, H_B, HEAD_DIM), vb.reshape(1, b, t, H_B, HEAD_DIM), logf.reshape(1, b, t, H_B))

    return (y_p, y_s) + rows(ka_p, va_p, kb_p, vb_p, logf_p, bp, tp) + rows(ka_s, va_s, kb_s, vb_s, logf_s, bs, ts)
```

```python
import functools
import math

import jax
import jax.numpy as jnp
from jax import lax
from jax.experimental import pallas as pl
from jax.experimental.pallas import tpu as pltpu

F32 = jnp.float32
BF16 = jnp.bfloat16

LANES = 128
VMEM_LIMIT_BYTES = 56 * 1024 * 1024

HEAD_DIM = 64
H_A = 4
H_B = 8
GROUP_W = 512
MAIN_W = 6 * GROUP_W
CHUNK = 64
N_BUCKETS = 32
MAX_DISTANCE = 128
N_GROUPS = 4
EXPERTS_PER_GROUP = 8
N_EXPERTS = N_GROUPS * EXPERTS_PER_GROUP
ROUTER_LANE0 = N_GROUPS
RMS_EPS = 1e-6
NEG = -1e30
LOG2E = 1.4426950408889634
QK_SCALE = HEAD_DIM ** -0.5
LAM_INIT = 0.8 - 0.6 * math.exp(-0.3 * 0)
N_CPARTS = 3

ONES_ROWS = 16
VA_ROWS = 2 * HEAD_DIM + ONES_ROWS
VB_ROWS = HEAD_DIM + ONES_ROWS

TOKEN_TILE = 512
ATTN_BLOCK = 512
KV_BLOCK = 256
CACHE_CHUNK = 1024
MOE_TILE = 1024
MOE_ROW_TILE = 256
ROUTED_MIN_ASSIGNMENTS = 4 * N_EXPERTS * MOE_ROW_TILE


def _cparams(sem):
    return pltpu.CompilerParams(dimension_semantics=sem, vmem_limit_bytes=VMEM_LIMIT_BYTES)


def _const_spec(shape):
    nd = len(shape)
    return pl.BlockSpec(shape, lambda *_: (0,) * nd)


def _split3(x):
    p1 = x.astype(BF16).astype(F32)
    r1 = x - p1
    p2 = r1.astype(BF16).astype(F32)
    p3 = (r1 - p2).astype(BF16).astype(F32)
    return p1, p2, p3


def _lane_groups(parts, lane):
    return jnp.where(lane < 8, parts[0], jnp.where(lane < 16, parts[1], parts[2]))


def _cumsum_rows(tri, x, carry):
    c = carry
    for part in _split3(x):
        c = c + jnp.dot(tri, part.astype(BF16), preferred_element_type=F32)
    return c


def _log_sigmoid(x):
    return jnp.minimum(x, 0.0) - jnp.log(1.0 + jnp.exp(-jnp.abs(x)))


def _group_rms(raw, gain_row, gn):
    ms = jnp.dot((raw * raw).astype(BF16), gn, preferred_element_type=F32)
    return raw * lax.rsqrt(ms + RMS_EPS) * gain_row


def _online_update(state, s, v):
    m, l, acc = state
    m_new = jnp.maximum(m, jnp.max(s, axis=-1, keepdims=True))
    p = jnp.exp2(s - m_new)
    alpha = jnp.exp2(m - m_new)
    l_new = alpha * l + jnp.sum(p, axis=-1, keepdims=True)
    acc_new = alpha * acc + jnp.dot(p.astype(BF16), v, preferred_element_type=F32)
    return m_new, l_new, acc_new


def _qk(q, k):
    return lax.dot_general(q, k, (((1,), (1,)), ((), ())), preferred_element_type=F32)


def _proj_body(*refs, with_aug, bt, bk):
    (x_ref, g1_ref, wm_ref, wf_ref, bf_ref, gains_ref, gn_ref, tri_ref, place_ref) = refs[:9]
    ka_ref, va_ref, kb_ref, vb_ref, logf_ref = refs[9:14]
    x = x_ref[0]
    ms = jnp.mean(x * x, axis=-1, keepdims=True)
    xn = (x * lax.rsqrt(ms + RMS_EPS)) * g1_ref[...]
    xb = xn.astype(BF16)
    proj = jnp.dot(xb, wm_ref[...], preferred_element_type=F32)
    gains = gains_ref[...]
    gn = gn_ref[...]
    w = GROUP_W
    qa = _group_rms(proj[:, 0:w], gains[0:1], gn)
    ka = _group_rms(proj[:, w:2 * w], gains[1:2], gn)
    va = proj[:, 2 * w:3 * w]
    qb = _group_rms(proj[:, 3 * w:4 * w], gains[2:3], gn)
    kb = _group_rms(proj[:, 4 * w:5 * w], gains[3:4], gn)
    vb = proj[:, 5 * w:6 * w]
    ka_ref[0] = ka
    va_ref[0] = va
    kb_ref[0] = kb
    vb_ref[0] = vb
    fl = jnp.dot(xb, wf_ref[...], preferred_element_type=F32) + bf_ref[...]
    logf = _log_sigmoid(fl)
    logf_ref[0] = logf[:, 0:H_B]
    qscale = QK_SCALE * LOG2E
    if not with_aug:
        qa_ref, qb_ref = refs[14:16]
        qa_ref[0] = (qa * qscale).astype(BF16)
        qb_ref[0] = (qb * qscale).astype(BF16)
        return
    qa_t, ka_bf, va_t, qb_t, kb_aug, vb_t, cedge_ref, carry_ref = refs[14:22]
    n_chunk = bt // bk
    row = lax.broadcasted_iota(jnp.int32, (LANES, bt), 0)
    ones_tail = jnp.where(lax.broadcasted_iota(jnp.int32, (ONES_ROWS, bt), 0) == 0, 1.0, 0.0)

    def put_chunks(ref, idx, vt):
        vt = vt.astype(BF16)
        for c in range(n_chunk):
            ref[0, idx, c] = vt[:, c * bk:(c + 1) * bk]

    for h in range(H_A):
        sl = slice(h * LANES, (h + 1) * LANES)
        q_t = (qa[:, sl] * qscale).T
        qa_t[0, 2 * h] = jnp.where(row < HEAD_DIM, q_t, 0.0).astype(BF16)
        qa_t[0, 2 * h + 1] = jnp.where(row >= HEAD_DIM, q_t, 0.0).astype(BF16)
        ka_bf[0, h] = ka[:, sl].astype(BF16)
        put_chunks(va_t, h, jnp.concatenate([va[:, sl].T, ones_tail], axis=0))
        vb_pair_t = vb[:, sl].T
        for hh in range(2):
            put_chunks(vb_t, 2 * h + hh,
                       jnp.concatenate([vb_pair_t[hh * HEAD_DIM:(hh + 1) * HEAD_DIM], ones_tail], axis=0))

    @pl.when(pl.program_id(1) == 0)
    def _():
        carry_ref[...] = jnp.zeros_like(carry_ref)

    c = _cumsum_rows(tri_ref[...], logf * LOG2E, carry_ref[0:1, :])
    carry_ref[0:1, :] = c[bt - 1:bt, :]
    edge_row = lax.broadcasted_iota(jnp.int32, (8, LANES), 0)
    cedge_ref[0, 0] = jnp.where(edge_row == 0, c[0:1, :], jnp.where(edge_row == 1, c[bt - 1:bt, :], 0.0))
    lane = lax.broadcasted_iota(jnp.int32, (bt, LANES), 1)
    cparts = _lane_groups(_split3(c), lane).astype(BF16)
    extras = jnp.dot(cparts, place_ref[...], preferred_element_type=F32)
    ones_q = jnp.where((lane >= HEAD_DIM + N_CPARTS) & (lane < HEAD_DIM + 2 * N_CPARTS), 1.0, 0.0)
    ones_k = jnp.where((lane >= HEAD_DIM) & (lane < HEAD_DIM + N_CPARTS), 1.0, 0.0)
    for h in range(H_B):
        sl = slice((h // 2) * LANES, (h // 2 + 1) * LANES)
        qp = qb[:, sl] * qscale
        kp = kb[:, sl]
        if h % 2:
            qp = pltpu.roll(qp, HEAD_DIM, 1)
            kp = pltpu.roll(kp, HEAD_DIM, 1)
        eq = extras[:, h * LANES:(h + 1) * LANES] + ones_q
        ek = extras[:, (H_B + h) * LANES:(H_B + h + 1) * LANES] + ones_k
        qb_t[0, h] = jnp.where(lane < HEAD_DIM, qp, eq).T.astype(BF16)
        kb_aug[0, h] = jnp.where(lane < HEAD_DIM, kp, ek).astype(BF16)


def _blockdiag_mean(n, group):
    r = jnp.arange(n)
    return jnp.where((r[:, None] // group) == (r[None, :] // group), 1.0 / group, 0.0).astype(BF16)


def _tri(n):
    r = jnp.arange(n)
    return (r[None, :] <= r[:, None]).astype(BF16)


def _place_matrix():
    rows = jnp.arange(LANES)[:, None]
    cols = jnp.arange(2 * H_B * LANES)[None, :]
    p, h = rows // 8, rows % 8
    valid = rows < 8 * N_CPARTS
    qcol = h * LANES + HEAD_DIM + p
    kcol = (H_B + h) * LANES + HEAD_DIM + N_CPARTS + p
    m = jnp.where(valid & (cols == qcol), 1.0, 0.0) - jnp.where(valid & (cols == kcol), 1.0, 0.0)
    return m.astype(BF16)


def _projection(x, g_norm, w_in, b_forget, g_qk_a, g_qk_b, *, with_aug):
    b, t, d = x.shape
    bt = min(TOKEN_TILE, t)
    nt = t // bt
    wm = w_in[:, :MAIN_W].astype(BF16)
    wf_cols = w_in[:, MAIN_W:MAIN_W + H_B]
    wf = jnp.concatenate([wf_cols] * N_CPARTS + [jnp.zeros((d, LANES - H_B * N_CPARTS), F32)], axis=1).astype(BF16)
    bfv = jnp.concatenate([b_forget] * N_CPARTS + [jnp.zeros((LANES - H_B * N_CPARTS,), F32)])[None, :]
    gains = jnp.stack([jnp.tile(g_qk_a[0], 2 * H_A), jnp.tile(g_qk_a[1], 2 * H_A),
                       jnp.tile(g_qk_b[0], H_B), jnp.tile(g_qk_b[1], H_B)])
    gn = _blockdiag_mean(GROUP_W, HEAD_DIM)
    tri = _tri(bt)
    place = _place_matrix()
    in_specs = [
        pl.BlockSpec((1, bt, d), lambda i, j: (i, j, 0)),
        _const_spec((1, d)),
        pl.BlockSpec((d, MAIN_W), lambda i, j: (0, 0), pipeline_mode=pl.Buffered(1)),
        _const_spec((d, LANES)), _const_spec((1, LANES)), _const_spec((4, GROUP_W)),
        _const_spec((GROUP_W, GROUP_W)), _const_spec((bt, bt)), _const_spec((LANES, 2 * H_B * LANES)),
    ]
    row_spec = pl.BlockSpec((1, bt, GROUP_W), lambda i, j: (i, j, 0))
    out_shape = [jax.ShapeDtypeStruct((b, t, GROUP_W), F32)] * 4 + [jax.ShapeDtypeStruct((b, t, H_B), F32)]
    out_specs = [row_spec] * 4 + [pl.BlockSpec((1, bt, H_B), lambda i, j: (i, j, 0))]
    scratch = []
    bk = min(KV_BLOCK, bt)
    if with_aug:
        def add(shape, block, index_map):
            out_shape.append(jax.ShapeDtypeStruct(shape, BF16))
            out_specs.append(pl.BlockSpec(block, index_map))

        rows_major = lambda i, j: (i, 0, j, 0)
        time_minor = lambda i, j: (i, 0, 0, j)
        chunked = lambda i, j: (i, 0, j, 0, 0)
        add((b, 2 * H_A, LANES, t), (1, 2 * H_A, LANES, bt), time_minor)
        add((b, H_A, t, LANES), (1, H_A, bt, LANES), rows_major)
        add((b, H_A, t // bk, VA_ROWS, bk), (1, H_A, bt // bk, VA_ROWS, bk), chunked)
        add((b, H_B, LANES, t), (1, H_B, LANES, bt), time_minor)
        add((b, H_B, t, LANES), (1, H_B, bt, LANES), rows_major)
        add((b, H_B, t // bk, VB_ROWS, bk), (1, H_B, bt // bk, VB_ROWS, bk), chunked)
        out_shape.append(jax.ShapeDtypeStruct((b, nt, 8, LANES), F32))
        out_specs.append(pl.BlockSpec((1, 1, 8, LANES), lambda i, j: (i, j, 0, 0)))
        scratch = [pltpu.VMEM((8, LANES), F32)]
    else:
        out_shape += [jax.ShapeDtypeStruct((b, t, GROUP_W), BF16)] * 2
        out_specs += [row_spec] * 2
    return pl.pallas_call(
        functools.partial(_proj_body, with_aug=with_aug, bt=bt, bk=bk),
        grid=(b, nt), in_specs=in_specs, out_specs=out_specs, out_shape=out_shape, scratch_shapes=scratch,
        compiler_params=_cparams(("arbitrary", "arbitrary")),
        name="proj_aug" if with_aug else "proj_plain",
    )(x, g_norm[None, :], wm, wf, bfv, gains, gn, tri, place)


def _t5_bucket(rel):
    nb = N_BUCKETS // 2
    max_exact = nb // 2
    base = jnp.where(rel > 0, nb, 0)
    n = jnp.abs(rel)
    large = max_exact + (jnp.log(jnp.maximum(n, max_exact).astype(jnp.float32) / max_exact)
                         / math.log(MAX_DISTANCE / max_exact) * (nb - max_exact)).astype(jnp.int32)
    large = jnp.minimum(large, nb - 1)
    return base + jnp.where(n < max_exact, n, large)


def _bucket_map(q_pos, k_pos):
    bkt = _t5_bucket(k_pos[None, :] - q_pos[:, None])
    visible = (k_pos[None, :] // CHUNK) <= (q_pos[:, None] // CHUNK)
    return jnp.where(visible, bkt, -1).astype(jnp.int32)


def _bias_body(rb_ref, bkt_ref, o_ref):
    h = pl.program_id(0)
    bkt = bkt_ref[0]
    far = rb_ref[N_BUCKETS // 2 - 1, h]
    acc = jnp.zeros(bkt.shape, F32)
    for b in range(N_BUCKETS):
        acc = jnp.where(bkt == b, rb_ref[b, h] - far, acc)
    o_ref[0, 0] = jnp.where(bkt < 0, NEG, acc * LOG2E)


def _bias_tiles(rel_bias, bkt):
    n, r, c = bkt.shape
    return pl.pallas_call(
        _bias_body,
        grid=(H_A, n),
        in_specs=[pl.BlockSpec(memory_space=pltpu.SMEM), pl.BlockSpec((1, r, c), lambda h, i: (i, 0, 0))],
        out_specs=pl.BlockSpec((1, 1, r, c), lambda h, i: (h, i, 0, 0)),
        out_shape=jax.ShapeDtypeStruct((H_A, n, r, c), F32),
        compiler_params=_cparams(("arbitrary", "arbitrary")),
        name="bias_tiles",
    )(rel_bias, bkt)


def _lam(lq):
    a = jnp.sum(lq[0:1, :] * lq[1:2, :], axis=-1, keepdims=True)
    b = jnp.sum(lq[2:3, :] * lq[3:4, :], axis=-1, keepdims=True)
    return jnp.exp(a) - jnp.exp(b) + LAM_INIT


def _attn_init_t(rows, bq):
    return (jnp.full((1, bq), NEG, F32), jnp.zeros((rows, bq), F32))


def _online_update_t(state, st, vt):
    m, acc = state
    m_new = jnp.maximum(m, jnp.max(st, axis=0, keepdims=True))
    p = jnp.exp2(st - m_new)
    alpha = jnp.exp2(m - m_new)
    return m_new, alpha * acc + jnp.dot(vt, p.astype(BF16), preferred_element_type=F32)


def _normalized_t(state, rows):
    acc = state[1]
    return acc[0:rows] / acc[rows:rows + 1]


def _pipelined_sweep(i, n_sub, n_near, slots, score_fn, value_fn, modify, rows, blk, first=0):
    assert n_sub % 2 == 0 and n_near in (1, 2)

    def run_block(jb, states, near, last):
        states = list(states)
        for s in range(n_sub):
            j = jb * n_sub + s
            cur, nxt = slots[s % 2], slots[(s + 1) % 2]
            for c in range(2):
                if not (last and s == n_sub - 1):
                    nxt[c] = score_fn(c, j + 1)
                st = cur[c]
                if near is not None:
                    st = modify(st, near, s)
                states[c] = _online_update_t(states[c], st, value_fn(c, j))
        return tuple(states)

    for c in range(2):
        slots[0][c] = score_fn(c, first * n_sub)
    states = (_attn_init_t(rows, blk), _attn_init_t(rows, blk))
    n_far = jnp.maximum(i + 1 - n_near, 0)
    states = lax.fori_loop(first, n_far, lambda jb, st: run_block(jb, st, None, False), states)
    if n_near == 2:
        states = lax.fori_loop(n_far, i, lambda jb, st: run_block(jb, st, 1, False), states)
    return run_block(i, states, 0, True)


def _fox_body(first_ref, qt_ref, k_ref, vt_ref, gout_ref, gn_ref, o_ref, s0_scr, s1_scr, *, blk, bk):
    i = pl.program_id(2)
    first = first_ref[(pl.program_id(0) * pl.num_programs(1) + pl.program_id(1)) * pl.num_programs(2) + i]
    krow = lax.broadcasted_iota(jnp.int32, (bk, blk), 0)
    qcol = lax.broadcasted_iota(jnp.int32, (bk, blk), 1)
    qts = (qt_ref[0, 0], qt_ref[0, 1])

    def score_fn(hh, j):
        off = pl.multiple_of(j * bk, bk)
        return jnp.dot(k_ref[0, hh, pl.ds(off, bk), :], qts[hh], preferred_element_type=F32)

    def causal(st, near, s):
        return jnp.where(krow + s * bk <= qcol, st, NEG)

    states = _pipelined_sweep(i, blk // bk, 1, (s0_scr, s1_scr), score_fn, lambda hh, j: vt_ref[0, hh, j],
                              causal, VB_ROWS, blk, first=first)
    o_t = jnp.concatenate([_normalized_t(states[0], HEAD_DIM), _normalized_t(states[1], HEAD_DIM)], axis=0)
    o_ref[0] = (_group_rms(o_t.T, gout_ref[0], gn_ref[...])).astype(BF16)


EXP2_ZERO_BELOW = -150.0


def _fox_skip_plan(c_edge, g_qk_b):
    b, nq = c_edge.shape[:2]
    c_first = c_edge[:, :, 0, :H_B]
    c_last = c_edge[:, :, 1, :H_B]
    slack = 1.02
    qk_max = (HEAD_DIM * jnp.max(jnp.abs(g_qk_b[0])) * jnp.max(jnp.abs(g_qk_b[1])) * QK_SCALE * LOG2E) * slack
    best = 2.0 * qk_max + (c_first[:, :, None, :] - c_last[:, None, :, :]) * (1.0 / slack)
    dead = best < EXP2_ZERO_BELOW - 1.0
    dead = jnp.logical_and(dead[..., 0::2], dead[..., 1::2])
    j_lt_i = (jnp.arange(nq)[None, :] < jnp.arange(nq)[:, None])[None, :, :, None]
    lead = jnp.cumprod(jnp.logical_and(dead, j_lt_i).astype(jnp.int32), axis=2)
    first = jnp.sum(lead, axis=2)
    return jnp.transpose(first, (0, 2, 1)).reshape(-1).astype(jnp.int32)


def _fox_attention(qb_t, kb_aug, vb_t, c_edge, g_qk_b, g_out_b):
    b, _, _, t = qb_t.shape
    blk = min(ATTN_BLOCK, t)
    bk = vb_t.shape[-1]
    pairs = H_B // 2
    gout = g_out_b.reshape(pairs, 1, LANES)
    first = _fox_skip_plan(c_edge, g_qk_b)
    return pl.pallas_call(
        functools.partial(_fox_body, blk=blk, bk=bk),
        grid_spec=pltpu.PrefetchScalarGridSpec(
            num_scalar_prefetch=1, grid=(b, pairs, t // blk),
            in_specs=[
                pl.BlockSpec((1, 2, LANES, blk), lambda bi, p, i, f: (bi, p, 0, i)),
                pl.BlockSpec((1, 2, t, LANES), lambda bi, p, i, f: (bi, p, 0, 0)),
                pl.BlockSpec((1, 2, t // bk, VB_ROWS, bk), lambda bi, p, i, f: (bi, p, 0, 0, 0)),
                pl.BlockSpec((1, 1, LANES), lambda bi, p, i, f: (p, 0, 0)),
                pl.BlockSpec((LANES, LANES), lambda bi, p, i, f: (0, 0)),
            ],
            out_specs=pl.BlockSpec((1, blk, LANES), lambda bi, p, i, f: (bi, i, p)),
            scratch_shapes=[pltpu.VMEM((2, bk, blk), F32), pltpu.VMEM((2, bk, blk), F32)]),
        out_shape=jax.ShapeDtypeStruct((b, t, H_B * HEAD_DIM), BF16),
        compiler_params=_cparams(("arbitrary", "arbitrary", "arbitrary")),
        name="fox_attention",
    )(first, qb_t, kb_aug, vb_t, gout, _blockdiag_mean(LANES, HEAD_DIM))


def _diff_finish(states, lam, gout):
    (_, l1, a1), (_, l2, a2) = states
    o = a1 / l1 - lam * (a2 / l2)
    ms = jnp.mean(o * o, axis=-1, keepdims=True)
    return (o * lax.rsqrt(ms + RMS_EPS)) * gout * (1.0 - LAM_INIT)


def _diff_body(lam_ref, qt_ref, k_ref, vt_ref, bias_ref, gout_ref, o_ref, s0_scr, s1_scr, *, blk, bk):
    i = pl.program_id(2)
    n_sub = blk // bk
    qts = (qt_ref[0, 0], qt_ref[0, 1])

    def score_fn(mi, j):
        off = pl.multiple_of(j * bk, bk)
        return jnp.dot(k_ref[0, 0, pl.ds(off, bk), :], qts[mi], preferred_element_type=F32)

    def add_bias(st, near, s):
        return st + bias_ref[0, (1 - near) * n_sub + s]

    states = _pipelined_sweep(i, n_sub, 2, (s0_scr, s1_scr), score_fn, lambda mi, j: vt_ref[0, 0, j],
                              add_bias, VA_ROWS, blk)
    lam = _lam(lam_ref[...])
    o = (_normalized_t(states[0], 2 * HEAD_DIM) - lam * _normalized_t(states[1], 2 * HEAD_DIM)).T
    ms = jnp.mean(o * o, axis=-1, keepdims=True)
    o_ref[0] = ((o * lax.rsqrt(ms + RMS_EPS)) * gout_ref[0] * (1.0 - LAM_INIT)).astype(BF16)


def _diff_attention(qa_t, ka_bf, va_t, bias, lambda_qk, g_out_a):
    b, _, _, t = qa_t.shape
    blk = min(ATTN_BLOCK, t)
    bk = va_t.shape[-1]
    gout = g_out_a.reshape(H_A, 1, LANES)
    return pl.pallas_call(
        functools.partial(_diff_body, blk=blk, bk=bk),
        grid=(b, H_A, t // blk),
        in_specs=[
            _const_spec((4, HEAD_DIM)),
            pl.BlockSpec((1, 2, LANES, blk), lambda bi, h, i: (bi, h, 0, i)),
            pl.BlockSpec((1, 1, t, LANES), lambda bi, h, i: (bi, h, 0, 0)),
            pl.BlockSpec((1, 1, t // bk, VA_ROWS, bk), lambda bi, h, i: (bi, h, 0, 0, 0)),
            pl.BlockSpec((1, 2 * (blk // bk), bk, blk), lambda bi, h, i: (h, 0, 0, 0)),
            pl.BlockSpec((1, 1, LANES), lambda bi, h, i: (h, 0, 0)),
        ],
        out_specs=pl.BlockSpec((1, blk, LANES), lambda bi, h, i: (bi, i, h)),
        out_shape=jax.ShapeDtypeStruct((b, t, H_A * 2 * HEAD_DIM), BF16),
        scratch_shapes=[pltpu.VMEM((2, bk, blk), F32), pltpu.VMEM((2, bk, blk), F32)],
        compiler_params=_cparams(("arbitrary", "arbitrary", "arbitrary")),
        name="diff_attention",
    )(lambda_qk, qa_t, ka_bf, va_t, bias, gout)


def _dec_load(m_scr, l_scr, acc_scr, idx):
    return m_scr[idx], l_scr[idx], acc_scr[idx]


def _dec_store(m_scr, l_scr, acc_scr, idx, state):
    m_scr[idx], l_scr[idx], acc_scr[idx] = state


def _dec_init(m_scr, l_scr, acc_scr):
    m_scr[...] = jnp.full(m_scr.shape, NEG, F32)
    l_scr[...] = jnp.zeros(l_scr.shape, F32)
    acc_scr[...] = jnp.zeros(acc_scr.shape, F32)


def _diff_dec_body(lam_ref, q_ref, kn_ref, vn_ref, kc_ref, vc_ref, bc_ref, bn_ref, gout_ref, o_ref,
                   m_scr, l_scr, acc_scr, *, nq):
    kc = pl.program_id(1)
    lane = lax.broadcasted_iota(jnp.int32, (nq, LANES), 1)

    @pl.when(kc == 0)
    def _():
        _dec_init(m_scr, l_scr, acc_scr)

    def sweep(k_all, v_all, bias_of):
        for h in range(H_A):
            sl = slice(h * LANES, (h + 1) * LANES)
            qfull = q_ref[0][:, sl]
            zero = jnp.zeros_like(qfull)
            k = k_all[:, sl].astype(BF16)
            v = v_all[:, sl].astype(BF16)
            for mi in range(2):
                qm = jnp.where((lane < HEAD_DIM) if mi == 0 else (lane >= HEAD_DIM), qfull, zero)
                s = _qk(qm, k) + bias_of(h)
                idx = 2 * h + mi
                _dec_store(m_scr, l_scr, acc_scr, idx,
                           _online_update(_dec_load(m_scr, l_scr, acc_scr, idx), s, v))

    sweep(kc_ref[0], vc_ref[0], lambda h: bc_ref[h, 0])

    @pl.when(kc == pl.num_programs(1) - 1)
    def _():
        sweep(kn_ref[0], vn_ref[0], lambda h: bn_ref[h, 0])
        lam = _lam(lam_ref[...])
        for h in range(H_A):
            states = (_dec_load(m_scr, l_scr, acc_scr, 2 * h), _dec_load(m_scr, l_scr, acc_scr, 2 * h + 1))
            o_ref[0, :, h * LANES:(h + 1) * LANES] = _diff_finish(states, lam, gout_ref[h]).astype(BF16)


def _diff_decode(qa, ka_new, va_new, cache_k, cache_v, bias_c, bias_n, lambda_qk, g_out_a):
    b, nq, w = qa.shape
    past = cache_k.shape[1]
    ck = min(CACHE_CHUNK, past)
    n_kc = past // ck
    new_spec = pl.BlockSpec((1, nq, w), lambda bi, c: (bi, 0, 0))
    cache_spec = pl.BlockSpec((1, ck, w), lambda bi, c: (bi, c, 0))
    return pl.pallas_call(
        functools.partial(_diff_dec_body, nq=nq),
        grid=(b, n_kc),
        in_specs=[
            _const_spec((4, HEAD_DIM)), new_spec, new_spec, new_spec, cache_spec, cache_spec,
            pl.BlockSpec((H_A, 1, nq, ck), lambda bi, c: (0, c, 0, 0)),
            _const_spec((H_A, 1, nq, nq)),
            _const_spec((H_A, 1, LANES)),
        ],
        out_specs=new_spec,
        out_shape=jax.ShapeDtypeStruct((b, nq, w), BF16),
        scratch_shapes=[pltpu.VMEM((2 * H_A, nq, 1), F32), pltpu.VMEM((2 * H_A, nq, 1), F32),
                        pltpu.VMEM((2 * H_A, nq, LANES), F32)],
        compiler_params=_cparams(("arbitrary", "arbitrary")),
        name="diff_decode",
    )(lambda_qk, qa, ka_new, va_new, cache_k, cache_v, bias_c, bias_n, g_out_a.reshape(H_A, 1, LANES))


def _fox_dec_body(q_ref, kn_ref, vn_ref, lfn_ref, kc_ref, vc_ref, lfc_ref, tri_ref, place_ref, gout_ref, gn_ref,
                  o_ref, ck_scr, ckn_scr, qc_scr, m_scr, l_scr, acc_scr, *, nq, past, ck_rows, sub):
    kc = pl.program_id(1)
    lane_q = lax.broadcasted_iota(jnp.int32, (nq, LANES), 1)
    n_c = 8 * N_CPARTS

    def key_aug(c, lane):
        body = _lane_groups(_split3(c), lane)
        tail = jnp.where(lane < n_c + N_CPARTS, 1.0, 0.0)
        return jnp.where(lane < n_c, body, tail).astype(BF16)

    @pl.when(kc == 0)
    def _():
        _dec_init(m_scr, l_scr, acc_scr)
        tri = tri_ref[...]
        lane_s = lax.broadcasted_iota(jnp.int32, (sub, LANES), 1)
        carry = jnp.zeros((1, LANES), F32)
        for blk in range(past // sub):
            c = _cumsum_rows(tri, lfc_ref[0, blk * sub:(blk + 1) * sub, :] * LOG2E, carry)
            carry = c[sub - 1:sub, :]
            ck_scr[blk * sub:(blk + 1) * sub, :] = key_aug(c, lane_s)
        r = lax.broadcasted_iota(jnp.int32, (nq, nq), 0)
        cc = lax.broadcasted_iota(jnp.int32, (nq, nq), 1)
        tri_n = jnp.where(cc <= r, 1.0, 0.0).astype(BF16)
        cn = _cumsum_rows(tri_n, lfn_ref[0] * LOG2E, carry)
        ckn_scr[...] = key_aug(cn, lane_q)
        extras = jnp.dot(_lane_groups(_split3(cn), lane_q).astype(BF16), place_ref[...],
                         preferred_element_type=F32)
        for h in range(H_B):
            neg_sel = jnp.where((lane_q < n_c) & ((lane_q & 7) == h), -1.0, 0.0)
            qc_scr[h] = (extras[:, h * LANES:(h + 1) * LANES] + neg_sel).astype(BF16)

    def sweep(k_all, v_all, ck_all, masked):
        row = lax.broadcasted_iota(jnp.int32, (nq, nq), 0)
        col = lax.broadcasted_iota(jnp.int32, (nq, nq), 1)
        for h in range(H_B):
            sl = slice((h // 2) * LANES, (h // 2 + 1) * LANES)
            qfull = q_ref[0][:, sl]
            qm = jnp.where((lane_q < HEAD_DIM) if h % 2 == 0 else (lane_q >= HEAD_DIM), qfull, jnp.zeros_like(qfull))
            s = _qk(qm, k_all[:, sl].astype(BF16)) + _qk(qc_scr[h], ck_all)
            if masked:
                s = jnp.where(col <= row, s, NEG)
            _dec_store(m_scr, l_scr, acc_scr, h,
                       _online_update(_dec_load(m_scr, l_scr, acc_scr, h), s, v_all[:, sl].astype(BF16)))

    off = pl.multiple_of(kc * ck_rows, ck_rows)
    sweep(kc_ref[0], vc_ref[0], ck_scr[pl.ds(off, ck_rows), :], False)

    @pl.when(kc == pl.num_programs(1) - 1)
    def _():
        sweep(kn_ref[0], vn_ref[0], ckn_scr[...], True)
        for p in range(H_B // 2):
            _, l0, a0 = _dec_load(m_scr, l_scr, acc_scr, 2 * p)
            _, l1, a1 = _dec_load(m_scr, l_scr, acc_scr, 2 * p + 1)
            o = jnp.where(lane_q < HEAD_DIM, a0 / l0, a1 / l1)
            o_ref[0, :, p * LANES:(p + 1) * LANES] = _group_rms(o, gout_ref[p], gn_ref[...]).astype(BF16)


def _place_matrix_dec():
    rows = jnp.arange(LANES)[:, None]
    cols = jnp.arange(H_B * LANES)[None, :]
    p, h = rows // 8, rows % 8
    return jnp.where((rows < 8 * N_CPARTS) & (cols == h * LANES + 8 * N_CPARTS + p), 1.0, 0.0).astype(BF16)


def _rep_lanes(logf):
    pad = jnp.zeros(logf.shape[:-1] + (LANES - H_B * N_CPARTS,), logf.dtype)
    return jnp.concatenate([logf] * N_CPARTS + [pad], axis=-1)


def _fox_decode(qb, kb_new, vb_new, logf_new, cache_k, cache_v, cache_logf, g_out_b):
    b, nq, w = qb.shape
    past = cache_k.shape[1]
    ck = min(CACHE_CHUNK, past)
    n_kc = past // ck
    sub = min(512, past)
    new_spec = pl.BlockSpec((1, nq, w), lambda bi, c: (bi, 0, 0))
    cache_spec = pl.BlockSpec((1, ck, w), lambda bi, c: (bi, c, 0))
    return pl.pallas_call(
        functools.partial(_fox_dec_body, nq=nq, past=past, ck_rows=ck, sub=sub),
        grid=(b, n_kc),
        in_specs=[
            new_spec, new_spec, new_spec,
            pl.BlockSpec((1, nq, LANES), lambda bi, c: (bi, 0, 0)),
            cache_spec, cache_spec,
            pl.BlockSpec((1, past, LANES), lambda bi, c: (bi, 0, 0)),
            _const_spec((sub, sub)), _const_spec((LANES, H_B * LANES)),
            _const_spec((H_B // 2, 1, LANES)), _const_spec((LANES, LANES)),
        ],
        out_specs=new_spec,
        out_shape=jax.ShapeDtypeStruct((b, nq, w), BF16),
        scratch_shapes=[pltpu.VMEM((past, LANES), BF16), pltpu.VMEM((nq, LANES), BF16),
                        pltpu.VMEM((H_B, nq, LANES), BF16),
                        pltpu.VMEM((H_B, nq, 1), F32), pltpu.VMEM((H_B, nq, 1), F32),
                        pltpu.VMEM((H_B, nq, LANES), F32)],
        compiler_params=_cparams(("arbitrary", "arbitrary")),
        name="fox_decode",
    )(qb, kb_new, vb_new, _rep_lanes(logf_new), cache_k, cache_v, _rep_lanes(cache_logf), _tri(sub),
      _place_matrix_dec(), g_out_b.reshape(H_B // 2, 1, LANES), _blockdiag_mean(LANES, HEAD_DIM))


def _route(logits):
    lane_i = lax.broadcasted_iota(jnp.int32, logits.shape, 1)
    lane = lane_i.astype(F32)
    big = float(LANES)
    lg = jnp.where(lane_i < N_GROUPS, logits, NEG)
    mx = jnp.max(lg, axis=-1, keepdims=True)
    grp = jnp.min(jnp.where(lg == mx, lane, big), axis=-1, keepdims=True)
    p_grp = 1.0 / jnp.sum(jnp.exp(lg - mx), axis=-1, keepdims=True)
    e = lane_i - ROUTER_LANE0
    e_grp = lax.shift_right_arithmetic(e, 3).astype(F32)
    sel = (e >= 0) & (e < N_EXPERTS) & (e_grp == grp)
    v = jnp.where(sel, logits, NEG)
    v1 = jnp.max(v, axis=-1, keepdims=True)
    i1 = jnp.min(jnp.where(sel & (v == v1), lane, big), axis=-1, keepdims=True)
    sel2 = sel & (lane != i1)
    vv = jnp.where(sel2, logits, NEG)
    v2 = jnp.max(vv, axis=-1, keepdims=True)
    i2 = jnp.min(jnp.where(sel2 & (vv == v2), lane, big), axis=-1, keepdims=True)
    e2 = jnp.exp(v2 - v1)
    w1 = p_grp / (1.0 + e2)
    w2 = p_grp * e2 / (1.0 + e2)
    gates = jnp.where(lane == i1, w1, 0.0) + jnp.where(lane == i2, w2, 0.0)
    return gates, (i1, i2, w1, w2)


R_E1, R_E2, R_RANK1, R_RANK2, R_W1, R_W2 = range(6)


def _mix_body(oa_ref, ob_ref, x_ref, wa_ref, wb_ref, g2_ref, wr1_ref, wr2_ref, br_ref, tri_ref,
              x1_ref, xn_ref, gates_ref, route_ref, counts_ref):
    y = (jnp.dot(oa_ref[...], wa_ref[...], preferred_element_type=F32)
         + jnp.dot(ob_ref[...], wb_ref[...], preferred_element_type=F32))
    x1 = x_ref[...] + y
    x1_ref[...] = x1
    ms = jnp.mean(x1 * x1, axis=-1, keepdims=True)
    xn = (x1 * lax.rsqrt(ms + RMS_EPS)) * g2_ref[...]
    xn_ref[...] = xn
    h1 = xn.astype(BF16)
    h2 = (xn - h1.astype(F32)).astype(BF16)
    logits = (jnp.dot(h1, wr1_ref[...], preferred_element_type=F32)
              + jnp.dot(h1, wr2_ref[...], preferred_element_type=F32)
              + jnp.dot(h2, wr1_ref[...], preferred_element_type=F32)) + br_ref[...]
    gates, (i1, i2, w1, w2) = _route(logits)
    gates_ref[...] = gates

    @pl.when(pl.program_id(0) == 0)
    def _():
        counts_ref[...] = jnp.zeros_like(counts_ref)

    lane_i = lax.broadcasted_iota(jnp.int32, gates.shape, 1)
    lane = lane_i.astype(F32)
    oh1 = jnp.where(lane == i1, 1.0, 0.0)
    oh2 = jnp.where(lane == i2, 1.0, 0.0)
    comb = oh1 + oh2
    running = counts_ref[0:1, :]
    before = jnp.dot(tri_ref[...], comb.astype(BF16), preferred_element_type=F32) + running
    rank1 = jnp.sum(before * oh1, axis=-1, keepdims=True)
    rank2 = jnp.sum(before * oh2, axis=-1, keepdims=True)
    counts_ref[0:1, :] = running + jnp.sum(comb, axis=0, keepdims=True)
    rec = jnp.zeros_like(gates)
    for idx, val in ((R_E1, i1 - ROUTER_LANE0), (R_E2, i2 - ROUTER_LANE0), (R_RANK1, rank1), (R_RANK2, rank2),
                     (R_W1, w1), (R_W2, w2)):
        rec = jnp.where(lane_i == idx, val, rec)
    route_ref[...] = rec


def _mix_and_route(o_a, o_b, x, w_out, g_norm, w_rg, b_rg, w_re, b_re):
    n, d = x.shape
    bt = min(TOKEN_TILE, n)
    r = jnp.arange(bt)
    tri_strict = (r[None, :] < r[:, None]).astype(BF16)
    wa = w_out[:GROUP_W].astype(BF16)
    wb = w_out[GROUP_W:].astype(BF16)
    n_r = N_GROUPS + N_EXPERTS
    wr = jnp.concatenate([w_rg, w_re, jnp.zeros((d, LANES - n_r), F32)], axis=1)
    wr1 = wr.astype(BF16)
    wr2 = (wr - wr1.astype(F32)).astype(BF16)
    br = jnp.concatenate([b_rg, b_re, jnp.zeros((LANES - n_r,), F32)])[None, :]
    row = lambda width: pl.BlockSpec((bt, width), lambda i: (i, 0))
    return pl.pallas_call(
        _mix_body,
        grid=(n // bt,),
        in_specs=[row(GROUP_W), row(GROUP_W), row(d), _const_spec((GROUP_W, d)), _const_spec((GROUP_W, d)),
                  _const_spec((1, d)), _const_spec((d, LANES)), _const_spec((d, LANES)), _const_spec((1, LANES)),
                  _const_spec((bt, bt))],
        out_specs=[row(d), row(d), row(LANES), row(LANES), _const_spec((8, LANES))],
        out_shape=[jax.ShapeDtypeStruct((n, d), F32), jax.ShapeDtypeStruct((n, d), F32),
                   jax.ShapeDtypeStruct((n, LANES), F32), jax.ShapeDtypeStruct((n, LANES), F32),
                   jax.ShapeDtypeStruct((8, LANES), F32)],
        compiler_params=_cparams(("arbitrary",)),
        name="mix_route",
    )(o_a, o_b, x, wa, wb, g_norm[None, :], wr1, wr2, br, tri_strict)


def _swiglu(xn, wg, wu, wd, gate=None):
    x = xn.astype(BF16)
    g = jnp.dot(x, wg, preferred_element_type=F32)
    u = jnp.dot(x, wu, preferred_element_type=F32)
    h = (g * jax.nn.sigmoid(g)) * u
    if gate is not None:
        h = h * gate
    return jnp.dot(h.astype(BF16), wd, preferred_element_type=F32)


def _expert_body(xn_ref, x1_ref, gates_ref, wg_ref, wu_ref, wd_ref, o_ref):
    e = pl.program_id(1)

    @pl.when(e == 0)
    def _():
        o_ref[...] = x1_ref[...]

    gates = gates_ref[...]
    lane = lax.broadcasted_iota(jnp.int32, gates.shape, 1)
    gate = jnp.sum(jnp.where(lane == e + ROUTER_LANE0, gates, 0.0), axis=-1, keepdims=True)
    o_ref[...] += _swiglu(xn_ref[...], wg_ref[0], wu_ref[0], wd_ref[0], gate)


def _experts(xn, x1, gates, w_gate, w_up, w_down):
    n, d = x1.shape
    ff = w_gate.shape[-1]
    bt = min(MOE_TILE, n)
    row = lambda width: pl.BlockSpec((bt, width), lambda i, e: (i, 0))
    return pl.pallas_call(
        _expert_body,
        grid=(n // bt, N_EXPERTS),
        in_specs=[row(d), row(d), row(LANES),
                  pl.BlockSpec((1, d, ff), lambda i, e: (e, 0, 0)),
                  pl.BlockSpec((1, d, ff), lambda i, e: (e, 0, 0)),
                  pl.BlockSpec((1, ff, d), lambda i, e: (e, 0, 0))],
        out_specs=row(d),
        out_shape=jax.ShapeDtypeStruct((n, d), F32),
        compiler_params=_cparams(("arbitrary", "arbitrary")),
        name="experts",
    )(xn, x1, gates, w_gate, w_up, w_down)


def _row_copies(n_rows, make_copy):
    def issue(r, carry):
        for s in range(2):
            make_copy(r, s).start()
        return carry

    lax.fori_loop(0, n_rows, issue, 0, unroll=8)

    def drain(r, carry):
        for s in range(2):
            make_copy(r, s).wait()
        return carry

    lax.fori_loop(0, n_rows, drain, 0, unroll=8)


def _row_position_body(route_ref, base_ref, pos_ref):
    rec = route_ref[...]
    lane_i = lax.broadcasted_iota(jnp.int32, rec.shape, 1)
    lane = lane_i.astype(F32)
    out = jnp.zeros(rec.shape, F32)
    for slot, (e_lane, r_lane) in enumerate(((R_E1, R_RANK1), (R_E2, R_RANK2))):
        onehot = lane == rec[:, e_lane:e_lane + 1] + float(ROUTER_LANE0)
        base = jnp.sum(jnp.where(onehot, base_ref[...], 0.0), axis=-1, keepdims=True)
        out = jnp.where(lane_i == slot, base + rec[:, r_lane:r_lane + 1], out)
    pos_ref[...] = out.astype(jnp.int32)


def _dispatch_body(pos_ref, x_ref, init_ref, xs_ref, sem):
    del init_ref
    _row_copies(x_ref.shape[0], lambda r, s: pltpu.make_async_copy(
        x_ref.at[pl.ds(r, 1)], xs_ref.at[pl.ds(pos_ref[0, 0, 2 * r + s], 1)], sem))


def _grouped_body(te_ref, nu_ref, xs_ref, wg_ref, wu_ref, wd_ref, ys_ref):
    del te_ref
    used = pl.program_id(0) < nu_ref[0]

    @pl.when(used)
    def _():
        ys_ref[...] = _swiglu(xs_ref[...], wg_ref[0], wu_ref[0], wd_ref[0])

    @pl.when(jnp.logical_not(used))
    def _():
        ys_ref[...] = jnp.zeros_like(ys_ref)


def _combine_body(pos_ref, route_ref, x1_ref, ys_ref, o_ref, buf_scr, sem):
    _row_copies(x1_ref.shape[0], lambda r, s: pltpu.make_async_copy(
        ys_ref.at[pl.ds(pos_ref[0, 0, 2 * r + s], 1)], buf_scr.at[s, pl.ds(r, 1)], sem))
    rec = route_ref[...]
    o_ref[...] = (x1_ref[...] + rec[:, R_W1:R_W1 + 1] * buf_scr[0] + rec[:, R_W2:R_W2 + 1] * buf_scr[1])


def _routed_experts(xn, x1, route, counts, w_gate, w_up, w_down):
    n, d = x1.shape
    ff = w_gate.shape[-1]
    bt = min(TOKEN_TILE, n)
    nt = n // bt
    tm = MOE_ROW_TILE
    n_tiles = (2 * n) // tm + N_EXPERTS
    cnt = counts[0, ROUTER_LANE0:ROUTER_LANE0 + N_EXPERTS].astype(jnp.int32)
    tiles = (cnt + tm - 1) // tm
    tile_end = jnp.cumsum(tiles)
    base_row = (tile_end - tiles) * tm
    n_used = tile_end[-1:]
    tile_expert = jnp.minimum(jnp.sum(jnp.arange(n_tiles)[:, None] >= tile_end[None, :], axis=1), N_EXPERTS - 1)
    base_lanes = jnp.zeros((1, LANES), F32).at[0, ROUTER_LANE0:ROUTER_LANE0 + N_EXPERTS].set(base_row.astype(F32))
    row = lambda width: pl.BlockSpec((bt, width), lambda i: (i, 0))
    pos = pl.pallas_call(
        _row_position_body,
        grid=(nt,),
        in_specs=[row(LANES), _const_spec((1, LANES))],
        out_specs=row(LANES),
        out_shape=jax.ShapeDtypeStruct((n, LANES), jnp.int32),
        compiler_params=_cparams(("arbitrary",)),
        name="moe_positions",
    )(route, base_lanes)
    pos = pos[:, :2].reshape(nt, 1, 2 * bt)

    pos_spec = pl.BlockSpec((1, 1, 2 * bt), lambda i: (i, 0, 0), memory_space=pltpu.SMEM)
    any_spec = pl.BlockSpec(memory_space=pl.ANY)
    xs = pl.pallas_call(
        _dispatch_body,
        grid=(nt,),
        in_specs=[pos_spec, row(d), any_spec],
        out_specs=any_spec,
        out_shape=jax.ShapeDtypeStruct((n_tiles * tm, d), F32),
        scratch_shapes=[pltpu.SemaphoreType.DMA(())],
        input_output_aliases={2: 0},
        compiler_params=_cparams(("arbitrary",)),
        name="moe_dispatch",
    )(pos, xn, jnp.zeros((n_tiles * tm, d), F32))

    w_spec = lambda shape: pl.BlockSpec(shape, lambda t, te, nu: (te[t], 0, 0))
    ys = pl.pallas_call(
        _grouped_body,
        grid_spec=pltpu.PrefetchScalarGridSpec(
            num_scalar_prefetch=2, grid=(n_tiles,),
            in_specs=[pl.BlockSpec((tm, d), lambda t, te, nu: (jnp.minimum(t, nu[0] - 1), 0)),
                      w_spec((1, d, ff)), w_spec((1, d, ff)), w_spec((1, ff, d))],
            out_specs=pl.BlockSpec((tm, d), lambda t, te, nu: (t, 0))),
        out_shape=jax.ShapeDtypeStruct((n_tiles * tm, d), F32),
        compiler_params=_cparams(("arbitrary",)),
        name="moe_experts",
    )(tile_expert.astype(jnp.int32), n_used.astype(jnp.int32), xs, w_gate, w_up, w_down)

    return pl.pallas_call(
        _combine_body,
        grid=(nt,),
        in_specs=[pos_spec, row(LANES), row(d), any_spec],
        out_specs=row(d),
        out_shape=jax.ShapeDtypeStruct((n, d), F32),
        scratch_shapes=[pltpu.VMEM((2, bt, d), F32), pltpu.SemaphoreType.DMA(())],
        compiler_params=_cparams(("arbitrary",)),
        name="moe_combine",
    )(pos, route, x1, ys)


def _ffn(o_a, o_b, x, w_out, g_norm_ffn, w_rg, b_rg, w_re, b_re, wg, wu, wd):
    b, t, d = x.shape
    n = b * t
    x1, xn, gates, route, counts = _mix_and_route(o_a.reshape(n, -1), o_b.reshape(n, -1), x.reshape(n, d), w_out,
                                                  g_norm_ffn, w_rg, b_rg, w_re, b_re)
    if 2 * n >= ROUTED_MIN_ASSIGNMENTS:
        y = _routed_experts(xn, x1, route, counts, wg, wu, wd)
    else:
        y = _experts(xn, x1, gates, wg, wu, wd)
    return y.reshape(b, t, d)


def kernel(x_prompt, x_sample, cache_a_k, cache_a_v, cache_b_k, cache_b_v, cache_b_logf, g_norm_mix, w_in, b_forget, g_qk_a, g_qk_b, lambda_qk, g_out_a, g_out_b, w_out, rel_bias, g_norm_ffn, w_router_group, b_router_group, w_router_expert, b_router_expert, w_exp_gate, w_exp_up, w_exp_down):
    depth = w_in.shape[0]
    assert depth == 1, "single-layer step only"
    bp, tp, d = x_prompt.shape
    bs, ts, _ = x_sample.shape
    past = cache_a_k.shape[2]
    w_in0, w_out0 = w_in[0], w_out[0]
    wg, wu, wd = w_exp_gate[0].astype(BF16), w_exp_up[0].astype(BF16), w_exp_down[0].astype(BF16)
    ffn_w = (w_out0, g_norm_ffn[0], w_router_group[0], b_router_group[0], w_router_expert[0], b_router_expert[0],
             wg, wu, wd)

    (ka_p, va_p, kb_p, vb_p, logf_p, qa_t, ka_bf, va_t, qb_t, kb_aug, vb_t, c_edge) = _projection(
        x_prompt, g_norm_mix[0], w_in0, b_forget[0], g_qk_a[0], g_qk_b[0], with_aug=True)
    blk = min(ATTN_BLOCK, tp)
    bk = va_t.shape[-1]
    assert blk % CHUNK == 0 and blk >= MAX_DISTANCE and blk % bk == 0
    q_pos = blk + jnp.arange(blk, dtype=jnp.int32)
    bkt_p = jnp.stack([_bucket_map(q_pos, s * bk + jnp.arange(bk, dtype=jnp.int32)).T
                       for s in range(2 * blk // bk)])
    bias_p = _bias_tiles(rel_bias, bkt_p)
    o_a = _diff_attention(qa_t, ka_bf, va_t, bias_p, lambda_qk[0], g_out_a[0])
    o_b = _fox_attention(qb_t, kb_aug, vb_t, c_edge, g_qk_b[0], g_out_b[0])
    y_p = _ffn(o_a, o_b, x_prompt, *ffn_w)

    xs = x_sample.reshape(1, bs * ts, d)
    (ka_s, va_s, kb_s, vb_s, logf_s, qa_s, qb_s) = _projection(
        xs, g_norm_mix[0], w_in0, b_forget[0], g_qk_a[0], g_qk_b[0], with_aug=False)
    per_stream = lambda a: a.reshape(bs, ts, a.shape[-1])
    ka_s, va_s, kb_s, vb_s, logf_s, qa_s, qb_s = map(per_stream, (ka_s, va_s, kb_s, vb_s, logf_s, qa_s, qb_s))
    ck = min(CACHE_CHUNK, past)
    q_pos = past + jnp.arange(ts, dtype=jnp.int32)
    bkt_c = _bucket_map(q_pos, jnp.arange(past, dtype=jnp.int32)).reshape(ts, past // ck, ck).transpose(1, 0, 2)
    bias_c = _bias_tiles(rel_bias, bkt_c)
    bias_n = _bias_tiles(rel_bias, _bucket_map(q_pos, q_pos)[None])
    w_a = H_A * 2 * HEAD_DIM
    o_a_s = _diff_decode(qa_s, ka_s, va_s, cache_a_k[0].reshape(bs, past, w_a), cache_a_v[0].reshape(bs, past, w_a),
                         bias_c, bias_n, lambda_qk[0], g_out_a[0])
    w_b = H_B * HEAD_DIM
    o_b_s = _fox_decode(qb_s, kb_s, vb_s, logf_s, cache_b_k[0].reshape(bs, past, w_b),
                        cache_b_v[0].reshape(bs, past, w_b), cache_b_logf[0], g_out_b[0])
    y_s = _ffn(o_a_s, o_b_s, x_sample, *ffn_w)

    def rows(ka, va, kb, vb, logf, b, t):
        return (ka.reshape(1, b, t, H_A, 2, HEAD_DIM), va.reshape(1, b, t, H_A, 2 * HEAD_DIM),
                kb.reshape(1, b, t, H_B, HEAD_DIM), vb.reshape(1, b, t, H_B, HEAD_DIM), logf.reshape(1, b, t, H_B))

    return (y_p, y_s) + rows(ka_p, va_p, kb_p, vb_p, logf_p, bp, tp) + rows(ka_s, va_s, kb_s, vb_s, logf_s, bs, ts)
```

```python
import functools
import math

import jax
import jax.numpy as jnp
from jax import lax
from jax.experimental import pallas as pl
from jax.experimental.pallas import tpu as pltpu

F32 = jnp.float32
BF16 = jnp.bfloat16

LANES = 128
VMEM_LIMIT_BYTES = 56 * 1024 * 1024

HEAD_DIM = 64
H_A = 4
H_B = 8
GROUP_W = 512
MAIN_W = 6 * GROUP_W
CHUNK = 64
N_BUCKETS = 32
MAX_DISTANCE = 128
N_GROUPS = 4
EXPERTS_PER_GROUP = 8
N_EXPERTS = N_GROUPS * EXPERTS_PER_GROUP
ROUTER_LANE0 = N_GROUPS
RMS_EPS = 1e-6
NEG = -1e30
LOG2E = 1.4426950408889634
QK_SCALE = HEAD_DIM ** -0.5
LAM_INIT = 0.8 - 0.6 * math.exp(-0.3 * 0)
N_CPARTS = 3

ONES_ROWS = 16
VA_ROWS = 2 * HEAD_DIM + ONES_ROWS
VB_ROWS = HEAD_DIM + ONES_ROWS

TOKEN_TILE = 512
ATTN_BLOCK = 512
KV_BLOCK = 256
CACHE_CHUNK = 1024
MOE_TILE = 1024
MOE_ROW_TILE = 256
ROUTED_MIN_ASSIGNMENTS = 4 * N_EXPERTS * MOE_ROW_TILE


def _cparams(sem):
    return pltpu.CompilerParams(dimension_semantics=sem, vmem_limit_bytes=VMEM_LIMIT_BYTES)


def _const_spec(shape):
    nd = len(shape)
    return pl.BlockSpec(shape, lambda *_: (0,) * nd)


def _split3(x):
    p1 = x.astype(BF16).astype(F32)
    r1 = x - p1
    p2 = r1.astype(BF16).astype(F32)
    p3 = (r1 - p2).astype(BF16).astype(F32)
    return p1, p2, p3


def _lane_groups(parts, lane):
    return jnp.where(lane < 8, parts[0], jnp.where(lane < 16, parts[1], parts[2]))


def _cumsum_rows(tri, x, carry):
    c = carry
    for part in _split3(x):
        c = c + jnp.dot(tri, part.astype(BF16), preferred_element_type=F32)
    return c


def _log_sigmoid(x):
    return jnp.minimum(x, 0.0) - jnp.log(1.0 + jnp.exp(-jnp.abs(x)))


def _group_rms(raw, gain_row, gn):
    ms = jnp.dot((raw * raw).astype(BF16), gn, preferred_element_type=F32)
    return raw * lax.rsqrt(ms + RMS_EPS) * gain_row


def _online_update(state, s, v):
    m, l, acc = state
    m_new = jnp.maximum(m, jnp.max(s, axis=-1, keepdims=True))
    p = jnp.exp2(s - m_new)
    alpha = jnp.exp2(m - m_new)
    l_new = alpha * l + jnp.sum(p, axis=-1, keepdims=True)
    acc_new = alpha * acc + jnp.dot(p.astype(BF16), v, preferred_element_type=F32)
    return m_new, l_new, acc_new


def _qk(q, k):
    return lax.dot_general(q, k, (((1,), (1,)), ((), ())), preferred_element_type=F32)


def _proj_body(*refs, with_aug, bt, bk):
    (x_ref, g1_ref, wm_ref, wf_ref, bf_ref, gains_ref, gn_ref, tri_ref, place_ref) = refs[:9]
    ka_ref, va_ref, kb_ref, vb_ref, logf_ref = refs[9:14]
    x = x_ref[0]
    ms = jnp.mean(x * x, axis=-1, keepdims=True)
    xn = (x * lax.rsqrt(ms + RMS_EPS)) * g1_ref[...]
    xb = xn.astype(BF16)
    proj = jnp.dot(xb, wm_ref[...], preferred_element_type=F32)
    gains = gains_ref[...]
    gn = gn_ref[...]
    w = GROUP_W
    qa = _group_rms(proj[:, 0:w], gains[0:1], gn)
    ka = _group_rms(proj[:, w:2 * w], gains[1:2], gn)
    va = proj[:, 2 * w:3 * w]
    qb = _group_rms(proj[:, 3 * w:4 * w], gains[2:3], gn)
    kb = _group_rms(proj[:, 4 * w:5 * w], gains[3:4], gn)
    vb = proj[:, 5 * w:6 * w]
    ka_ref[0] = ka
    va_ref[0] = va
    kb_ref[0] = kb
    vb_ref[0] = vb
    fl = jnp.dot(xb, wf_ref[...], preferred_element_type=F32) + bf_ref[...]
    logf = _log_sigmoid(fl)
    logf_ref[0] = logf[:, 0:H_B]
    qscale = QK_SCALE * LOG2E
    if not with_aug:
        qa_ref, qb_ref = refs[14:16]
        qa_ref[0] = (qa * qscale).astype(BF16)
        qb_ref[0] = (qb * qscale).astype(BF16)
        return
    qa_t, ka_bf, va_t, qb_t, kb_aug, vb_t, cedge_ref, carry_ref = refs[14:22]
    n_chunk = bt // bk
    row = lax.broadcasted_iota(jnp.int32, (LANES, bt), 0)
    ones_tail = jnp.where(lax.broadcasted_iota(jnp.int32, (ONES_ROWS, bt), 0) == 0, 1.0, 0.0)

    def put_chunks(ref, idx, vt):
        vt = vt.astype(BF16)
        for c in range(n_chunk):
            ref[0, idx, c] = vt[:, c * bk:(c + 1) * bk]

    for h in range(H_A):
        sl = slice(h * LANES, (h + 1) * LANES)
        q_t = (qa[:, sl] * qscale).T
        qa_t[0, 2 * h] = jnp.where(row < HEAD_DIM, q_t, 0.0).astype(BF16)
        qa_t[0, 2 * h + 1] = jnp.where(row >= HEAD_DIM, q_t, 0.0).astype(BF16)
        ka_bf[0, h] = ka[:, sl].astype(BF16)
        put_chunks(va_t, h, jnp.concatenate([va[:, sl].T, ones_tail], axis=0))
        vb_pair_t = vb[:, sl].T
        for hh in range(2):
            put_chunks(vb_t, 2 * h + hh,
                       jnp.concatenate([vb_pair_t[hh * HEAD_DIM:(hh + 1) * HEAD_DIM], ones_tail], axis=0))

    @pl.when(pl.program_id(1) == 0)
    def _():
        carry_ref[...] = jnp.zeros_like(carry_ref)

    c = _cumsum_rows(tri_ref[...], logf * LOG2E, carry_ref[0:1, :])
    carry_ref[0:1, :] = c[bt - 1:bt, :]
    edge_row = lax.broadcasted_iota(jnp.int32, (8, LANES), 0)
    cedge_ref[0, 0] = jnp.where(edge_row == 0, c[0:1, :], jnp.where(edge_row == 1, c[bt - 1:bt, :], 0.0))
    lane = lax.broadcasted_iota(jnp.int32, (bt, LANES), 1)
    cparts = _lane_groups(_split3(c), lane).astype(BF16)
    extras = jnp.dot(cparts, place_ref[...], preferred_element_type=F32)
    ones_q = jnp.where((lane >= HEAD_DIM + N_CPARTS) & (lane < HEAD_DIM + 2 * N_CPARTS), 1.0, 0.0)
    ones_k = jnp.where((lane >= HEAD_DIM) & (lane < HEAD_DIM + N_CPARTS), 1.0, 0.0)
    for h in range(H_B):
        sl = slice((h // 2) * LANES, (h // 2 + 1) * LANES)
        qp = qb[:, sl] * qscale
        kp = kb[:, sl]
        if h % 2:
            qp = pltpu.roll(qp, HEAD_DIM, 1)
            kp = pltpu.roll(kp, HEAD_DIM, 1)
        eq = extras[:, h * LANES:(h + 1) * LANES] + ones_q
        ek = extras[:, (H_B + h) * LANES:(H_B + h + 1) * LANES] + ones_k
        qb_t[0, h] = jnp.where(lane < HEAD_DIM, qp, eq).T.astype(BF16)
        kb_aug[0, h] = jnp.where(lane < HEAD_DIM, kp, ek).astype(BF16)


def _blockdiag_mean(n, group):
    r = jnp.arange(n)
    return jnp.where((r[:, None] // group) == (r[None, :] // group), 1.0 / group, 0.0).astype(BF16)


def _tri(n):
    r = jnp.arange(n)
    return (r[None, :] <= r[:, None]).astype(BF16)


def _place_matrix():
    rows = jnp.arange(LANES)[:, None]
    cols = jnp.arange(2 * H_B * LANES)[None, :]
    p, h = rows // 8, rows % 8
    valid = rows < 8 * N_CPARTS
    qcol = h * LANES + HEAD_DIM + p
    kcol = (H_B + h) * LANES + HEAD_DIM + N_CPARTS + p
    m = jnp.where(valid & (cols == qcol), 1.0, 0.0) - jnp.where(valid & (cols == kcol), 1.0, 0.0)
    return m.astype(BF16)


def _projection(x, g_norm, w_in, b_forget, g_qk_a, g_qk_b, *, with_aug):
    b, t, d = x.shape
    bt = min(TOKEN_TILE, t)
    nt = t // bt
    wm = w_in[:, :MAIN_W].astype(BF16)
    wf_cols = w_in[:, MAIN_W:MAIN_W + H_B]
    wf = jnp.concatenate([wf_cols] * N_CPARTS + [jnp.zeros((d, LANES - H_B * N_CPARTS), F32)], axis=1).astype(BF16)
    bfv = jnp.concatenate([b_forget] * N_CPARTS + [jnp.zeros((LANES - H_B * N_CPARTS,), F32)])[None, :]
    gains = jnp.stack([jnp.tile(g_qk_a[0], 2 * H_A), jnp.tile(g_qk_a[1], 2 * H_A),
                       jnp.tile(g_qk_b[0], H_B), jnp.tile(g_qk_b[1], H_B)])
    gn = _blockdiag_mean(GROUP_W, HEAD_DIM)
    tri = _tri(bt)
    place = _place_matrix()
    in_specs = [
        pl.BlockSpec((1, bt, d), lambda i, j: (i, j, 0)),
        _const_spec((1, d)),
        pl.BlockSpec((d, MAIN_W), lambda i, j: (0, 0), pipeline_mode=pl.Buffered(1)),
        _const_spec((d, LANES)), _const_spec((1, LANES)), _const_spec((4, GROUP_W)),
        _const_spec((GROUP_W, GROUP_W)), _const_spec((bt, bt)), _const_spec((LANES, 2 * H_B * LANES)),
    ]
    row_spec = pl.BlockSpec((1, bt, GROUP_W), lambda i, j: (i, j, 0))
    out_shape = [jax.ShapeDtypeStruct((b, t, GROUP_W), F32)] * 4 + [jax.ShapeDtypeStruct((b, t, H_B), F32)]
    out_specs = [row_spec] * 4 + [pl.BlockSpec((1, bt, H_B), lambda i, j: (i, j, 0))]
    scratch = []
    bk = min(KV_BLOCK, bt)
    if with_aug:
        def add(shape, block, index_map):
            out_shape.append(jax.ShapeDtypeStruct(shape, BF16))
            out_specs.append(pl.BlockSpec(block, index_map))

        rows_major = lambda i, j: (i, 0, j, 0)
        time_minor = lambda i, j: (i, 0, 0, j)
        chunked = lambda i, j: (i, 0, j, 0, 0)
        add((b, 2 * H_A, LANES, t), (1, 2 * H_A, LANES, bt), time_minor)
        add((b, H_A, t, LANES), (1, H_A, bt, LANES), rows_major)
        add((b, H_A, t // bk, VA_ROWS, bk), (1, H_A, bt // bk, VA_ROWS, bk), chunked)
        add((b, H_B, LANES, t), (1, H_B, LANES, bt), time_minor)
        add((b, H_B, t, LANES), (1, H_B, bt, LANES), rows_major)
        add((b, H_B, t // bk, VB_ROWS, bk), (1, H_B, bt // bk, VB_ROWS, bk), chunked)
        out_shape.append(jax.ShapeDtypeStruct((b, nt, 8, LANES), F32))
        out_specs.append(pl.BlockSpec((1, 1, 8, LANES), lambda i, j: (i, j, 0, 0)))
        scratch = [pltpu.VMEM((8, LANES), F32)]
    else:
        out_shape += [jax.ShapeDtypeStruct((b, t, GROUP_W), BF16)] * 2
        out_specs += [row_spec] * 2
    return pl.pallas_call(
        functools.partial(_proj_body, with_aug=with_aug, bt=bt, bk=bk),
        grid=(b, nt), in_specs=in_specs, out_specs=out_specs, out_shape=out_shape, scratch_shapes=scratch,
        compiler_params=_cparams(("arbitrary", "arbitrary")),
        name="proj_aug" if with_aug else "proj_plain",
    )(x, g_norm[None, :], wm, wf, bfv, gains, gn, tri, place)


def _t5_bucket(rel):
    nb = N_BUCKETS // 2
    max_exact = nb // 2
    base = jnp.where(rel > 0, nb, 0)
    n = jnp.abs(rel)
    large = max_exact + (jnp.log(jnp.maximum(n, max_exact).astype(jnp.float32) / max_exact)
                         / math.log(MAX_DISTANCE / max_exact) * (nb - max_exact)).astype(jnp.int32)
    large = jnp.minimum(large, nb - 1)
    return base + jnp.where(n < max_exact, n, large)


def _bucket_map(q_pos, k_pos):
    bkt = _t5_bucket(k_pos[None, :] - q_pos[:, None])
    visible = (k_pos[None, :] // CHUNK) <= (q_pos[:, None] // CHUNK)
    return jnp.where(visible, bkt, -1).astype(jnp.int32)


def _bias_body(rb_ref, bkt_ref, o_ref):
    h = pl.program_id(0)
    bkt = bkt_ref[0]
    far = rb_ref[N_BUCKETS // 2 - 1, h]
    acc = jnp.zeros(bkt.shape, F32)
    for b in range(N_BUCKETS):
        acc = jnp.where(bkt == b, rb_ref[b, h] - far, acc)
    o_ref[0, 0] = jnp.where(bkt < 0, NEG, acc * LOG2E)


def _bias_tiles(rel_bias, bkt):
    n, r, c = bkt.shape
    return pl.pallas_call(
        _bias_body,
        grid=(H_A, n),
        in_specs=[pl.BlockSpec(memory_space=pltpu.SMEM), pl.BlockSpec((1, r, c), lambda h, i: (i, 0, 0))],
        out_specs=pl.BlockSpec((1, 1, r, c), lambda h, i: (h, i, 0, 0)),
        out_shape=jax.ShapeDtypeStruct((H_A, n, r, c), F32),
        compiler_params=_cparams(("arbitrary", "arbitrary")),
        name="bias_tiles",
    )(rel_bias, bkt)


def _lam(lq):
    a = jnp.sum(lq[0:1, :] * lq[1:2, :], axis=-1, keepdims=True)
    b = jnp.sum(lq[2:3, :] * lq[3:4, :], axis=-1, keepdims=True)
    return jnp.exp(a) - jnp.exp(b) + LAM_INIT


def _attn_init_t(rows, bq):
    return (jnp.full((1, bq), NEG, F32), jnp.zeros((rows, bq), F32))


def _online_update_t(state, st, vt):
    m, acc = state
    m_new = jnp.maximum(m, jnp.max(st, axis=0, keepdims=True))
    p = jnp.exp2(st - m_new)
    alpha = jnp.exp2(m - m_new)
    return m_new, alpha * acc + jnp.dot(vt, p.astype(BF16), preferred_element_type=F32)


def _normalized_t(state, rows):
    acc = state[1]
    return acc[0:rows] / acc[rows:rows + 1]


def _pipelined_sweep(i, n_sub, n_near, slots, score_fn, value_fn, modify, rows, blk, first=0):
    assert n_sub % 2 == 0 and n_near in (1, 2)

    def run_block(jb, states, near, last):
        states = list(states)
        for s in range(n_sub):
            j = jb * n_sub + s
            cur, nxt = slots[s % 2], slots[(s + 1) % 2]
            for c in range(2):
                if not (last and s == n_sub - 1):
                    nxt[c] = score_fn(c, j + 1)
                st = cur[c]
                if near is not None:
                    st = modify(st, near, s)
                states[c] = _online_update_t(states[c], st, value_fn(c, j))
        return tuple(states)

    for c in range(2):
        slots[0][c] = score_fn(c, first * n_sub)
    states = (_attn_init_t(rows, blk), _attn_init_t(rows, blk))
    n_far = jnp.maximum(i + 1 - n_near, 0)
    states = lax.fori_loop(first, n_far, lambda jb, st: run_block(jb, st, None, False), states)
    if n_near == 2:
        states = lax.fori_loop(n_far, i, lambda jb, st: run_block(jb, st, 1, False), states)
    return run_block(i, states, 0, True)


def _fox_body(first_ref, qt_ref, k_ref, vt_ref, gout_ref, gn_ref, o_ref, s0_scr, s1_scr, *, blk, bk):
    i = pl.program_id(2)
    first = first_ref[(pl.program_id(0) * pl.num_programs(1) + pl.program_id(1)) * pl.num_programs(2) + i]
    krow = lax.broadcasted_iota(jnp.int32, (bk, blk), 0)
    qcol = lax.broadcasted_iota(jnp.int32, (bk, blk), 1)
    qts = (qt_ref[0, 0], qt_ref[0, 1])

    def score_fn(hh, j):
        off = pl.multiple_of(j * bk, bk)
        return jnp.dot(k_ref[0, hh, pl.ds(off, bk), :], qts[hh], preferred_element_type=F32)

    def causal(st, near, s):
        return jnp.where(krow + s * bk <= qcol, st, NEG)

    states = _pipelined_sweep(i, blk // bk, 1, (s0_scr, s1_scr), score_fn, lambda hh, j: vt_ref[0, hh, j],
                              causal, VB_ROWS, blk, first=first)
    o_t = jnp.concatenate([_normalized_t(states[0], HEAD_DIM), _normalized_t(states[1], HEAD_DIM)], axis=0)
    o_ref[0] = (_group_rms(o_t.T, gout_ref[0], gn_ref[...])).astype(BF16)


EXP2_ZERO_BELOW = -150.0


def _fox_skip_plan(c_edge, g_qk_b):
    b, nq = c_edge.shape[:2]
    c_first = c_edge[:, :, 0, :H_B]
    c_last = c_edge[:, :, 1, :H_B]
    slack = 1.02
    qk_max = (HEAD_DIM * jnp.max(jnp.abs(g_qk_b[0])) * jnp.max(jnp.abs(g_qk_b[1])) * QK_SCALE * LOG2E) * slack
    best = 2.0 * qk_max + (c_first[:, :, None, :] - c_last[:, None, :, :]) * (1.0 / slack)
    dead = best < EXP2_ZERO_BELOW - 1.0
    dead = jnp.logical_and(dead[..., 0::2], dead[..., 1::2])
    j_lt_i = (jnp.arange(nq)[None, :] < jnp.arange(nq)[:, None])[None, :, :, None]
    lead = jnp.cumprod(jnp.logical_and(dead, j_lt_i).astype(jnp.int32), axis=2)
    first = jnp.sum(lead, axis=2)
    return jnp.transpose(first, (0, 2, 1)).reshape(-1).astype(jnp.int32)


def _fox_attention(qb_t, kb_aug, vb_t, c_edge, g_qk_b, g_out_b):
    b, _, _, t = qb_t.shape
    blk = min(ATTN_BLOCK, t)
    bk = vb_t.shape[-1]
    pairs = H_B // 2
    gout = g_out_b.reshape(pairs, 1, LANES)
    first = _fox_skip_plan(c_edge, g_qk_b)
    return pl.pallas_call(
        functools.partial(_fox_body, blk=blk, bk=bk),
        grid_spec=pltpu.PrefetchScalarGridSpec(
            num_scalar_prefetch=1, grid=(b, pairs, t // blk),
            in_specs=[
                pl.BlockSpec((1, 2, LANES, blk), lambda bi, p, i, f: (bi, p, 0, i)),
                pl.BlockSpec((1, 2, t, LANES), lambda bi, p, i, f: (bi, p, 0, 0)),
                pl.BlockSpec((1, 2, t // bk, VB_ROWS, bk), lambda bi, p, i, f: (bi, p, 0, 0, 0)),
                pl.BlockSpec((1, 1, LANES), lambda bi, p, i, f: (p, 0, 0)),
                pl.BlockSpec((LANES, LANES), lambda bi, p, i, f: (0, 0)),
            ],
            out_specs=pl.BlockSpec((1, blk, LANES), lambda bi, p, i, f: (bi, i, p)),
            scratch_shapes=[pltpu.VMEM((2, bk, blk), F32), pltpu.VMEM((2, bk, blk), F32)]),
        out_shape=jax.ShapeDtypeStruct((b, t, H_B * HEAD_DIM), BF16),
        compiler_params=_cparams(("arbitrary", "arbitrary", "arbitrary")),
        name="fox_attention",
    )(first, qb_t, kb_aug, vb_t, gout, _blockdiag_mean(LANES, HEAD_DIM))


def _diff_finish(states, lam, gout):
    (_, l1, a1), (_, l2, a2) = states
    o = a1 / l1 - lam * (a2 / l2)
    ms = jnp.mean(o * o, axis=-1, keepdims=True)
    return (o * lax.rsqrt(ms + RMS_EPS)) * gout * (1.0 - LAM_INIT)


def _diff_body(lam_ref, qt_ref, k_ref, vt_ref, bias_ref, gout_ref, o_ref, s0_scr, s1_scr, *, blk, bk):
    i = pl.program_id(2)
    n_sub = blk // bk
    qts = (qt_ref[0, 0], qt_ref[0, 1])

    def score_fn(mi, j):
        off = pl.multiple_of(j * bk, bk)
        return jnp.dot(k_ref[0, 0, pl.ds(off, bk), :], qts[mi], preferred_element_type=F32)

    def add_bias(st, near, s):
        return st + bias_ref[0, (1 - near) * n_sub + s]

    states = _pipelined_sweep(i, n_sub, 2, (s0_scr, s1_scr), score_fn, lambda mi, j: vt_ref[0, 0, j],
                              add_bias, VA_ROWS, blk)
    lam = _lam(lam_ref[...])
    o = (_normalized_t(states[0], 2 * HEAD_DIM) - lam * _normalized_t(states[1], 2 * HEAD_DIM)).T
    ms = jnp.mean(o * o, axis=-1, keepdims=True)
    o_ref[0] = ((o * lax.rsqrt(ms + RMS_EPS)) * gout_ref[0] * (1.0 - LAM_INIT)).astype(BF16)


def _diff_attention(qa_t, ka_bf, va_t, bias, lambda_qk, g_out_a):
    b, _, _, t = qa_t.shape
    blk = min(ATTN_BLOCK, t)
    bk = va_t.shape[-1]
    gout = g_out_a.reshape(H_A, 1, LANES)
    return pl.pallas_call(
        functools.partial(_diff_body, blk=blk, bk=bk),
        grid=(b, H_A, t // blk),
        in_specs=[
            _const_spec((4, HEAD_DIM)),
            pl.BlockSpec((1, 2, LANES, blk), lambda bi, h, i: (bi, h, 0, i)),
            pl.BlockSpec((1, 1, t, LANES), lambda bi, h, i: (bi, h, 0, 0)),
            pl.BlockSpec((1, 1, t // bk, VA_ROWS, bk), lambda bi, h, i: (bi, h, 0, 0, 0)),
            pl.BlockSpec((1, 2 * (blk // bk), bk, blk), lambda bi, h, i: (h, 0, 0, 0)),
            pl.BlockSpec((1, 1, LANES), lambda bi, h, i: (h, 0, 0)),
        ],
        out_specs=pl.BlockSpec((1, blk, LANES), lambda bi, h, i: (bi, i, h)),
        out_shape=jax.ShapeDtypeStruct((b, t, H_A * 2 * HEAD_DIM), BF16),
        scratch_shapes=[pltpu.VMEM((2, bk, blk), F32), pltpu.VMEM((2, bk, blk), F32)],
        compiler_params=_cparams(("arbitrary", "arbitrary", "arbitrary")),
        name="diff_attention",
    )(lambda_qk, qa_t, ka_bf, va_t, bias, gout)


def _dec_load(m_scr, l_scr, acc_scr, idx):
    return m_scr[idx], l_scr[idx], acc_scr[idx]


def _dec_store(m_scr, l_scr, acc_scr, idx, state):
    m_scr[idx], l_scr[idx], acc_scr[idx] = state


def _dec_init(m_scr, l_scr, acc_scr):
    m_scr[...] = jnp.full(m_scr.shape, NEG, F32)
    l_scr[...] = jnp.zeros(l_scr.shape, F32)
    acc_scr[...] = jnp.zeros(acc_scr.shape, F32)


def _diff_dec_body(lam_ref, q_ref, kn_ref, vn_ref, kt_ref, v_ref, bc_ref, bn_ref, gout_ref, o_ref,
                   m_scr, l_scr, acc_scr, *, ck):
    kc = pl.program_id(1)

    @pl.when(kc == 0)
    def _():
        _dec_init(m_scr, l_scr, acc_scr)

    q = q_ref[0]

    def update(idx, s, v):
        _dec_store(m_scr, l_scr, acc_scr, idx, _online_update(_dec_load(m_scr, l_scr, acc_scr, idx), s, v))

    def q_map(h, mi):
        col = (2 * h + mi) * HEAD_DIM
        return q[:, col:col + HEAD_DIM]

    scores = [jnp.dot(q_map(h, mi), kt_ref[0, h, mi].astype(BF16), preferred_element_type=F32)
              for h in range(H_A) for mi in range(2)]
    for h in range(H_A):
        v = v_ref[0, pl.ds(h, ck, stride=H_A), :].astype(BF16)
        for mi in range(2):
            update(2 * h + mi, scores[2 * h + mi] + bc_ref[h, 0], v)

    @pl.when(kc == pl.num_programs(1) - 1)
    def _():
        lam = _lam(lam_ref[...])
        for h in range(H_A):
            sl = slice(h * LANES, (h + 1) * LANES)
            k_new = kn_ref[0][:, sl].astype(BF16)
            v_new = vn_ref[0][:, sl].astype(BF16)
            for mi in range(2):
                s = _qk(q_map(h, mi), k_new[:, mi * HEAD_DIM:(mi + 1) * HEAD_DIM]) + bn_ref[h, 0]
                update(2 * h + mi, s, v_new)
            states = (_dec_load(m_scr, l_scr, acc_scr, 2 * h), _dec_load(m_scr, l_scr, acc_scr, 2 * h + 1))
            o_ref[0, :, sl] = _diff_finish(states, lam, gout_ref[h]).astype(BF16)


def _diff_decode(qa, ka_new, va_new, cache_kt, cache_v, bias_c, bias_n, lambda_qk, g_out_a):
    b, nq, w = qa.shape
    past = cache_kt.shape[-1]
    ck = min(CACHE_CHUNK, past)
    n_kc = past // ck
    new_spec = pl.BlockSpec((1, nq, w), lambda bi, c: (bi, 0, 0))
    return pl.pallas_call(
        functools.partial(_diff_dec_body, ck=ck),
        grid=(b, n_kc),
        in_specs=[
            _const_spec((4, HEAD_DIM)), new_spec, new_spec, new_spec,
            pl.BlockSpec((1, H_A, 2, HEAD_DIM, ck), lambda bi, c: (bi, 0, 0, 0, c)),
            pl.BlockSpec((1, H_A * ck, LANES), lambda bi, c: (bi, c, 0)),
            pl.BlockSpec((H_A, 1, nq, ck), lambda bi, c: (0, c, 0, 0)),
            _const_spec((H_A, 1, nq, nq)),
            _const_spec((H_A, 1, LANES)),
        ],
        out_specs=new_spec,
        out_shape=jax.ShapeDtypeStruct((b, nq, w), BF16),
        scratch_shapes=[pltpu.VMEM((2 * H_A, nq, 1), F32), pltpu.VMEM((2 * H_A, nq, 1), F32),
                        pltpu.VMEM((2 * H_A, nq, LANES), F32)],
        compiler_params=_cparams(("arbitrary", "arbitrary")),
        name="diff_decode",
    )(lambda_qk, qa, ka_new, va_new, cache_kt, cache_v, bias_c, bias_n, g_out_a.reshape(H_A, 1, LANES))


def _suffix_sum_lanes(x):
    n = x.shape[-1]
    lane = lax.broadcasted_iota(jnp.int32, x.shape, x.ndim - 1)
    shift = 1
    while shift < n:
        x = x + jnp.where(lane + shift < n, pltpu.roll(x, n - shift, x.ndim - 1), 0.0)
        shift *= 2
    return x


def _fox_dec_body(q_ref, kn_ref, vn_ref, lfn_col_ref, lfn_row_ref, kt_ref, vt_ref, lfc_ref, gout_ref, gn_ref,
                  o_ref, cq_scr, cn_scr, after_scr, m_scr, l_scr, acc_scr, *, nq):
    kc = pl.program_id(1)
    q = q_ref[0]
    q_head = lambda h: q[:, h * HEAD_DIM:(h + 1) * HEAD_DIM]

    def update(h, s, v, trans_v):
        m, l, acc = _dec_load(m_scr, l_scr, acc_scr, h)
        m_new = jnp.maximum(m, jnp.max(s, axis=-1, keepdims=True))
        p = jnp.exp2(s - m_new)
        alpha = jnp.exp2(m - m_new)
        pv = _qk(p.astype(BF16), v) if trans_v else jnp.dot(p.astype(BF16), v, preferred_element_type=F32)
        _dec_store(m_scr, l_scr, acc_scr, h, (m_new, alpha * l + jnp.sum(p, axis=-1, keepdims=True),
                                              alpha * acc + pv))

    n_kc = after_scr.shape[0]
    ck = after_scr.shape[-1]

    @pl.when(kc == 0)
    def _():
        _dec_init(m_scr, l_scr, acc_scr)
        r = lax.broadcasted_iota(jnp.int32, (nq, nq), 0)
        c = lax.broadcasted_iota(jnp.int32, (nq, nq), 1)
        tri_n = jnp.where(c <= r, 1.0, 0.0).astype(BF16)
        cq_scr[...] = _cumsum_rows(tri_n, lfn_col_ref[0] * LOG2E, jnp.zeros((1, LANES), F32))
        lf_row = lfn_row_ref[0] * LOG2E
        total = jnp.sum(lf_row, axis=-1, keepdims=True)
        cn_scr[...] = total - _suffix_sum_lanes(lf_row) + lf_row
        lf = lfc_ref[0] * LOG2E
        after = _suffix_sum_lanes(lf) - lf
        for c_idx in range(n_kc):
            after_scr[c_idx] = after[:, c_idx * ck:(c_idx + 1) * ck]

    after = after_scr[n_kc - 1 - kc]
    cq = cq_scr[...]
    scores = [jnp.dot(q_head(h), kt_ref[0, h].astype(BF16), preferred_element_type=F32) for h in range(H_B)]
    for h in range(H_B):
        update(h, scores[h] + cq[:, h:h + 1] + after[h:h + 1, :], vt_ref[0, h].astype(BF16), True)

    @pl.when(kc == pl.num_programs(1) - 1)
    def _():
        row = lax.broadcasted_iota(jnp.int32, (nq, nq), 0)
        col = lax.broadcasted_iota(jnp.int32, (nq, nq), 1)
        cn = cn_scr[...]
        outs = []
        for h in range(H_B):
            sl = slice(h * HEAD_DIM, (h + 1) * HEAD_DIM)
            s = _qk(q_head(h), kn_ref[0][:, sl].astype(BF16)) + cq[:, h:h + 1] - cn[h:h + 1, 0:nq]
            update(h, jnp.where(col <= row, s, NEG), vn_ref[0][:, sl].astype(BF16), False)
            _, l, acc = _dec_load(m_scr, l_scr, acc_scr, h)
            outs.append(acc / l)
        for p in range(H_B // 2):
            o = jnp.concatenate([outs[2 * p], outs[2 * p + 1]], axis=-1)
            o_ref[0, :, p * LANES:(p + 1) * LANES] = _group_rms(o, gout_ref[p], gn_ref[...]).astype(BF16)


def _fox_decode(qb, kb_new, vb_new, logf_new, cache_kt, cache_vt, cache_logf_t, g_out_b):
    b, nq, w = qb.shape
    past = cache_kt.shape[-1]
    ck = min(CACHE_CHUNK, past)
    n_kc = past // ck
    assert nq <= LANES
    pad = jnp.zeros((b, nq, LANES - H_B), F32)
    lfn_col = jnp.concatenate([logf_new, pad], axis=-1)
    lfn_row = jnp.concatenate([jnp.transpose(logf_new, (0, 2, 1)), jnp.zeros((b, H_B, LANES - nq), F32)], axis=-1)
    new_spec = pl.BlockSpec((1, nq, w), lambda bi, c: (bi, 0, 0))
    newest_first = lambda bi, c: (bi, 0, 0, n_kc - 1 - c)
    return pl.pallas_call(
        functools.partial(_fox_dec_body, nq=nq),
        grid=(b, n_kc),
        in_specs=[
            new_spec, new_spec, new_spec,
            pl.BlockSpec((1, nq, LANES), lambda bi, c: (bi, 0, 0)),
            pl.BlockSpec((1, H_B, LANES), lambda bi, c: (bi, 0, 0)),
            pl.BlockSpec((1, H_B, HEAD_DIM, ck), newest_first),
            pl.BlockSpec((1, H_B, HEAD_DIM, ck), newest_first),
            pl.BlockSpec((1, H_B, past), lambda bi, c: (bi, 0, 0)),
            _const_spec((H_B // 2, 1, LANES)), _const_spec((LANES, LANES)),
        ],
        out_specs=new_spec,
        out_shape=jax.ShapeDtypeStruct((b, nq, w), BF16),
        scratch_shapes=[pltpu.VMEM((nq, LANES), F32), pltpu.VMEM((H_B, LANES), F32),
                        pltpu.VMEM((n_kc, H_B, ck), F32),
                        pltpu.VMEM((H_B, nq, 1), F32), pltpu.VMEM((H_B, nq, 1), F32),
                        pltpu.VMEM((H_B, nq, HEAD_DIM), F32)],
        compiler_params=_cparams(("arbitrary", "arbitrary")),
        name="fox_decode",
    )(qb, kb_new, vb_new, lfn_col, lfn_row, cache_kt, cache_vt, cache_logf_t,
      g_out_b.reshape(H_B // 2, 1, LANES), _blockdiag_mean(LANES, HEAD_DIM))


def _route(logits):
    lane_i = lax.broadcasted_iota(jnp.int32, logits.shape, 1)
    lane = lane_i.astype(F32)
    big = float(LANES)
    lg = jnp.where(lane_i < N_GROUPS, logits, NEG)
    mx = jnp.max(lg, axis=-1, keepdims=True)
    grp = jnp.min(jnp.where(lg == mx, lane, big), axis=-1, keepdims=True)
    p_grp = 1.0 / jnp.sum(jnp.exp(lg - mx), axis=-1, keepdims=True)
    e = lane_i - ROUTER_LANE0
    e_grp = lax.shift_right_arithmetic(e, 3).astype(F32)
    sel = (e >= 0) & (e < N_EXPERTS) & (e_grp == grp)
    v = jnp.where(sel, logits, NEG)
    v1 = jnp.max(v, axis=-1, keepdims=True)
    i1 = jnp.min(jnp.where(sel & (v == v1), lane, big), axis=-1, keepdims=True)
    sel2 = sel & (lane != i1)
    vv = jnp.where(sel2, logits, NEG)
    v2 = jnp.max(vv, axis=-1, keepdims=True)
    i2 = jnp.min(jnp.where(sel2 & (vv == v2), lane, big), axis=-1, keepdims=True)
    e2 = jnp.exp(v2 - v1)
    w1 = p_grp / (1.0 + e2)
    w2 = p_grp * e2 / (1.0 + e2)
    gates = jnp.where(lane == i1, w1, 0.0) + jnp.where(lane == i2, w2, 0.0)
    return gates, (i1, i2, w1, w2)


R_E1, R_E2, R_RANK1, R_RANK2, R_W1, R_W2 = range(6)


def _mix_body(oa_ref, ob_ref, x_ref, wa_ref, wb_ref, g2_ref, wr1_ref, wr2_ref, br_ref, tri_ref,
              x1_ref, xn_ref, gates_ref, route_ref, counts_ref):
    y = (jnp.dot(oa_ref[...], wa_ref[...], preferred_element_type=F32)
         + jnp.dot(ob_ref[...], wb_ref[...], preferred_element_type=F32))
    x1 = x_ref[...] + y
    x1_ref[...] = x1
    ms = jnp.mean(x1 * x1, axis=-1, keepdims=True)
    xn = (x1 * lax.rsqrt(ms + RMS_EPS)) * g2_ref[...]
    xn_ref[...] = xn
    h1 = xn.astype(BF16)
    h2 = (xn - h1.astype(F32)).astype(BF16)
    logits = (jnp.dot(h1, wr1_ref[...], preferred_element_type=F32)
              + jnp.dot(h1, wr2_ref[...], preferred_element_type=F32)
              + jnp.dot(h2, wr1_ref[...], preferred_element_type=F32)) + br_ref[...]
    gates, (i1, i2, w1, w2) = _route(logits)
    gates_ref[...] = gates

    @pl.when(pl.program_id(0) == 0)
    def _():
        counts_ref[...] = jnp.zeros_like(counts_ref)

    lane_i = lax.broadcasted_iota(jnp.int32, gates.shape, 1)
    lane = lane_i.astype(F32)
    oh1 = jnp.where(lane == i1, 1.0, 0.0)
    oh2 = jnp.where(lane == i2, 1.0, 0.0)
    comb = oh1 + oh2
    running = counts_ref[0:1, :]
    before = jnp.dot(tri_ref[...], comb.astype(BF16), preferred_element_type=F32) + running
    rank1 = jnp.sum(before * oh1, axis=-1, keepdims=True)
    rank2 = jnp.sum(before * oh2, axis=-1, keepdims=True)
    counts_ref[0:1, :] = running + jnp.sum(comb, axis=0, keepdims=True)
    rec = jnp.zeros_like(gates)
    for idx, val in ((R_E1, i1 - ROUTER_LANE0), (R_E2, i2 - ROUTER_LANE0), (R_RANK1, rank1), (R_RANK2, rank2),
                     (R_W1, w1), (R_W2, w2)):
        rec = jnp.where(lane_i == idx, val, rec)
    route_ref[...] = rec


def _mix_and_route(o_a, o_b, x, w_out, g_norm, w_rg, b_rg, w_re, b_re):
    n, d = x.shape
    bt = min(TOKEN_TILE, n)
    r = jnp.arange(bt)
    tri_strict = (r[None, :] < r[:, None]).astype(BF16)
    wa = w_out[:GROUP_W].astype(BF16)
    wb = w_out[GROUP_W:].astype(BF16)
    n_r = N_GROUPS + N_EXPERTS
    wr = jnp.concatenate([w_rg, w_re, jnp.zeros((d, LANES - n_r), F32)], axis=1)
    wr1 = wr.astype(BF16)
    wr2 = (wr - wr1.astype(F32)).astype(BF16)
    br = jnp.concatenate([b_rg, b_re, jnp.zeros((LANES - n_r,), F32)])[None, :]
    row = lambda width: pl.BlockSpec((bt, width), lambda i: (i, 0))
    return pl.pallas_call(
        _mix_body,
        grid=(n // bt,),
        in_specs=[row(GROUP_W), row(GROUP_W), row(d), _const_spec((GROUP_W, d)), _const_spec((GROUP_W, d)),
                  _const_spec((1, d)), _const_spec((d, LANES)), _const_spec((d, LANES)), _const_spec((1, LANES)),
                  _const_spec((bt, bt))],
        out_specs=[row(d), row(d), row(LANES), row(LANES), _const_spec((8, LANES))],
        out_shape=[jax.ShapeDtypeStruct((n, d), F32), jax.ShapeDtypeStruct((n, d), F32),
                   jax.ShapeDtypeStruct((n, LANES), F32), jax.ShapeDtypeStruct((n, LANES), F32),
                   jax.ShapeDtypeStruct((8, LANES), F32)],
        compiler_params=_cparams(("arbitrary",)),
        name="mix_route",
    )(o_a, o_b, x, wa, wb, g_norm[None, :], wr1, wr2, br, tri_strict)


def _swiglu(xn, wg, wu, wd, gate=None):
    x = xn.astype(BF16)
    g = jnp.dot(x, wg, preferred_element_type=F32)
    u = jnp.dot(x, wu, preferred_element_type=F32)
    h = (g * jax.nn.sigmoid(g)) * u
    if gate is not None:
        h = h * gate
    return jnp.dot(h.astype(BF16), wd, preferred_element_type=F32)


def _expert_body(xn_ref, x1_ref, gates_ref, wg_ref, wu_ref, wd_ref, o_ref):
    e = pl.program_id(1)

    @pl.when(e == 0)
    def _():
        o_ref[...] = x1_ref[...]

    gates = gates_ref[...]
    lane = lax.broadcasted_iota(jnp.int32, gates.shape, 1)
    gate = jnp.sum(jnp.where(lane == e + ROUTER_LANE0, gates, 0.0), axis=-1, keepdims=True)
    o_ref[...] += _swiglu(xn_ref[...], wg_ref[0], wu_ref[0], wd_ref[0], gate)


def _experts(xn, x1, gates, w_gate, w_up, w_down):
    n, d = x1.shape
    ff = w_gate.shape[-1]
    bt = min(MOE_TILE, n)
    row = lambda width: pl.BlockSpec((bt, width), lambda i, e: (i, 0))
    return pl.pallas_call(
        _expert_body,
        grid=(n // bt, N_EXPERTS),
        in_specs=[row(d), row(d), row(LANES),
                  pl.BlockSpec((1, d, ff), lambda i, e: (e, 0, 0)),
                  pl.BlockSpec((1, d, ff), lambda i, e: (e, 0, 0)),
                  pl.BlockSpec((1, ff, d), lambda i, e: (e, 0, 0))],
        out_specs=row(d),
        out_shape=jax.ShapeDtypeStruct((n, d), F32),
        compiler_params=_cparams(("arbitrary", "arbitrary")),
        name="experts",
    )(xn, x1, gates, w_gate, w_up, w_down)


def _row_copies(n_rows, make_copy):
    def issue(r, carry):
        for s in range(2):
            make_copy(r, s).start()
        return carry

    lax.fori_loop(0, n_rows, issue, 0, unroll=8)

    def drain(r, carry):
        for s in range(2):
            make_copy(r, s).wait()
        return carry

    lax.fori_loop(0, n_rows, drain, 0, unroll=8)


def _row_position_body(route_ref, base_ref, pos_ref):
    rec = route_ref[...]
    lane_i = lax.broadcasted_iota(jnp.int32, rec.shape, 1)
    lane = lane_i.astype(F32)
    out = jnp.zeros(rec.shape, F32)
    for slot, (e_lane, r_lane) in enumerate(((R_E1, R_RANK1), (R_E2, R_RANK2))):
        onehot = lane == rec[:, e_lane:e_lane + 1] + float(ROUTER_LANE0)
        base = jnp.sum(jnp.where(onehot, base_ref[...], 0.0), axis=-1, keepdims=True)
        out = jnp.where(lane_i == slot, base + rec[:, r_lane:r_lane + 1], out)
    pos_ref[...] = out.astype(jnp.int32)


def _dispatch_body(pos_ref, x_ref, init_ref, xs_ref, sem):
    del init_ref
    _row_copies(x_ref.shape[0], lambda r, s: pltpu.make_async_copy(
        x_ref.at[pl.ds(r, 1)], xs_ref.at[pl.ds(pos_ref[0, 0, 2 * r + s], 1)], sem))


def _grouped_body(te_ref, nu_ref, xs_ref, wg_ref, wu_ref, wd_ref, ys_ref):
    del te_ref
    used = pl.program_id(0) < nu_ref[0]

    @pl.when(used)
    def _():
        ys_ref[...] = _swiglu(xs_ref[...], wg_ref[0], wu_ref[0], wd_ref[0])

    @pl.when(jnp.logical_not(used))
    def _():
        ys_ref[...] = jnp.zeros_like(ys_ref)


def _combine_body(pos_ref, route_ref, x1_ref, ys_ref, o_ref, buf_scr, sem):
    _row_copies(x1_ref.shape[0], lambda r, s: pltpu.make_async_copy(
        ys_ref.at[pl.ds(pos_ref[0, 0, 2 * r + s], 1)], buf_scr.at[s, pl.ds(r, 1)], sem))
    rec = route_ref[...]
    o_ref[...] = (x1_ref[...] + rec[:, R_W1:R_W1 + 1] * buf_scr[0] + rec[:, R_W2:R_W2 + 1] * buf_scr[1])


def _routed_experts(xn, x1, route, counts, w_gate, w_up, w_down):
    n, d = x1.shape
    ff = w_gate.shape[-1]
    bt = min(TOKEN_TILE, n)
    nt = n // bt
    tm = MOE_ROW_TILE
    n_tiles = (2 * n) // tm + N_EXPERTS
    cnt = counts[0, ROUTER_LANE0:ROUTER_LANE0 + N_EXPERTS].astype(jnp.int32)
    tiles = (cnt + tm - 1) // tm
    tile_end = jnp.cumsum(tiles)
    base_row = (tile_end - tiles) * tm
    n_used = tile_end[-1:]
    tile_expert = jnp.minimum(jnp.sum(jnp.arange(n_tiles)[:, None] >= tile_end[None, :], axis=1), N_EXPERTS - 1)
    base_lanes = jnp.zeros((1, LANES), F32).at[0, ROUTER_LANE0:ROUTER_LANE0 + N_EXPERTS].set(base_row.astype(F32))
    row = lambda width: pl.BlockSpec((bt, width), lambda i: (i, 0))
    pos = pl.pallas_call(
        _row_position_body,
        grid=(nt,),
        in_specs=[row(LANES), _const_spec((1, LANES))],
        out_specs=row(LANES),
        out_shape=jax.ShapeDtypeStruct((n, LANES), jnp.int32),
        compiler_params=_cparams(("arbitrary",)),
        name="moe_positions",
    )(route, base_lanes)
    pos = pos[:, :2].reshape(nt, 1, 2 * bt)

    pos_spec = pl.BlockSpec((1, 1, 2 * bt), lambda i: (i, 0, 0), memory_space=pltpu.SMEM)
    any_spec = pl.BlockSpec(memory_space=pl.ANY)
    xs = pl.pallas_call(
        _dispatch_body,
        grid=(nt,),
        in_specs=[pos_spec, row(d), any_spec],
        out_specs=any_spec,
        out_shape=jax.ShapeDtypeStruct((n_tiles * tm, d), F32),
        scratch_shapes=[pltpu.SemaphoreType.DMA(())],
        input_output_aliases={2: 0},
        compiler_params=_cparams(("arbitrary",)),
        name="moe_dispatch",
    )(pos, xn, jnp.zeros((n_tiles * tm, d), F32))

    w_spec = lambda shape: pl.BlockSpec(shape, lambda t, te, nu: (te[t], 0, 0))
    ys = pl.pallas_call(
        _grouped_body,
        grid_spec=pltpu.PrefetchScalarGridSpec(
            num_scalar_prefetch=2, grid=(n_tiles,),
            in_specs=[pl.BlockSpec((tm, d), lambda t, te, nu: (jnp.minimum(t, nu[0] - 1), 0)),
                      w_spec((1, d, ff)), w_spec((1, d, ff)), w_spec((1, ff, d))],
            out_specs=pl.BlockSpec((tm, d), lambda t, te, nu: (t, 0))),
        out_shape=jax.ShapeDtypeStruct((n_tiles * tm, d), F32),
        compiler_params=_cparams(("arbitrary",)),
        name="moe_experts",
    )(tile_expert.astype(jnp.int32), n_used.astype(jnp.int32), xs, w_gate, w_up, w_down)

    return pl.pallas_call(
        _combine_body,
        grid=(nt,),
        in_specs=[pos_spec, row(LANES), row(d), any_spec],
        out_specs=row(d),
        out_shape=jax.ShapeDtypeStruct((n, d), F32),
        scratch_shapes=[pltpu.VMEM((2, bt, d), F32), pltpu.SemaphoreType.DMA(())],
        compiler_params=_cparams(("arbitrary",)),
        name="moe_combine",
    )(pos, route, x1, ys)


def _ffn(o_a, o_b, x, w_out, g_norm_ffn, w_rg, b_rg, w_re, b_re, wg, wu, wd):
    b, t, d = x.shape
    n = b * t
    x1, xn, gates, route, counts = _mix_and_route(o_a.reshape(n, -1), o_b.reshape(n, -1), x.reshape(n, d), w_out,
                                                  g_norm_ffn, w_rg, b_rg, w_re, b_re)
    if 2 * n >= ROUTED_MIN_ASSIGNMENTS:
        y = _routed_experts(xn, x1, route, counts, wg, wu, wd)
    else:
        y = _experts(xn, x1, gates, wg, wu, wd)
    return y.reshape(b, t, d)


def kernel(x_prompt, x_sample, cache_a_k, cache_a_v, cache_b_k, cache_b_v, cache_b_logf, g_norm_mix, w_in, b_forget, g_qk_a, g_qk_b, lambda_qk, g_out_a, g_out_b, w_out, rel_bias, g_norm_ffn, w_router_group, b_router_group, w_router_expert, b_router_expert, w_exp_gate, w_exp_up, w_exp_down):
    depth = w_in.shape[0]
    assert depth == 1, "single-layer step only"
    bp, tp, d = x_prompt.shape
    bs, ts, _ = x_sample.shape
    past = cache_a_k.shape[2]
    w_in0, w_out0 = w_in[0], w_out[0]
    wg, wu, wd = w_exp_gate[0].astype(BF16), w_exp_up[0].astype(BF16), w_exp_down[0].astype(BF16)
    ffn_w = (w_out0, g_norm_ffn[0], w_router_group[0], b_router_group[0], w_router_expert[0], b_router_expert[0],
             wg, wu, wd)

    (ka_p, va_p, kb_p, vb_p, logf_p, qa_t, ka_bf, va_t, qb_t, kb_aug, vb_t, c_edge) = _projection(
        x_prompt, g_norm_mix[0], w_in0, b_forget[0], g_qk_a[0], g_qk_b[0], with_aug=True)
    blk = min(ATTN_BLOCK, tp)
    bk = va_t.shape[-1]
    assert blk % CHUNK == 0 and blk >= MAX_DISTANCE and blk % bk == 0
    q_pos = blk + jnp.arange(blk, dtype=jnp.int32)
    bkt_p = jnp.stack([_bucket_map(q_pos, s * bk + jnp.arange(bk, dtype=jnp.int32)).T
                       for s in range(2 * blk // bk)])
    bias_p = _bias_tiles(rel_bias, bkt_p)
    o_a = _diff_attention(qa_t, ka_bf, va_t, bias_p, lambda_qk[0], g_out_a[0])
    o_b = _fox_attention(qb_t, kb_aug, vb_t, c_edge, g_qk_b[0], g_out_b[0])
    y_p = _ffn(o_a, o_b, x_prompt, *ffn_w)

    xs = x_sample.reshape(1, bs * ts, d)
    (ka_s, va_s, kb_s, vb_s, logf_s, qa_s, qb_s) = _projection(
        xs, g_norm_mix[0], w_in0, b_forget[0], g_qk_a[0], g_qk_b[0], with_aug=False)
    per_stream = lambda a: a.reshape(bs, ts, a.shape[-1])
    ka_s, va_s, kb_s, vb_s, logf_s, qa_s, qb_s = map(per_stream, (ka_s, va_s, kb_s, vb_s, logf_s, qa_s, qb_s))
    ck = min(CACHE_CHUNK, past)
    q_pos = past + jnp.arange(ts, dtype=jnp.int32)
    bkt_c = _bucket_map(q_pos, jnp.arange(past, dtype=jnp.int32)).reshape(ts, past // ck, ck).transpose(1, 0, 2)
    bias_c = _bias_tiles(rel_bias, bkt_c)
    bias_n = _bias_tiles(rel_bias, _bucket_map(q_pos, q_pos)[None])
    o_a_s = _diff_decode(qa_s, ka_s, va_s, jnp.transpose(cache_a_k[0], (0, 2, 3, 4, 1)),
                         cache_a_v[0].reshape(bs, past * H_A, 2 * HEAD_DIM), bias_c, bias_n, lambda_qk[0], g_out_a[0])
    o_b_s = _fox_decode(qb_s, kb_s, vb_s, logf_s, jnp.transpose(cache_b_k[0], (0, 2, 3, 1)),
                        jnp.transpose(cache_b_v[0], (0, 2, 3, 1)), jnp.transpose(cache_b_logf[0], (0, 2, 1)),
                        g_out_b[0])
    y_s = _ffn(o_a_s, o_b_s, x_sample, *ffn_w)

    def rows(ka, va, kb, vb, logf, b, t):
        return (ka.reshape(1, b, t, H_A, 2, HEAD_DIM), va.reshape(1, b, t, H_A, 2 * HEAD_DIM),
                kb.reshape(1, b, t, H_B, HEAD_DIM), vb.reshape(1, b, t, H_B, HEAD_DIM), logf.reshape(1, b, t, H_B))

    return (y_p, y_s) + rows(ka_p, va_p, kb_p, vb_p, logf_p, bp, tp) + rows(ka_s, va_s, kb_s, vb_s, logf_s, bs, ts)
```

```python
import functools
import math

import jax
import jax.numpy as jnp
from jax import lax
from jax.experimental import pallas as pl
from jax.experimental.pallas import tpu as pltpu

F32 = jnp.float32
BF16 = jnp.bfloat16

LANES = 128
VMEM_LIMIT_BYTES = 56 * 1024 * 1024

HEAD_DIM = 64
H_A = 4
H_B = 8
GROUP_W = 512
MAIN_W = 6 * GROUP_W
CHUNK = 64
N_BUCKETS = 32
MAX_DISTANCE = 128
N_GROUPS = 4
EXPERTS_PER_GROUP = 8
N_EXPERTS = N_GROUPS * EXPERTS_PER_GROUP
ROUTER_LANE0 = N_GROUPS
RMS_EPS = 1e-6
NEG = -1e30
LOG2E = 1.4426950408889634
QK_SCALE = HEAD_DIM ** -0.5
LAM_INIT = 0.8 - 0.6 * math.exp(-0.3 * 0)
N_CPARTS = 3

ONES_ROWS = 16
VA_ROWS = 2 * HEAD_DIM + ONES_ROWS
VB_ROWS = HEAD_DIM + ONES_ROWS

TOKEN_TILE = 512
ATTN_BLOCK = 512
KV_BLOCK = 256
FAR_UNROLL = 4
CACHE_CHUNK = 1024
MOE_TILE = 1024
MOE_ROW_TILE = 256
ROUTED_MIN_ASSIGNMENTS = 4 * N_EXPERTS * MOE_ROW_TILE


def _cparams(sem):
    return pltpu.CompilerParams(dimension_semantics=sem, vmem_limit_bytes=VMEM_LIMIT_BYTES)


def _const_spec(shape):
    nd = len(shape)
    return pl.BlockSpec(shape, lambda *_: (0,) * nd)


def _split3(x):
    p1 = x.astype(BF16).astype(F32)
    r1 = x - p1
    p2 = r1.astype(BF16).astype(F32)
    p3 = (r1 - p2).astype(BF16).astype(F32)
    return p1, p2, p3


def _lane_groups(parts, lane):
    return jnp.where(lane < 8, parts[0], jnp.where(lane < 16, parts[1], parts[2]))


def _cumsum_rows(tri, x, carry):
    c = carry
    for part in _split3(x):
        c = c + jnp.dot(tri, part.astype(BF16), preferred_element_type=F32)
    return c


def _log_sigmoid(x):
    return jnp.minimum(x, 0.0) - jnp.log(1.0 + jnp.exp(-jnp.abs(x)))


def _group_rms(raw, gain_row, gn):
    ms = jnp.dot((raw * raw).astype(BF16), gn, preferred_element_type=F32)
    return raw * lax.rsqrt(ms + RMS_EPS) * gain_row


def _online_update(state, s, v):
    m, l, acc = state
    m_new = jnp.maximum(m, jnp.max(s, axis=-1, keepdims=True))
    p = jnp.exp2(s - m_new)
    alpha = jnp.exp2(m - m_new)
    l_new = alpha * l + jnp.sum(p, axis=-1, keepdims=True)
    acc_new = alpha * acc + jnp.dot(p.astype(BF16), v, preferred_element_type=F32)
    return m_new, l_new, acc_new


def _qk(q, k):
    return lax.dot_general(q, k, (((1,), (1,)), ((), ())), preferred_element_type=F32)


def _proj_body(*refs, with_aug, bt, bk):
    (x_ref, g1_ref, wm_ref, wf_ref, bf_ref, gains_ref, gn_ref, tri_ref, place_ref) = refs[:9]
    ka_ref, va_ref, kb_ref, vb_ref, logf_ref = refs[9:14]
    x = x_ref[0]
    ms = jnp.mean(x * x, axis=-1, keepdims=True)
    xn = (x * lax.rsqrt(ms + RMS_EPS)) * g1_ref[...]
    xb = xn.astype(BF16)
    proj = jnp.dot(xb, wm_ref[...], preferred_element_type=F32)
    gains = gains_ref[...]
    gn = gn_ref[...]
    w = GROUP_W
    qa = _group_rms(proj[:, 0:w], gains[0:1], gn)
    ka = _group_rms(proj[:, w:2 * w], gains[1:2], gn)
    va = proj[:, 2 * w:3 * w]
    qb = _group_rms(proj[:, 3 * w:4 * w], gains[2:3], gn)
    kb = _group_rms(proj[:, 4 * w:5 * w], gains[3:4], gn)
    vb = proj[:, 5 * w:6 * w]
    ka_ref[0] = ka
    va_ref[0] = va
    kb_ref[0] = kb
    vb_ref[0] = vb
    fl = jnp.dot(xb, wf_ref[...], preferred_element_type=F32) + bf_ref[...]
    logf = _log_sigmoid(fl)
    logf_ref[0] = logf[:, 0:H_B]
    qscale = QK_SCALE * LOG2E
    if not with_aug:
        qa_ref, qb_ref = refs[14:16]
        qa_ref[0] = (qa * qscale).astype(BF16)
        qb_ref[0] = (qb * qscale).astype(BF16)
        return
    qa_t, ka_bf, va_t, qb_t, kb_aug, vb_t, cedge_ref, carry_ref = refs[14:22]
    n_chunk = bt // bk
    row = lax.broadcasted_iota(jnp.int32, (LANES, bt), 0)
    ones_tail = jnp.where(lax.broadcasted_iota(jnp.int32, (ONES_ROWS, bt), 0) == 0, 1.0, 0.0)

    def put_chunks(ref, idx, vt):
        vt = vt.astype(BF16)
        for c in range(n_chunk):
            ref[0, idx, c] = vt[:, c * bk:(c + 1) * bk]

    for h in range(H_A):
        sl = slice(h * LANES, (h + 1) * LANES)
        q_t = (qa[:, sl] * qscale).T
        qa_t[0, 2 * h] = jnp.where(row < HEAD_DIM, q_t, 0.0).astype(BF16)
        qa_t[0, 2 * h + 1] = jnp.where(row >= HEAD_DIM, q_t, 0.0).astype(BF16)
        ka_bf[0, h] = ka[:, sl].astype(BF16)
        put_chunks(va_t, h, jnp.concatenate([va[:, sl].T, ones_tail], axis=0))
        vb_pair_t = vb[:, sl].T
        for hh in range(2):
            put_chunks(vb_t, 2 * h + hh,
                       jnp.concatenate([vb_pair_t[hh * HEAD_DIM:(hh + 1) * HEAD_DIM], ones_tail], axis=0))

    @pl.when(pl.program_id(1) == 0)
    def _():
        carry_ref[...] = jnp.zeros_like(carry_ref)

    c = _cumsum_rows(tri_ref[...], logf * LOG2E, carry_ref[0:1, :])
    carry_ref[0:1, :] = c[bt - 1:bt, :]
    edge_row = lax.broadcasted_iota(jnp.int32, (8, LANES), 0)
    cedge_ref[0, 0] = jnp.where(edge_row == 0, c[0:1, :], jnp.where(edge_row == 1, c[bt - 1:bt, :], 0.0))
    lane = lax.broadcasted_iota(jnp.int32, (bt, LANES), 1)
    cparts = _lane_groups(_split3(c), lane).astype(BF16)
    extras = jnp.dot(cparts, place_ref[...], preferred_element_type=F32)
    ones_q = jnp.where((lane >= HEAD_DIM + N_CPARTS) & (lane < HEAD_DIM + 2 * N_CPARTS), 1.0, 0.0)
    ones_k = jnp.where((lane >= HEAD_DIM) & (lane < HEAD_DIM + N_CPARTS), 1.0, 0.0)
    for h in range(H_B):
        sl = slice((h // 2) * LANES, (h // 2 + 1) * LANES)
        qp = qb[:, sl] * qscale
        kp = kb[:, sl]
        if h % 2:
            qp = pltpu.roll(qp, HEAD_DIM, 1)
            kp = pltpu.roll(kp, HEAD_DIM, 1)
        eq = extras[:, h * LANES:(h + 1) * LANES] + ones_q
        ek = extras[:, (H_B + h) * LANES:(H_B + h + 1) * LANES] + ones_k
        qb_t[0, h] = jnp.where(lane < HEAD_DIM, qp, eq).T.astype(BF16)
        kb_aug[0, h] = jnp.where(lane < HEAD_DIM, kp, ek).astype(BF16)


def _blockdiag_mean(n, group):
    r = jnp.arange(n)
    return jnp.where((r[:, None] // group) == (r[None, :] // group), 1.0 / group, 0.0).astype(BF16)


def _tri(n):
    r = jnp.arange(n)
    return (r[None, :] <= r[:, None]).astype(BF16)


def _place_matrix():
    rows = jnp.arange(LANES)[:, None]
    cols = jnp.arange(2 * H_B * LANES)[None, :]
    p, h = rows // 8, rows % 8
    valid = rows < 8 * N_CPARTS
    qcol = h * LANES + HEAD_DIM + p
    kcol = (H_B + h) * LANES + HEAD_DIM + N_CPARTS + p
    m = jnp.where(valid & (cols == qcol), 1.0, 0.0) - jnp.where(valid & (cols == kcol), 1.0, 0.0)
    return m.astype(BF16)


def _projection(x, g_norm, w_in, b_forget, g_qk_a, g_qk_b, *, with_aug):
    b, t, d = x.shape
    bt = min(TOKEN_TILE, t)
    nt = t // bt
    wm = w_in[:, :MAIN_W].astype(BF16)
    wf_cols = w_in[:, MAIN_W:MAIN_W + H_B]
    wf = jnp.concatenate([wf_cols] * N_CPARTS + [jnp.zeros((d, LANES - H_B * N_CPARTS), F32)], axis=1).astype(BF16)
    bfv = jnp.concatenate([b_forget] * N_CPARTS + [jnp.zeros((LANES - H_B * N_CPARTS,), F32)])[None, :]
    gains = jnp.stack([jnp.tile(g_qk_a[0], 2 * H_A), jnp.tile(g_qk_a[1], 2 * H_A),
                       jnp.tile(g_qk_b[0], H_B), jnp.tile(g_qk_b[1], H_B)])
    gn = _blockdiag_mean(GROUP_W, HEAD_DIM)
    tri = _tri(bt)
    place = _place_matrix()
    in_specs = [
        pl.BlockSpec((1, bt, d), lambda i, j: (i, j, 0)),
        _const_spec((1, d)),
        pl.BlockSpec((d, MAIN_W), lambda i, j: (0, 0), pipeline_mode=pl.Buffered(1)),
        _const_spec((d, LANES)), _const_spec((1, LANES)), _const_spec((4, GROUP_W)),
        _const_spec((GROUP_W, GROUP_W)), _const_spec((bt, bt)), _const_spec((LANES, 2 * H_B * LANES)),
    ]
    row_spec = pl.BlockSpec((1, bt, GROUP_W), lambda i, j: (i, j, 0))
    out_shape = [jax.ShapeDtypeStruct((b, t, GROUP_W), F32)] * 4 + [jax.ShapeDtypeStruct((b, t, H_B), F32)]
    out_specs = [row_spec] * 4 + [pl.BlockSpec((1, bt, H_B), lambda i, j: (i, j, 0))]
    scratch = []
    bk = min(KV_BLOCK, bt)
    if with_aug:
        def add(shape, block, index_map):
            out_shape.append(jax.ShapeDtypeStruct(shape, BF16))
            out_specs.append(pl.BlockSpec(block, index_map))

        rows_major = lambda i, j: (i, 0, j, 0)
        time_minor = lambda i, j: (i, 0, 0, j)
        chunked = lambda i, j: (i, 0, j, 0, 0)
        add((b, 2 * H_A, LANES, t), (1, 2 * H_A, LANES, bt), time_minor)
        add((b, H_A, t, LANES), (1, H_A, bt, LANES), rows_major)
        add((b, H_A, t // bk, VA_ROWS, bk), (1, H_A, bt // bk, VA_ROWS, bk), chunked)
        add((b, H_B, LANES, t), (1, H_B, LANES, bt), time_minor)
        add((b, H_B, t, LANES), (1, H_B, bt, LANES), rows_major)
        add((b, H_B, t // bk, VB_ROWS, bk), (1, H_B, bt // bk, VB_ROWS, bk), chunked)
        out_shape.append(jax.ShapeDtypeStruct((b, nt, 8, LANES), F32))
        out_specs.append(pl.BlockSpec((1, 1, 8, LANES), lambda i, j: (i, j, 0, 0)))
        scratch = [pltpu.VMEM((8, LANES), F32)]
    else:
        out_shape += [jax.ShapeDtypeStruct((b, t, GROUP_W), BF16)] * 2
        out_specs += [row_spec] * 2
    return pl.pallas_call(
        functools.partial(_proj_body, with_aug=with_aug, bt=bt, bk=bk),
        grid=(b, nt), in_specs=in_specs, out_specs=out_specs, out_shape=out_shape, scratch_shapes=scratch,
        compiler_params=_cparams(("arbitrary", "arbitrary")),
        name="proj_aug" if with_aug else "proj_plain",
    )(x, g_norm[None, :], wm, wf, bfv, gains, gn, tri, place)


def _t5_bucket(rel):
    nb = N_BUCKETS // 2
    max_exact = nb // 2
    base = jnp.where(rel > 0, nb, 0)
    n = jnp.abs(rel)
    large = max_exact + (jnp.log(jnp.maximum(n, max_exact).astype(jnp.float32) / max_exact)
                         / math.log(MAX_DISTANCE / max_exact) * (nb - max_exact)).astype(jnp.int32)
    large = jnp.minimum(large, nb - 1)
    return base + jnp.where(n < max_exact, n, large)


def _bucket_map(q_pos, k_pos):
    bkt = _t5_bucket(k_pos[None, :] - q_pos[:, None])
    visible = (k_pos[None, :] // CHUNK) <= (q_pos[:, None] // CHUNK)
    return jnp.where(visible, bkt, -1).astype(jnp.int32)


def _bias_body(rb_ref, bkt_ref, o_ref):
    h = pl.program_id(0)
    bkt = bkt_ref[0]
    far = rb_ref[N_BUCKETS // 2 - 1, h]
    acc = jnp.zeros(bkt.shape, F32)
    for b in range(N_BUCKETS):
        acc = jnp.where(bkt == b, rb_ref[b, h] - far, acc)
    o_ref[0, 0] = jnp.where(bkt < 0, NEG, acc * LOG2E)


def _bias_tiles(rel_bias, bkt):
    n, r, c = bkt.shape
    return pl.pallas_call(
        _bias_body,
        grid=(H_A, n),
        in_specs=[pl.BlockSpec(memory_space=pltpu.SMEM), pl.BlockSpec((1, r, c), lambda h, i: (i, 0, 0))],
        out_specs=pl.BlockSpec((1, 1, r, c), lambda h, i: (h, i, 0, 0)),
        out_shape=jax.ShapeDtypeStruct((H_A, n, r, c), F32),
        compiler_params=_cparams(("arbitrary", "arbitrary")),
        name="bias_tiles",
    )(rel_bias, bkt)


def _lam(lq):
    a = jnp.sum(lq[0:1, :] * lq[1:2, :], axis=-1, keepdims=True)
    b = jnp.sum(lq[2:3, :] * lq[3:4, :], axis=-1, keepdims=True)
    return jnp.exp(a) - jnp.exp(b) + LAM_INIT


def _attn_init_t(rows, bq):
    return (jnp.full((1, bq), NEG, F32), jnp.zeros((rows, bq), F32))


def _online_update_t(state, st, vt):
    m, acc = state
    m_new = jnp.maximum(m, jnp.max(st, axis=0, keepdims=True))
    p = jnp.exp2(st - m_new)
    alpha = jnp.exp2(m - m_new)
    return m_new, alpha * acc + jnp.dot(vt, p.astype(BF16), preferred_element_type=F32)


def _normalized_t(state, rows):
    acc = state[1]
    return acc[0:rows] / acc[rows:rows + 1]


def _pipelined_sweep(i, n_sub, n_near, slots, score_fn, value_fn, modify, rows, blk, first=0):
    assert n_sub % 2 == 0 and n_near in (1, 2)

    def run_block(jb, states, near, last):
        states = list(states)
        for s in range(n_sub):
            j = jb * n_sub + s
            cur, nxt = slots[s % 2], slots[(s + 1) % 2]
            for c in range(2):
                if not (last and s == n_sub - 1):
                    nxt[c] = score_fn(c, j + 1)
                st = cur[c]
                if near is not None:
                    st = modify(st, near, s)
                states[c] = _online_update_t(states[c], st, value_fn(c, j))
        return tuple(states)

    for c in range(2):
        slots[0][c] = score_fn(c, first * n_sub)
    states = (_attn_init_t(rows, blk), _attn_init_t(rows, blk))
    n_far = jnp.maximum(i + 1 - n_near, 0)

    def run_far(jb, states, n_blocks):
        for d in range(n_blocks):
            states = run_block(jb + d, states, None, False)
        return states

    n_groups = jnp.maximum(n_far - first, 0) // FAR_UNROLL
    states = lax.fori_loop(0, n_groups, lambda g, st: run_far(first + g * FAR_UNROLL, st, FAR_UNROLL), states)
    states = lax.fori_loop(first + n_groups * FAR_UNROLL, n_far, lambda jb, st: run_far(jb, st, 1), states)
    if n_near == 2:
        states = lax.fori_loop(n_far, i, lambda jb, st: run_block(jb, st, 1, False), states)
    return run_block(i, states, 0, True)


def _fox_body(first_ref, qt_ref, k_ref, vt_ref, gout_ref, gn_ref, o_ref, s0_scr, s1_scr, *, blk, bk):
    i = pl.program_id(2)
    first = first_ref[(pl.program_id(0) * pl.num_programs(1) + pl.program_id(1)) * pl.num_programs(2) + i]
    krow = lax.broadcasted_iota(jnp.int32, (bk, blk), 0)
    qcol = lax.broadcasted_iota(jnp.int32, (bk, blk), 1)
    qts = (qt_ref[0, 0], qt_ref[0, 1])

    def score_fn(hh, j):
        off = pl.multiple_of(j * bk, bk)
        return jnp.dot(k_ref[0, hh, pl.ds(off, bk), :], qts[hh], preferred_element_type=F32)

    def causal(st, near, s):
        return jnp.where(krow + s * bk <= qcol, st, NEG)

    states = _pipelined_sweep(i, blk // bk, 1, (s0_scr, s1_scr), score_fn, lambda hh, j: vt_ref[0, hh, j],
                              causal, VB_ROWS, blk, first=first)
    o_t = jnp.concatenate([_normalized_t(states[0], HEAD_DIM), _normalized_t(states[1], HEAD_DIM)], axis=0)
    o_ref[0] = (_group_rms(o_t.T, gout_ref[0], gn_ref[...])).astype(BF16)


EXP2_ZERO_BELOW = -150.0


def _fox_skip_plan(c_edge, g_qk_b):
    b, nq = c_edge.shape[:2]
    c_first = c_edge[:, :, 0, :H_B]
    c_last = c_edge[:, :, 1, :H_B]
    slack = 1.02
    qk_max = (HEAD_DIM * jnp.max(jnp.abs(g_qk_b[0])) * jnp.max(jnp.abs(g_qk_b[1])) * QK_SCALE * LOG2E) * slack
    best = 2.0 * qk_max + (c_first[:, :, None, :] - c_last[:, None, :, :]) * (1.0 / slack)
    dead = best < EXP2_ZERO_BELOW - 1.0
    dead = jnp.logical_and(dead[..., 0::2], dead[..., 1::2])
    j_lt_i = (jnp.arange(nq)[None, :] < jnp.arange(nq)[:, None])[None, :, :, None]
    lead = jnp.cumprod(jnp.logical_and(dead, j_lt_i).astype(jnp.int32), axis=2)
    first = jnp.sum(lead, axis=2)
    return jnp.transpose(first, (0, 2, 1)).reshape(-1).astype(jnp.int32)


def _fox_attention(qb_t, kb_aug, vb_t, c_edge, g_qk_b, g_out_b):
    b, _, _, t = qb_t.shape
    blk = min(ATTN_BLOCK, t)
    bk = vb_t.shape[-1]
    pairs = H_B // 2
    gout = g_out_b.reshape(pairs, 1, LANES)
    first = _fox_skip_plan(c_edge, g_qk_b)
    return pl.pallas_call(
        functools.partial(_fox_body, blk=blk, bk=bk),
        grid_spec=pltpu.PrefetchScalarGridSpec(
            num_scalar_prefetch=1, grid=(b, pairs, t // blk),
            in_specs=[
                pl.BlockSpec((1, 2, LANES, blk), lambda bi, p, i, f: (bi, p, 0, i)),
                pl.BlockSpec((1, 2, t, LANES), lambda bi, p, i, f: (bi, p, 0, 0)),
                pl.BlockSpec((1, 2, t // bk, VB_ROWS, bk), lambda bi, p, i, f: (bi, p, 0, 0, 0)),
                pl.BlockSpec((1, 1, LANES), lambda bi, p, i, f: (p, 0, 0)),
                pl.BlockSpec((LANES, LANES), lambda bi, p, i, f: (0, 0)),
            ],
            out_specs=pl.BlockSpec((1, blk, LANES), lambda bi, p, i, f: (bi, i, p)),
            scratch_shapes=[pltpu.VMEM((2, bk, blk), F32), pltpu.VMEM((2, bk, blk), F32)]),
        out_shape=jax.ShapeDtypeStruct((b, t, H_B * HEAD_DIM), BF16),
        compiler_params=_cparams(("arbitrary", "arbitrary", "arbitrary")),
        name="fox_attention",
    )(first, qb_t, kb_aug, vb_t, gout, _blockdiag_mean(LANES, HEAD_DIM))


def _diff_finish(states, lam, gout):
    (_, l1, a1), (_, l2, a2) = states
    o = a1 / l1 - lam * (a2 / l2)
    ms = jnp.mean(o * o, axis=-1, keepdims=True)
    return (o * lax.rsqrt(ms + RMS_EPS)) * gout * (1.0 - LAM_INIT)


def _diff_body(lam_ref, qt_ref, k_ref, vt_ref, bias_ref, gout_ref, o_ref, s0_scr, s1_scr, *, blk, bk):
    i = pl.program_id(2)
    n_sub = blk // bk
    qts = (qt_ref[0, 0], qt_ref[0, 1])

    def score_fn(mi, j):
        off = pl.multiple_of(j * bk, bk)
        return jnp.dot(k_ref[0, 0, pl.ds(off, bk), :], qts[mi], preferred_element_type=F32)

    def add_bias(st, near, s):
        return st + bias_ref[0, (1 - near) * n_sub + s]

    states = _pipelined_sweep(i, n_sub, 2, (s0_scr, s1_scr), score_fn, lambda mi, j: vt_ref[0, 0, j],
                              add_bias, VA_ROWS, blk)
    lam = _lam(lam_ref[...])
    o = (_normalized_t(states[0], 2 * HEAD_DIM) - lam * _normalized_t(states[1], 2 * HEAD_DIM)).T
    ms = jnp.mean(o * o, axis=-1, keepdims=True)
    o_ref[0] = ((o * lax.rsqrt(ms + RMS_EPS)) * gout_ref[0] * (1.0 - LAM_INIT)).astype(BF16)


def _diff_attention(qa_t, ka_bf, va_t, bias, lambda_qk, g_out_a):
    b, _, _, t = qa_t.shape
    blk = min(ATTN_BLOCK, t)
    bk = va_t.shape[-1]
    gout = g_out_a.reshape(H_A, 1, LANES)
    return pl.pallas_call(
        functools.partial(_diff_body, blk=blk, bk=bk),
        grid=(b, H_A, t // blk),
        in_specs=[
            _const_spec((4, HEAD_DIM)),
            pl.BlockSpec((1, 2, LANES, blk), lambda bi, h, i: (bi, h, 0, i)),
            pl.BlockSpec((1, 1, t, LANES), lambda bi, h, i: (bi, h, 0, 0)),
            pl.BlockSpec((1, 1, t // bk, VA_ROWS, bk), lambda bi, h, i: (bi, h, 0, 0, 0)),
            pl.BlockSpec((1, 2 * (blk // bk), bk, blk), lambda bi, h, i: (h, 0, 0, 0)),
            pl.BlockSpec((1, 1, LANES), lambda bi, h, i: (h, 0, 0)),
        ],
        out_specs=pl.BlockSpec((1, blk, LANES), lambda bi, h, i: (bi, i, h)),
        out_shape=jax.ShapeDtypeStruct((b, t, H_A * 2 * HEAD_DIM), BF16),
        scratch_shapes=[pltpu.VMEM((2, bk, blk), F32), pltpu.VMEM((2, bk, blk), F32)],
        compiler_params=_cparams(("arbitrary", "arbitrary", "arbitrary")),
        name="diff_attention",
    )(lambda_qk, qa_t, ka_bf, va_t, bias, gout)


def _dec_load(m_scr, l_scr, acc_scr, idx):
    return m_scr[idx], l_scr[idx], acc_scr[idx]


def _dec_store(m_scr, l_scr, acc_scr, idx, state):
    m_scr[idx], l_scr[idx], acc_scr[idx] = state


def _dec_init(m_scr, l_scr, acc_scr):
    m_scr[...] = jnp.full(m_scr.shape, NEG, F32)
    l_scr[...] = jnp.zeros(l_scr.shape, F32)
    acc_scr[...] = jnp.zeros(acc_scr.shape, F32)


def _diff_dec_body(lam_ref, q_ref, kn_ref, vn_ref, kt_ref, v_ref, bc_ref, bn_ref, gout_ref, o_ref,
                   m_scr, l_scr, acc_scr, *, ck):
    kc = pl.program_id(1)

    @pl.when(kc == 0)
    def _():
        _dec_init(m_scr, l_scr, acc_scr)

    q = q_ref[0]

    def update(idx, s, v):
        _dec_store(m_scr, l_scr, acc_scr, idx, _online_update(_dec_load(m_scr, l_scr, acc_scr, idx), s, v))

    def q_map(h, mi):
        col = (2 * h + mi) * HEAD_DIM
        return q[:, col:col + HEAD_DIM]

    scores = [jnp.dot(q_map(h, mi), kt_ref[0, h, mi].astype(BF16), preferred_element_type=F32)
              for h in range(H_A) for mi in range(2)]
    for h in range(H_A):
        v = v_ref[0, pl.ds(h, ck, stride=H_A), :].astype(BF16)
        for mi in range(2):
            update(2 * h + mi, scores[2 * h + mi] + bc_ref[h, 0], v)

    @pl.when(kc == pl.num_programs(1) - 1)
    def _():
        lam = _lam(lam_ref[...])
        for h in range(H_A):
            sl = slice(h * LANES, (h + 1) * LANES)
            k_new = kn_ref[0][:, sl].astype(BF16)
            v_new = vn_ref[0][:, sl].astype(BF16)
            for mi in range(2):
                s = _qk(q_map(h, mi), k_new[:, mi * HEAD_DIM:(mi + 1) * HEAD_DIM]) + bn_ref[h, 0]
                update(2 * h + mi, s, v_new)
            states = (_dec_load(m_scr, l_scr, acc_scr, 2 * h), _dec_load(m_scr, l_scr, acc_scr, 2 * h + 1))
            o_ref[0, :, sl] = _diff_finish(states, lam, gout_ref[h]).astype(BF16)


def _diff_decode(qa, ka_new, va_new, cache_kt, cache_v, bias_c, bias_n, lambda_qk, g_out_a):
    b, nq, w = qa.shape
    past = cache_kt.shape[-1]
    ck = min(CACHE_CHUNK, past)
    n_kc = past // ck
    new_spec = pl.BlockSpec((1, nq, w), lambda bi, c: (bi, 0, 0))
    return pl.pallas_call(
        functools.partial(_diff_dec_body, ck=ck),
        grid=(b, n_kc),
        in_specs=[
            _const_spec((4, HEAD_DIM)), new_spec, new_spec, new_spec,
            pl.BlockSpec((1, H_A, 2, HEAD_DIM, ck), lambda bi, c: (bi, 0, 0, 0, c)),
            pl.BlockSpec((1, H_A * ck, LANES), lambda bi, c: (bi, c, 0)),
            pl.BlockSpec((H_A, 1, nq, ck), lambda bi, c: (0, c, 0, 0)),
            _const_spec((H_A, 1, nq, nq)),
            _const_spec((H_A, 1, LANES)),
        ],
        out_specs=new_spec,
        out_shape=jax.ShapeDtypeStruct((b, nq, w), BF16),
        scratch_shapes=[pltpu.VMEM((2 * H_A, nq, 1), F32), pltpu.VMEM((2 * H_A, nq, 1), F32),
                        pltpu.VMEM((2 * H_A, nq, LANES), F32)],
        compiler_params=_cparams(("arbitrary", "arbitrary")),
        name="diff_decode",
    )(lambda_qk, qa, ka_new, va_new, cache_kt, cache_v, bias_c, bias_n, g_out_a.reshape(H_A, 1, LANES))


def _suffix_sum_lanes(x):
    n = x.shape[-1]
    lane = lax.broadcasted_iota(jnp.int32, x.shape, x.ndim - 1)
    shift = 1
    while shift < n:
        x = x + jnp.where(lane + shift < n, pltpu.roll(x, n - shift, x.ndim - 1), 0.0)
        shift *= 2
    return x


def _fox_dec_body(q_ref, kn_ref, vn_ref, lfn_col_ref, lfn_row_ref, kt_ref, vt_ref, lfc_ref, gout_ref, gn_ref,
                  o_ref, cq_scr, cn_scr, after_scr, m_scr, l_scr, acc_scr, *, nq):
    kc = pl.program_id(1)
    q = q_ref[0]
    q_head = lambda h: q[:, h * HEAD_DIM:(h + 1) * HEAD_DIM]

    def update(h, s, v, trans_v):
        m, l, acc = _dec_load(m_scr, l_scr, acc_scr, h)
        m_new = jnp.maximum(m, jnp.max(s, axis=-1, keepdims=True))
        p = jnp.exp2(s - m_new)
        alpha = jnp.exp2(m - m_new)
        pv = _qk(p.astype(BF16), v) if trans_v else jnp.dot(p.astype(BF16), v, preferred_element_type=F32)
        _dec_store(m_scr, l_scr, acc_scr, h, (m_new, alpha * l + jnp.sum(p, axis=-1, keepdims=True),
                                              alpha * acc + pv))

    n_kc = after_scr.shape[0]
    ck = after_scr.shape[-1]

    @pl.when(kc == 0)
    def _():
        _dec_init(m_scr, l_scr, acc_scr)
        r = lax.broadcasted_iota(jnp.int32, (nq, nq), 0)
        c = lax.broadcasted_iota(jnp.int32, (nq, nq), 1)
        tri_n = jnp.where(c <= r, 1.0, 0.0).astype(BF16)
        cq_scr[...] = _cumsum_rows(tri_n, lfn_col_ref[0] * LOG2E, jnp.zeros((1, LANES), F32))
        lf_row = lfn_row_ref[0] * LOG2E
        total = jnp.sum(lf_row, axis=-1, keepdims=True)
        cn_scr[...] = total - _suffix_sum_lanes(lf_row) + lf_row
        lf = lfc_ref[0] * LOG2E
        after = _suffix_sum_lanes(lf) - lf
        for c_idx in range(n_kc):
            after_scr[c_idx] = after[:, c_idx * ck:(c_idx + 1) * ck]

    after = after_scr[n_kc - 1 - kc]
    cq = cq_scr[...]
    scores = [jnp.dot(q_head(h), kt_ref[0, h].astype(BF16), preferred_element_type=F32) for h in range(H_B)]
    for h in range(H_B):
        update(h, scores[h] + cq[:, h:h + 1] + after[h:h + 1, :], vt_ref[0, h].astype(BF16), True)

    @pl.when(kc == pl.num_programs(1) - 1)
    def _():
        row = lax.broadcasted_iota(jnp.int32, (nq, nq), 0)
        col = lax.broadcasted_iota(jnp.int32, (nq, nq), 1)
        cn = cn_scr[...]
        outs = []
        for h in range(H_B):
            sl = slice(h * HEAD_DIM, (h + 1) * HEAD_DIM)
            s = _qk(q_head(h), kn_ref[0][:, sl].astype(BF16)) + cq[:, h:h + 1] - cn[h:h + 1, 0:nq]
            update(h, jnp.where(col <= row, s, NEG), vn_ref[0][:, sl].astype(BF16), False)
            _, l, acc = _dec_load(m_scr, l_scr, acc_scr, h)
            outs.append(acc / l)
        for p in range(H_B // 2):
            o = jnp.concatenate([outs[2 * p], outs[2 * p + 1]], axis=-1)
            o_ref[0, :, p * LANES:(p + 1) * LANES] = _group_rms(o, gout_ref[p], gn_ref[...]).astype(BF16)


def _fox_decode(qb, kb_new, vb_new, logf_new, cache_kt, cache_vt, cache_logf_t, g_out_b):
    b, nq, w = qb.shape
    past = cache_kt.shape[-1]
    ck = min(CACHE_CHUNK, past)
    n_kc = past // ck
    assert nq <= LANES
    pad = jnp.zeros((b, nq, LANES - H_B), F32)
    lfn_col = jnp.concatenate([logf_new, pad], axis=-1)
    lfn_row = jnp.concatenate([jnp.transpose(logf_new, (0, 2, 1)), jnp.zeros((b, H_B, LANES - nq), F32)], axis=-1)
    new_spec = pl.BlockSpec((1, nq, w), lambda bi, c: (bi, 0, 0))
    newest_first = lambda bi, c: (bi, 0, 0, n_kc - 1 - c)
    return pl.pallas_call(
        functools.partial(_fox_dec_body, nq=nq),
        grid=(b, n_kc),
        in_specs=[
            new_spec, new_spec, new_spec,
            pl.BlockSpec((1, nq, LANES), lambda bi, c: (bi, 0, 0)),
            pl.BlockSpec((1, H_B, LANES), lambda bi, c: (bi, 0, 0)),
            pl.BlockSpec((1, H_B, HEAD_DIM, ck), newest_first),
            pl.BlockSpec((1, H_B, HEAD_DIM, ck), newest_first),
            pl.BlockSpec((1, H_B, past), lambda bi, c: (bi, 0, 0)),
            _const_spec((H_B // 2, 1, LANES)), _const_spec((LANES, LANES)),
        ],
        out_specs=new_spec,
        out_shape=jax.ShapeDtypeStruct((b, nq, w), BF16),
        scratch_shapes=[pltpu.VMEM((nq, LANES), F32), pltpu.VMEM((H_B, LANES), F32),
                        pltpu.VMEM((n_kc, H_B, ck), F32),
                        pltpu.VMEM((H_B, nq, 1), F32), pltpu.VMEM((H_B, nq, 1), F32),
                        pltpu.VMEM((H_B, nq, HEAD_DIM), F32)],
        compiler_params=_cparams(("arbitrary", "arbitrary")),
        name="fox_decode",
    )(qb, kb_new, vb_new, lfn_col, lfn_row, cache_kt, cache_vt, cache_logf_t,
      g_out_b.reshape(H_B // 2, 1, LANES), _blockdiag_mean(LANES, HEAD_DIM))


def _route(logits):
    lane_i = lax.broadcasted_iota(jnp.int32, logits.shape, 1)
    lane = lane_i.astype(F32)
    big = float(LANES)
    lg = jnp.where(lane_i < N_GROUPS, logits, NEG)
    mx = jnp.max(lg, axis=-1, keepdims=True)
    grp = jnp.min(jnp.where(lg == mx, lane, big), axis=-1, keepdims=True)
    p_grp = 1.0 / jnp.sum(jnp.exp(lg - mx), axis=-1, keepdims=True)
    e = lane_i - ROUTER_LANE0
    e_grp = lax.shift_right_arithmetic(e, 3).astype(F32)
    sel = (e >= 0) & (e < N_EXPERTS) & (e_grp == grp)
    v = jnp.where(sel, logits, NEG)
    v1 = jnp.max(v, axis=-1, keepdims=True)
    i1 = jnp.min(jnp.where(sel & (v == v1), lane, big), axis=-1, keepdims=True)
    sel2 = sel & (lane != i1)
    vv = jnp.where(sel2, logits, NEG)
    v2 = jnp.max(vv, axis=-1, keepdims=True)
    i2 = jnp.min(jnp.where(sel2 & (vv == v2), lane, big), axis=-1, keepdims=True)
    e2 = jnp.exp(v2 - v1)
    w1 = p_grp / (1.0 + e2)
    w2 = p_grp * e2 / (1.0 + e2)
    gates = jnp.where(lane == i1, w1, 0.0) + jnp.where(lane == i2, w2, 0.0)
    return gates, (i1, i2, w1, w2)


R_E1, R_E2, R_RANK1, R_RANK2, R_W1, R_W2 = range(6)


def _mix_body(oa_ref, ob_ref, x_ref, wa_ref, wb_ref, g2_ref, wr1_ref, wr2_ref, br_ref, tri_ref,
              x1_ref, xn_ref, gates_ref, route_ref, counts_ref):
    y = (jnp.dot(oa_ref[...], wa_ref[...], preferred_element_type=F32)
         + jnp.dot(ob_ref[...], wb_ref[...], preferred_element_type=F32))
    x1 = x_ref[...] + y
    x1_ref[...] = x1
    ms = jnp.mean(x1 * x1, axis=-1, keepdims=True)
    xn = (x1 * lax.rsqrt(ms + RMS_EPS)) * g2_ref[...]
    xn_ref[...] = xn
    h1 = xn.astype(BF16)
    h2 = (xn - h1.astype(F32)).astype(BF16)
    logits = (jnp.dot(h1, wr1_ref[...], preferred_element_type=F32)
              + jnp.dot(h1, wr2_ref[...], preferred_element_type=F32)
              + jnp.dot(h2, wr1_ref[...], preferred_element_type=F32)) + br_ref[...]
    gates, (i1, i2, w1, w2) = _route(logits)
    gates_ref[...] = gates

    @pl.when(pl.program_id(0) == 0)
    def _():
        counts_ref[...] = jnp.zeros_like(counts_ref)

    lane_i = lax.broadcasted_iota(jnp.int32, gates.shape, 1)
    lane = lane_i.astype(F32)
    oh1 = jnp.where(lane == i1, 1.0, 0.0)
    oh2 = jnp.where(lane == i2, 1.0, 0.0)
    comb = oh1 + oh2
    running = counts_ref[0:1, :]
    before = jnp.dot(tri_ref[...], comb.astype(BF16), preferred_element_type=F32) + running
    rank1 = jnp.sum(before * oh1, axis=-1, keepdims=True)
    rank2 = jnp.sum(before * oh2, axis=-1, keepdims=True)
    counts_ref[0:1, :] = running + jnp.sum(comb, axis=0, keepdims=True)
    rec = jnp.zeros_like(gates)
    for idx, val in ((R_E1, i1 - ROUTER_LANE0), (R_E2, i2 - ROUTER_LANE0), (R_RANK1, rank1), (R_RANK2, rank2),
                     (R_W1, w1), (R_W2, w2)):
        rec = jnp.where(lane_i == idx, val, rec)
    route_ref[...] = rec


def _mix_and_route(o_a, o_b, x, w_out, g_norm, w_rg, b_rg, w_re, b_re):
    n, d = x.shape
    bt = min(TOKEN_TILE, n)
    r = jnp.arange(bt)
    tri_strict = (r[None, :] < r[:, None]).astype(BF16)
    wa = w_out[:GROUP_W].astype(BF16)
    wb = w_out[GROUP_W:].astype(BF16)
    n_r = N_GROUPS + N_EXPERTS
    wr = jnp.concatenate([w_rg, w_re, jnp.zeros((d, LANES - n_r), F32)], axis=1)
    wr1 = wr.astype(BF16)
    wr2 = (wr - wr1.astype(F32)).astype(BF16)
    br = jnp.concatenate([b_rg, b_re, jnp.zeros((LANES - n_r,), F32)])[None, :]
    row = lambda width: pl.BlockSpec((bt, width), lambda i: (i, 0))
    return pl.pallas_call(
        _mix_body,
        grid=(n // bt,),
        in_specs=[row(GROUP_W), row(GROUP_W), row(d), _const_spec((GROUP_W, d)), _const_spec((GROUP_W, d)),
                  _const_spec((1, d)), _const_spec((d, LANES)), _const_spec((d, LANES)), _const_spec((1, LANES)),
                  _const_spec((bt, bt))],
        out_specs=[row(d), row(d), row(LANES), row(LANES), _const_spec((8, LANES))],
        out_shape=[jax.ShapeDtypeStruct((n, d), F32), jax.ShapeDtypeStruct((n, d), F32),
                   jax.ShapeDtypeStruct((n, LANES), F32), jax.ShapeDtypeStruct((n, LANES), F32),
                   jax.ShapeDtypeStruct((8, LANES), F32)],
        compiler_params=_cparams(("arbitrary",)),
        name="mix_route",
    )(o_a, o_b, x, wa, wb, g_norm[None, :], wr1, wr2, br, tri_strict)


def _swiglu(xn, wg, wu, wd, gate=None):
    x = xn.astype(BF16)
    g = jnp.dot(x, wg, preferred_element_type=F32)
    u = jnp.dot(x, wu, preferred_element_type=F32)
    h = (g * jax.nn.sigmoid(g)) * u
    if gate is not None:
        h = h * gate
    return jnp.dot(h.astype(BF16), wd, preferred_element_type=F32)


def _expert_body(xn_ref, x1_ref, gates_ref, wg_ref, wu_ref, wd_ref, o_ref):
    e = pl.program_id(1)

    @pl.when(e == 0)
    def _():
        o_ref[...] = x1_ref[...]

    gates = gates_ref[...]
    lane = lax.broadcasted_iota(jnp.int32, gates.shape, 1)
    gate = jnp.sum(jnp.where(lane == e + ROUTER_LANE0, gates, 0.0), axis=-1, keepdims=True)
    o_ref[...] += _swiglu(xn_ref[...], wg_ref[0], wu_ref[0], wd_ref[0], gate)


def _experts(xn, x1, gates, w_gate, w_up, w_down):
    n, d = x1.shape
    ff = w_gate.shape[-1]
    bt = min(MOE_TILE, n)
    row = lambda width: pl.BlockSpec((bt, width), lambda i, e: (i, 0))
    return pl.pallas_call(
        _expert_body,
        grid=(n // bt, N_EXPERTS),
        in_specs=[row(d), row(d), row(LANES),
                  pl.BlockSpec((1, d, ff), lambda i, e: (e, 0, 0)),
                  pl.BlockSpec((1, d, ff), lambda i, e: (e, 0, 0)),
                  pl.BlockSpec((1, ff, d), lambda i, e: (e, 0, 0))],
        out_specs=row(d),
        out_shape=jax.ShapeDtypeStruct((n, d), F32),
        compiler_params=_cparams(("arbitrary", "arbitrary")),
        name="experts",
    )(xn, x1, gates, w_gate, w_up, w_down)


def _row_copies(n_rows, make_copy):
    def issue(r, carry):
        for s in range(2):
            make_copy(r, s).start()
        return carry

    lax.fori_loop(0, n_rows, issue, 0, unroll=8)

    def drain(r, carry):
        for s in range(2):
            make_copy(r, s).wait()
        return carry

    lax.fori_loop(0, n_rows, drain, 0, unroll=8)


def _row_position_body(route_ref, base_ref, pos_ref):
    rec = route_ref[...]
    lane_i = lax.broadcasted_iota(jnp.int32, rec.shape, 1)
    lane = lane_i.astype(F32)
    out = jnp.zeros(rec.shape, F32)
    for slot, (e_lane, r_lane) in enumerate(((R_E1, R_RANK1), (R_E2, R_RANK2))):
        onehot = lane == rec[:, e_lane:e_lane + 1] + float(ROUTER_LANE0)
        base = jnp.sum(jnp.where(onehot, base_ref[...], 0.0), axis=-1, keepdims=True)
        out = jnp.where(lane_i == slot, base + rec[:, r_lane:r_lane + 1], out)
    pos_ref[...] = out.astype(jnp.int32)


def _dispatch_body(pos_ref, x_ref, init_ref, xs_ref, sem):
    del init_ref
    _row_copies(x_ref.shape[0], lambda r, s: pltpu.make_async_copy(
        x_ref.at[pl.ds(r, 1)], xs_ref.at[pl.ds(pos_ref[0, 0, 2 * r + s], 1)], sem))


def _grouped_body(te_ref, nu_ref, xs_ref, wg_ref, wu_ref, wd_ref, ys_ref):
    del te_ref
    used = pl.program_id(0) < nu_ref[0]

    @pl.when(used)
    def _():
        ys_ref[...] = _swiglu(xs_ref[...], wg_ref[0], wu_ref[0], wd_ref[0])

    @pl.when(jnp.logical_not(used))
    def _():
        ys_ref[...] = jnp.zeros_like(ys_ref)


def _combine_body(pos_ref, route_ref, x1_ref, ys_ref, o_ref, buf_scr, sem):
    _row_copies(x1_ref.shape[0], lambda r, s: pltpu.make_async_copy(
        ys_ref.at[pl.ds(pos_ref[0, 0, 2 * r + s], 1)], buf_scr.at[s, pl.ds(r, 1)], sem))
    rec = route_ref[...]
    o_ref[...] = (x1_ref[...] + rec[:, R_W1:R_W1 + 1] * buf_scr[0] + rec[:, R_W2:R_W2 + 1] * buf_scr[1])


def _routed_experts(xn, x1, route, counts, w_gate, w_up, w_down):
    n, d = x1.shape
    ff = w_gate.shape[-1]
    bt = min(TOKEN_TILE, n)
    nt = n // bt
    tm = MOE_ROW_TILE
    n_tiles = (2 * n) // tm + N_EXPERTS
    cnt = counts[0, ROUTER_LANE0:ROUTER_LANE0 + N_EXPERTS].astype(jnp.int32)
    tiles = (cnt + tm - 1) // tm
    tile_end = jnp.cumsum(tiles)
    base_row = (tile_end - tiles) * tm
    n_used = tile_end[-1:]
    tile_expert = jnp.minimum(jnp.sum(jnp.arange(n_tiles)[:, None] >= tile_end[None, :], axis=1), N_EXPERTS - 1)
    base_lanes = jnp.zeros((1, LANES), F32).at[0, ROUTER_LANE0:ROUTER_LANE0 + N_EXPERTS].set(base_row.astype(F32))
    row = lambda width: pl.BlockSpec((bt, width), lambda i: (i, 0))
    pos = pl.pallas_call(
        _row_position_body,
        grid=(nt,),
        in_specs=[row(LANES), _const_spec((1, LANES))],
        out_specs=row(LANES),
        out_shape=jax.ShapeDtypeStruct((n, LANES), jnp.int32),
        compiler_params=_cparams(("arbitrary",)),
        name="moe_positions",
    )(route, base_lanes)
    pos = pos[:, :2].reshape(nt, 1, 2 * bt)

    pos_spec = pl.BlockSpec((1, 1, 2 * bt), lambda i: (i, 0, 0), memory_space=pltpu.SMEM)
    any_spec = pl.BlockSpec(memory_space=pl.ANY)
    xs = pl.pallas_call(
        _dispatch_body,
        grid=(nt,),
        in_specs=[pos_spec, row(d), any_spec],
        out_specs=any_spec,
        out_shape=jax.ShapeDtypeStruct((n_tiles * tm, d), F32),
        scratch_shapes=[pltpu.SemaphoreType.DMA(())],
        input_output_aliases={2: 0},
        compiler_params=_cparams(("arbitrary",)),
        name="moe_dispatch",
    )(pos, xn, jnp.zeros((n_tiles * tm, d), F32))

    w_spec = lambda shape: pl.BlockSpec(shape, lambda t, te, nu: (te[t], 0, 0))
    ys = pl.pallas_call(
        _grouped_body,
        grid_spec=pltpu.PrefetchScalarGridSpec(
            num_scalar_prefetch=2, grid=(n_tiles,),
            in_specs=[pl.BlockSpec((tm, d), lambda t, te, nu: (jnp.minimum(t, nu[0] - 1), 0)),
                      w_spec((1, d, ff)), w_spec((1, d, ff)), w_spec((1, ff, d))],
            out_specs=pl.BlockSpec((tm, d), lambda t, te, nu: (t, 0))),
        out_shape=jax.ShapeDtypeStruct((n_tiles * tm, d), F32),
        compiler_params=_cparams(("arbitrary",)),
        name="moe_experts",
    )(tile_expert.astype(jnp.int32), n_used.astype(jnp.int32), xs, w_gate, w_up, w_down)

    return pl.pallas_call(
        _combine_body,
        grid=(nt,),
        in_specs=[pos_spec, row(LANES), row(d), any_spec],
        out_specs=row(d),
        out_shape=jax.ShapeDtypeStruct((n, d), F32),
        scratch_shapes=[pltpu.VMEM((2, bt, d), F32), pltpu.SemaphoreType.DMA(())],
        compiler_params=_cparams(("arbitrary",)),
        name="moe_combine",
    )(pos, route, x1, ys)


def _ffn(o_a, o_b, x, w_out, g_norm_ffn, w_rg, b_rg, w_re, b_re, wg, wu, wd):
    b, t, d = x.shape
    n = b * t
    x1, xn, gates, route, counts = _mix_and_route(o_a.reshape(n, -1), o_b.reshape(n, -1), x.reshape(n, d), w_out,
                                                  g_norm_ffn, w_rg, b_rg, w_re, b_re)
    if 2 * n >= ROUTED_MIN_ASSIGNMENTS:
        y = _routed_experts(xn, x1, route, counts, wg, wu, wd)
    else:
        y = _experts(xn, x1, gates, wg, wu, wd)
    return y.reshape(b, t, d)


def kernel(x_prompt, x_sample, cache_a_k, cache_a_v, cache_b_k, cache_b_v, cache_b_logf, g_norm_mix, w_in, b_forget, g_qk_a, g_qk_b, lambda_qk, g_out_a, g_out_b, w_out, rel_bias, g_norm_ffn, w_router_group, b_router_group, w_router_expert, b_router_expert, w_exp_gate, w_exp_up, w_exp_down):
    depth = w_in.shape[0]
    assert depth == 1, "single-layer step only"
    bp, tp, d = x_prompt.shape
    bs, ts, _ = x_sample.shape
    past = cache_a_k.shape[2]
    w_in0, w_out0 = w_in[0], w_out[0]
    wg, wu, wd = w_exp_gate[0].astype(BF16), w_exp_up[0].astype(BF16), w_exp_down[0].astype(BF16)
    ffn_w = (w_out0, g_norm_ffn[0], w_router_group[0], b_router_group[0], w_router_expert[0], b_router_expert[0],
             wg, wu, wd)

    (ka_p, va_p, kb_p, vb_p, logf_p, qa_t, ka_bf, va_t, qb_t, kb_aug, vb_t, c_edge) = _projection(
        x_prompt, g_norm_mix[0], w_in0, b_forget[0], g_qk_a[0], g_qk_b[0], with_aug=True)
    blk = min(ATTN_BLOCK, tp)
    bk = va_t.shape[-1]
    assert blk % CHUNK == 0 and blk >= MAX_DISTANCE and blk % bk == 0
    q_pos = blk + jnp.arange(blk, dtype=jnp.int32)
    bkt_p = jnp.stack([_bucket_map(q_pos, s * bk + jnp.arange(bk, dtype=jnp.int32)).T
                       for s in range(2 * blk // bk)])
    bias_p = _bias_tiles(rel_bias, bkt_p)
    o_a = _diff_attention(qa_t, ka_bf, va_t, bias_p, lambda_qk[0], g_out_a[0])
    o_b = _fox_attention(qb_t, kb_aug, vb_t, c_edge, g_qk_b[0], g_out_b[0])
    y_p = _ffn(o_a, o_b, x_prompt, *ffn_w)

    xs = x_sample.reshape(1, bs * ts, d)
    (ka_s, va_s, kb_s, vb_s, logf_s, qa_s, qb_s) = _projection(
        xs, g_norm_mix[0], w_in0, b_forget[0], g_qk_a[0], g_qk_b[0], with_aug=False)
    per_stream = lambda a: a.reshape(bs, ts, a.shape[-1])
    ka_s, va_s, kb_s, vb_s, logf_s, qa_s, qb_s = map(per_stream, (ka_s, va_s, kb_s, vb_s, logf_s, qa_s, qb_s))
    ck = min(CACHE_CHUNK, past)
    q_pos = past + jnp.arange(ts, dtype=jnp.int32)
    bkt_c = _bucket_map(q_pos, jnp.arange(past, dtype=jnp.int32)).reshape(ts, past // ck, ck).transpose(1, 0, 2)
    bias_c = _bias_tiles(rel_bias, bkt_c)
    bias_n = _bias_tiles(rel_bias, _bucket_map(q_pos, q_pos)[None])
    o_a_s = _diff_decode(qa_s, ka_s, va_s, jnp.transpose(cache_a_k[0], (0, 2, 3, 4, 1)),
                         cache_a_v[0].reshape(bs, past * H_A, 2 * HEAD_DIM), bias_c, bias_n, lambda_qk[0], g_out_a[0])
    o_b_s = _fox_decode(qb_s, kb_s, vb_s, logf_s, jnp.transpose(cache_b_k[0], (0, 2, 3, 1)),
                        jnp.transpose(cache_b_v[0], (0, 2, 3, 1)), jnp.transpose(cache_b_logf[0], (0, 2, 1)),
                        g_out_b[0])
    y_s = _ffn(o_a_s, o_b_s, x_sample, *ffn_w)

    def rows(ka, va, kb, vb, logf, b, t):
        return (ka.reshape(1, b, t, H_A, 2, HEAD_DIM), va.reshape(1, b, t, H_A, 2 * HEAD_DIM),
                kb.reshape(1, b, t, H_B, HEAD_DIM), vb.reshape(1, b, t, H_B, HEAD_DIM), logf.reshape(1, b, t, H_B))

    return (y_p, y_s) + rows(ka_p, va_p, kb_p, vb_p, logf_p, bp, tp) + rows(ka_s, va_s, kb_s, vb_s, logf_s, bs, ts)
```

```python
import functools
import math

import jax
import jax.numpy as jnp
from jax import lax
from jax.experimental import pallas as pl
from jax.experimental.pallas import tpu as pltpu

F32 = jnp.float32
BF16 = jnp.bfloat16

LANES = 128
VMEM_LIMIT_BYTES = 56 * 1024 * 1024

HEAD_DIM = 64
H_A = 4
H_B = 8
GROUP_W = 512
MAIN_W = 6 * GROUP_W
CHUNK = 64
N_BUCKETS = 32
MAX_DISTANCE = 128
N_GROUPS = 4
EXPERTS_PER_GROUP = 8
N_EXPERTS = N_GROUPS * EXPERTS_PER_GROUP
ROUTER_LANE0 = N_GROUPS
RMS_EPS = 1e-6
NEG = -1e30
LOG2E = 1.4426950408889634
QK_SCALE = HEAD_DIM ** -0.5
LAM_INIT = 0.8 - 0.6 * math.exp(-0.3 * 0)
N_CPARTS = 3

ONES_ROWS = 16
VA_ROWS = 2 * HEAD_DIM + ONES_ROWS
VB_ROWS = HEAD_DIM + ONES_ROWS

TOKEN_TILE = 512
ATTN_BLOCK = 512
KV_BLOCK = 256
FAR_UNROLL = 4
CACHE_CHUNK = 1024
MOE_TILE = 1024
MOE_ROW_TILE = 256
ROUTED_MIN_ASSIGNMENTS = 4 * N_EXPERTS * MOE_ROW_TILE


def _cparams(sem):
    return pltpu.CompilerParams(dimension_semantics=sem, vmem_limit_bytes=VMEM_LIMIT_BYTES)


def _const_spec(shape):
    nd = len(shape)
    return pl.BlockSpec(shape, lambda *_: (0,) * nd)


def _split3(x):
    p1 = x.astype(BF16).astype(F32)
    r1 = x - p1
    p2 = r1.astype(BF16).astype(F32)
    p3 = (r1 - p2).astype(BF16).astype(F32)
    return p1, p2, p3


def _lane_groups(parts, lane):
    return jnp.where(lane < 8, parts[0], jnp.where(lane < 16, parts[1], parts[2]))


def _cumsum_rows(tri, x, carry):
    c = carry
    for part in _split3(x):
        c = c + jnp.dot(tri, part.astype(BF16), preferred_element_type=F32)
    return c


def _log_sigmoid(x):
    return jnp.minimum(x, 0.0) - jnp.log(1.0 + jnp.exp(-jnp.abs(x)))


def _group_rms(raw, gain_row, gn):
    ms = jnp.dot((raw * raw).astype(BF16), gn, preferred_element_type=F32)
    return raw * lax.rsqrt(ms + RMS_EPS) * gain_row


def _online_update(state, s, v):
    m, l, acc = state
    m_new = jnp.maximum(m, jnp.max(s, axis=-1, keepdims=True))
    p = jnp.exp2(s - m_new)
    alpha = jnp.exp2(m - m_new)
    l_new = alpha * l + jnp.sum(p, axis=-1, keepdims=True)
    acc_new = alpha * acc + jnp.dot(p.astype(BF16), v, preferred_element_type=F32)
    return m_new, l_new, acc_new


def _qk(q, k):
    return lax.dot_general(q, k, (((1,), (1,)), ((), ())), preferred_element_type=F32)


def _proj_body(*refs, with_aug, bt, bk):
    (x_ref, g1_ref, wm_ref, wf_ref, bf_ref, gains_ref, gn_ref, tri_ref, place_ref) = refs[:9]
    ka_ref, va_ref, kb_ref, vb_ref, logf_ref = refs[9:14]
    x = x_ref[0]
    ms = jnp.mean(x * x, axis=-1, keepdims=True)
    xn = (x * lax.rsqrt(ms + RMS_EPS)) * g1_ref[...]
    xb = xn.astype(BF16)
    proj = jnp.dot(xb, wm_ref[...], preferred_element_type=F32)
    gains = gains_ref[...]
    gn = gn_ref[...]
    w = GROUP_W
    qa = _group_rms(proj[:, 0:w], gains[0:1], gn)
    ka = _group_rms(proj[:, w:2 * w], gains[1:2], gn)
    va = proj[:, 2 * w:3 * w]
    qb = _group_rms(proj[:, 3 * w:4 * w], gains[2:3], gn)
    kb = _group_rms(proj[:, 4 * w:5 * w], gains[3:4], gn)
    vb = proj[:, 5 * w:6 * w]
    ka_ref[0] = ka
    va_ref[0] = va
    kb_ref[0] = kb
    vb_ref[0] = vb
    fl = jnp.dot(xb, wf_ref[...], preferred_element_type=F32) + bf_ref[...]
    logf = _log_sigmoid(fl)
    logf_ref[0] = logf[:, 0:H_B]
    qscale = QK_SCALE * LOG2E
    if not with_aug:
        qa_ref, qb_ref = refs[14:16]
        qa_ref[0] = (qa * qscale).astype(BF16)
        qb_ref[0] = (qb * qscale).astype(BF16)
        return
    qa_t, ka_bf, va_t, qb_t, kb_aug, vb_t, cedge_ref, carry_ref = refs[14:22]
    n_chunk = bt // bk
    row = lax.broadcasted_iota(jnp.int32, (LANES, bt), 0)
    ones_tail = jnp.where(lax.broadcasted_iota(jnp.int32, (ONES_ROWS, bt), 0) == 0, 1.0, 0.0)

    def put_chunks(ref, idx, vt):
        vt = vt.astype(BF16)
        for c in range(n_chunk):
            ref[0, idx, c] = vt[:, c * bk:(c + 1) * bk]

    for h in range(H_A):
        sl = slice(h * LANES, (h + 1) * LANES)
        q_t = (qa[:, sl] * qscale).T
        qa_t[0, 2 * h] = jnp.where(row < HEAD_DIM, q_t, 0.0).astype(BF16)
        qa_t[0, 2 * h + 1] = jnp.where(row >= HEAD_DIM, q_t, 0.0).astype(BF16)
        ka_bf[0, h] = ka[:, sl].astype(BF16)
        put_chunks(va_t, h, jnp.concatenate([va[:, sl].T, ones_tail], axis=0))
        vb_pair_t = vb[:, sl].T
        for hh in range(2):
            put_chunks(vb_t, 2 * h + hh,
                       jnp.concatenate([vb_pair_t[hh * HEAD_DIM:(hh + 1) * HEAD_DIM], ones_tail], axis=0))

    @pl.when(pl.program_id(1) == 0)
    def _():
        carry_ref[...] = jnp.zeros_like(carry_ref)

    c = _cumsum_rows(tri_ref[...], logf * LOG2E, carry_ref[0:1, :])
    carry_ref[0:1, :] = c[bt - 1:bt, :]
    edge_row = lax.broadcasted_iota(jnp.int32, (8, LANES), 0)
    cedge_ref[0, 0] = jnp.where(edge_row == 0, c[0:1, :], jnp.where(edge_row == 1, c[bt - 1:bt, :], 0.0))
    lane = lax.broadcasted_iota(jnp.int32, (bt, LANES), 1)
    cparts = _lane_groups(_split3(c), lane).astype(BF16)
    extras = jnp.dot(cparts, place_ref[...], preferred_element_type=F32)
    ones_q = jnp.where((lane >= HEAD_DIM + N_CPARTS) & (lane < HEAD_DIM + 2 * N_CPARTS), 1.0, 0.0)
    ones_k = jnp.where((lane >= HEAD_DIM) & (lane < HEAD_DIM + N_CPARTS), 1.0, 0.0)
    for h in range(H_B):
        sl = slice((h // 2) * LANES, (h // 2 + 1) * LANES)
        qp = qb[:, sl] * qscale
        kp = kb[:, sl]
        if h % 2:
            qp = pltpu.roll(qp, HEAD_DIM, 1)
            kp = pltpu.roll(kp, HEAD_DIM, 1)
        eq = extras[:, h * LANES:(h + 1) * LANES] + ones_q
        ek = extras[:, (H_B + h) * LANES:(H_B + h + 1) * LANES] + ones_k
        qb_t[0, h] = jnp.where(lane < HEAD_DIM, qp, eq).T.astype(BF16)
        kb_aug[0, h] = jnp.where(lane < HEAD_DIM, kp, ek).astype(BF16)


def _blockdiag_mean(n, group):
    r = jnp.arange(n)
    return jnp.where((r[:, None] // group) == (r[None, :] // group), 1.0 / group, 0.0).astype(BF16)


def _tri(n):
    r = jnp.arange(n)
    return (r[None, :] <= r[:, None]).astype(BF16)


def _place_matrix():
    rows = jnp.arange(LANES)[:, None]
    cols = jnp.arange(2 * H_B * LANES)[None, :]
    p, h = rows // 8, rows % 8
    valid = rows < 8 * N_CPARTS
    qcol = h * LANES + HEAD_DIM + p
    kcol = (H_B + h) * LANES + HEAD_DIM + N_CPARTS + p
    m = jnp.where(valid & (cols == qcol), 1.0, 0.0) - jnp.where(valid & (cols == kcol), 1.0, 0.0)
    return m.astype(BF16)


def _projection(x, g_norm, w_in, b_forget, g_qk_a, g_qk_b, *, with_aug):
    b, t, d = x.shape
    bt = min(TOKEN_TILE, t)
    nt = t // bt
    wm = w_in[:, :MAIN_W].astype(BF16)
    wf_cols = w_in[:, MAIN_W:MAIN_W + H_B]
    wf = jnp.concatenate([wf_cols] * N_CPARTS + [jnp.zeros((d, LANES - H_B * N_CPARTS), F32)], axis=1).astype(BF16)
    bfv = jnp.concatenate([b_forget] * N_CPARTS + [jnp.zeros((LANES - H_B * N_CPARTS,), F32)])[None, :]
    gains = jnp.stack([jnp.tile(g_qk_a[0], 2 * H_A), jnp.tile(g_qk_a[1], 2 * H_A),
                       jnp.tile(g_qk_b[0], H_B), jnp.tile(g_qk_b[1], H_B)])
    gn = _blockdiag_mean(GROUP_W, HEAD_DIM)
    tri = _tri(bt)
    place = _place_matrix()
    in_specs = [
        pl.BlockSpec((1, bt, d), lambda i, j: (i, j, 0)),
        _const_spec((1, d)),
        pl.BlockSpec((d, MAIN_W), lambda i, j: (0, 0), pipeline_mode=pl.Buffered(1)),
        _const_spec((d, LANES)), _const_spec((1, LANES)), _const_spec((4, GROUP_W)),
        _const_spec((GROUP_W, GROUP_W)), _const_spec((bt, bt)), _const_spec((LANES, 2 * H_B * LANES)),
    ]
    row_spec = pl.BlockSpec((1, bt, GROUP_W), lambda i, j: (i, j, 0))
    out_shape = [jax.ShapeDtypeStruct((b, t, GROUP_W), F32)] * 4 + [jax.ShapeDtypeStruct((b, t, H_B), F32)]
    out_specs = [row_spec] * 4 + [pl.BlockSpec((1, bt, H_B), lambda i, j: (i, j, 0))]
    scratch = []
    bk = min(KV_BLOCK, bt)
    if with_aug:
        def add(shape, block, index_map):
            out_shape.append(jax.ShapeDtypeStruct(shape, BF16))
            out_specs.append(pl.BlockSpec(block, index_map))

        rows_major = lambda i, j: (i, 0, j, 0)
        time_minor = lambda i, j: (i, 0, 0, j)
        chunked = lambda i, j: (i, 0, j, 0, 0)
        add((b, 2 * H_A, LANES, t), (1, 2 * H_A, LANES, bt), time_minor)
        add((b, H_A, t, LANES), (1, H_A, bt, LANES), rows_major)
        add((b, H_A, t // bk, VA_ROWS, bk), (1, H_A, bt // bk, VA_ROWS, bk), chunked)
        add((b, H_B, LANES, t), (1, H_B, LANES, bt), time_minor)
        add((b, H_B, t, LANES), (1, H_B, bt, LANES), rows_major)
        add((b, H_B, t // bk, VB_ROWS, bk), (1, H_B, bt // bk, VB_ROWS, bk), chunked)
        out_shape.append(jax.ShapeDtypeStruct((b, nt, 8, LANES), F32))
        out_specs.append(pl.BlockSpec((1, 1, 8, LANES), lambda i, j: (i, j, 0, 0)))
        scratch = [pltpu.VMEM((8, LANES), F32)]
    else:
        out_shape += [jax.ShapeDtypeStruct((b, t, GROUP_W), BF16)] * 2
        out_specs += [row_spec] * 2
    return pl.pallas_call(
        functools.partial(_proj_body, with_aug=with_aug, bt=bt, bk=bk),
        grid=(b, nt), in_specs=in_specs, out_specs=out_specs, out_shape=out_shape, scratch_shapes=scratch,
        compiler_params=_cparams(("arbitrary", "arbitrary")),
        name="proj_aug" if with_aug else "proj_plain",
    )(x, g_norm[None, :], wm, wf, bfv, gains, gn, tri, place)


def _t5_bucket(rel):
    nb = N_BUCKETS // 2
    max_exact = nb // 2
    base = jnp.where(rel > 0, nb, 0)
    n = jnp.abs(rel)
    large = max_exact + (jnp.log(jnp.maximum(n, max_exact).astype(jnp.float32) / max_exact)
                         / math.log(MAX_DISTANCE / max_exact) * (nb - max_exact)).astype(jnp.int32)
    large = jnp.minimum(large, nb - 1)
    return base + jnp.where(n < max_exact, n, large)


def _bucket_map(q_pos, k_pos):
    bkt = _t5_bucket(k_pos[None, :] - q_pos[:, None])
    visible = (k_pos[None, :] // CHUNK) <= (q_pos[:, None] // CHUNK)
    return jnp.where(visible, bkt, -1).astype(jnp.int32)


def _bias_body(rb_ref, bkt_ref, o_ref):
    h = pl.program_id(0)
    bkt = bkt_ref[0]
    far = rb_ref[N_BUCKETS // 2 - 1, h]
    acc = jnp.zeros(bkt.shape, F32)
    for b in range(N_BUCKETS):
        acc = jnp.where(bkt == b, rb_ref[b, h] - far, acc)
    o_ref[0, 0] = jnp.where(bkt < 0, NEG, acc * LOG2E)


def _bias_tiles(rel_bias, bkt):
    n, r, c = bkt.shape
    return pl.pallas_call(
        _bias_body,
        grid=(H_A, n),
        in_specs=[pl.BlockSpec(memory_space=pltpu.SMEM), pl.BlockSpec((1, r, c), lambda h, i: (i, 0, 0))],
        out_specs=pl.BlockSpec((1, 1, r, c), lambda h, i: (h, i, 0, 0)),
        out_shape=jax.ShapeDtypeStruct((H_A, n, r, c), F32),
        compiler_params=_cparams(("arbitrary", "arbitrary")),
        name="bias_tiles",
    )(rel_bias, bkt)


def _lam(lq):
    a = jnp.sum(lq[0:1, :] * lq[1:2, :], axis=-1, keepdims=True)
    b = jnp.sum(lq[2:3, :] * lq[3:4, :], axis=-1, keepdims=True)
    return jnp.exp(a) - jnp.exp(b) + LAM_INIT


def _attn_init_t(rows, bq):
    return (jnp.full((1, bq), NEG, F32), jnp.zeros((rows, bq), F32))


def _online_update_t(state, st, vt):
    m, acc = state
    m_new = jnp.maximum(m, jnp.max(st, axis=0, keepdims=True))
    p = jnp.exp2(st - m_new)
    alpha = jnp.exp2(m - m_new)
    return m_new, alpha * acc + jnp.dot(vt, p.astype(BF16), preferred_element_type=F32)


def _normalized_t(state, rows):
    acc = state[1]
    return acc[0:rows] / acc[rows:rows + 1]


def _pipelined_sweep(i, n_sub, n_near, slots, score_fn, value_fn, modify, rows, blk, first=0):
    assert n_sub % 2 == 0 and n_near in (1, 2)

    def run_block(jb, states, near, last, next_start=None):
        states = list(states)
        for s in range(n_sub):
            j = jb * n_sub + s
            cur, nxt = slots[s % 2], slots[(s + 1) % 2]
            ahead = next_start if (next_start is not None and s == n_sub - 1) else j + 1
            for c in range(2):
                if not (last and s == n_sub - 1):
                    nxt[c] = score_fn(c, ahead)
                st = cur[c]
                if near is not None:
                    st = modify(st, near, s)
                states[c] = _online_update_t(states[c], st, value_fn(c, j))
        return tuple(states)

    for c in range(2):
        slots[0][c] = score_fn(c, first * n_sub)
    states = (_attn_init_t(rows, blk), _attn_init_t(rows, blk))
    n_far = jnp.maximum(i + 1 - n_near, 0)

    def run_far(jb, states, n_blocks):
        for d in range(n_blocks):
            states = run_block(jb + d, states, None, False)
        return states

    n_groups = jnp.maximum(n_far - first, 0) // FAR_UNROLL
    states = lax.fori_loop(0, n_groups, lambda g, st: run_far(first + g * FAR_UNROLL, st, FAR_UNROLL), states)
    states = lax.fori_loop(first + n_groups * FAR_UNROLL, n_far, lambda jb, st: run_far(jb, st, 1), states)
    if n_near == 2:
        states = run_block(jnp.maximum(i - 1, 0), states, 1, False, next_start=i * n_sub)
    return run_block(i, states, 0, True)


def _fox_body(first_ref, qt_ref, k_ref, vt_ref, gout_ref, gn_ref, o_ref, s0_scr, s1_scr, *, blk, bk):
    i = pl.program_id(2)
    first = first_ref[(pl.program_id(0) * pl.num_programs(1) + pl.program_id(1)) * pl.num_programs(2) + i]
    krow = lax.broadcasted_iota(jnp.int32, (bk, blk), 0)
    qcol = lax.broadcasted_iota(jnp.int32, (bk, blk), 1)
    qts = (qt_ref[0, 0], qt_ref[0, 1])

    def score_fn(hh, j):
        off = pl.multiple_of(j * bk, bk)
        return jnp.dot(k_ref[0, hh, pl.ds(off, bk), :], qts[hh], preferred_element_type=F32)

    def causal(st, near, s):
        return jnp.where(krow + s * bk <= qcol, st, NEG)

    states = _pipelined_sweep(i, blk // bk, 1, (s0_scr, s1_scr), score_fn, lambda hh, j: vt_ref[0, hh, j],
                              causal, VB_ROWS, blk, first=first)
    o_t = jnp.concatenate([_normalized_t(states[0], HEAD_DIM), _normalized_t(states[1], HEAD_DIM)], axis=0)
    o_ref[0] = (_group_rms(o_t.T, gout_ref[0], gn_ref[...])).astype(BF16)


EXP2_ZERO_BELOW = -150.0
BOUND_SLACK = 1.02


def _fox_score_spread(g_qk_b):
    qk_max = HEAD_DIM * jnp.max(jnp.abs(g_qk_b[0])) * jnp.max(jnp.abs(g_qk_b[1])) * QK_SCALE * LOG2E
    return 2.0 * qk_max * BOUND_SLACK


def _fox_skip_plan(c_edge, g_qk_b):
    b, nq = c_edge.shape[:2]
    c_first = c_edge[:, :, 0, :H_B]
    c_last = c_edge[:, :, 1, :H_B]
    best = (_fox_score_spread(g_qk_b)
            + (c_first[:, :, None, :] - c_last[:, None, :, :]) * (1.0 / BOUND_SLACK))
    dead = best < EXP2_ZERO_BELOW - 1.0
    dead = jnp.logical_and(dead[..., 0::2], dead[..., 1::2])
    j_lt_i = (jnp.arange(nq)[None, :] < jnp.arange(nq)[:, None])[None, :, :, None]
    lead = jnp.cumprod(jnp.logical_and(dead, j_lt_i).astype(jnp.int32), axis=2)
    first = jnp.sum(lead, axis=2)
    return jnp.transpose(first, (0, 2, 1)).reshape(-1).astype(jnp.int32)


def _fox_attention(qb_t, kb_aug, vb_t, c_edge, g_qk_b, g_out_b):
    b, _, _, t = qb_t.shape
    blk = min(ATTN_BLOCK, t)
    bk = vb_t.shape[-1]
    pairs = H_B // 2
    gout = g_out_b.reshape(pairs, 1, LANES)
    first = _fox_skip_plan(c_edge, g_qk_b)
    return pl.pallas_call(
        functools.partial(_fox_body, blk=blk, bk=bk),
        grid_spec=pltpu.PrefetchScalarGridSpec(
            num_scalar_prefetch=1, grid=(b, pairs, t // blk),
            in_specs=[
                pl.BlockSpec((1, 2, LANES, blk), lambda bi, p, i, f: (bi, p, 0, i)),
                pl.BlockSpec((1, 2, t, LANES), lambda bi, p, i, f: (bi, p, 0, 0)),
                pl.BlockSpec((1, 2, t // bk, VB_ROWS, bk), lambda bi, p, i, f: (bi, p, 0, 0, 0)),
                pl.BlockSpec((1, 1, LANES), lambda bi, p, i, f: (p, 0, 0)),
                pl.BlockSpec((LANES, LANES), lambda bi, p, i, f: (0, 0)),
            ],
            out_specs=pl.BlockSpec((1, blk, LANES), lambda bi, p, i, f: (bi, i, p)),
            scratch_shapes=[pltpu.VMEM((2, bk, blk), F32), pltpu.VMEM((2, bk, blk), F32)]),
        out_shape=jax.ShapeDtypeStruct((b, t, H_B * HEAD_DIM), BF16),
        compiler_params=_cparams(("arbitrary", "arbitrary", "arbitrary")),
        name="fox_attention",
    )(first, qb_t, kb_aug, vb_t, gout, _blockdiag_mean(LANES, HEAD_DIM))


def _diff_finish(states, lam, gout):
    (_, l1, a1), (_, l2, a2) = states
    o = a1 / l1 - lam * (a2 / l2)
    ms = jnp.mean(o * o, axis=-1, keepdims=True)
    return (o * lax.rsqrt(ms + RMS_EPS)) * gout * (1.0 - LAM_INIT)


def _diff_body(lam_ref, qt_ref, k_ref, vt_ref, bias_ref, gout_ref, o_ref, s0_scr, s1_scr, *, blk, bk):
    i = pl.program_id(2)
    n_sub = blk // bk
    qts = (qt_ref[0, 0], qt_ref[0, 1])

    def score_fn(mi, j):
        off = pl.multiple_of(j * bk, bk)
        return jnp.dot(k_ref[0, 0, pl.ds(off, bk), :], qts[mi], preferred_element_type=F32)

    def add_bias(st, near, s):
        if near == 0:
            return st + bias_ref[0, n_sub + s]
        return st + bias_ref[0, jnp.where(i == 0, 2 * n_sub + s, s)]

    states = _pipelined_sweep(i, n_sub, 2, (s0_scr, s1_scr), score_fn, lambda mi, j: vt_ref[0, 0, j],
                              add_bias, VA_ROWS, blk)
    lam = _lam(lam_ref[...])
    o = (_normalized_t(states[0], 2 * HEAD_DIM) - lam * _normalized_t(states[1], 2 * HEAD_DIM)).T
    ms = jnp.mean(o * o, axis=-1, keepdims=True)
    o_ref[0] = ((o * lax.rsqrt(ms + RMS_EPS)) * gout_ref[0] * (1.0 - LAM_INIT)).astype(BF16)


def _diff_attention(qa_t, ka_bf, va_t, bias, lambda_qk, g_out_a):
    b, _, _, t = qa_t.shape
    blk = min(ATTN_BLOCK, t)
    bk = va_t.shape[-1]
    gout = g_out_a.reshape(H_A, 1, LANES)
    return pl.pallas_call(
        functools.partial(_diff_body, blk=blk, bk=bk),
        grid=(b, H_A, t // blk),
        in_specs=[
            _const_spec((4, HEAD_DIM)),
            pl.BlockSpec((1, 2, LANES, blk), lambda bi, h, i: (bi, h, 0, i)),
            pl.BlockSpec((1, 1, t, LANES), lambda bi, h, i: (bi, h, 0, 0)),
            pl.BlockSpec((1, 1, t // bk, VA_ROWS, bk), lambda bi, h, i: (bi, h, 0, 0, 0)),
            pl.BlockSpec((1, 3 * (blk // bk), bk, blk), lambda bi, h, i: (h, 0, 0, 0)),
            pl.BlockSpec((1, 1, LANES), lambda bi, h, i: (h, 0, 0)),
        ],
        out_specs=pl.BlockSpec((1, blk, LANES), lambda bi, h, i: (bi, i, h)),
        out_shape=jax.ShapeDtypeStruct((b, t, H_A * 2 * HEAD_DIM), BF16),
        scratch_shapes=[pltpu.VMEM((2, bk, blk), F32), pltpu.VMEM((2, bk, blk), F32)],
        compiler_params=_cparams(("arbitrary", "arbitrary", "arbitrary")),
        name="diff_attention",
    )(lambda_qk, qa_t, ka_bf, va_t, bias, gout)


def _dec_load(m_scr, l_scr, acc_scr, idx):
    return m_scr[idx], l_scr[idx], acc_scr[idx]


def _dec_store(m_scr, l_scr, acc_scr, idx, state):
    m_scr[idx], l_scr[idx], acc_scr[idx] = state


def _dec_init(m_scr, l_scr, acc_scr):
    m_scr[...] = jnp.full(m_scr.shape, NEG, F32)
    l_scr[...] = jnp.zeros(l_scr.shape, F32)
    acc_scr[...] = jnp.zeros(acc_scr.shape, F32)


def _diff_dec_body(lam_ref, q_ref, kn_ref, vn_ref, kt_ref, v_ref, bc_ref, bn_ref, gout_ref, o_ref,
                   m_scr, l_scr, acc_scr, *, ck):
    kc = pl.program_id(1)

    @pl.when(kc == 0)
    def _():
        _dec_init(m_scr, l_scr, acc_scr)

    q = q_ref[0]

    def update(idx, s, v):
        _dec_store(m_scr, l_scr, acc_scr, idx, _online_update(_dec_load(m_scr, l_scr, acc_scr, idx), s, v))

    def q_map(h, mi):
        col = (2 * h + mi) * HEAD_DIM
        return q[:, col:col + HEAD_DIM]

    scores = [jnp.dot(q_map(h, mi), kt_ref[0, h, mi].astype(BF16), preferred_element_type=F32)
              for h in range(H_A) for mi in range(2)]
    for h in range(H_A):
        v = v_ref[0, pl.ds(h, ck, stride=H_A), :].astype(BF16)
        for mi in range(2):
            update(2 * h + mi, scores[2 * h + mi] + bc_ref[h, 0], v)

    @pl.when(kc == pl.num_programs(1) - 1)
    def _():
        lam = _lam(lam_ref[...])
        for h in range(H_A):
            sl = slice(h * LANES, (h + 1) * LANES)
            k_new = kn_ref[0][:, sl].astype(BF16)
            v_new = vn_ref[0][:, sl].astype(BF16)
            for mi in range(2):
                s = _qk(q_map(h, mi), k_new[:, mi * HEAD_DIM:(mi + 1) * HEAD_DIM]) + bn_ref[h, 0]
                update(2 * h + mi, s, v_new)
            states = (_dec_load(m_scr, l_scr, acc_scr, 2 * h), _dec_load(m_scr, l_scr, acc_scr, 2 * h + 1))
            o_ref[0, :, sl] = _diff_finish(states, lam, gout_ref[h]).astype(BF16)


def _diff_decode(qa, ka_new, va_new, cache_kt, cache_v, bias_c, bias_n, lambda_qk, g_out_a):
    b, nq, w = qa.shape
    past = cache_kt.shape[-1]
    ck = min(CACHE_CHUNK, past)
    n_kc = past // ck
    new_spec = pl.BlockSpec((1, nq, w), lambda bi, c: (bi, 0, 0))
    return pl.pallas_call(
        functools.partial(_diff_dec_body, ck=ck),
        grid=(b, n_kc),
        in_specs=[
            _const_spec((4, HEAD_DIM)), new_spec, new_spec, new_spec,
            pl.BlockSpec((1, H_A, 2, HEAD_DIM, ck), lambda bi, c: (bi, 0, 0, 0, c)),
            pl.BlockSpec((1, H_A * ck, LANES), lambda bi, c: (bi, c, 0)),
            pl.BlockSpec((H_A, 1, nq, ck), lambda bi, c: (0, c, 0, 0)),
            _const_spec((H_A, 1, nq, nq)),
            _const_spec((H_A, 1, LANES)),
        ],
        out_specs=new_spec,
        out_shape=jax.ShapeDtypeStruct((b, nq, w), BF16),
        scratch_shapes=[pltpu.VMEM((2 * H_A, nq, 1), F32), pltpu.VMEM((2 * H_A, nq, 1), F32),
                        pltpu.VMEM((2 * H_A, nq, LANES), F32)],
        compiler_params=_cparams(("arbitrary", "arbitrary")),
        name="diff_decode",
    )(lambda_qk, qa, ka_new, va_new, cache_kt, cache_v, bias_c, bias_n, g_out_a.reshape(H_A, 1, LANES))


def _suffix_sum_lanes(x):
    n = x.shape[-1]
    lane = lax.broadcasted_iota(jnp.int32, x.shape, x.ndim - 1)
    shift = 1
    while shift < n:
        x = x + jnp.where(lane + shift < n, pltpu.roll(x, n - shift, x.ndim - 1), 0.0)
        shift *= 2
    return x


def _fox_dec_body(qk2_ref, q_ref, kn_ref, vn_ref, lfn_col_ref, lfn_row_ref, kt_ref, vt_ref, lfc_ref, gout_ref, gn_ref,
                  o_ref, cq_scr, cn_scr, after_scr, m_scr, l_scr, acc_scr, *, nq):
    kc = pl.program_id(1)
    q = q_ref[0]
    q_head = lambda h: q[:, h * HEAD_DIM:(h + 1) * HEAD_DIM]

    def update(h, s, v, trans_v):
        m, l, acc = _dec_load(m_scr, l_scr, acc_scr, h)
        m_new = jnp.maximum(m, jnp.max(s, axis=-1, keepdims=True))
        p = jnp.exp2(s - m_new)
        alpha = jnp.exp2(m - m_new)
        pv = _qk(p.astype(BF16), v) if trans_v else jnp.dot(p.astype(BF16), v, preferred_element_type=F32)
        _dec_store(m_scr, l_scr, acc_scr, h, (m_new, alpha * l + jnp.sum(p, axis=-1, keepdims=True),
                                              alpha * acc + pv))

    n_kc = after_scr.shape[0]
    ck = after_scr.shape[-1]

    @pl.when(kc == 0)
    def _():
        _dec_init(m_scr, l_scr, acc_scr)
        r = lax.broadcasted_iota(jnp.int32, (nq, nq), 0)
        c = lax.broadcasted_iota(jnp.int32, (nq, nq), 1)
        tri_n = jnp.where(c <= r, 1.0, 0.0).astype(BF16)
        cq_scr[...] = _cumsum_rows(tri_n, lfn_col_ref[0] * LOG2E, jnp.zeros((1, LANES), F32))
        lf_row = lfn_row_ref[0] * LOG2E
        total = jnp.sum(lf_row, axis=-1, keepdims=True)
        cn_scr[...] = total - _suffix_sum_lanes(lf_row) + lf_row
        lf = lfc_ref[0] * LOG2E
        after = _suffix_sum_lanes(lf) - lf
        for c_idx in range(n_kc):
            after_scr[c_idx] = after[:, c_idx * ck:(c_idx + 1) * ck]

    after = after_scr[n_kc - 1 - kc]
    cq = cq_scr[...]
    live = jnp.max(after[:, ck - 1:ck]) + qk2_ref[0] >= EXP2_ZERO_BELOW - 1.0

    @pl.when(live)
    def _():
        for h in range(H_B):
            s = jnp.dot(q_head(h), kt_ref[0, h].astype(BF16), preferred_element_type=F32)
            update(h, s + cq[:, h:h + 1] + after[h:h + 1, :], vt_ref[0, h].astype(BF16), True)

    @pl.when(kc == pl.num_programs(1) - 1)
    def _():
        row = lax.broadcasted_iota(jnp.int32, (nq, nq), 0)
        col = lax.broadcasted_iota(jnp.int32, (nq, nq), 1)
        cn = cn_scr[...]
        outs = []
        for h in range(H_B):
            sl = slice(h * HEAD_DIM, (h + 1) * HEAD_DIM)
            s = _qk(q_head(h), kn_ref[0][:, sl].astype(BF16)) + cq[:, h:h + 1] - cn[h:h + 1, 0:nq]
            update(h, jnp.where(col <= row, s, NEG), vn_ref[0][:, sl].astype(BF16), False)
            _, l, acc = _dec_load(m_scr, l_scr, acc_scr, h)
            outs.append(acc / l)
        for p in range(H_B // 2):
            o = jnp.concatenate([outs[2 * p], outs[2 * p + 1]], axis=-1)
            o_ref[0, :, p * LANES:(p + 1) * LANES] = _group_rms(o, gout_ref[p], gn_ref[...]).astype(BF16)


def _fox_decode(qb, kb_new, vb_new, logf_new, cache_kt, cache_vt, cache_logf_t, g_qk_b, g_out_b):
    b, nq, w = qb.shape
    past = cache_kt.shape[-1]
    ck = min(CACHE_CHUNK, past)
    n_kc = past // ck
    assert nq <= LANES
    pad = jnp.zeros((b, nq, LANES - H_B), F32)
    lfn_col = jnp.concatenate([logf_new, pad], axis=-1)
    lfn_row = jnp.concatenate([jnp.transpose(logf_new, (0, 2, 1)), jnp.zeros((b, H_B, LANES - nq), F32)], axis=-1)
    new_spec = pl.BlockSpec((1, nq, w), lambda bi, c: (bi, 0, 0))
    newest_first = lambda bi, c: (bi, 0, 0, n_kc - 1 - c)
    return pl.pallas_call(
        functools.partial(_fox_dec_body, nq=nq),
        grid=(b, n_kc),
        in_specs=[
            pl.BlockSpec(memory_space=pltpu.SMEM),
            new_spec, new_spec, new_spec,
            pl.BlockSpec((1, nq, LANES), lambda bi, c: (bi, 0, 0)),
            pl.BlockSpec((1, H_B, LANES), lambda bi, c: (bi, 0, 0)),
            pl.BlockSpec((1, H_B, HEAD_DIM, ck), newest_first),
            pl.BlockSpec((1, H_B, HEAD_DIM, ck), newest_first),
            pl.BlockSpec((1, H_B, past), lambda bi, c: (bi, 0, 0)),
            _const_spec((H_B // 2, 1, LANES)), _const_spec((LANES, LANES)),
        ],
        out_specs=new_spec,
        out_shape=jax.ShapeDtypeStruct((b, nq, w), BF16),
        scratch_shapes=[pltpu.VMEM((nq, LANES), F32), pltpu.VMEM((H_B, LANES), F32),
                        pltpu.VMEM((n_kc, H_B, ck), F32),
                        pltpu.VMEM((H_B, nq, 1), F32), pltpu.VMEM((H_B, nq, 1), F32),
                        pltpu.VMEM((H_B, nq, HEAD_DIM), F32)],
        compiler_params=_cparams(("arbitrary", "arbitrary")),
        name="fox_decode",
    )(_fox_score_spread(g_qk_b).reshape(1), qb, kb_new, vb_new, lfn_col, lfn_row, cache_kt, cache_vt, cache_logf_t,
      g_out_b.reshape(H_B // 2, 1, LANES), _blockdiag_mean(LANES, HEAD_DIM))


def _route(logits):
    lane_i = lax.broadcasted_iota(jnp.int32, logits.shape, 1)
    lane = lane_i.astype(F32)
    big = float(LANES)
    lg = jnp.where(lane_i < N_GROUPS, logits, NEG)
    mx = jnp.max(lg, axis=-1, keepdims=True)
    grp = jnp.min(jnp.where(lg == mx, lane, big), axis=-1, keepdims=True)
    p_grp = 1.0 / jnp.sum(jnp.exp(lg - mx), axis=-1, keepdims=True)
    e = lane_i - ROUTER_LANE0
    e_grp = lax.shift_right_arithmetic(e, 3).astype(F32)
    sel = (e >= 0) & (e < N_EXPERTS) & (e_grp == grp)
    v = jnp.where(sel, logits, NEG)
    v1 = jnp.max(v, axis=-1, keepdims=True)
    i1 = jnp.min(jnp.where(sel & (v == v1), lane, big), axis=-1, keepdims=True)
    sel2 = sel & (lane != i1)
    vv = jnp.where(sel2, logits, NEG)
    v2 = jnp.max(vv, axis=-1, keepdims=True)
    i2 = jnp.min(jnp.where(sel2 & (vv == v2), lane, big), axis=-1, keepdims=True)
    e2 = jnp.exp(v2 - v1)
    w1 = p_grp / (1.0 + e2)
    w2 = p_grp * e2 / (1.0 + e2)
    gates = jnp.where(lane == i1, w1, 0.0) + jnp.where(lane == i2, w2, 0.0)
    return gates, (i1, i2, w1, w2)


R_E1, R_E2, R_RANK1, R_RANK2, R_W1, R_W2 = range(6)


def _mix_body(oa_ref, ob_ref, x_ref, wa_ref, wb_ref, g2_ref, wr1_ref, wr2_ref, br_ref, tri_ref,
              x1_ref, xn_ref, gates_ref, route_ref, counts_ref):
    y = (jnp.dot(oa_ref[...], wa_ref[...], preferred_element_type=F32)
         + jnp.dot(ob_ref[...], wb_ref[...], preferred_element_type=F32))
    x1 = x_ref[...] + y
    x1_ref[...] = x1
    ms = jnp.mean(x1 * x1, axis=-1, keepdims=True)
    xn = (x1 * lax.rsqrt(ms + RMS_EPS)) * g2_ref[...]
    xn_ref[...] = xn
    h1 = xn.astype(BF16)
    h2 = (xn - h1.astype(F32)).astype(BF16)
    logits = (jnp.dot(h1, wr1_ref[...], preferred_element_type=F32)
              + jnp.dot(h1, wr2_ref[...], preferred_element_type=F32)
              + jnp.dot(h2, wr1_ref[...], preferred_element_type=F32)) + br_ref[...]
    gates, (i1, i2, w1, w2) = _route(logits)
    gates_ref[...] = gates

    @pl.when(pl.program_id(0) == 0)
    def _():
        counts_ref[...] = jnp.zeros_like(counts_ref)

    lane_i = lax.broadcasted_iota(jnp.int32, gates.shape, 1)
    lane = lane_i.astype(F32)
    oh1 = jnp.where(lane == i1, 1.0, 0.0)
    oh2 = jnp.where(lane == i2, 1.0, 0.0)
    comb = oh1 + oh2
    running = counts_ref[0:1, :]
    before = jnp.dot(tri_ref[...], comb.astype(BF16), preferred_element_type=F32) + running
    rank1 = jnp.sum(before * oh1, axis=-1, keepdims=True)
    rank2 = jnp.sum(before * oh2, axis=-1, keepdims=True)
    counts_ref[0:1, :] = running + jnp.sum(comb, axis=0, keepdims=True)
    rec = jnp.zeros_like(gates)
    for idx, val in ((R_E1, i1 - ROUTER_LANE0), (R_E2, i2 - ROUTER_LANE0), (R_RANK1, rank1), (R_RANK2, rank2),
                     (R_W1, w1), (R_W2, w2)):
        rec = jnp.where(lane_i == idx, val, rec)
    route_ref[...] = rec


def _mix_and_route(o_a, o_b, x, w_out, g_norm, w_rg, b_rg, w_re, b_re):
    n, d = x.shape
    bt = min(TOKEN_TILE, n)
    r = jnp.arange(bt)
    tri_strict = (r[None, :] < r[:, None]).astype(BF16)
    wa = w_out[:GROUP_W].astype(BF16)
    wb = w_out[GROUP_W:].astype(BF16)
    n_r = N_GROUPS + N_EXPERTS
    wr = jnp.concatenate([w_rg, w_re, jnp.zeros((d, LANES - n_r), F32)], axis=1)
    wr1 = wr.astype(BF16)
    wr2 = (wr - wr1.astype(F32)).astype(BF16)
    br = jnp.concatenate([b_rg, b_re, jnp.zeros((LANES - n_r,), F32)])[None, :]
    row = lambda width: pl.BlockSpec((bt, width), lambda i: (i, 0))
    return pl.pallas_call(
        _mix_body,
        grid=(n // bt,),
        in_specs=[row(GROUP_W), row(GROUP_W), row(d), _const_spec((GROUP_W, d)), _const_spec((GROUP_W, d)),
                  _const_spec((1, d)), _const_spec((d, LANES)), _const_spec((d, LANES)), _const_spec((1, LANES)),
                  _const_spec((bt, bt))],
        out_specs=[row(d), row(d), row(LANES), row(LANES), _const_spec((8, LANES))],
        out_shape=[jax.ShapeDtypeStruct((n, d), F32), jax.ShapeDtypeStruct((n, d), F32),
                   jax.ShapeDtypeStruct((n, LANES), F32), jax.ShapeDtypeStruct((n, LANES), F32),
                   jax.ShapeDtypeStruct((8, LANES), F32)],
        compiler_params=_cparams(("arbitrary",)),
        name="mix_route",
    )(o_a, o_b, x, wa, wb, g_norm[None, :], wr1, wr2, br, tri_strict)


def _swiglu(xn, wg, wu, wd, gate=None):
    x = xn.astype(BF16)
    g = jnp.dot(x, wg, preferred_element_type=F32)
    u = jnp.dot(x, wu, preferred_element_type=F32)
    h = (g * jax.nn.sigmoid(g)) * u
    if gate is not None:
        h = h * gate
    return jnp.dot(h.astype(BF16), wd, preferred_element_type=F32)


def _expert_body(xn_ref, x1_ref, gates_ref, wg_ref, wu_ref, wd_ref, o_ref):
    e = pl.program_id(1)

    @pl.when(e == 0)
    def _():
        o_ref[...] = x1_ref[...]

    gates = gates_ref[...]
    lane = lax.broadcasted_iota(jnp.int32, gates.shape, 1)
    gate = jnp.sum(jnp.where(lane == e + ROUTER_LANE0, gates, 0.0), axis=-1, keepdims=True)
    o_ref[...] += _swiglu(xn_ref[...], wg_ref[0], wu_ref[0], wd_ref[0], gate)


def _experts(xn, x1, gates, w_gate, w_up, w_down):
    n, d = x1.shape
    ff = w_gate.shape[-1]
    bt = min(MOE_TILE, n)
    row = lambda width: pl.BlockSpec((bt, width), lambda i, e: (i, 0))
    return pl.pallas_call(
        _expert_body,
        grid=(n // bt, N_EXPERTS),
        in_specs=[row(d), row(d), row(LANES),
                  pl.BlockSpec((1, d, ff), lambda i, e: (e, 0, 0)),
                  pl.BlockSpec((1, d, ff), lambda i, e: (e, 0, 0)),
                  pl.BlockSpec((1, ff, d), lambda i, e: (e, 0, 0))],
        out_specs=row(d),
        out_shape=jax.ShapeDtypeStruct((n, d), F32),
        compiler_params=_cparams(("arbitrary", "arbitrary")),
        name="experts",
    )(xn, x1, gates, w_gate, w_up, w_down)


def _row_copies(n_rows, make_copy):
    def issue(r, carry):
        for s in range(2):
            make_copy(r, s).start()
        return carry

    lax.fori_loop(0, n_rows, issue, 0, unroll=8)

    def drain(r, carry):
        for s in range(2):
            make_copy(r, s).wait()
        return carry

    lax.fori_loop(0, n_rows, drain, 0, unroll=8)


def _row_position_body(route_ref, base_ref, pos_ref):
    rec = route_ref[...]
    lane_i = lax.broadcasted_iota(jnp.int32, rec.shape, 1)
    lane = lane_i.astype(F32)
    out = jnp.zeros(rec.shape, F32)
    for slot, (e_lane, r_lane) in enumerate(((R_E1, R_RANK1), (R_E2, R_RANK2))):
        onehot = lane == rec[:, e_lane:e_lane + 1] + float(ROUTER_LANE0)
        base = jnp.sum(jnp.where(onehot, base_ref[...], 0.0), axis=-1, keepdims=True)
        out = jnp.where(lane_i == slot, base + rec[:, r_lane:r_lane + 1], out)
    pos_ref[...] = out.astype(jnp.int32)


def _dispatch_body(last_tile_ref, pos_ref, x_ref, xs_ref, zero_scr, sem):
    @pl.when(pl.program_id(0) == 0)
    def _():
        zero_scr[...] = jnp.zeros_like(zero_scr)
        tm = zero_scr.shape[0]
        fills = [pltpu.make_async_copy(zero_scr, xs_ref.at[pl.ds(pl.multiple_of(last_tile_ref[e], tm), tm)], sem)
                 for e in range(N_EXPERTS)]
        for f in fills:
            f.start()
        for f in fills:
            f.wait()

    _row_copies(x_ref.shape[0], lambda r, s: pltpu.make_async_copy(
        x_ref.at[pl.ds(r, 1)], xs_ref.at[pl.ds(pos_ref[0, 0, 2 * r + s], 1)], sem))


def _grouped_body(te_ref, nu_ref, xs_ref, wg_ref, wu_ref, wd_ref, ys_ref):
    del te_ref
    used = pl.program_id(0) < nu_ref[0]

    @pl.when(used)
    def _():
        ys_ref[...] = _swiglu(xs_ref[...], wg_ref[0], wu_ref[0], wd_ref[0])

    @pl.when(jnp.logical_not(used))
    def _():
        ys_ref[...] = jnp.zeros_like(ys_ref)


def _combine_body(pos_ref, route_ref, x1_ref, ys_ref, o_ref, buf_scr, sem):
    _row_copies(x1_ref.shape[0], lambda r, s: pltpu.make_async_copy(
        ys_ref.at[pl.ds(pos_ref[0, 0, 2 * r + s], 1)], buf_scr.at[s, pl.ds(r, 1)], sem))
    rec = route_ref[...]
    o_ref[...] = (x1_ref[...] + rec[:, R_W1:R_W1 + 1] * buf_scr[0] + rec[:, R_W2:R_W2 + 1] * buf_scr[1])


def _routed_experts(xn, x1, route, counts, w_gate, w_up, w_down):
    n, d = x1.shape
    ff = w_gate.shape[-1]
    bt = min(TOKEN_TILE, n)
    nt = n // bt
    tm = MOE_ROW_TILE
    n_tiles = (2 * n) // tm + N_EXPERTS
    cnt = counts[0, ROUTER_LANE0:ROUTER_LANE0 + N_EXPERTS].astype(jnp.int32)
    tiles = (cnt + tm - 1) // tm
    tile_end = jnp.cumsum(tiles)
    base_row = (tile_end - tiles) * tm
    n_used = tile_end[-1:]
    tile_expert = jnp.minimum(jnp.sum(jnp.arange(n_tiles)[:, None] >= tile_end[None, :], axis=1), N_EXPERTS - 1)
    base_lanes = jnp.zeros((1, LANES), F32).at[0, ROUTER_LANE0:ROUTER_LANE0 + N_EXPERTS].set(base_row.astype(F32))
    row = lambda width: pl.BlockSpec((bt, width), lambda i: (i, 0))
    pos = pl.pallas_call(
        _row_position_body,
        grid=(nt,),
        in_specs=[row(LANES), _const_spec((1, LANES))],
        out_specs=row(LANES),
        out_shape=jax.ShapeDtypeStruct((n, LANES), jnp.int32),
        compiler_params=_cparams(("arbitrary",)),
        name="moe_positions",
    )(route, base_lanes)
    pos = pos[:, :2].reshape(nt, 1, 2 * bt)

    pos_spec = pl.BlockSpec((1, 1, 2 * bt), lambda i: (i, 0, 0), memory_space=pltpu.SMEM)
    any_spec = pl.BlockSpec(memory_space=pl.ANY)
    last_tile = jnp.minimum(base_row + jnp.maximum(tiles - 1, 0) * tm, (n_tiles - 1) * tm).astype(jnp.int32)
    xs = pl.pallas_call(
        _dispatch_body,
        grid_spec=pltpu.PrefetchScalarGridSpec(
            num_scalar_prefetch=1, grid=(nt,),
            in_specs=[pl.BlockSpec((1, 1, 2 * bt), lambda i, lt: (i, 0, 0), memory_space=pltpu.SMEM),
                      pl.BlockSpec((bt, d), lambda i, lt: (i, 0))],
            out_specs=any_spec,
            scratch_shapes=[pltpu.VMEM((tm, d), F32), pltpu.SemaphoreType.DMA(())]),
        out_shape=jax.ShapeDtypeStruct((n_tiles * tm, d), F32),
        compiler_params=_cparams(("arbitrary",)),
        name="moe_dispatch",
    )(last_tile, pos, xn)

    w_spec = lambda shape: pl.BlockSpec(shape, lambda t, te, nu: (te[t], 0, 0))
    ys = pl.pallas_call(
        _grouped_body,
        grid_spec=pltpu.PrefetchScalarGridSpec(
            num_scalar_prefetch=2, grid=(n_tiles,),
            in_specs=[pl.BlockSpec((tm, d), lambda t, te, nu: (jnp.minimum(t, nu[0] - 1), 0)),
                      w_spec((1, d, ff)), w_spec((1, d, ff)), w_spec((1, ff, d))],
            out_specs=pl.BlockSpec((tm, d), lambda t, te, nu: (t, 0))),
        out_shape=jax.ShapeDtypeStruct((n_tiles * tm, d), F32),
        compiler_params=_cparams(("arbitrary",)),
        name="moe_experts",
    )(tile_expert.astype(jnp.int32), n_used.astype(jnp.int32), xs, w_gate, w_up, w_down)

    return pl.pallas_call(
        _combine_body,
        grid=(nt,),
        in_specs=[pos_spec, row(LANES), row(d), any_spec],
        out_specs=row(d),
        out_shape=jax.ShapeDtypeStruct((n, d), F32),
        scratch_shapes=[pltpu.VMEM((2, bt, d), F32), pltpu.SemaphoreType.DMA(())],
        compiler_params=_cparams(("arbitrary",)),
        name="moe_combine",
    )(pos, route, x1, ys)


def _ffn(o_a, o_b, x, w_out, g_norm_ffn, w_rg, b_rg, w_re, b_re, wg, wu, wd):
    b, t, d = x.shape
    n = b * t
    x1, xn, gates, route, counts = _mix_and_route(o_a.reshape(n, -1), o_b.reshape(n, -1), x.reshape(n, d), w_out,
                                                  g_norm_ffn, w_rg, b_rg, w_re, b_re)
    if 2 * n >= ROUTED_MIN_ASSIGNMENTS:
        y = _routed_experts(xn, x1, route, counts, wg, wu, wd)
    else:
        y = _experts(xn, x1, gates, wg, wu, wd)
    return y.reshape(b, t, d)


def kernel(x_prompt, x_sample, cache_a_k, cache_a_v, cache_b_k, cache_b_v, cache_b_logf, g_norm_mix, w_in, b_forget, g_qk_a, g_qk_b, lambda_qk, g_out_a, g_out_b, w_out, rel_bias, g_norm_ffn, w_router_group, b_router_group, w_router_expert, b_router_expert, w_exp_gate, w_exp_up, w_exp_down):
    depth = w_in.shape[0]
    assert depth == 1, "single-layer step only"
    bp, tp, d = x_prompt.shape
    bs, ts, _ = x_sample.shape
    past = cache_a_k.shape[2]
    w_in0, w_out0 = w_in[0], w_out[0]
    wg, wu, wd = w_exp_gate[0].astype(BF16), w_exp_up[0].astype(BF16), w_exp_down[0].astype(BF16)
    ffn_w = (w_out0, g_norm_ffn[0], w_router_group[0], b_router_group[0], w_router_expert[0], b_router_expert[0],
             wg, wu, wd)

    (ka_p, va_p, kb_p, vb_p, logf_p, qa_t, ka_bf, va_t, qb_t, kb_aug, vb_t, c_edge) = _projection(
        x_prompt, g_norm_mix[0], w_in0, b_forget[0], g_qk_a[0], g_qk_b[0], with_aug=True)
    blk = min(ATTN_BLOCK, tp)
    bk = va_t.shape[-1]
    assert blk % CHUNK == 0 and blk >= MAX_DISTANCE and blk % bk == 0
    q_pos = blk + jnp.arange(blk, dtype=jnp.int32)
    bkt_p = jnp.stack([_bucket_map(q_pos, s * bk + jnp.arange(bk, dtype=jnp.int32)).T
                       for s in range(2 * blk // bk)] + [jnp.full((bk, blk), -1, jnp.int32)] * (blk // bk))
    bias_p = _bias_tiles(rel_bias, bkt_p)
    o_a = _diff_attention(qa_t, ka_bf, va_t, bias_p, lambda_qk[0], g_out_a[0])
    o_b = _fox_attention(qb_t, kb_aug, vb_t, c_edge, g_qk_b[0], g_out_b[0])
    y_p = _ffn(o_a, o_b, x_prompt, *ffn_w)

    xs = x_sample.reshape(1, bs * ts, d)
    (ka_s, va_s, kb_s, vb_s, logf_s, qa_s, qb_s) = _projection(
        xs, g_norm_mix[0], w_in0, b_forget[0], g_qk_a[0], g_qk_b[0], with_aug=False)
    per_stream = lambda a: a.reshape(bs, ts, a.shape[-1])
    ka_s, va_s, kb_s, vb_s, logf_s, qa_s, qb_s = map(per_stream, (ka_s, va_s, kb_s, vb_s, logf_s, qa_s, qb_s))
    ck = min(CACHE_CHUNK, past)
    q_pos = past + jnp.arange(ts, dtype=jnp.int32)
    bkt_c = _bucket_map(q_pos, jnp.arange(past, dtype=jnp.int32)).reshape(ts, past // ck, ck).transpose(1, 0, 2)
    bias_c = _bias_tiles(rel_bias, bkt_c)
    bias_n = _bias_tiles(rel_bias, _bucket_map(q_pos, q_pos)[None])
    o_a_s = _diff_decode(qa_s, ka_s, va_s, jnp.transpose(cache_a_k[0], (0, 2, 3, 4, 1)),
                         cache_a_v[0].reshape(bs, past * H_A, 2 * HEAD_DIM), bias_c, bias_n, lambda_qk[0], g_out_a[0])
    o_b_s = _fox_decode(qb_s, kb_s, vb_s, logf_s, jnp.transpose(cache_b_k[0], (0, 2, 3, 1)),
                        jnp.transpose(cache_b_v[0], (0, 2, 3, 1)), jnp.transpose(cache_b_logf[0], (0, 2, 1)),
                        g_qk_b[0], g_out_b[0])
    y_s = _ffn(o_a_s, o_b_s, x_sample, *ffn_w)

    def rows(ka, va, kb, vb, logf, b, t):
        return (ka.reshape(1, b, t, H_A, 2, HEAD_DIM), va.reshape(1, b, t, H_A, 2 * HEAD_DIM),
                kb.reshape(1, b, t, H_B, HEAD_DIM), vb.reshape(1, b, t, H_B, HEAD_DIM), logf.reshape(1, b, t, H_B))

    return (y_p, y_s) + rows(ka_p, va_p, kb_p, vb_p, logf_p, bp, tp) + rows(ka_s, va_s, kb_s, vb_s, logf_s, bs, ts)
```

```python
import functools
import math

import jax
import jax.numpy as jnp
from jax import lax
from jax.experimental import pallas as pl
from jax.experimental.pallas import tpu as pltpu

F32 = jnp.float32
BF16 = jnp.bfloat16

LANES = 128
VMEM_LIMIT_BYTES = 56 * 1024 * 1024

HEAD_DIM = 64
H_A = 4
H_B = 8
GROUP_W = 512
MAIN_W = 6 * GROUP_W
CHUNK = 64
N_BUCKETS = 32
MAX_DISTANCE = 128
N_GROUPS = 4
EXPERTS_PER_GROUP = 8
N_EXPERTS = N_GROUPS * EXPERTS_PER_GROUP
ROUTER_LANE0 = N_GROUPS
RMS_EPS = 1e-6
NEG = -1e30
LOG2E = 1.4426950408889634
QK_SCALE = HEAD_DIM ** -0.5
LAM_INIT = 0.8 - 0.6 * math.exp(-0.3 * 0)
N_CPARTS = 3

ONES_ROWS = 16
VA_ROWS = 2 * HEAD_DIM + ONES_ROWS
VB_ROWS = HEAD_DIM + ONES_ROWS

TOKEN_TILE = 512
ATTN_BLOCK = 512
KV_BLOCK = 256
FAR_UNROLL = 4
CACHE_CHUNK = 1024
MOE_TILE = 1024
MOE_ROW_TILE = 256
ROUTED_MIN_ASSIGNMENTS = 4 * N_EXPERTS * MOE_ROW_TILE


def _cparams(sem):
    return pltpu.CompilerParams(dimension_semantics=sem, vmem_limit_bytes=VMEM_LIMIT_BYTES)


def _const_spec(shape):
    nd = len(shape)
    return pl.BlockSpec(shape, lambda *_: (0,) * nd)


def _split3(x):
    p1 = x.astype(BF16).astype(F32)
    r1 = x - p1
    p2 = r1.astype(BF16).astype(F32)
    p3 = (r1 - p2).astype(BF16).astype(F32)
    return p1, p2, p3


def _lane_groups(parts, lane):
    return jnp.where(lane < 8, parts[0], jnp.where(lane < 16, parts[1], parts[2]))


def _cumsum_rows(tri, x, carry):
    c = carry
    for part in _split3(x):
        c = c + jnp.dot(tri, part.astype(BF16), preferred_element_type=F32)
    return c


def _log_sigmoid(x):
    return jnp.minimum(x, 0.0) - jnp.log(1.0 + jnp.exp(-jnp.abs(x)))


def _group_rms(raw, gain_row, gn):
    ms = jnp.dot((raw * raw).astype(BF16), gn, preferred_element_type=F32)
    return raw * lax.rsqrt(ms + RMS_EPS) * gain_row


def _online_update(state, s, v):
    m, l, acc = state
    m_new = jnp.maximum(m, jnp.max(s, axis=-1, keepdims=True))
    p = jnp.exp2(s - m_new)
    alpha = jnp.exp2(m - m_new)
    l_new = alpha * l + jnp.sum(p, axis=-1, keepdims=True)
    acc_new = alpha * acc + jnp.dot(p.astype(BF16), v, preferred_element_type=F32)
    return m_new, l_new, acc_new


def _qk(q, k):
    return lax.dot_general(q, k, (((1,), (1,)), ((), ())), preferred_element_type=F32)


def _proj_body(*refs, with_aug, bt, bk):
    (x_ref, g1_ref, wm_ref, wf_ref, bf_ref, gains_ref, gn_ref, tri_ref, place_ref) = refs[:9]
    ka_ref, va_ref, kb_ref, vb_ref, logf_ref = refs[9:14]
    x = x_ref[0]
    ms = jnp.mean(x * x, axis=-1, keepdims=True)
    xn = (x * lax.rsqrt(ms + RMS_EPS)) * g1_ref[...]
    xb = xn.astype(BF16)
    proj = jnp.dot(xb, wm_ref[...], preferred_element_type=F32)
    gains = gains_ref[...]
    gn = gn_ref[...]
    w = GROUP_W
    qa = _group_rms(proj[:, 0:w], gains[0:1], gn)
    ka = _group_rms(proj[:, w:2 * w], gains[1:2], gn)
    va = proj[:, 2 * w:3 * w]
    qb = _group_rms(proj[:, 3 * w:4 * w], gains[2:3], gn)
    kb = _group_rms(proj[:, 4 * w:5 * w], gains[3:4], gn)
    vb = proj[:, 5 * w:6 * w]
    ka_ref[0] = ka
    if with_aug:
        for h in range(H_A):
            va_ref[0, pl.ds(h, bt, stride=H_A), :] = va[:, h * LANES:(h + 1) * LANES]
    else:
        va_ref[0] = va
    kb_ref[0] = kb
    vb_ref[0] = vb
    fl = jnp.dot(xb, wf_ref[...], preferred_element_type=F32) + bf_ref[...]
    logf = _log_sigmoid(fl)
    logf_ref[0] = logf[:, 0:H_B]
    qscale = QK_SCALE * LOG2E
    if not with_aug:
        qa_ref, qb_ref = refs[14:16]
        qa_ref[0] = (qa * qscale).astype(BF16)
        qb_ref[0] = (qb * qscale).astype(BF16)
        return
    qa_t, ka_bf, va_t, qb_t, kb_aug, vb_t, cedge_ref, carry_ref = refs[14:22]
    n_chunk = bt // bk
    row = lax.broadcasted_iota(jnp.int32, (LANES, bt), 0)
    ones_tail = jnp.where(lax.broadcasted_iota(jnp.int32, (ONES_ROWS, bt), 0) == 0, 1.0, 0.0)

    def put_chunks(ref, idx, vt):
        vt = vt.astype(BF16)
        for c in range(n_chunk):
            ref[0, idx, c] = vt[:, c * bk:(c + 1) * bk]

    for h in range(H_A):
        sl = slice(h * LANES, (h + 1) * LANES)
        q_t = (qa[:, sl] * qscale).T
        qa_t[0, 2 * h] = jnp.where(row < HEAD_DIM, q_t, 0.0).astype(BF16)
        qa_t[0, 2 * h + 1] = jnp.where(row >= HEAD_DIM, q_t, 0.0).astype(BF16)
        ka_bf[0, h] = ka[:, sl].astype(BF16)
        put_chunks(va_t, h, jnp.concatenate([va[:, sl].T, ones_tail], axis=0))
        vb_pair_t = vb[:, sl].T
        for hh in range(2):
            put_chunks(vb_t, 2 * h + hh,
                       jnp.concatenate([vb_pair_t[hh * HEAD_DIM:(hh + 1) * HEAD_DIM], ones_tail], axis=0))

    @pl.when(pl.program_id(1) == 0)
    def _():
        carry_ref[...] = jnp.zeros_like(carry_ref)

    c = _cumsum_rows(tri_ref[...], logf * LOG2E, carry_ref[0:1, :])
    carry_ref[0:1, :] = c[bt - 1:bt, :]
    edge_row = lax.broadcasted_iota(jnp.int32, (8, LANES), 0)
    cedge_ref[0, 0] = jnp.where(edge_row == 0, c[0:1, :], jnp.where(edge_row == 1, c[bt - 1:bt, :], 0.0))
    lane = lax.broadcasted_iota(jnp.int32, (bt, LANES), 1)
    cparts = _lane_groups(_split3(c), lane).astype(BF16)
    extras = jnp.dot(cparts, place_ref[...], preferred_element_type=F32)
    ones_q = jnp.where((lane >= HEAD_DIM + N_CPARTS) & (lane < HEAD_DIM + 2 * N_CPARTS), 1.0, 0.0)
    ones_k = jnp.where((lane >= HEAD_DIM) & (lane < HEAD_DIM + N_CPARTS), 1.0, 0.0)
    for h in range(H_B):
        sl = slice((h // 2) * LANES, (h // 2 + 1) * LANES)
        qp = qb[:, sl] * qscale
        kp = kb[:, sl]
        if h % 2:
            qp = pltpu.roll(qp, HEAD_DIM, 1)
            kp = pltpu.roll(kp, HEAD_DIM, 1)
        eq = extras[:, h * LANES:(h + 1) * LANES] + ones_q
        ek = extras[:, (H_B + h) * LANES:(H_B + h + 1) * LANES] + ones_k
        qb_t[0, h] = jnp.where(lane < HEAD_DIM, qp, eq).T.astype(BF16)
        kb_aug[0, h] = jnp.where(lane < HEAD_DIM, kp, ek).astype(BF16)


def _blockdiag_mean(n, group):
    r = jnp.arange(n)
    return jnp.where((r[:, None] // group) == (r[None, :] // group), 1.0 / group, 0.0).astype(BF16)


def _tri(n):
    r = jnp.arange(n)
    return (r[None, :] <= r[:, None]).astype(BF16)


def _place_matrix():
    rows = jnp.arange(LANES)[:, None]
    cols = jnp.arange(2 * H_B * LANES)[None, :]
    p, h = rows // 8, rows % 8
    valid = rows < 8 * N_CPARTS
    qcol = h * LANES + HEAD_DIM + p
    kcol = (H_B + h) * LANES + HEAD_DIM + N_CPARTS + p
    m = jnp.where(valid & (cols == qcol), 1.0, 0.0) - jnp.where(valid & (cols == kcol), 1.0, 0.0)
    return m.astype(BF16)


def _projection(x, g_norm, w_in, b_forget, g_qk_a, g_qk_b, *, with_aug):
    b, t, d = x.shape
    bt = min(TOKEN_TILE, t)
    nt = t // bt
    wm = w_in[:, :MAIN_W].astype(BF16)
    wf_cols = w_in[:, MAIN_W:MAIN_W + H_B]
    wf = jnp.concatenate([wf_cols] * N_CPARTS + [jnp.zeros((d, LANES - H_B * N_CPARTS), F32)], axis=1).astype(BF16)
    bfv = jnp.concatenate([b_forget] * N_CPARTS + [jnp.zeros((LANES - H_B * N_CPARTS,), F32)])[None, :]
    gains = jnp.stack([jnp.tile(g_qk_a[0], 2 * H_A), jnp.tile(g_qk_a[1], 2 * H_A),
                       jnp.tile(g_qk_b[0], H_B), jnp.tile(g_qk_b[1], H_B)])
    gn = _blockdiag_mean(GROUP_W, HEAD_DIM)
    tri = _tri(bt)
    place = _place_matrix()
    in_specs = [
        pl.BlockSpec((1, bt, d), lambda i, j: (i, j, 0)),
        _const_spec((1, d)),
        pl.BlockSpec((d, MAIN_W), lambda i, j: (0, 0), pipeline_mode=pl.Buffered(1)),
        _const_spec((d, LANES)), _const_spec((1, LANES)), _const_spec((4, GROUP_W)),
        _const_spec((GROUP_W, GROUP_W)), _const_spec((bt, bt)), _const_spec((LANES, 2 * H_B * LANES)),
    ]
    row_spec = pl.BlockSpec((1, bt, GROUP_W), lambda i, j: (i, j, 0))
    out_shape = [jax.ShapeDtypeStruct((b, t, GROUP_W), F32)] * 4 + [jax.ShapeDtypeStruct((b, t, H_B), F32)]
    out_specs = [row_spec] * 4 + [pl.BlockSpec((1, bt, H_B), lambda i, j: (i, j, 0))]
    scratch = []
    bk = min(KV_BLOCK, bt)
    if with_aug:
        out_shape[1] = jax.ShapeDtypeStruct((b, t * H_A, 2 * HEAD_DIM), F32)
        out_specs[1] = pl.BlockSpec((1, bt * H_A, 2 * HEAD_DIM), lambda i, j: (i, j, 0))

        def add(shape, block, index_map):
            out_shape.append(jax.ShapeDtypeStruct(shape, BF16))
            out_specs.append(pl.BlockSpec(block, index_map))

        rows_major = lambda i, j: (i, 0, j, 0)
        time_minor = lambda i, j: (i, 0, 0, j)
        chunked = lambda i, j: (i, 0, j, 0, 0)
        add((b, 2 * H_A, LANES, t), (1, 2 * H_A, LANES, bt), time_minor)
        add((b, H_A, t, LANES), (1, H_A, bt, LANES), rows_major)
        add((b, H_A, t // bk, VA_ROWS, bk), (1, H_A, bt // bk, VA_ROWS, bk), chunked)
        add((b, H_B, LANES, t), (1, H_B, LANES, bt), time_minor)
        add((b, H_B, t, LANES), (1, H_B, bt, LANES), rows_major)
        add((b, H_B, t // bk, VB_ROWS, bk), (1, H_B, bt // bk, VB_ROWS, bk), chunked)
        out_shape.append(jax.ShapeDtypeStruct((b, nt, 8, LANES), F32))
        out_specs.append(pl.BlockSpec((1, 1, 8, LANES), lambda i, j: (i, j, 0, 0)))
        scratch = [pltpu.VMEM((8, LANES), F32)]
    else:
        out_shape += [jax.ShapeDtypeStruct((b, t, GROUP_W), BF16)] * 2
        out_specs += [row_spec] * 2
    return pl.pallas_call(
        functools.partial(_proj_body, with_aug=with_aug, bt=bt, bk=bk),
        grid=(b, nt), in_specs=in_specs, out_specs=out_specs, out_shape=out_shape, scratch_shapes=scratch,
        compiler_params=_cparams(("arbitrary", "arbitrary")),
        name="proj_aug" if with_aug else "proj_plain",
    )(x, g_norm[None, :], wm, wf, bfv, gains, gn, tri, place)


def _t5_bucket(rel):
    nb = N_BUCKETS // 2
    max_exact = nb // 2
    base = jnp.where(rel > 0, nb, 0)
    n = jnp.abs(rel)
    large = max_exact + (jnp.log(jnp.maximum(n, max_exact).astype(jnp.float32) / max_exact)
                         / math.log(MAX_DISTANCE / max_exact) * (nb - max_exact)).astype(jnp.int32)
    large = jnp.minimum(large, nb - 1)
    return base + jnp.where(n < max_exact, n, large)


def _bucket_map(q_pos, k_pos):
    bkt = _t5_bucket(k_pos[None, :] - q_pos[:, None])
    visible = (k_pos[None, :] // CHUNK) <= (q_pos[:, None] // CHUNK)
    return jnp.where(visible, bkt, -1).astype(jnp.int32)


def _bias_body(rb_ref, bkt_ref, o_ref):
    h = pl.program_id(0)
    bkt = bkt_ref[0]
    far = rb_ref[N_BUCKETS // 2 - 1, h]
    acc = jnp.zeros(bkt.shape, F32)
    for b in range(N_BUCKETS):
        acc = jnp.where(bkt == b, rb_ref[b, h] - far, acc)
    o_ref[0, 0] = jnp.where(bkt < 0, NEG, acc * LOG2E)


def _bias_tiles(rel_bias, bkt):
    n, r, c = bkt.shape
    return pl.pallas_call(
        _bias_body,
        grid=(H_A, n),
        in_specs=[pl.BlockSpec(memory_space=pltpu.SMEM), pl.BlockSpec((1, r, c), lambda h, i: (i, 0, 0))],
        out_specs=pl.BlockSpec((1, 1, r, c), lambda h, i: (h, i, 0, 0)),
        out_shape=jax.ShapeDtypeStruct((H_A, n, r, c), F32),
        compiler_params=_cparams(("arbitrary", "arbitrary")),
        name="bias_tiles",
    )(rel_bias, bkt)


def _lam(lq):
    a = jnp.sum(lq[0:1, :] * lq[1:2, :], axis=-1, keepdims=True)
    b = jnp.sum(lq[2:3, :] * lq[3:4, :], axis=-1, keepdims=True)
    return jnp.exp(a) - jnp.exp(b) + LAM_INIT


def _attn_init_t(rows, bq):
    return (jnp.full((1, bq), NEG, F32), jnp.zeros((rows, bq), F32))


def _online_update_t(state, st, vt):
    m, acc = state
    m_new = jnp.maximum(m, jnp.max(st, axis=0, keepdims=True))
    p = jnp.exp2(st - m_new)
    alpha = jnp.exp2(m - m_new)
    return m_new, alpha * acc + jnp.dot(vt, p.astype(BF16), preferred_element_type=F32)


def _normalized_t(state, rows):
    acc = state[1]
    return acc[0:rows] / acc[rows:rows + 1]


def _pipelined_sweep(i, n_sub, n_near, slots, score_fn, value_fn, modify, rows, blk, first=0):
    assert n_sub % 2 == 0 and n_near in (1, 2)

    def run_block(jb, states, near, last, next_start=None):
        states = list(states)
        for s in range(n_sub):
            j = jb * n_sub + s
            cur, nxt = slots[s % 2], slots[(s + 1) % 2]
            ahead = next_start if (next_start is not None and s == n_sub - 1) else j + 1
            for c in range(2):
                if not (last and s == n_sub - 1):
                    nxt[c] = score_fn(c, ahead)
                st = cur[c]
                if near is not None:
                    st = modify(st, near, s)
                states[c] = _online_update_t(states[c], st, value_fn(c, j))
        return tuple(states)

    for c in range(2):
        slots[0][c] = score_fn(c, first * n_sub)
    states = (_attn_init_t(rows, blk), _attn_init_t(rows, blk))
    n_far = jnp.maximum(i + 1 - n_near, 0)

    def run_far(jb, states, n_blocks):
        for d in range(n_blocks):
            states = run_block(jb + d, states, None, False)
        return states

    n_groups = jnp.maximum(n_far - first, 0) // FAR_UNROLL
    states = lax.fori_loop(0, n_groups, lambda g, st: run_far(first + g * FAR_UNROLL, st, FAR_UNROLL), states)
    states = lax.fori_loop(first + n_groups * FAR_UNROLL, n_far, lambda jb, st: run_far(jb, st, 1), states)
    if n_near == 2:
        states = run_block(jnp.maximum(i - 1, 0), states, 1, False, next_start=i * n_sub)
    return run_block(i, states, 0, True)


def _fox_body(first_ref, qt_ref, k_ref, vt_ref, gout_ref, gn_ref, o_ref, s0_scr, s1_scr, *, blk, bk):
    i = pl.program_id(2)
    first = first_ref[(pl.program_id(0) * pl.num_programs(1) + pl.program_id(1)) * pl.num_programs(2) + i]
    krow = lax.broadcasted_iota(jnp.int32, (bk, blk), 0)
    qcol = lax.broadcasted_iota(jnp.int32, (bk, blk), 1)
    qts = (qt_ref[0, 0], qt_ref[0, 1])

    def score_fn(hh, j):
        off = pl.multiple_of(j * bk, bk)
        return jnp.dot(k_ref[0, hh, pl.ds(off, bk), :], qts[hh], preferred_element_type=F32)

    def causal(st, near, s):
        return jnp.where(krow + s * bk <= qcol, st, NEG)

    states = _pipelined_sweep(i, blk // bk, 1, (s0_scr, s1_scr), score_fn, lambda hh, j: vt_ref[0, hh, j],
                              causal, VB_ROWS, blk, first=first)
    o_t = jnp.concatenate([_normalized_t(states[0], HEAD_DIM), _normalized_t(states[1], HEAD_DIM)], axis=0)
    o_ref[0] = (_group_rms(o_t.T, gout_ref[0], gn_ref[...])).astype(BF16)


EXP2_ZERO_BELOW = -150.0
BOUND_SLACK = 1.02


def _fox_score_spread(g_qk_b):
    qk_max = HEAD_DIM * jnp.max(jnp.abs(g_qk_b[0])) * jnp.max(jnp.abs(g_qk_b[1])) * QK_SCALE * LOG2E
    return 2.0 * qk_max * BOUND_SLACK


def _fox_skip_plan(c_edge, g_qk_b):
    b, nq = c_edge.shape[:2]
    c_first = c_edge[:, :, 0, :H_B]
    c_last = c_edge[:, :, 1, :H_B]
    best = (_fox_score_spread(g_qk_b)
            + (c_first[:, :, None, :] - c_last[:, None, :, :]) * (1.0 / BOUND_SLACK))
    dead = best < EXP2_ZERO_BELOW - 1.0
    dead = jnp.logical_and(dead[..., 0::2], dead[..., 1::2])
    j_lt_i = (jnp.arange(nq)[None, :] < jnp.arange(nq)[:, None])[None, :, :, None]
    lead = jnp.cumprod(jnp.logical_and(dead, j_lt_i).astype(jnp.int32), axis=2)
    first = jnp.sum(lead, axis=2)
    return jnp.transpose(first, (0, 2, 1)).reshape(-1).astype(jnp.int32)


def _fox_attention(qb_t, kb_aug, vb_t, c_edge, g_qk_b, g_out_b):
    b, _, _, t = qb_t.shape
    blk = min(ATTN_BLOCK, t)
    bk = vb_t.shape[-1]
    pairs = H_B // 2
    gout = g_out_b.reshape(pairs, 1, LANES)
    first = _fox_skip_plan(c_edge, g_qk_b)
    return pl.pallas_call(
        functools.partial(_fox_body, blk=blk, bk=bk),
        grid_spec=pltpu.PrefetchScalarGridSpec(
            num_scalar_prefetch=1, grid=(b, pairs, t // blk),
            in_specs=[
                pl.BlockSpec((1, 2, LANES, blk), lambda bi, p, i, f: (bi, p, 0, i)),
                pl.BlockSpec((1, 2, t, LANES), lambda bi, p, i, f: (bi, p, 0, 0)),
                pl.BlockSpec((1, 2, t // bk, VB_ROWS, bk), lambda bi, p, i, f: (bi, p, 0, 0, 0)),
                pl.BlockSpec((1, 1, LANES), lambda bi, p, i, f: (p, 0, 0)),
                pl.BlockSpec((LANES, LANES), lambda bi, p, i, f: (0, 0)),
            ],
            out_specs=pl.BlockSpec((1, blk, LANES), lambda bi, p, i, f: (bi, i, p)),
            scratch_shapes=[pltpu.VMEM((2, bk, blk), F32), pltpu.VMEM((2, bk, blk), F32)]),
        out_shape=jax.ShapeDtypeStruct((b, t, H_B * HEAD_DIM), BF16),
        compiler_params=_cparams(("arbitrary", "arbitrary", "arbitrary")),
        name="fox_attention",
    )(first, qb_t, kb_aug, vb_t, gout, _blockdiag_mean(LANES, HEAD_DIM))


def _diff_finish(states, lam, gout):
    (_, l1, a1), (_, l2, a2) = states
    o = a1 / l1 - lam * (a2 / l2)
    ms = jnp.mean(o * o, axis=-1, keepdims=True)
    return (o * lax.rsqrt(ms + RMS_EPS)) * gout * (1.0 - LAM_INIT)


def _diff_body(lam_ref, qt_ref, k_ref, vt_ref, bias_ref, gout_ref, o_ref, s0_scr, s1_scr, *, blk, bk):
    i = pl.program_id(2)
    n_sub = blk // bk
    qts = (qt_ref[0, 0], qt_ref[0, 1])

    def score_fn(mi, j):
        off = pl.multiple_of(j * bk, bk)
        return jnp.dot(k_ref[0, 0, pl.ds(off, bk), :], qts[mi], preferred_element_type=F32)

    def add_bias(st, near, s):
        if near == 0:
            return st + bias_ref[0, n_sub + s]
        return st + bias_ref[0, jnp.where(i == 0, 2 * n_sub + s, s)]

    states = _pipelined_sweep(i, n_sub, 2, (s0_scr, s1_scr), score_fn, lambda mi, j: vt_ref[0, 0, j],
                              add_bias, VA_ROWS, blk)
    lam = _lam(lam_ref[...])
    o = (_normalized_t(states[0], 2 * HEAD_DIM) - lam * _normalized_t(states[1], 2 * HEAD_DIM)).T
    ms = jnp.mean(o * o, axis=-1, keepdims=True)
    o_ref[0] = ((o * lax.rsqrt(ms + RMS_EPS)) * gout_ref[0] * (1.0 - LAM_INIT)).astype(BF16)


def _diff_attention(qa_t, ka_bf, va_t, bias, lambda_qk, g_out_a):
    b, _, _, t = qa_t.shape
    blk = min(ATTN_BLOCK, t)
    bk = va_t.shape[-1]
    gout = g_out_a.reshape(H_A, 1, LANES)
    return pl.pallas_call(
        functools.partial(_diff_body, blk=blk, bk=bk),
        grid=(b, H_A, t // blk),
        in_specs=[
            _const_spec((4, HEAD_DIM)),
            pl.BlockSpec((1, 2, LANES, blk), lambda bi, h, i: (bi, h, 0, i)),
            pl.BlockSpec((1, 1, t, LANES), lambda bi, h, i: (bi, h, 0, 0)),
            pl.BlockSpec((1, 1, t // bk, VA_ROWS, bk), lambda bi, h, i: (bi, h, 0, 0, 0)),
            pl.BlockSpec((1, 3 * (blk // bk), bk, blk), lambda bi, h, i: (h, 0, 0, 0)),
            pl.BlockSpec((1, 1, LANES), lambda bi, h, i: (h, 0, 0)),
        ],
        out_specs=pl.BlockSpec((1, blk, LANES), lambda bi, h, i: (bi, i, h)),
        out_shape=jax.ShapeDtypeStruct((b, t, H_A * 2 * HEAD_DIM), BF16),
        scratch_shapes=[pltpu.VMEM((2, bk, blk), F32), pltpu.VMEM((2, bk, blk), F32)],
        compiler_params=_cparams(("arbitrary", "arbitrary", "arbitrary")),
        name="diff_attention",
    )(lambda_qk, qa_t, ka_bf, va_t, bias, gout)


def _dec_load(m_scr, l_scr, acc_scr, idx):
    return m_scr[idx], l_scr[idx], acc_scr[idx]


def _dec_store(m_scr, l_scr, acc_scr, idx, state):
    m_scr[idx], l_scr[idx], acc_scr[idx] = state


def _dec_init(m_scr, l_scr, acc_scr):
    m_scr[...] = jnp.full(m_scr.shape, NEG, F32)
    l_scr[...] = jnp.zeros(l_scr.shape, F32)
    acc_scr[...] = jnp.zeros(acc_scr.shape, F32)


def _diff_dec_body(lam_ref, q_ref, kn_ref, vn_ref, kt_ref, v_ref, bc_ref, bn_ref, gout_ref, o_ref,
                   m_scr, l_scr, acc_scr, *, ck):
    kc = pl.program_id(1)

    @pl.when(kc == 0)
    def _():
        _dec_init(m_scr, l_scr, acc_scr)

    q = q_ref[0]

    def q_map(h, mi):
        col = (2 * h + mi) * HEAD_DIM
        return q[:, col:col + HEAD_DIM]

    states = [_dec_load(m_scr, l_scr, acc_scr, idx) for idx in range(2 * H_A)]
    for h in range(H_A):
        v = v_ref[0, pl.ds(h, ck, stride=H_A), :].astype(BF16)
        for mi in range(2):
            s = jnp.dot(q_map(h, mi), kt_ref[0, h, mi].astype(BF16), preferred_element_type=F32) + bc_ref[h, 0]
            states[2 * h + mi] = _online_update(states[2 * h + mi], s, v)
    for idx in range(2 * H_A):
        _dec_store(m_scr, l_scr, acc_scr, idx, states[idx])

    @pl.when(kc == pl.num_programs(1) - 1)
    def _():
        lam = _lam(lam_ref[...])
        for h in range(H_A):
            sl = slice(h * LANES, (h + 1) * LANES)
            k_new = kn_ref[0][:, sl].astype(BF16)
            v_new = vn_ref[0][:, sl].astype(BF16)
            final = []
            for mi in range(2):
                s = _qk(q_map(h, mi), k_new[:, mi * HEAD_DIM:(mi + 1) * HEAD_DIM]) + bn_ref[h, 0]
                final.append(_online_update(states[2 * h + mi], s, v_new))
            o_ref[0, :, sl] = _diff_finish(final, lam, gout_ref[h]).astype(BF16)


def _diff_decode(qa, ka_new, va_new, cache_kt, cache_v, bias_c, bias_n, lambda_qk, g_out_a):
    b, nq, w = qa.shape
    past = cache_kt.shape[-1]
    ck = min(CACHE_CHUNK, past)
    n_kc = past // ck
    new_spec = pl.BlockSpec((1, nq, w), lambda bi, c: (bi, 0, 0))
    return pl.pallas_call(
        functools.partial(_diff_dec_body, ck=ck),
        grid=(b, n_kc),
        in_specs=[
            _const_spec((4, HEAD_DIM)), new_spec, new_spec, new_spec,
            pl.BlockSpec((1, H_A, 2, HEAD_DIM, ck), lambda bi, c: (bi, 0, 0, 0, c)),
            pl.BlockSpec((1, H_A * ck, LANES), lambda bi, c: (bi, c, 0)),
            pl.BlockSpec((H_A, 1, nq, ck), lambda bi, c: (0, c, 0, 0)),
            _const_spec((H_A, 1, nq, nq)),
            _const_spec((H_A, 1, LANES)),
        ],
        out_specs=new_spec,
        out_shape=jax.ShapeDtypeStruct((b, nq, w), BF16),
        scratch_shapes=[pltpu.VMEM((2 * H_A, nq, 1), F32), pltpu.VMEM((2 * H_A, nq, 1), F32),
                        pltpu.VMEM((2 * H_A, nq, LANES), F32)],
        compiler_params=_cparams(("arbitrary", "arbitrary")),
        name="diff_decode",
    )(lambda_qk, qa, ka_new, va_new, cache_kt, cache_v, bias_c, bias_n, g_out_a.reshape(H_A, 1, LANES))


def _suffix_sum_lanes(x):
    n = x.shape[-1]
    lane = lax.broadcasted_iota(jnp.int32, x.shape, x.ndim - 1)
    shift = 1
    while shift < n:
        x = x + jnp.where(lane + shift < n, pltpu.roll(x, n - shift, x.ndim - 1), 0.0)
        shift *= 2
    return x


def _fox_dec_body(qk2_ref, q_ref, kn_ref, vn_ref, lfn_col_ref, lfn_row_ref, kt_ref, vt_ref, lfc_ref, gout_ref, gn_ref,
                  o_ref, cq_scr, cn_scr, after_scr, m_scr, l_scr, acc_scr, *, nq):
    kc = pl.program_id(1)
    q = q_ref[0]
    q_head = lambda h: q[:, h * HEAD_DIM:(h + 1) * HEAD_DIM]

    def updated(state, s, v, trans_v):
        m, l, acc = state
        m_new = jnp.maximum(m, jnp.max(s, axis=-1, keepdims=True))
        p = jnp.exp2(s - m_new)
        alpha = jnp.exp2(m - m_new)
        pv = _qk(p.astype(BF16), v) if trans_v else jnp.dot(p.astype(BF16), v, preferred_element_type=F32)
        return m_new, alpha * l + jnp.sum(p, axis=-1, keepdims=True), alpha * acc + pv

    n_kc = after_scr.shape[0]
    ck = after_scr.shape[-1]

    @pl.when(kc == 0)
    def _():
        _dec_init(m_scr, l_scr, acc_scr)
        r = lax.broadcasted_iota(jnp.int32, (nq, nq), 0)
        c = lax.broadcasted_iota(jnp.int32, (nq, nq), 1)
        tri_n = jnp.where(c <= r, 1.0, 0.0).astype(BF16)
        cq_scr[...] = _cumsum_rows(tri_n, lfn_col_ref[0] * LOG2E, jnp.zeros((1, LANES), F32))
        lf_row = lfn_row_ref[0] * LOG2E
        total = jnp.sum(lf_row, axis=-1, keepdims=True)
        cn_scr[...] = total - _suffix_sum_lanes(lf_row) + lf_row
        lf = lfc_ref[0] * LOG2E
        after = _suffix_sum_lanes(lf) - lf
        for c_idx in range(n_kc):
            after_scr[c_idx] = after[:, c_idx * ck:(c_idx + 1) * ck]

    after = after_scr[n_kc - 1 - kc]
    cq = cq_scr[...]
    live = jnp.max(after[:, ck - 1:ck]) + qk2_ref[0] >= EXP2_ZERO_BELOW - 1.0

    @pl.when(live)
    def _():
        states = [_dec_load(m_scr, l_scr, acc_scr, h) for h in range(H_B)]
        for h in range(H_B):
            s = jnp.dot(q_head(h), kt_ref[0, h].astype(BF16), preferred_element_type=F32)
            states[h] = updated(states[h], s + cq[:, h:h + 1] + after[h:h + 1, :], vt_ref[0, h].astype(BF16), True)
        for h in range(H_B):
            _dec_store(m_scr, l_scr, acc_scr, h, states[h])

    @pl.when(kc == pl.num_programs(1) - 1)
    def _():
        row = lax.broadcasted_iota(jnp.int32, (nq, nq), 0)
        col = lax.broadcasted_iota(jnp.int32, (nq, nq), 1)
        cn = cn_scr[...]
        states = [_dec_load(m_scr, l_scr, acc_scr, h) for h in range(H_B)]
        outs = []
        for h in range(H_B):
            sl = slice(h * HEAD_DIM, (h + 1) * HEAD_DIM)
            s = _qk(q_head(h), kn_ref[0][:, sl].astype(BF16)) + cq[:, h:h + 1] - cn[h:h + 1, 0:nq]
            _, l, acc = updated(states[h], jnp.where(col <= row, s, NEG), vn_ref[0][:, sl].astype(BF16), False)
            outs.append(acc / l)
        for p in range(H_B // 2):
            o = jnp.concatenate([outs[2 * p], outs[2 * p + 1]], axis=-1)
            o_ref[0, :, p * LANES:(p + 1) * LANES] = _group_rms(o, gout_ref[p], gn_ref[...]).astype(BF16)


def _fox_decode(qb, kb_new, vb_new, logf_new, cache_kt, cache_vt, cache_logf_t, g_qk_b, g_out_b):
    b, nq, w = qb.shape
    past = cache_kt.shape[-1]
    ck = min(CACHE_CHUNK, past)
    n_kc = past // ck
    assert nq <= LANES
    pad = jnp.zeros((b, nq, LANES - H_B), F32)
    lfn_col = jnp.concatenate([logf_new, pad], axis=-1)
    lfn_row = jnp.concatenate([jnp.transpose(logf_new, (0, 2, 1)), jnp.zeros((b, H_B, LANES - nq), F32)], axis=-1)
    new_spec = pl.BlockSpec((1, nq, w), lambda bi, c: (bi, 0, 0))
    newest_first = lambda bi, c: (bi, 0, 0, n_kc - 1 - c)
    return pl.pallas_call(
        functools.partial(_fox_dec_body, nq=nq),
        grid=(b, n_kc),
        in_specs=[
            pl.BlockSpec(memory_space=pltpu.SMEM),
            new_spec, new_spec, new_spec,
            pl.BlockSpec((1, nq, LANES), lambda bi, c: (bi, 0, 0)),
            pl.BlockSpec((1, H_B, LANES), lambda bi, c: (bi, 0, 0)),
            pl.BlockSpec((1, H_B, HEAD_DIM, ck), newest_first),
            pl.BlockSpec((1, H_B, HEAD_DIM, ck), newest_first),
            pl.BlockSpec((1, H_B, past), lambda bi, c: (bi, 0, 0)),
            _const_spec((H_B // 2, 1, LANES)), _const_spec((LANES, LANES)),
        ],
        out_specs=new_spec,
        out_shape=jax.ShapeDtypeStruct((b, nq, w), BF16),
        scratch_shapes=[pltpu.VMEM((nq, LANES), F32), pltpu.VMEM((H_B, LANES), F32),
                        pltpu.VMEM((n_kc, H_B, ck), F32),
                        pltpu.VMEM((H_B, nq, 1), F32), pltpu.VMEM((H_B, nq, 1), F32),
                        pltpu.VMEM((H_B, nq, HEAD_DIM), F32)],
        compiler_params=_cparams(("arbitrary", "arbitrary")),
        name="fox_decode",
    )(_fox_score_spread(g_qk_b).reshape(1), qb, kb_new, vb_new, lfn_col, lfn_row, cache_kt, cache_vt, cache_logf_t,
      g_out_b.reshape(H_B // 2, 1, LANES), _blockdiag_mean(LANES, HEAD_DIM))


def _route(logits):
    lane_i = lax.broadcasted_iota(jnp.int32, logits.shape, 1)
    lane = lane_i.astype(F32)
    big = float(LANES)
    lg = jnp.where(lane_i < N_GROUPS, logits, NEG)
    mx = jnp.max(lg, axis=-1, keepdims=True)
    grp = jnp.min(jnp.where(lg == mx, lane, big), axis=-1, keepdims=True)
    p_grp = 1.0 / jnp.sum(jnp.exp(lg - mx), axis=-1, keepdims=True)
    e = lane_i - ROUTER_LANE0
    e_grp = lax.shift_right_arithmetic(e, 3).astype(F32)
    sel = (e >= 0) & (e < N_EXPERTS) & (e_grp == grp)
    v = jnp.where(sel, logits, NEG)
    v1 = jnp.max(v, axis=-1, keepdims=True)
    i1 = jnp.min(jnp.where(sel & (v == v1), lane, big), axis=-1, keepdims=True)
    sel2 = sel & (lane != i1)
    vv = jnp.where(sel2, logits, NEG)
    v2 = jnp.max(vv, axis=-1, keepdims=True)
    i2 = jnp.min(jnp.where(sel2 & (vv == v2), lane, big), axis=-1, keepdims=True)
    e2 = jnp.exp(v2 - v1)
    w1 = p_grp / (1.0 + e2)
    w2 = p_grp * e2 / (1.0 + e2)
    gates = jnp.where(lane == i1, w1, 0.0) + jnp.where(lane == i2, w2, 0.0)
    return gates, (i1, i2, w1, w2)


R_E1, R_E2, R_RANK1, R_RANK2, R_W1, R_W2 = range(6)


def _mix_body(oa_ref, ob_ref, x_ref, wa_ref, wb_ref, g2_ref, wr1_ref, wr2_ref, br_ref, tri_ref,
              x1_ref, xn_ref, gates_ref, route_ref, counts_ref):
    y = (jnp.dot(oa_ref[...], wa_ref[...], preferred_element_type=F32)
         + jnp.dot(ob_ref[...], wb_ref[...], preferred_element_type=F32))
    x1 = x_ref[...] + y
    x1_ref[...] = x1
    ms = jnp.mean(x1 * x1, axis=-1, keepdims=True)
    xn = (x1 * lax.rsqrt(ms + RMS_EPS)) * g2_ref[...]
    xn_ref[...] = xn
    h1 = xn.astype(BF16)
    h2 = (xn - h1.astype(F32)).astype(BF16)
    logits = (jnp.dot(h1, wr1_ref[...], preferred_element_type=F32)
              + jnp.dot(h1, wr2_ref[...], preferred_element_type=F32)
              + jnp.dot(h2, wr1_ref[...], preferred_element_type=F32)) + br_ref[...]
    gates, (i1, i2, w1, w2) = _route(logits)
    gates_ref[...] = gates

    @pl.when(pl.program_id(0) == 0)
    def _():
        counts_ref[...] = jnp.zeros_like(counts_ref)

    lane_i = lax.broadcasted_iota(jnp.int32, gates.shape, 1)
    lane = lane_i.astype(F32)
    oh1 = jnp.where(lane == i1, 1.0, 0.0)
    oh2 = jnp.where(lane == i2, 1.0, 0.0)
    comb = oh1 + oh2
    running = counts_ref[0:1, :]
    before = jnp.dot(tri_ref[...], comb.astype(BF16), preferred_element_type=F32) + running
    rank1 = jnp.sum(before * oh1, axis=-1, keepdims=True)
    rank2 = jnp.sum(before * oh2, axis=-1, keepdims=True)
    counts_ref[0:1, :] = running + jnp.sum(comb, axis=0, keepdims=True)
    rec = jnp.zeros_like(gates)
    for idx, val in ((R_E1, i1 - ROUTER_LANE0), (R_E2, i2 - ROUTER_LANE0), (R_RANK1, rank1), (R_RANK2, rank2),
                     (R_W1, w1), (R_W2, w2)):
        rec = jnp.where(lane_i == idx, val, rec)
    route_ref[...] = rec


def _mix_and_route(o_a, o_b, x, w_out, g_norm, w_rg, b_rg, w_re, b_re):
    n, d = x.shape
    bt = min(TOKEN_TILE, n)
    r = jnp.arange(bt)
    tri_strict = (r[None, :] < r[:, None]).astype(BF16)
    wa = w_out[:GROUP_W].astype(BF16)
    wb = w_out[GROUP_W:].astype(BF16)
    n_r = N_GROUPS + N_EXPERTS
    wr = jnp.concatenate([w_rg, w_re, jnp.zeros((d, LANES - n_r), F32)], axis=1)
    wr1 = wr.astype(BF16)
    wr2 = (wr - wr1.astype(F32)).astype(BF16)
    br = jnp.concatenate([b_rg, b_re, jnp.zeros((LANES - n_r,), F32)])[None, :]
    row = lambda width: pl.BlockSpec((bt, width), lambda i: (i, 0))
    return pl.pallas_call(
        _mix_body,
        grid=(n // bt,),
        in_specs=[row(GROUP_W), row(GROUP_W), row(d), _const_spec((GROUP_W, d)), _const_spec((GROUP_W, d)),
                  _const_spec((1, d)), _const_spec((d, LANES)), _const_spec((d, LANES)), _const_spec((1, LANES)),
                  _const_spec((bt, bt))],
        out_specs=[row(d), row(d), row(LANES), row(LANES), _const_spec((8, LANES))],
        out_shape=[jax.ShapeDtypeStruct((n, d), F32), jax.ShapeDtypeStruct((n, d), F32),
                   jax.ShapeDtypeStruct((n, LANES), F32), jax.ShapeDtypeStruct((n, LANES), F32),
                   jax.ShapeDtypeStruct((8, LANES), F32)],
        compiler_params=_cparams(("arbitrary",)),
        name="mix_route",
    )(o_a, o_b, x, wa, wb, g_norm[None, :], wr1, wr2, br, tri_strict)


def _swiglu(xn, wg, wu, wd, gate=None):
    x = xn.astype(BF16)
    g = jnp.dot(x, wg, preferred_element_type=F32)
    u = jnp.dot(x, wu, preferred_element_type=F32)
    h = (g * jax.nn.sigmoid(g)) * u
    if gate is not None:
        h = h * gate
    return jnp.dot(h.astype(BF16), wd, preferred_element_type=F32)


def _expert_body(xn_ref, x1_ref, gates_ref, wg_ref, wu_ref, wd_ref, o_ref):
    e = pl.program_id(1)

    @pl.when(e == 0)
    def _():
        o_ref[...] = x1_ref[...]

    gates = gates_ref[...]
    lane = lax.broadcasted_iota(jnp.int32, gates.shape, 1)
    gate = jnp.sum(jnp.where(lane == e + ROUTER_LANE0, gates, 0.0), axis=-1, keepdims=True)
    o_ref[...] += _swiglu(xn_ref[...], wg_ref[0], wu_ref[0], wd_ref[0], gate)


def _experts(xn, x1, gates, w_gate, w_up, w_down):
    n, d = x1.shape
    ff = w_gate.shape[-1]
    bt = min(MOE_TILE, n)
    row = lambda width: pl.BlockSpec((bt, width), lambda i, e: (i, 0))
    return pl.pallas_call(
        _expert_body,
        grid=(n // bt, N_EXPERTS),
        in_specs=[row(d), row(d), row(LANES),
                  pl.BlockSpec((1, d, ff), lambda i, e: (e, 0, 0)),
                  pl.BlockSpec((1, d, ff), lambda i, e: (e, 0, 0)),
                  pl.BlockSpec((1, ff, d), lambda i, e: (e, 0, 0))],
        out_specs=row(d),
        out_shape=jax.ShapeDtypeStruct((n, d), F32),
        compiler_params=_cparams(("arbitrary", "arbitrary")),
        name="experts",
    )(xn, x1, gates, w_gate, w_up, w_down)


def _row_copies(n_rows, make_copy):
    def issue(r, carry):
        for s in range(2):
            make_copy(r, s).start()
        return carry

    lax.fori_loop(0, n_rows, issue, 0, unroll=8)

    def drain(r, carry):
        for s in range(2):
            make_copy(r, s).wait()
        return carry

    lax.fori_loop(0, n_rows, drain, 0, unroll=8)


def _row_position_body(route_ref, base_ref, pos_ref):
    rec = route_ref[...]
    lane_i = lax.broadcasted_iota(jnp.int32, rec.shape, 1)
    lane = lane_i.astype(F32)
    out = jnp.zeros(rec.shape, F32)
    for slot, (e_lane, r_lane) in enumerate(((R_E1, R_RANK1), (R_E2, R_RANK2))):
        onehot = lane == rec[:, e_lane:e_lane + 1] + float(ROUTER_LANE0)
        base = jnp.sum(jnp.where(onehot, base_ref[...], 0.0), axis=-1, keepdims=True)
        out = jnp.where(lane_i == slot, base + rec[:, r_lane:r_lane + 1], out)
    pos_ref[...] = out.astype(jnp.int32)


def _dispatch_body(last_tile_ref, pos_ref, x_ref, xs_ref, zero_scr, sem):
    @pl.when(pl.program_id(0) == 0)
    def _():
        zero_scr[...] = jnp.zeros_like(zero_scr)
        tm = zero_scr.shape[0]
        fills = [pltpu.make_async_copy(zero_scr, xs_ref.at[pl.ds(pl.multiple_of(last_tile_ref[e], tm), tm)], sem)
                 for e in range(N_EXPERTS)]
        for f in fills:
            f.start()
        for f in fills:
            f.wait()

        def fill_unused(t, carry):
            f = pltpu.make_async_copy(zero_scr, xs_ref.at[pl.ds(pl.multiple_of(t * tm, tm), tm)], sem)
            f.start()
            f.wait()
            return carry

        lax.fori_loop(last_tile_ref[N_EXPERTS], xs_ref.shape[0] // tm, fill_unused, 0)

    _row_copies(x_ref.shape[0], lambda r, s: pltpu.make_async_copy(
        x_ref.at[pl.ds(r, 1)], xs_ref.at[pl.ds(pos_ref[0, 0, 2 * r + s], 1)], sem))


def _grouped_body(te_ref, nu_ref, xs_ref, wg_ref, wu_ref, wd_ref, ys_ref):
    del te_ref
    used = pl.program_id(0) < nu_ref[0]

    @pl.when(used)
    def _():
        ys_ref[...] = _swiglu(xs_ref[...], wg_ref[0], wu_ref[0], wd_ref[0])

    @pl.when(jnp.logical_not(used))
    def _():
        ys_ref[...] = jnp.zeros_like(ys_ref)


def _combine_body(pos_ref, route_ref, x1_ref, ys_ref, o_ref, buf_scr, sem):
    _row_copies(x1_ref.shape[0], lambda r, s: pltpu.make_async_copy(
        ys_ref.at[pl.ds(pos_ref[0, 0, 2 * r + s], 1)], buf_scr.at[s, pl.ds(r, 1)], sem))
    rec = route_ref[...]
    o_ref[...] = (x1_ref[...] + rec[:, R_W1:R_W1 + 1] * buf_scr[0] + rec[:, R_W2:R_W2 + 1] * buf_scr[1])


def _routed_experts(xn, x1, route, counts, w_gate, w_up, w_down):
    n, d = x1.shape
    ff = w_gate.shape[-1]
    bt = min(TOKEN_TILE, n)
    nt = n // bt
    tm = MOE_ROW_TILE
    n_tiles = (2 * n) // tm + N_EXPERTS
    cnt = counts[0, ROUTER_LANE0:ROUTER_LANE0 + N_EXPERTS].astype(jnp.int32)
    tiles = (cnt + tm - 1) // tm
    tile_end = jnp.cumsum(tiles)
    base_row = (tile_end - tiles) * tm
    n_used = tile_end[-1:]
    tile_expert = jnp.minimum(jnp.sum(jnp.arange(n_tiles)[:, None] >= tile_end[None, :], axis=1), N_EXPERTS - 1)
    base_lanes = jnp.zeros((1, LANES), F32).at[0, ROUTER_LANE0:ROUTER_LANE0 + N_EXPERTS].set(base_row.astype(F32))
    row = lambda width: pl.BlockSpec((bt, width), lambda i: (i, 0))
    pos = pl.pallas_call(
        _row_position_body,
        grid=(nt,),
        in_specs=[row(LANES), _const_spec((1, LANES))],
        out_specs=row(LANES),
        out_shape=jax.ShapeDtypeStruct((n, LANES), jnp.int32),
        compiler_params=_cparams(("arbitrary",)),
        name="moe_positions",
    )(route, base_lanes)
    pos = pos[:, :2].reshape(nt, 1, 2 * bt)

    pos_spec = pl.BlockSpec((1, 1, 2 * bt), lambda i: (i, 0, 0), memory_space=pltpu.SMEM)
    any_spec = pl.BlockSpec(memory_space=pl.ANY)
    last_tile = jnp.minimum(base_row + jnp.maximum(tiles - 1, 0) * tm, (n_tiles - 1) * tm)
    last_tile = jnp.concatenate([last_tile, n_used]).astype(jnp.int32)
    xs = pl.pallas_call(
        _dispatch_body,
        grid_spec=pltpu.PrefetchScalarGridSpec(
            num_scalar_prefetch=1, grid=(nt,),
            in_specs=[pl.BlockSpec((1, 1, 2 * bt), lambda i, lt: (i, 0, 0), memory_space=pltpu.SMEM),
                      pl.BlockSpec((bt, d), lambda i, lt: (i, 0))],
            out_specs=any_spec,
            scratch_shapes=[pltpu.VMEM((tm, d), F32), pltpu.SemaphoreType.DMA(())]),
        out_shape=jax.ShapeDtypeStruct((n_tiles * tm, d), F32),
        compiler_params=_cparams(("arbitrary",)),
        name="moe_dispatch",
    )(last_tile, pos, xn)

    w_spec = lambda shape: pl.BlockSpec(shape, lambda t, te, nu: (te[t], 0, 0))
    ys = pl.pallas_call(
        _grouped_body,
        grid_spec=pltpu.PrefetchScalarGridSpec(
            num_scalar_prefetch=2, grid=(n_tiles,),
            in_specs=[pl.BlockSpec((tm, d), lambda t, te, nu: (jnp.minimum(t, nu[0] - 1), 0)),
                      w_spec((1, d, ff)), w_spec((1, d, ff)), w_spec((1, ff, d))],
            out_specs=pl.BlockSpec((tm, d), lambda t, te, nu: (t, 0))),
        out_shape=jax.ShapeDtypeStruct((n_tiles * tm, d), F32),
        compiler_params=_cparams(("arbitrary",)),
        name="moe_experts",
    )(tile_expert.astype(jnp.int32), n_used.astype(jnp.int32), xs, w_gate, w_up, w_down)

    return pl.pallas_call(
        _combine_body,
        grid=(nt,),
        in_specs=[pos_spec, row(LANES), row(d), any_spec],
        out_specs=row(d),
        out_shape=jax.ShapeDtypeStruct((n, d), F32),
        scratch_shapes=[pltpu.VMEM((2, bt, d), F32), pltpu.SemaphoreType.DMA(())],
        compiler_params=_cparams(("arbitrary",)),
        name="moe_combine",
    )(pos, route, x1, ys)


def _ffn(o_a, o_b, x, w_out, g_norm_ffn, w_rg, b_rg, w_re, b_re, wg, wu, wd):
    b, t, d = x.shape
    n = b * t
    x1, xn, gates, route, counts = _mix_and_route(o_a.reshape(n, -1), o_b.reshape(n, -1), x.reshape(n, d), w_out,
                                                  g_norm_ffn, w_rg, b_rg, w_re, b_re)
    if 2 * n >= ROUTED_MIN_ASSIGNMENTS:
        y = _routed_experts(xn, x1, route, counts, wg, wu, wd)
    else:
        y = _experts(xn, x1, gates, wg, wu, wd)
    return y.reshape(b, t, d)


def kernel(x_prompt, x_sample, cache_a_k, cache_a_v, cache_b_k, cache_b_v, cache_b_logf, g_norm_mix, w_in, b_forget, g_qk_a, g_qk_b, lambda_qk, g_out_a, g_out_b, w_out, rel_bias, g_norm_ffn, w_router_group, b_router_group, w_router_expert, b_router_expert, w_exp_gate, w_exp_up, w_exp_down):
    depth = w_in.shape[0]
    assert depth == 1, "single-layer step only"
    bp, tp, d = x_prompt.shape
    bs, ts, _ = x_sample.shape
    past = cache_a_k.shape[2]
    w_in0, w_out0 = w_in[0], w_out[0]
    wg, wu, wd = w_exp_gate[0].astype(BF16), w_exp_up[0].astype(BF16), w_exp_down[0].astype(BF16)
    ffn_w = (w_out0, g_norm_ffn[0], w_router_group[0], b_router_group[0], w_router_expert[0], b_router_expert[0],
             wg, wu, wd)

    (ka_p, va_p, kb_p, vb_p, logf_p, qa_t, ka_bf, va_t, qb_t, kb_aug, vb_t, c_edge) = _projection(
        x_prompt, g_norm_mix[0], w_in0, b_forget[0], g_qk_a[0], g_qk_b[0], with_aug=True)
    blk = min(ATTN_BLOCK, tp)
    bk = va_t.shape[-1]
    assert blk % CHUNK == 0 and blk >= MAX_DISTANCE and blk % bk == 0
    q_pos = blk + jnp.arange(blk, dtype=jnp.int32)
    bkt_p = jnp.stack([_bucket_map(q_pos, s * bk + jnp.arange(bk, dtype=jnp.int32)).T
                       for s in range(2 * blk // bk)] + [jnp.full((bk, blk), -1, jnp.int32)] * (blk // bk))
    bias_p = _bias_tiles(rel_bias, bkt_p)
    o_a = _diff_attention(qa_t, ka_bf, va_t, bias_p, lambda_qk[0], g_out_a[0])
    o_b = _fox_attention(qb_t, kb_aug, vb_t, c_edge, g_qk_b[0], g_out_b[0])
    y_p = _ffn(o_a, o_b, x_prompt, *ffn_w)

    xs = x_sample.reshape(1, bs * ts, d)
    (ka_s, va_s, kb_s, vb_s, logf_s, qa_s, qb_s) = _projection(
        xs, g_norm_mix[0], w_in0, b_forget[0], g_qk_a[0], g_qk_b[0], with_aug=False)
    per_stream = lambda a: a.reshape(bs, ts, a.shape[-1])
    ka_s, va_s, kb_s, vb_s, logf_s, qa_s, qb_s = map(per_stream, (ka_s, va_s, kb_s, vb_s, logf_s, qa_s, qb_s))
    ck = min(CACHE_CHUNK, past)
    q_pos = past + jnp.arange(ts, dtype=jnp.int32)
    bkt_c = _bucket_map(q_pos, jnp.arange(past, dtype=jnp.int32)).reshape(ts, past // ck, ck).transpose(1, 0, 2)
    bias_c = _bias_tiles(rel_bias, bkt_c)
    bias_n = _bias_tiles(rel_bias, _bucket_map(q_pos, q_pos)[None])
    o_a_s = _diff_decode(qa_s, ka_s, va_s, jnp.transpose(cache_a_k[0], (0, 2, 3, 4, 1)),
                         cache_a_v[0].reshape(bs, past * H_A, 2 * HEAD_DIM), bias_c, bias_n, lambda_qk[0], g_out_a[0])
    o_b_s = _fox_decode(qb_s, kb_s, vb_s, logf_s, jnp.transpose(cache_b_k[0], (0, 2, 3, 1)),
                        jnp.transpose(cache_b_v[0], (0, 2, 3, 1)), jnp.transpose(cache_b_logf[0], (0, 2, 1)),
                        g_qk_b[0], g_out_b[0])
    y_s = _ffn(o_a_s, o_b_s, x_sample, *ffn_w)

    def rows(ka, va, kb, vb, logf, b, t):
        return (ka.reshape(1, b, t, H_A, 2, HEAD_DIM), va.reshape(1, b, t, H_A, 2 * HEAD_DIM),
                kb.reshape(1, b, t, H_B, HEAD_DIM), vb.reshape(1, b, t, H_B, HEAD_DIM), logf.reshape(1, b, t, H_B))

    return (y_p, y_s) + rows(ka_p, va_p, kb_p, vb_p, logf_p, bp, tp) + rows(ka_s, va_s, kb_s, vb_s, logf_s, bs, ts)
```

```python
import functools
import math

import jax
import jax.numpy as jnp
from jax import lax
from jax.experimental import pallas as pl
from jax.experimental.pallas import tpu as pltpu

F32 = jnp.float32
BF16 = jnp.bfloat16

LANES = 128
VMEM_LIMIT_BYTES = 56 * 1024 * 1024

HEAD_DIM = 64
H_A = 4
H_B = 8
GROUP_W = 512
MAIN_W = 6 * GROUP_W
CHUNK = 64
N_BUCKETS = 32
MAX_DISTANCE = 128
N_GROUPS = 4
EXPERTS_PER_GROUP = 8
N_EXPERTS = N_GROUPS * EXPERTS_PER_GROUP
ROUTER_LANE0 = N_GROUPS
RMS_EPS = 1e-6
NEG = -1e30
LOG2E = 1.4426950408889634
QK_SCALE = HEAD_DIM ** -0.5
LAM_INIT = 0.8 - 0.6 * math.exp(-0.3 * 0)
N_CPARTS = 3

ONES_ROWS = 16
VA_ROWS = 2 * HEAD_DIM + ONES_ROWS
VB_ROWS = HEAD_DIM + ONES_ROWS

TOKEN_TILE = 512
ATTN_BLOCK = 512
KV_BLOCK = 256
FAR_UNROLL = 4
CACHE_CHUNK = 1024
MOE_TILE = 1024
MOE_ROW_TILE = 256
ROUTED_MIN_ASSIGNMENTS = 4 * N_EXPERTS * MOE_ROW_TILE


def _cparams(sem):
    return pltpu.CompilerParams(dimension_semantics=sem, vmem_limit_bytes=VMEM_LIMIT_BYTES)


def _const_spec(shape):
    nd = len(shape)
    return pl.BlockSpec(shape, lambda *_: (0,) * nd)


def _split3(x):
    p1 = x.astype(BF16).astype(F32)
    r1 = x - p1
    p2 = r1.astype(BF16).astype(F32)
    p3 = (r1 - p2).astype(BF16).astype(F32)
    return p1, p2, p3


def _lane_groups(parts, lane):
    return jnp.where(lane < 8, parts[0], jnp.where(lane < 16, parts[1], parts[2]))


def _cumsum_rows(tri, x, carry):
    c = carry
    for part in _split3(x):
        c = c + jnp.dot(tri, part.astype(BF16), preferred_element_type=F32)
    return c


def _log_sigmoid(x):
    return jnp.minimum(x, 0.0) - jnp.log(1.0 + jnp.exp(-jnp.abs(x)))


def _group_rms(raw, gain_row, gn):
    ms = jnp.dot((raw * raw).astype(BF16), gn, preferred_element_type=F32)
    return raw * lax.rsqrt(ms + RMS_EPS) * gain_row


def _online_update(state, s, v):
    m, l, acc = state
    m_new = jnp.maximum(m, jnp.max(s, axis=-1, keepdims=True))
    p = jnp.exp2(s - m_new)
    alpha = jnp.exp2(m - m_new)
    l_new = alpha * l + jnp.sum(p, axis=-1, keepdims=True)
    acc_new = alpha * acc + jnp.dot(p.astype(BF16), v, preferred_element_type=F32)
    return m_new, l_new, acc_new


def _qk(q, k):
    return lax.dot_general(q, k, (((1,), (1,)), ((), ())), preferred_element_type=F32)


def _proj_body(*refs, with_aug, bt, bk):
    (x_ref, g1_ref, wm_ref, wf_ref, bf_ref, gains_ref, gn_ref, tri_ref, place_ref) = refs[:9]
    ka_ref, va_ref, kb_ref, vb_ref, logf_ref = refs[9:14]
    x = x_ref[0]
    ms = jnp.mean(x * x, axis=-1, keepdims=True)
    xn = (x * lax.rsqrt(ms + RMS_EPS)) * g1_ref[...]
    xb = xn.astype(BF16)
    proj = jnp.dot(xb, wm_ref[...], preferred_element_type=F32)
    gains = gains_ref[...]
    gn = gn_ref[...]
    w = GROUP_W
    qa = _group_rms(proj[:, 0:w], gains[0:1], gn)
    ka = _group_rms(proj[:, w:2 * w], gains[1:2], gn)
    va = proj[:, 2 * w:3 * w]
    qb = _group_rms(proj[:, 3 * w:4 * w], gains[2:3], gn)
    kb = _group_rms(proj[:, 4 * w:5 * w], gains[3:4], gn)
    vb = proj[:, 5 * w:6 * w]
    ka_ref[0] = ka
    if with_aug:
        for h in range(H_A):
            va_ref[0, pl.ds(h, bt, stride=H_A), :] = va[:, h * LANES:(h + 1) * LANES]
    else:
        va_ref[0] = va
    kb_ref[0] = kb
    vb_ref[0] = vb
    fl = jnp.dot(xb, wf_ref[...], preferred_element_type=F32) + bf_ref[...]
    logf = _log_sigmoid(fl)
    logf_ref[0] = logf[:, 0:H_B]
    qscale = QK_SCALE * LOG2E
    if not with_aug:
        qa_ref, qb_ref = refs[14:16]
        qa_ref[0] = (qa * qscale).astype(BF16)
        qb_ref[0] = (qb * qscale).astype(BF16)
        return
    qa_t, ka_bf, va_t, qb_t, kb_aug, vb_t, cedge_ref, carry_ref = refs[14:22]
    n_chunk = bt // bk
    row = lax.broadcasted_iota(jnp.int32, (LANES, bt), 0)
    ones_tail = jnp.where(lax.broadcasted_iota(jnp.int32, (ONES_ROWS, bt), 0) == 0, 1.0, 0.0)

    def put_chunks(ref, idx, vt):
        vt = vt.astype(BF16)
        for c in range(n_chunk):
            ref[0, idx, c] = vt[:, c * bk:(c + 1) * bk]

    for h in range(H_A):
        sl = slice(h * LANES, (h + 1) * LANES)
        q_t = (qa[:, sl] * qscale).T
        qa_t[0, 2 * h] = jnp.where(row < HEAD_DIM, q_t, 0.0).astype(BF16)
        qa_t[0, 2 * h + 1] = jnp.where(row >= HEAD_DIM, q_t, 0.0).astype(BF16)
        ka_bf[0, h] = ka[:, sl].astype(BF16)
        put_chunks(va_t, h, jnp.concatenate([va[:, sl].T, ones_tail], axis=0))
        vb_pair_t = vb[:, sl].T
        for hh in range(2):
            put_chunks(vb_t, 2 * h + hh,
                       jnp.concatenate([vb_pair_t[hh * HEAD_DIM:(hh + 1) * HEAD_DIM], ones_tail], axis=0))

    @pl.when(pl.program_id(1) == 0)
    def _():
        carry_ref[...] = jnp.zeros_like(carry_ref)

    c = _cumsum_rows(tri_ref[...], logf * LOG2E, carry_ref[0:1, :])
    carry_ref[0:1, :] = c[bt - 1:bt, :]
    edge_row = lax.broadcasted_iota(jnp.int32, (8, LANES), 0)
    cedge_ref[0, 0] = jnp.where(edge_row == 0, c[0:1, :], jnp.where(edge_row == 1, c[bt - 1:bt, :], 0.0))
    lane = lax.broadcasted_iota(jnp.int32, (bt, LANES), 1)
    cparts = _lane_groups(_split3(c), lane).astype(BF16)
    extras = jnp.dot(cparts, place_ref[...], preferred_element_type=F32)
    ones_q = jnp.where((lane >= HEAD_DIM + N_CPARTS) & (lane < HEAD_DIM + 2 * N_CPARTS), 1.0, 0.0)
    ones_k = jnp.where((lane >= HEAD_DIM) & (lane < HEAD_DIM + N_CPARTS), 1.0, 0.0)
    for h in range(H_B):
        sl = slice((h // 2) * LANES, (h // 2 + 1) * LANES)
        qp = qb[:, sl] * qscale
        kp = kb[:, sl]
        if h % 2:
            qp = pltpu.roll(qp, HEAD_DIM, 1)
            kp = pltpu.roll(kp, HEAD_DIM, 1)
        eq = extras[:, h * LANES:(h + 1) * LANES] + ones_q
        ek = extras[:, (H_B + h) * LANES:(H_B + h + 1) * LANES] + ones_k
        qb_t[0, h] = jnp.where(lane < HEAD_DIM, qp, eq).T.astype(BF16)
        kb_aug[0, h] = jnp.where(lane < HEAD_DIM, kp, ek).astype(BF16)


def _blockdiag_mean(n, group):
    r = jnp.arange(n)
    return jnp.where((r[:, None] // group) == (r[None, :] // group), 1.0 / group, 0.0).astype(BF16)


def _tri(n):
    r = jnp.arange(n)
    return (r[None, :] <= r[:, None]).astype(BF16)


def _place_matrix():
    rows = jnp.arange(LANES)[:, None]
    cols = jnp.arange(2 * H_B * LANES)[None, :]
    p, h = rows // 8, rows % 8
    valid = rows < 8 * N_CPARTS
    qcol = h * LANES + HEAD_DIM + p
    kcol = (H_B + h) * LANES + HEAD_DIM + N_CPARTS + p
    m = jnp.where(valid & (cols == qcol), 1.0, 0.0) - jnp.where(valid & (cols == kcol), 1.0, 0.0)
    return m.astype(BF16)


def _projection(x, g_norm, w_in, b_forget, g_qk_a, g_qk_b, *, with_aug):
    b, t, d = x.shape
    bt = min(TOKEN_TILE, t)
    nt = t // bt
    wm = w_in[:, :MAIN_W].astype(BF16)
    wf_cols = w_in[:, MAIN_W:MAIN_W + H_B]
    wf = jnp.concatenate([wf_cols] * N_CPARTS + [jnp.zeros((d, LANES - H_B * N_CPARTS), F32)], axis=1).astype(BF16)
    bfv = jnp.concatenate([b_forget] * N_CPARTS + [jnp.zeros((LANES - H_B * N_CPARTS,), F32)])[None, :]
    gains = jnp.stack([jnp.tile(g_qk_a[0], 2 * H_A), jnp.tile(g_qk_a[1], 2 * H_A),
                       jnp.tile(g_qk_b[0], H_B), jnp.tile(g_qk_b[1], H_B)])
    gn = _blockdiag_mean(GROUP_W, HEAD_DIM)
    tri = _tri(bt)
    place = _place_matrix()
    in_specs = [
        pl.BlockSpec((1, bt, d), lambda i, j: (i, j, 0)),
        _const_spec((1, d)),
        pl.BlockSpec((d, MAIN_W), lambda i, j: (0, 0), pipeline_mode=pl.Buffered(1)),
        _const_spec((d, LANES)), _const_spec((1, LANES)), _const_spec((4, GROUP_W)),
        _const_spec((GROUP_W, GROUP_W)), _const_spec((bt, bt)), _const_spec((LANES, 2 * H_B * LANES)),
    ]
    row_spec = pl.BlockSpec((1, bt, GROUP_W), lambda i, j: (i, j, 0))
    out_shape = [jax.ShapeDtypeStruct((b, t, GROUP_W), F32)] * 4 + [jax.ShapeDtypeStruct((b, t, H_B), F32)]
    out_specs = [row_spec] * 4 + [pl.BlockSpec((1, bt, H_B), lambda i, j: (i, j, 0))]
    scratch = []
    bk = min(KV_BLOCK, bt)
    if with_aug:
        out_shape[1] = jax.ShapeDtypeStruct((b, t * H_A, 2 * HEAD_DIM), F32)
        out_specs[1] = pl.BlockSpec((1, bt * H_A, 2 * HEAD_DIM), lambda i, j: (i, j, 0))

        def add(shape, block, index_map):
            out_shape.append(jax.ShapeDtypeStruct(shape, BF16))
            out_specs.append(pl.BlockSpec(block, index_map))

        rows_major = lambda i, j: (i, 0, j, 0)
        time_minor = lambda i, j: (i, 0, 0, j)
        chunked = lambda i, j: (i, 0, j, 0, 0)
        add((b, 2 * H_A, LANES, t), (1, 2 * H_A, LANES, bt), time_minor)
        add((b, H_A, t, LANES), (1, H_A, bt, LANES), rows_major)
        add((b, H_A, t // bk, VA_ROWS, bk), (1, H_A, bt // bk, VA_ROWS, bk), chunked)
        add((b, H_B, LANES, t), (1, H_B, LANES, bt), time_minor)
        add((b, H_B, t, LANES), (1, H_B, bt, LANES), rows_major)
        add((b, H_B, t // bk, VB_ROWS, bk), (1, H_B, bt // bk, VB_ROWS, bk), chunked)
        out_shape.append(jax.ShapeDtypeStruct((b, nt, 8, LANES), F32))
        out_specs.append(pl.BlockSpec((1, 1, 8, LANES), lambda i, j: (i, j, 0, 0)))
        scratch = [pltpu.VMEM((8, LANES), F32)]
    else:
        out_shape += [jax.ShapeDtypeStruct((b, t, GROUP_W), BF16)] * 2
        out_specs += [row_spec] * 2
    return pl.pallas_call(
        functools.partial(_proj_body, with_aug=with_aug, bt=bt, bk=bk),
        grid=(b, nt), in_specs=in_specs, out_specs=out_specs, out_shape=out_shape, scratch_shapes=scratch,
        compiler_params=_cparams(("arbitrary", "arbitrary")),
        name="proj_aug" if with_aug else "proj_plain",
    )(x, g_norm[None, :], wm, wf, bfv, gains, gn, tri, place)


def _t5_bucket(rel):
    nb = N_BUCKETS // 2
    max_exact = nb // 2
    base = jnp.where(rel > 0, nb, 0)
    n = jnp.abs(rel)
    large = max_exact + (jnp.log(jnp.maximum(n, max_exact).astype(jnp.float32) / max_exact)
                         / math.log(MAX_DISTANCE / max_exact) * (nb - max_exact)).astype(jnp.int32)
    large = jnp.minimum(large, nb - 1)
    return base + jnp.where(n < max_exact, n, large)


def _bucket_map(q_pos, k_pos):
    bkt = _t5_bucket(k_pos[None, :] - q_pos[:, None])
    visible = (k_pos[None, :] // CHUNK) <= (q_pos[:, None] // CHUNK)
    return jnp.where(visible, bkt, -1).astype(jnp.int32)


def _bias_body(rb_ref, bkt_ref, o_ref):
    h = pl.program_id(0)
    bkt = bkt_ref[0]
    far = rb_ref[N_BUCKETS // 2 - 1, h]
    acc = jnp.zeros(bkt.shape, F32)
    for b in range(N_BUCKETS):
        acc = jnp.where(bkt == b, rb_ref[b, h] - far, acc)
    o_ref[0, 0] = jnp.where(bkt < 0, NEG, acc * LOG2E)


def _bias_tiles(rel_bias, bkt):
    n, r, c = bkt.shape
    return pl.pallas_call(
        _bias_body,
        grid=(H_A, n),
        in_specs=[pl.BlockSpec(memory_space=pltpu.SMEM), pl.BlockSpec((1, r, c), lambda h, i: (i, 0, 0))],
        out_specs=pl.BlockSpec((1, 1, r, c), lambda h, i: (h, i, 0, 0)),
        out_shape=jax.ShapeDtypeStruct((H_A, n, r, c), F32),
        compiler_params=_cparams(("arbitrary", "arbitrary")),
        name="bias_tiles",
    )(rel_bias, bkt)


def _lam(lq):
    a = jnp.sum(lq[0:1, :] * lq[1:2, :], axis=-1, keepdims=True)
    b = jnp.sum(lq[2:3, :] * lq[3:4, :], axis=-1, keepdims=True)
    return jnp.exp(a) - jnp.exp(b) + LAM_INIT


def _attn_init_t(rows, bq):
    return (jnp.full((1, bq), NEG, F32), jnp.zeros((rows, bq), F32))


def _online_update_t(state, st, vt):
    m, acc = state
    m_new = jnp.maximum(m, jnp.max(st, axis=0, keepdims=True))
    p = jnp.exp2(st - m_new)
    alpha = jnp.exp2(m - m_new)
    return m_new, alpha * acc + jnp.dot(vt, p.astype(BF16), preferred_element_type=F32)


def _normalized_t(state, rows):
    acc = state[1]
    return acc[0:rows] / acc[rows:rows + 1]


def _pipelined_sweep(i, n_sub, n_near, slots, score_fn, value_fn, modify, rows, blk, first=0):
    assert n_sub % 2 == 0 and n_near in (1, 2)

    def run_block(jb, states, near, last, next_start=None):
        states = list(states)
        for s in range(n_sub):
            j = jb * n_sub + s
            cur, nxt = slots[s % 2], slots[(s + 1) % 2]
            ahead = next_start if (next_start is not None and s == n_sub - 1) else j + 1
            for c in range(2):
                if not (last and s == n_sub - 1):
                    nxt[c] = score_fn(c, ahead)
                st = cur[c]
                if near is not None:
                    st = modify(st, near, s)
                states[c] = _online_update_t(states[c], st, value_fn(c, j))
        return tuple(states)

    for c in range(2):
        slots[0][c] = score_fn(c, first * n_sub)
    states = (_attn_init_t(rows, blk), _attn_init_t(rows, blk))
    n_far = jnp.maximum(i + 1 - n_near, 0)

    def run_far(jb, states, n_blocks):
        for d in range(n_blocks):
            states = run_block(jb + d, states, None, False)
        return states

    n_groups = jnp.maximum(n_far - first, 0) // FAR_UNROLL
    states = lax.fori_loop(0, n_groups, lambda g, st: run_far(first + g * FAR_UNROLL, st, FAR_UNROLL), states)
    states = lax.fori_loop(first + n_groups * FAR_UNROLL, n_far, lambda jb, st: run_far(jb, st, 1), states)
    if n_near == 2:
        states = run_block(jnp.maximum(i - 1, 0), states, 1, False, next_start=i * n_sub)
    return run_block(i, states, 0, True)


def _fox_body(first_ref, qt_ref, k_ref, vt_ref, gout_ref, gn_ref, o_ref, s0_scr, s1_scr, *, blk, bk):
    i = pl.program_id(2)
    first = first_ref[(pl.program_id(0) * pl.num_programs(1) + pl.program_id(1)) * pl.num_programs(2) + i]
    krow = lax.broadcasted_iota(jnp.int32, (bk, blk), 0)
    qcol = lax.broadcasted_iota(jnp.int32, (bk, blk), 1)
    qts = (qt_ref[0, 0], qt_ref[0, 1])

    def score_fn(hh, j):
        off = pl.multiple_of(j * bk, bk)
        return jnp.dot(k_ref[0, hh, pl.ds(off, bk), :], qts[hh], preferred_element_type=F32)

    def causal(st, near, s):
        return jnp.where(krow + s * bk <= qcol, st, NEG)

    states = _pipelined_sweep(i, blk // bk, 1, (s0_scr, s1_scr), score_fn, lambda hh, j: vt_ref[0, hh, j],
                              causal, VB_ROWS, blk, first=first)
    o_t = jnp.concatenate([_normalized_t(states[0], HEAD_DIM), _normalized_t(states[1], HEAD_DIM)], axis=0)
    o_ref[0] = (_group_rms(o_t.T, gout_ref[0], gn_ref[...])).astype(BF16)


EXP2_ZERO_BELOW = -150.0
BOUND_SLACK = 1.02


def _fox_score_spread(g_qk_b):
    qk_max = HEAD_DIM * jnp.max(jnp.abs(g_qk_b[0])) * jnp.max(jnp.abs(g_qk_b[1])) * QK_SCALE * LOG2E
    return 2.0 * qk_max * BOUND_SLACK


def _fox_skip_plan(c_edge, g_qk_b):
    b, nq = c_edge.shape[:2]
    c_first = c_edge[:, :, 0, :H_B]
    c_last = c_edge[:, :, 1, :H_B]
    best = (_fox_score_spread(g_qk_b)
            + (c_first[:, :, None, :] - c_last[:, None, :, :]) * (1.0 / BOUND_SLACK))
    dead = best < EXP2_ZERO_BELOW - 1.0
    dead = jnp.logical_and(dead[..., 0::2], dead[..., 1::2])
    j_lt_i = (jnp.arange(nq)[None, :] < jnp.arange(nq)[:, None])[None, :, :, None]
    lead = jnp.cumprod(jnp.logical_and(dead, j_lt_i).astype(jnp.int32), axis=2)
    first = jnp.sum(lead, axis=2)
    return jnp.transpose(first, (0, 2, 1)).reshape(-1).astype(jnp.int32)


def _fox_attention(qb_t, kb_aug, vb_t, c_edge, g_qk_b, g_out_b):
    b, _, _, t = qb_t.shape
    blk = min(ATTN_BLOCK, t)
    bk = vb_t.shape[-1]
    pairs = H_B // 2
    gout = g_out_b.reshape(pairs, 1, LANES)
    first = _fox_skip_plan(c_edge, g_qk_b)
    return pl.pallas_call(
        functools.partial(_fox_body, blk=blk, bk=bk),
        grid_spec=pltpu.PrefetchScalarGridSpec(
            num_scalar_prefetch=1, grid=(b, pairs, t // blk),
            in_specs=[
                pl.BlockSpec((1, 2, LANES, blk), lambda bi, p, i, f: (bi, p, 0, i)),
                pl.BlockSpec((1, 2, t, LANES), lambda bi, p, i, f: (bi, p, 0, 0)),
                pl.BlockSpec((1, 2, t // bk, VB_ROWS, bk), lambda bi, p, i, f: (bi, p, 0, 0, 0)),
                pl.BlockSpec((1, 1, LANES), lambda bi, p, i, f: (p, 0, 0)),
                pl.BlockSpec((LANES, LANES), lambda bi, p, i, f: (0, 0)),
            ],
            out_specs=pl.BlockSpec((1, blk, LANES), lambda bi, p, i, f: (bi, i, p)),
            scratch_shapes=[pltpu.VMEM((2, bk, blk), F32), pltpu.VMEM((2, bk, blk), F32)]),
        out_shape=jax.ShapeDtypeStruct((b, t, H_B * HEAD_DIM), BF16),
        compiler_params=_cparams(("arbitrary", "arbitrary", "arbitrary")),
        name="fox_attention",
    )(first, qb_t, kb_aug, vb_t, gout, _blockdiag_mean(LANES, HEAD_DIM))


def _diff_finish(states, lam, gout):
    (_, l1, a1), (_, l2, a2) = states
    o = a1 / l1 - lam * (a2 / l2)
    ms = jnp.mean(o * o, axis=-1, keepdims=True)
    return (o * lax.rsqrt(ms + RMS_EPS)) * gout * (1.0 - LAM_INIT)


def _diff_body(lam_ref, qt_ref, k_ref, vt_ref, bias_ref, gout_ref, o_ref, s0_scr, s1_scr, *, blk, bk):
    i = pl.program_id(2)
    n_sub = blk // bk
    qts = (qt_ref[0, 0], qt_ref[0, 1])

    def score_fn(mi, j):
        off = pl.multiple_of(j * bk, bk)
        return jnp.dot(k_ref[0, 0, pl.ds(off, bk), :], qts[mi], preferred_element_type=F32)

    def add_bias(st, near, s):
        if near == 0:
            return st + bias_ref[0, n_sub + s]
        return st + bias_ref[0, jnp.where(i == 0, 2 * n_sub + s, s)]

    states = _pipelined_sweep(i, n_sub, 2, (s0_scr, s1_scr), score_fn, lambda mi, j: vt_ref[0, 0, j],
                              add_bias, VA_ROWS, blk)
    lam = _lam(lam_ref[...])
    o = (_normalized_t(states[0], 2 * HEAD_DIM) - lam * _normalized_t(states[1], 2 * HEAD_DIM)).T
    ms = jnp.mean(o * o, axis=-1, keepdims=True)
    o_ref[0] = ((o * lax.rsqrt(ms + RMS_EPS)) * gout_ref[0] * (1.0 - LAM_INIT)).astype(BF16)


def _diff_attention(qa_t, ka_bf, va_t, bias, lambda_qk, g_out_a):
    b, _, _, t = qa_t.shape
    blk = min(ATTN_BLOCK, t)
    bk = va_t.shape[-1]
    gout = g_out_a.reshape(H_A, 1, LANES)
    return pl.pallas_call(
        functools.partial(_diff_body, blk=blk, bk=bk),
        grid=(b, H_A, t // blk),
        in_specs=[
            _const_spec((4, HEAD_DIM)),
            pl.BlockSpec((1, 2, LANES, blk), lambda bi, h, i: (bi, h, 0, i)),
            pl.BlockSpec((1, 1, t, LANES), lambda bi, h, i: (bi, h, 0, 0)),
            pl.BlockSpec((1, 1, t // bk, VA_ROWS, bk), lambda bi, h, i: (bi, h, 0, 0, 0)),
            pl.BlockSpec((1, 3 * (blk // bk), bk, blk), lambda bi, h, i: (h, 0, 0, 0)),
            pl.BlockSpec((1, 1, LANES), lambda bi, h, i: (h, 0, 0)),
        ],
        out_specs=pl.BlockSpec((1, blk, LANES), lambda bi, h, i: (bi, i, h)),
        out_shape=jax.ShapeDtypeStruct((b, t, H_A * 2 * HEAD_DIM), BF16),
        scratch_shapes=[pltpu.VMEM((2, bk, blk), F32), pltpu.VMEM((2, bk, blk), F32)],
        compiler_params=_cparams(("arbitrary", "arbitrary", "arbitrary")),
        name="diff_attention",
    )(lambda_qk, qa_t, ka_bf, va_t, bias, gout)


def _dec_load(m_scr, l_scr, acc_scr, idx):
    return m_scr[idx], l_scr[idx], acc_scr[idx]


def _dec_store(m_scr, l_scr, acc_scr, idx, state):
    m_scr[idx], l_scr[idx], acc_scr[idx] = state


def _dec_init(m_scr, l_scr, acc_scr):
    m_scr[...] = jnp.full(m_scr.shape, NEG, F32)
    l_scr[...] = jnp.zeros(l_scr.shape, F32)
    acc_scr[...] = jnp.zeros(acc_scr.shape, F32)


def _stack_queries(q, n_blocks):
    col = lax.broadcasted_iota(jnp.int32, q.shape, 1)
    zero = jnp.zeros_like(q)
    return jnp.concatenate([jnp.where((col >= c * HEAD_DIM) & (col < (c + 1) * HEAD_DIM), q, zero)
                            for c in range(n_blocks)], axis=0)


def _stacked_update(state, s, pv_fn):
    m, l, acc = state
    m_new = jnp.maximum(m, jnp.max(s, axis=-1, keepdims=True))
    p = jnp.exp2(s - m_new)
    alpha = jnp.exp2(m - m_new)
    return m_new, alpha * l + jnp.sum(p, axis=-1, keepdims=True), alpha * acc + pv_fn(p.astype(BF16))


def _diff_dec_body(lam_ref, q_ref, kn_ref, vn_ref, kt_ref, v_ref, bc_ref, bn_ref, gout_ref, o_ref,
                   qbig_scr, m_scr, l_scr, acc_scr, *, ck, nq):
    kc = pl.program_id(1)
    n_maps = 2 * H_A

    @pl.when(kc == 0)
    def _():
        qbig_scr[...] = _stack_queries(q_ref[0], n_maps)
        _dec_init(m_scr, l_scr, acc_scr)

    qbig = qbig_scr[...]

    def bias_rows(ref):
        return jnp.concatenate([ref[h, 0] for h in range(H_A) for _ in range(2)], axis=0)

    def per_head(p, values_of):
        return jnp.concatenate([jnp.dot(p[2 * h * nq:(2 * h + 2) * nq], values_of(h), preferred_element_type=F32)
                                for h in range(H_A)], axis=0)

    kt_all = kt_ref[0].reshape(n_maps * HEAD_DIM, ck).astype(BF16)
    s = jnp.dot(qbig, kt_all, preferred_element_type=F32) + bias_rows(bc_ref)
    state = _stacked_update((m_scr[...], l_scr[...], acc_scr[...]), s, lambda p: per_head(
        p, lambda h: v_ref[0, pl.ds(h, ck, stride=H_A), :].astype(BF16)))
    m_scr[...], l_scr[...], acc_scr[...] = state

    @pl.when(kc == pl.num_programs(1) - 1)
    def _():
        v_new = vn_ref[0].astype(BF16)
        s_new = _qk(qbig, kn_ref[0].astype(BF16)) + bias_rows(bn_ref)
        m, l, acc = _stacked_update(state, s_new, lambda p: per_head(
            p, lambda h: v_new[:, h * LANES:(h + 1) * LANES]))
        lam = _lam(lam_ref[...])
        for h in range(H_A):
            r = 2 * h * nq
            maps = tuple((m[a:a + nq], l[a:a + nq], acc[a:a + nq]) for a in (r, r + nq))
            o_ref[0, :, h * LANES:(h + 1) * LANES] = _diff_finish(maps, lam, gout_ref[h]).astype(BF16)


def _diff_decode(qa, ka_new, va_new, cache_kt, cache_v, bias_c, bias_n, lambda_qk, g_out_a):
    b, nq, w = qa.shape
    past = cache_kt.shape[-1]
    ck = min(CACHE_CHUNK, past)
    n_kc = past // ck
    new_spec = pl.BlockSpec((1, nq, w), lambda bi, c: (bi, 0, 0))
    return pl.pallas_call(
        functools.partial(_diff_dec_body, ck=ck, nq=nq),
        grid=(b, n_kc),
        in_specs=[
            _const_spec((4, HEAD_DIM)), new_spec, new_spec, new_spec,
            pl.BlockSpec((1, H_A, 2, HEAD_DIM, ck), lambda bi, c: (bi, 0, 0, 0, c)),
            pl.BlockSpec((1, H_A * ck, LANES), lambda bi, c: (bi, c, 0)),
            pl.BlockSpec((H_A, 1, nq, ck), lambda bi, c: (0, c, 0, 0)),
            _const_spec((H_A, 1, nq, nq)),
            _const_spec((H_A, 1, LANES)),
        ],
        out_specs=new_spec,
        out_shape=jax.ShapeDtypeStruct((b, nq, w), BF16),
        scratch_shapes=[pltpu.VMEM((2 * H_A * nq, w), BF16),
                        pltpu.VMEM((2 * H_A * nq, 1), F32), pltpu.VMEM((2 * H_A * nq, 1), F32),
                        pltpu.VMEM((2 * H_A * nq, LANES), F32)],
        compiler_params=_cparams(("arbitrary", "arbitrary")),
        name="diff_decode",
    )(lambda_qk, qa, ka_new, va_new, cache_kt, cache_v, bias_c, bias_n, g_out_a.reshape(H_A, 1, LANES))


def _suffix_sum_lanes(x):
    n = x.shape[-1]
    lane = lax.broadcasted_iota(jnp.int32, x.shape, x.ndim - 1)
    shift = 1
    while shift < n:
        x = x + jnp.where(lane + shift < n, pltpu.roll(x, n - shift, x.ndim - 1), 0.0)
        shift *= 2
    return x


def _fox_dec_body(qk2_ref, q_ref, kn_ref, vn_ref, lfn_col_ref, lfn_row_ref, kt_ref, vt_ref, lfc_ref, gout_ref, gn_ref,
                  o_ref, qbig_scr, cq_scr, cn_scr, after_scr, m_scr, l_scr, acc_scr, *, nq):
    kc = pl.program_id(1)
    n_kc = after_scr.shape[0]
    ck = after_scr.shape[-1]

    def head_rows(x):
        return jnp.concatenate([jnp.broadcast_to(x[h:h + 1, :], (nq, x.shape[-1])) for h in range(H_B)], axis=0)

    @pl.when(kc == 0)
    def _():
        qbig_scr[...] = _stack_queries(q_ref[0], H_B)
        _dec_init(m_scr, l_scr, acc_scr)
        r = lax.broadcasted_iota(jnp.int32, (nq, nq), 0)
        c = lax.broadcasted_iota(jnp.int32, (nq, nq), 1)
        tri_n = jnp.where(c <= r, 1.0, 0.0).astype(BF16)
        cq = _cumsum_rows(tri_n, lfn_col_ref[0] * LOG2E, jnp.zeros((1, LANES), F32))
        cq_scr[...] = jnp.concatenate([jnp.broadcast_to(cq[:, h:h + 1], (nq, LANES)) for h in range(H_B)], axis=0)
        lf_row = lfn_row_ref[0] * LOG2E
        total = jnp.sum(lf_row, axis=-1, keepdims=True)
        cn_scr[...] = total - _suffix_sum_lanes(lf_row) + lf_row
        lf = lfc_ref[0] * LOG2E
        after = _suffix_sum_lanes(lf) - lf
        for c_idx in range(n_kc):
            after_scr[c_idx] = after[:, c_idx * ck:(c_idx + 1) * ck]

    after = after_scr[n_kc - 1 - kc]
    qbig = qbig_scr[...]
    cq_col = cq_scr[...][:, 0:1]
    live = jnp.max(after[:, ck - 1:ck]) + qk2_ref[0] >= EXP2_ZERO_BELOW - 1.0

    @pl.when(live)
    def _():
        kt_all = kt_ref[0].reshape(H_B * HEAD_DIM, ck).astype(BF16)
        vt_all = vt_ref[0].reshape(H_B * HEAD_DIM, ck).astype(BF16)
        s = jnp.dot(qbig, kt_all, preferred_element_type=F32) + cq_col + head_rows(after)
        m_scr[...], l_scr[...], acc_scr[...] = _stacked_update(
            (m_scr[...], l_scr[...], acc_scr[...]), s, lambda p: _qk(p, vt_all))

    @pl.when(kc == pl.num_programs(1) - 1)
    def _():
        row = jnp.concatenate([lax.broadcasted_iota(jnp.int32, (nq, nq), 0)] * H_B, axis=0)
        causal = lax.broadcasted_iota(jnp.int32, (H_B * nq, nq), 1) <= row
        s = _qk(qbig, kn_ref[0].astype(BF16)) + cq_col - head_rows(cn_scr[...][:, 0:nq])
        v_new = vn_ref[0].astype(BF16)
        _, l, acc = _stacked_update((m_scr[...], l_scr[...], acc_scr[...]), jnp.where(causal, s, NEG),
                                    lambda p: jnp.dot(p, v_new, preferred_element_type=F32))
        outs = [acc[h * nq:(h + 1) * nq, h * HEAD_DIM:(h + 1) * HEAD_DIM] / l[h * nq:(h + 1) * nq]
                for h in range(H_B)]
        for p in range(H_B // 2):
            o = jnp.concatenate([outs[2 * p], outs[2 * p + 1]], axis=-1)
            o_ref[0, :, p * LANES:(p + 1) * LANES] = _group_rms(o, gout_ref[p], gn_ref[...]).astype(BF16)


def _fox_decode(qb, kb_new, vb_new, logf_new, cache_kt, cache_vt, cache_logf_t, g_qk_b, g_out_b):
    b, nq, w = qb.shape
    past = cache_kt.shape[-1]
    ck = min(CACHE_CHUNK, past)
    n_kc = past // ck
    assert nq <= LANES
    pad = jnp.zeros((b, nq, LANES - H_B), F32)
    lfn_col = jnp.concatenate([logf_new, pad], axis=-1)
    lfn_row = jnp.concatenate([jnp.transpose(logf_new, (0, 2, 1)), jnp.zeros((b, H_B, LANES - nq), F32)], axis=-1)
    new_spec = pl.BlockSpec((1, nq, w), lambda bi, c: (bi, 0, 0))
    newest_first = lambda bi, c: (bi, 0, 0, n_kc - 1 - c)
    return pl.pallas_call(
        functools.partial(_fox_dec_body, nq=nq),
        grid=(b, n_kc),
        in_specs=[
            pl.BlockSpec(memory_space=pltpu.SMEM),
            new_spec, new_spec, new_spec,
            pl.BlockSpec((1, nq, LANES), lambda bi, c: (bi, 0, 0)),
            pl.BlockSpec((1, H_B, LANES), lambda bi, c: (bi, 0, 0)),
            pl.BlockSpec((1, H_B, HEAD_DIM, ck), newest_first),
            pl.BlockSpec((1, H_B, HEAD_DIM, ck), newest_first),
            pl.BlockSpec((1, H_B, past), lambda bi, c: (bi, 0, 0)),
            _const_spec((H_B // 2, 1, LANES)), _const_spec((LANES, LANES)),
        ],
        out_specs=new_spec,
        out_shape=jax.ShapeDtypeStruct((b, nq, w), BF16),
        scratch_shapes=[pltpu.VMEM((H_B * nq, w), BF16),
                        pltpu.VMEM((H_B * nq, LANES), F32), pltpu.VMEM((H_B, LANES), F32),
                        pltpu.VMEM((n_kc, H_B, ck), F32),
                        pltpu.VMEM((H_B * nq, 1), F32), pltpu.VMEM((H_B * nq, 1), F32),
                        pltpu.VMEM((H_B * nq, w), F32)],
        compiler_params=_cparams(("arbitrary", "arbitrary")),
        name="fox_decode",
    )(_fox_score_spread(g_qk_b).reshape(1), qb, kb_new, vb_new, lfn_col, lfn_row, cache_kt, cache_vt, cache_logf_t,
      g_out_b.reshape(H_B // 2, 1, LANES), _blockdiag_mean(LANES, HEAD_DIM))


def _route(logits):
    lane_i = lax.broadcasted_iota(jnp.int32, logits.shape, 1)
    lane = lane_i.astype(F32)
    big = float(LANES)
    lg = jnp.where(lane_i < N_GROUPS, logits, NEG)
    mx = jnp.max(lg, axis=-1, keepdims=True)
    grp = jnp.min(jnp.where(lg == mx, lane, big), axis=-1, keepdims=True)
    p_grp = 1.0 / jnp.sum(jnp.exp(lg - mx), axis=-1, keepdims=True)
    e = lane_i - ROUTER_LANE0
    e_grp = lax.shift_right_arithmetic(e, 3).astype(F32)
    sel = (e >= 0) & (e < N_EXPERTS) & (e_grp == grp)
    v = jnp.where(sel, logits, NEG)
    v1 = jnp.max(v, axis=-1, keepdims=True)
    i1 = jnp.min(jnp.where(sel & (v == v1), lane, big), axis=-1, keepdims=True)
    sel2 = sel & (lane != i1)
    vv = jnp.where(sel2, logits, NEG)
    v2 = jnp.max(vv, axis=-1, keepdims=True)
    i2 = jnp.min(jnp.where(sel2 & (vv == v2), lane, big), axis=-1, keepdims=True)
    e2 = jnp.exp(v2 - v1)
    w1 = p_grp / (1.0 + e2)
    w2 = p_grp * e2 / (1.0 + e2)
    gates = jnp.where(lane == i1, w1, 0.0) + jnp.where(lane == i2, w2, 0.0)
    return gates, (i1, i2, w1, w2)


R_E1, R_E2, R_RANK1, R_RANK2, R_W1, R_W2 = range(6)


def _mix_body(oa_ref, ob_ref, x_ref, wa_ref, wb_ref, g2_ref, wr1_ref, wr2_ref, br_ref, tri_ref,
              x1_ref, xn_ref, gates_ref, route_ref, counts_ref):
    y = (jnp.dot(oa_ref[...], wa_ref[...], preferred_element_type=F32)
         + jnp.dot(ob_ref[...], wb_ref[...], preferred_element_type=F32))
    x1 = x_ref[...] + y
    x1_ref[...] = x1
    ms = jnp.mean(x1 * x1, axis=-1, keepdims=True)
    xn = (x1 * lax.rsqrt(ms + RMS_EPS)) * g2_ref[...]
    xn_ref[...] = xn
    h1 = xn.astype(BF16)
    h2 = (xn - h1.astype(F32)).astype(BF16)
    logits = (jnp.dot(h1, wr1_ref[...], preferred_element_type=F32)
              + jnp.dot(h1, wr2_ref[...], preferred_element_type=F32)
              + jnp.dot(h2, wr1_ref[...], preferred_element_type=F32)) + br_ref[...]
    gates, (i1, i2, w1, w2) = _route(logits)
    gates_ref[...] = gates

    @pl.when(pl.program_id(0) == 0)
    def _():
        counts_ref[...] = jnp.zeros_like(counts_ref)

    lane_i = lax.broadcasted_iota(jnp.int32, gates.shape, 1)
    lane = lane_i.astype(F32)
    oh1 = jnp.where(lane == i1, 1.0, 0.0)
    oh2 = jnp.where(lane == i2, 1.0, 0.0)
    comb = oh1 + oh2
    running = counts_ref[0:1, :]
    before = jnp.dot(tri_ref[...], comb.astype(BF16), preferred_element_type=F32) + running
    rank1 = jnp.sum(before * oh1, axis=-1, keepdims=True)
    rank2 = jnp.sum(before * oh2, axis=-1, keepdims=True)
    counts_ref[0:1, :] = running + jnp.sum(comb, axis=0, keepdims=True)
    rec = jnp.zeros_like(gates)
    for idx, val in ((R_E1, i1 - ROUTER_LANE0), (R_E2, i2 - ROUTER_LANE0), (R_RANK1, rank1), (R_RANK2, rank2),
                     (R_W1, w1), (R_W2, w2)):
        rec = jnp.where(lane_i == idx, val, rec)
    route_ref[...] = rec


def _mix_and_route(o_a, o_b, x, w_out, g_norm, w_rg, b_rg, w_re, b_re):
    n, d = x.shape
    bt = min(TOKEN_TILE, n)
    r = jnp.arange(bt)
    tri_strict = (r[None, :] < r[:, None]).astype(BF16)
    wa = w_out[:GROUP_W].astype(BF16)
    wb = w_out[GROUP_W:].astype(BF16)
    n_r = N_GROUPS + N_EXPERTS
    wr = jnp.concatenate([w_rg, w_re, jnp.zeros((d, LANES - n_r), F32)], axis=1)
    wr1 = wr.astype(BF16)
    wr2 = (wr - wr1.astype(F32)).astype(BF16)
    br = jnp.concatenate([b_rg, b_re, jnp.zeros((LANES - n_r,), F32)])[None, :]
    row = lambda width: pl.BlockSpec((bt, width), lambda i: (i, 0))
    return pl.pallas_call(
        _mix_body,
        grid=(n // bt,),
        in_specs=[row(GROUP_W), row(GROUP_W), row(d), _const_spec((GROUP_W, d)), _const_spec((GROUP_W, d)),
                  _const_spec((1, d)), _const_spec((d, LANES)), _const_spec((d, LANES)), _const_spec((1, LANES)),
                  _const_spec((bt, bt))],
        out_specs=[row(d), row(d), row(LANES), row(LANES), _const_spec((8, LANES))],
        out_shape=[jax.ShapeDtypeStruct((n, d), F32), jax.ShapeDtypeStruct((n, d), F32),
                   jax.ShapeDtypeStruct((n, LANES), F32), jax.ShapeDtypeStruct((n, LANES), F32),
                   jax.ShapeDtypeStruct((8, LANES), F32)],
        compiler_params=_cparams(("arbitrary",)),
        name="mix_route",
    )(o_a, o_b, x, wa, wb, g_norm[None, :], wr1, wr2, br, tri_strict)


def _swiglu(xn, wg, wu, wd, gate=None):
    x = xn.astype(BF16)
    g = jnp.dot(x, wg, preferred_element_type=F32)
    u = jnp.dot(x, wu, preferred_element_type=F32)
    h = (g * jax.nn.sigmoid(g)) * u
    if gate is not None:
        h = h * gate
    return jnp.dot(h.astype(BF16), wd, preferred_element_type=F32)


def _expert_body(xn_ref, x1_ref, gates_ref, wg_ref, wu_ref, wd_ref, o_ref):
    e = pl.program_id(1)

    @pl.when(e == 0)
    def _():
        o_ref[...] = x1_ref[...]

    gates = gates_ref[...]
    lane = lax.broadcasted_iota(jnp.int32, gates.shape, 1)
    gate = jnp.sum(jnp.where(lane == e + ROUTER_LANE0, gates, 0.0), axis=-1, keepdims=True)
    o_ref[...] += _swiglu(xn_ref[...], wg_ref[0], wu_ref[0], wd_ref[0], gate)


def _experts(xn, x1, gates, w_gate, w_up, w_down):
    n, d = x1.shape
    ff = w_gate.shape[-1]
    bt = min(MOE_TILE, n)
    row = lambda width: pl.BlockSpec((bt, width), lambda i, e: (i, 0))
    return pl.pallas_call(
        _expert_body,
        grid=(n // bt, N_EXPERTS),
        in_specs=[row(d), row(d), row(LANES),
                  pl.BlockSpec((1, d, ff), lambda i, e: (e, 0, 0)),
                  pl.BlockSpec((1, d, ff), lambda i, e: (e, 0, 0)),
                  pl.BlockSpec((1, ff, d), lambda i, e: (e, 0, 0))],
        out_specs=row(d),
        out_shape=jax.ShapeDtypeStruct((n, d), F32),
        compiler_params=_cparams(("arbitrary", "arbitrary")),
        name="experts",
    )(xn, x1, gates, w_gate, w_up, w_down)


def _row_copies(n_rows, make_copy):
    def issue(r, carry):
        for s in range(2):
            make_copy(r, s).start()
        return carry

    lax.fori_loop(0, n_rows, issue, 0, unroll=8)

    def drain(r, carry):
        for s in range(2):
            make_copy(r, s).wait()
        return carry

    lax.fori_loop(0, n_rows, drain, 0, unroll=8)


def _row_position_body(route_ref, base_ref, pos_ref):
    rec = route_ref[...]
    lane_i = lax.broadcasted_iota(jnp.int32, rec.shape, 1)
    lane = lane_i.astype(F32)
    out = jnp.zeros(rec.shape, F32)
    for slot, (e_lane, r_lane) in enumerate(((R_E1, R_RANK1), (R_E2, R_RANK2))):
        onehot = lane == rec[:, e_lane:e_lane + 1] + float(ROUTER_LANE0)
        base = jnp.sum(jnp.where(onehot, base_ref[...], 0.0), axis=-1, keepdims=True)
        out = jnp.where(lane_i == slot, base + rec[:, r_lane:r_lane + 1], out)
    pos_ref[...] = out.astype(jnp.int32)


def _dispatch_body(last_tile_ref, pos_ref, x_ref, xs_ref, zero_scr, sem):
    @pl.when(pl.program_id(0) == 0)
    def _():
        zero_scr[...] = jnp.zeros_like(zero_scr)
        tm = zero_scr.shape[0]
        fills = [pltpu.make_async_copy(zero_scr, xs_ref.at[pl.ds(pl.multiple_of(last_tile_ref[e], tm), tm)], sem)
                 for e in range(N_EXPERTS)]
        for f in fills:
            f.start()
        for f in fills:
            f.wait()

        def fill_unused(t, carry):
            f = pltpu.make_async_copy(zero_scr, xs_ref.at[pl.ds(pl.multiple_of(t * tm, tm), tm)], sem)
            f.start()
            f.wait()
            return carry

        lax.fori_loop(last_tile_ref[N_EXPERTS], xs_ref.shape[0] // tm, fill_unused, 0)

    _row_copies(x_ref.shape[0], lambda r, s: pltpu.make_async_copy(
        x_ref.at[pl.ds(r, 1)], xs_ref.at[pl.ds(pos_ref[0, 0, 2 * r + s], 1)], sem))


def _grouped_body(te_ref, nu_ref, xs_ref, wg_ref, wu_ref, wd_ref, ys_ref):
    del te_ref
    used = pl.program_id(0) < nu_ref[0]

    @pl.when(used)
    def _():
        ys_ref[...] = _swiglu(xs_ref[...], wg_ref[0], wu_ref[0], wd_ref[0])

    @pl.when(jnp.logical_not(used))
    def _():
        ys_ref[...] = jnp.zeros_like(ys_ref)


def _combine_body(pos_ref, route_ref, x1_ref, ys_ref, o_ref, buf_scr, sem):
    _row_copies(x1_ref.shape[0], lambda r, s: pltpu.make_async_copy(
        ys_ref.at[pl.ds(pos_ref[0, 0, 2 * r + s], 1)], buf_scr.at[s, pl.ds(r, 1)], sem))
    rec = route_ref[...]
    o_ref[...] = (x1_ref[...] + rec[:, R_W1:R_W1 + 1] * buf_scr[0] + rec[:, R_W2:R_W2 + 1] * buf_scr[1])


def _routed_experts(xn, x1, route, counts, w_gate, w_up, w_down):
    n, d = x1.shape
    ff = w_gate.shape[-1]
    bt = min(TOKEN_TILE, n)
    nt = n // bt
    tm = MOE_ROW_TILE
    n_tiles = (2 * n) // tm + N_EXPERTS
    cnt = counts[0, ROUTER_LANE0:ROUTER_LANE0 + N_EXPERTS].astype(jnp.int32)
    tiles = (cnt + tm - 1) // tm
    tile_end = jnp.cumsum(tiles)
    base_row = (tile_end - tiles) * tm
    n_used = tile_end[-1:]
    tile_expert = jnp.minimum(jnp.sum(jnp.arange(n_tiles)[:, None] >= tile_end[None, :], axis=1), N_EXPERTS - 1)
    base_lanes = jnp.zeros((1, LANES), F32).at[0, ROUTER_LANE0:ROUTER_LANE0 + N_EXPERTS].set(base_row.astype(F32))
    row = lambda width: pl.BlockSpec((bt, width), lambda i: (i, 0))
    pos = pl.pallas_call(
        _row_position_body,
        grid=(nt,),
        in_specs=[row(LANES), _const_spec((1, LANES))],
        out_specs=row(LANES),
        out_shape=jax.ShapeDtypeStruct((n, LANES), jnp.int32),
        compiler_params=_cparams(("arbitrary",)),
        name="moe_positions",
    )(route, base_lanes)
    pos = pos[:, :2].reshape(nt, 1, 2 * bt)

    pos_spec = pl.BlockSpec((1, 1, 2 * bt), lambda i: (i, 0, 0), memory_space=pltpu.SMEM)
    any_spec = pl.BlockSpec(memory_space=pl.ANY)
    last_tile = jnp.minimum(base_row + jnp.maximum(tiles - 1, 0) * tm, (n_tiles - 1) * tm)
    last_tile = jnp.concatenate([last_tile, n_used]).astype(jnp.int32)
    xs = pl.pallas_call(
        _dispatch_body,
        grid_spec=pltpu.PrefetchScalarGridSpec(
            num_scalar_prefetch=1, grid=(nt,),
            in_specs=[pl.BlockSpec((1, 1, 2 * bt), lambda i, lt: (i, 0, 0), memory_space=pltpu.SMEM),
                      pl.BlockSpec((bt, d), lambda i, lt: (i, 0))],
            out_specs=any_spec,
            scratch_shapes=[pltpu.VMEM((tm, d), F32), pltpu.SemaphoreType.DMA(())]),
        out_shape=jax.ShapeDtypeStruct((n_tiles * tm, d), F32),
        compiler_params=_cparams(("arbitrary",)),
        name="moe_dispatch",
    )(last_tile, pos, xn)

    w_spec = lambda shape: pl.BlockSpec(shape, lambda t, te, nu: (te[t], 0, 0))
    ys = pl.pallas_call(
        _grouped_body,
        grid_spec=pltpu.PrefetchScalarGridSpec(
            num_scalar_prefetch=2, grid=(n_tiles,),
            in_specs=[pl.BlockSpec((tm, d), lambda t, te, nu: (jnp.minimum(t, nu[0] - 1), 0)),
                      w_spec((1, d, ff)), w_spec((1, d, ff)), w_spec((1, ff, d))],
            out_specs=pl.BlockSpec((tm, d), lambda t, te, nu: (t, 0))),
        out_shape=jax.ShapeDtypeStruct((n_tiles * tm, d), F32),
        compiler_params=_cparams(("arbitrary",)),
        name="moe_experts",
    )(tile_expert.astype(jnp.int32), n_used.astype(jnp.int32), xs, w_gate, w_up, w_down)

    return pl.pallas_call(
        _combine_body,
        grid=(nt,),
        in_specs=[pos_spec, row(LANES), row(d), any_spec],
        out_specs=row(d),
        out_shape=jax.ShapeDtypeStruct((n, d), F32),
        scratch_shapes=[pltpu.VMEM((2, bt, d), F32), pltpu.SemaphoreType.DMA(())],
        compiler_params=_cparams(("arbitrary",)),
        name="moe_combine",
    )(pos, route, x1, ys)


def _ffn(o_a, o_b, x, w_out, g_norm_ffn, w_rg, b_rg, w_re, b_re, wg, wu, wd):
    b, t, d = x.shape
    n = b * t
    x1, xn, gates, route, counts = _mix_and_route(o_a.reshape(n, -1), o_b.reshape(n, -1), x.reshape(n, d), w_out,
                                                  g_norm_ffn, w_rg, b_rg, w_re, b_re)
    if 2 * n >= ROUTED_MIN_ASSIGNMENTS:
        y = _routed_experts(xn, x1, route, counts, wg, wu, wd)
    else:
        y = _experts(xn, x1, gates, wg, wu, wd)
    return y.reshape(b, t, d)


def kernel(x_prompt, x_sample, cache_a_k, cache_a_v, cache_b_k, cache_b_v, cache_b_logf, g_norm_mix, w_in, b_forget, g_qk_a, g_qk_b, lambda_qk, g_out_a, g_out_b, w_out, rel_bias, g_norm_ffn, w_router_group, b_router_group, w_router_expert, b_router_expert, w_exp_gate, w_exp_up, w_exp_down):
    depth = w_in.shape[0]
    assert depth == 1, "single-layer step only"
    bp, tp, d = x_prompt.shape
    bs, ts, _ = x_sample.shape
    past = cache_a_k.shape[2]
    w_in0, w_out0 = w_in[0], w_out[0]
    wg, wu, wd = w_exp_gate[0].astype(BF16), w_exp_up[0].astype(BF16), w_exp_down[0].astype(BF16)
    ffn_w = (w_out0, g_norm_ffn[0], w_router_group[0], b_router_group[0], w_router_expert[0], b_router_expert[0],
             wg, wu, wd)

    (ka_p, va_p, kb_p, vb_p, logf_p, qa_t, ka_bf, va_t, qb_t, kb_aug, vb_t, c_edge) = _projection(
        x_prompt, g_norm_mix[0], w_in0, b_forget[0], g_qk_a[0], g_qk_b[0], with_aug=True)
    blk = min(ATTN_BLOCK, tp)
    bk = va_t.shape[-1]
    assert blk % CHUNK == 0 and blk >= MAX_DISTANCE and blk % bk == 0
    q_pos = blk + jnp.arange(blk, dtype=jnp.int32)
    bkt_p = jnp.stack([_bucket_map(q_pos, s * bk + jnp.arange(bk, dtype=jnp.int32)).T
                       for s in range(2 * blk // bk)] + [jnp.full((bk, blk), -1, jnp.int32)] * (blk // bk))
    bias_p = _bias_tiles(rel_bias, bkt_p)
    o_a = _diff_attention(qa_t, ka_bf, va_t, bias_p, lambda_qk[0], g_out_a[0])
    o_b = _fox_attention(qb_t, kb_aug, vb_t, c_edge, g_qk_b[0], g_out_b[0])
    y_p = _ffn(o_a, o_b, x_prompt, *ffn_w)

    xs = x_sample.reshape(1, bs * ts, d)
    (ka_s, va_s, kb_s, vb_s, logf_s, qa_s, qb_s) = _projection(
        xs, g_norm_mix[0], w_in0, b_forget[0], g_qk_a[0], g_qk_b[0], with_aug=False)
    per_stream = lambda a: a.reshape(bs, ts, a.shape[-1])
    ka_s, va_s, kb_s, vb_s, logf_s, qa_s, qb_s = map(per_stream, (ka_s, va_s, kb_s, vb_s, logf_s, qa_s, qb_s))
    ck = min(CACHE_CHUNK, past)
    q_pos = past + jnp.arange(ts, dtype=jnp.int32)
    bkt_c = _bucket_map(q_pos, jnp.arange(past, dtype=jnp.int32)).reshape(ts, past // ck, ck).transpose(1, 0, 2)
    bias_c = _bias_tiles(rel_bias, bkt_c)
    bias_n = _bias_tiles(rel_bias, _bucket_map(q_pos, q_pos)[None])
    o_a_s = _diff_decode(qa_s, ka_s, va_s, jnp.transpose(cache_a_k[0], (0, 2, 3, 4, 1)),
                         cache_a_v[0].reshape(bs, past * H_A, 2 * HEAD_DIM), bias_c, bias_n, lambda_qk[0], g_out_a[0])
    o_b_s = _fox_decode(qb_s, kb_s, vb_s, logf_s, jnp.transpose(cache_b_k[0], (0, 2, 3, 1)),
                        jnp.transpose(cache_b_v[0], (0, 2, 3, 1)), jnp.transpose(cache_b_logf[0], (0, 2, 1)),
                        g_qk_b[0], g_out_b[0])
    y_s = _ffn(o_a_s, o_b_s, x_sample, *ffn_w)

    def rows(ka, va, kb, vb, logf, b, t):
        return (ka.reshape(1, b, t, H_A, 2, HEAD_DIM), va.reshape(1, b, t, H_A, 2 * HEAD_DIM),
                kb.reshape(1, b, t, H_B, HEAD_DIM), vb.reshape(1, b, t, H_B, HEAD_DIM), logf.reshape(1, b, t, H_B))

    return (y_p, y_s) + rows(ka_p, va_p, kb_p, vb_p, logf_p, bp, tp) + rows(ka_s, va_s, kb_s, vb_s, logf_s, bs, ts)
```

```python
import functools
import math

import jax
import jax.numpy as jnp
from jax import lax
from jax.experimental import pallas as pl
from jax.experimental.pallas import tpu as pltpu

F32 = jnp.float32
BF16 = jnp.bfloat16

LANES = 128
VMEM_LIMIT_BYTES = 56 * 1024 * 1024

HEAD_DIM = 64
H_A = 4
H_B = 8
GROUP_W = 512
MAIN_W = 6 * GROUP_W
CHUNK = 64
N_BUCKETS = 32
MAX_DISTANCE = 128
N_GROUPS = 4
EXPERTS_PER_GROUP = 8
N_EXPERTS = N_GROUPS * EXPERTS_PER_GROUP
ROUTER_LANE0 = N_GROUPS
RMS_EPS = 1e-6
NEG = -1e30
LOG2E = 1.4426950408889634
QK_SCALE = HEAD_DIM ** -0.5
LAM_INIT = 0.8 - 0.6 * math.exp(-0.3 * 0)
N_CPARTS = 3

ONES_ROWS = 16
VA_ROWS = 2 * HEAD_DIM + ONES_ROWS
VB_ROWS = HEAD_DIM + ONES_ROWS

TOKEN_TILE = 512
ATTN_BLOCK = 512
KV_BLOCK = 256
FAR_UNROLL = 4
CACHE_CHUNK = 1024
MOE_TILE = 1024
MOE_ROW_TILE = 256
ROUTED_MIN_ASSIGNMENTS = 4 * N_EXPERTS * MOE_ROW_TILE


def _cparams(sem):
    return pltpu.CompilerParams(dimension_semantics=sem, vmem_limit_bytes=VMEM_LIMIT_BYTES)


def _const_spec(shape):
    nd = len(shape)
    return pl.BlockSpec(shape, lambda *_: (0,) * nd)


def _split3(x):
    p1 = x.astype(BF16).astype(F32)
    r1 = x - p1
    p2 = r1.astype(BF16).astype(F32)
    p3 = (r1 - p2).astype(BF16).astype(F32)
    return p1, p2, p3


def _lane_groups(parts, lane):
    return jnp.where(lane < 8, parts[0], jnp.where(lane < 16, parts[1], parts[2]))


def _cumsum_rows(tri, x, carry):
    c = carry
    for part in _split3(x):
        c = c + jnp.dot(tri, part.astype(BF16), preferred_element_type=F32)
    return c


def _log_sigmoid(x):
    return jnp.minimum(x, 0.0) - jnp.log(1.0 + jnp.exp(-jnp.abs(x)))


def _group_rms(raw, gain_row, gn):
    ms = jnp.dot((raw * raw).astype(BF16), gn, preferred_element_type=F32)
    return raw * lax.rsqrt(ms + RMS_EPS) * gain_row


def _online_update(state, s, v):
    m, l, acc = state
    m_new = jnp.maximum(m, jnp.max(s, axis=-1, keepdims=True))
    p = jnp.exp2(s - m_new)
    alpha = jnp.exp2(m - m_new)
    l_new = alpha * l + jnp.sum(p, axis=-1, keepdims=True)
    acc_new = alpha * acc + jnp.dot(p.astype(BF16), v, preferred_element_type=F32)
    return m_new, l_new, acc_new


def _qk(q, k):
    return lax.dot_general(q, k, (((1,), (1,)), ((), ())), preferred_element_type=F32)


def _proj_body(*refs, with_aug, bt, bk):
    (x_ref, g1_ref, wm_ref, wf_ref, bf_ref, gains_ref, gn_ref, tri_ref, place_ref) = refs[:9]
    ka_ref, va_ref, kb_ref, vb_ref, logf_ref = refs[9:14]
    x = x_ref[0]
    ms = jnp.mean(x * x, axis=-1, keepdims=True)
    xn = (x * lax.rsqrt(ms + RMS_EPS)) * g1_ref[...]
    xb = xn.astype(BF16)
    proj = jnp.dot(xb, wm_ref[...], preferred_element_type=F32)
    gains = gains_ref[...]
    gn = gn_ref[...]
    w = GROUP_W
    qa = _group_rms(proj[:, 0:w], gains[0:1], gn)
    ka = _group_rms(proj[:, w:2 * w], gains[1:2], gn)
    va = proj[:, 2 * w:3 * w]
    qb = _group_rms(proj[:, 3 * w:4 * w], gains[2:3], gn)
    kb = _group_rms(proj[:, 4 * w:5 * w], gains[3:4], gn)
    vb = proj[:, 5 * w:6 * w]
    ka_ref[0] = ka
    if with_aug:
        for h in range(H_A):
            va_ref[0, pl.ds(h, bt, stride=H_A), :] = va[:, h * LANES:(h + 1) * LANES]
    else:
        va_ref[0] = va
    kb_ref[0] = kb
    vb_ref[0] = vb
    fl = jnp.dot(xb, wf_ref[...], preferred_element_type=F32) + bf_ref[...]
    logf = _log_sigmoid(fl)
    logf_ref[0] = logf[:, 0:H_B]
    qscale = QK_SCALE * LOG2E
    if not with_aug:
        qa_ref, qb_ref = refs[14:16]
        qa_ref[0] = (qa * qscale).astype(BF16)
        qb_ref[0] = (qb * qscale).astype(BF16)
        return
    qa_t, ka_bf, va_t, qb_t, kb_aug, vb_t, cedge_ref, carry_ref = refs[14:22]
    n_chunk = bt // bk
    row = lax.broadcasted_iota(jnp.int32, (LANES, bt), 0)
    ones_tail = jnp.where(lax.broadcasted_iota(jnp.int32, (ONES_ROWS, bt), 0) == 0, 1.0, 0.0)

    def put_chunks(ref, idx, vt):
        vt = vt.astype(BF16)
        for c in range(n_chunk):
            ref[0, idx, c] = vt[:, c * bk:(c + 1) * bk]

    for h in range(H_A):
        sl = slice(h * LANES, (h + 1) * LANES)
        q_t = (qa[:, sl] * qscale).T
        qa_t[0, 2 * h] = jnp.where(row < HEAD_DIM, q_t, 0.0).astype(BF16)
        qa_t[0, 2 * h + 1] = jnp.where(row >= HEAD_DIM, q_t, 0.0).astype(BF16)
        ka_bf[0, h] = ka[:, sl].astype(BF16)
        put_chunks(va_t, h, jnp.concatenate([va[:, sl].T, ones_tail], axis=0))
        vb_pair_t = vb[:, sl].T
        for hh in range(2):
            put_chunks(vb_t, 2 * h + hh,
                       jnp.concatenate([vb_pair_t[hh * HEAD_DIM:(hh + 1) * HEAD_DIM], ones_tail], axis=0))

    @pl.when(pl.program_id(1) == 0)
    def _():
        carry_ref[...] = jnp.zeros_like(carry_ref)

    c = _cumsum_rows(tri_ref[...], logf * LOG2E, carry_ref[0:1, :])
    carry_ref[0:1, :] = c[bt - 1:bt, :]
    edge_row = lax.broadcasted_iota(jnp.int32, (8, LANES), 0)
    cedge_ref[0, 0] = jnp.where(edge_row == 0, c[0:1, :], jnp.where(edge_row == 1, c[bt - 1:bt, :], 0.0))
    lane = lax.broadcasted_iota(jnp.int32, (bt, LANES), 1)
    cparts = _lane_groups(_split3(c), lane).astype(BF16)
    extras = jnp.dot(cparts, place_ref[...], preferred_element_type=F32)
    q_parts_end = HEAD_DIM + N_CPARTS
    ones_q = jnp.where((lane >= q_parts_end) & (lane < q_parts_end + N_CPARTS), 1.0, 0.0)
    ones_k = jnp.where((lane >= HEAD_DIM) & (lane < q_parts_end), 1.0, 0.0)
    for h in range(H_B):
        sl = slice((h // 2) * LANES, (h // 2 + 1) * LANES)
        qp = qb[:, sl] * qscale
        kp = kb[:, sl]
        if h % 2:
            qp = pltpu.roll(qp, HEAD_DIM, 1)
            kp = pltpu.roll(kp, HEAD_DIM, 1)
        e = extras[:, h * LANES:(h + 1) * LANES]
        qb_t[0, h] = jnp.where(lane < HEAD_DIM, qp, jnp.where(lane < q_parts_end, e, ones_q)).T.astype(BF16)
        kb_aug[0, h] = jnp.where(lane < HEAD_DIM, kp, jnp.where(lane < q_parts_end, ones_k, e)).astype(BF16)


def _blockdiag_mean(n, group):
    r = jnp.arange(n)
    return jnp.where((r[:, None] // group) == (r[None, :] // group), 1.0 / group, 0.0).astype(BF16)


def _tri(n):
    r = jnp.arange(n)
    return (r[None, :] <= r[:, None]).astype(BF16)


def _place_matrix():
    rows = jnp.arange(LANES)[:, None]
    cols = jnp.arange(H_B * LANES)[None, :]
    p, h = rows // 8, rows % 8
    valid = rows < 8 * N_CPARTS
    qcol = h * LANES + HEAD_DIM + p
    kcol = h * LANES + HEAD_DIM + N_CPARTS + p
    m = jnp.where(valid & (cols == qcol), 1.0, 0.0) - jnp.where(valid & (cols == kcol), 1.0, 0.0)
    return m.astype(BF16)


def _projection(x, g_norm, w_in, b_forget, g_qk_a, g_qk_b, *, with_aug):
    b, t, d = x.shape
    bt = min(TOKEN_TILE, t)
    nt = t // bt
    wm = w_in[:, :MAIN_W].astype(BF16)
    wf_cols = w_in[:, MAIN_W:MAIN_W + H_B]
    wf = jnp.concatenate([wf_cols] * N_CPARTS + [jnp.zeros((d, LANES - H_B * N_CPARTS), F32)], axis=1).astype(BF16)
    bfv = jnp.concatenate([b_forget] * N_CPARTS + [jnp.zeros((LANES - H_B * N_CPARTS,), F32)])[None, :]
    gains = jnp.stack([jnp.tile(g_qk_a[0], 2 * H_A), jnp.tile(g_qk_a[1], 2 * H_A),
                       jnp.tile(g_qk_b[0], H_B), jnp.tile(g_qk_b[1], H_B)])
    gn = _blockdiag_mean(GROUP_W, HEAD_DIM)
    tri = _tri(bt)
    place = _place_matrix()
    in_specs = [
        pl.BlockSpec((1, bt, d), lambda i, j: (i, j, 0)),
        _const_spec((1, d)),
        pl.BlockSpec((d, MAIN_W), lambda i, j: (0, 0), pipeline_mode=pl.Buffered(1)),
        _const_spec((d, LANES)), _const_spec((1, LANES)), _const_spec((4, GROUP_W)),
        _const_spec((GROUP_W, GROUP_W)), _const_spec((bt, bt)), _const_spec((LANES, H_B * LANES)),
    ]
    row_spec = pl.BlockSpec((1, bt, GROUP_W), lambda i, j: (i, j, 0))
    out_shape = [jax.ShapeDtypeStruct((b, t, GROUP_W), F32)] * 4 + [jax.ShapeDtypeStruct((b, t, H_B), F32)]
    out_specs = [row_spec] * 4 + [pl.BlockSpec((1, bt, H_B), lambda i, j: (i, j, 0))]
    scratch = []
    bk = min(KV_BLOCK, bt)
    if with_aug:
        out_shape[1] = jax.ShapeDtypeStruct((b, t * H_A, 2 * HEAD_DIM), F32)
        out_specs[1] = pl.BlockSpec((1, bt * H_A, 2 * HEAD_DIM), lambda i, j: (i, j, 0))

        def add(shape, block, index_map):
            out_shape.append(jax.ShapeDtypeStruct(shape, BF16))
            out_specs.append(pl.BlockSpec(block, index_map))

        rows_major = lambda i, j: (i, 0, j, 0)
        time_minor = lambda i, j: (i, 0, 0, j)
        chunked = lambda i, j: (i, 0, j, 0, 0)
        add((b, 2 * H_A, LANES, t), (1, 2 * H_A, LANES, bt), time_minor)
        add((b, H_A, t, LANES), (1, H_A, bt, LANES), rows_major)
        add((b, H_A, t // bk, VA_ROWS, bk), (1, H_A, bt // bk, VA_ROWS, bk), chunked)
        add((b, H_B, LANES, t), (1, H_B, LANES, bt), time_minor)
        add((b, H_B, t, LANES), (1, H_B, bt, LANES), rows_major)
        add((b, H_B, t // bk, VB_ROWS, bk), (1, H_B, bt // bk, VB_ROWS, bk), chunked)
        out_shape.append(jax.ShapeDtypeStruct((b, nt, 8, LANES), F32))
        out_specs.append(pl.BlockSpec((1, 1, 8, LANES), lambda i, j: (i, j, 0, 0)))
        scratch = [pltpu.VMEM((8, LANES), F32)]
    else:
        out_shape += [jax.ShapeDtypeStruct((b, t, GROUP_W), BF16)] * 2
        out_specs += [row_spec] * 2
    return pl.pallas_call(
        functools.partial(_proj_body, with_aug=with_aug, bt=bt, bk=bk),
        grid=(b, nt), in_specs=in_specs, out_specs=out_specs, out_shape=out_shape, scratch_shapes=scratch,
        compiler_params=_cparams(("arbitrary", "arbitrary")),
        name="proj_aug" if with_aug else "proj_plain",
    )(x, g_norm[None, :], wm, wf, bfv, gains, gn, tri, place)


def _t5_bucket(rel):
    nb = N_BUCKETS // 2
    max_exact = nb // 2
    base = jnp.where(rel > 0, nb, 0)
    n = jnp.abs(rel)
    large = max_exact + (jnp.log(jnp.maximum(n, max_exact).astype(jnp.float32) / max_exact)
                         / math.log(MAX_DISTANCE / max_exact) * (nb - max_exact)).astype(jnp.int32)
    large = jnp.minimum(large, nb - 1)
    return base + jnp.where(n < max_exact, n, large)


def _bucket_map(q_pos, k_pos):
    bkt = _t5_bucket(k_pos[None, :] - q_pos[:, None])
    visible = (k_pos[None, :] // CHUNK) <= (q_pos[:, None] // CHUNK)
    return jnp.where(visible, bkt, -1).astype(jnp.int32)


def _bias_body(rb_ref, bkt_ref, o_ref):
    h = pl.program_id(0)
    bkt = bkt_ref[0]
    far = rb_ref[N_BUCKETS // 2 - 1, h]
    acc = jnp.zeros(bkt.shape, F32)
    for b in range(N_BUCKETS):
        acc = jnp.where(bkt == b, rb_ref[b, h] - far, acc)
    o_ref[0, 0] = jnp.where(bkt < 0, NEG, acc * LOG2E)


def _bias_tiles(rel_bias, bkt):
    n, r, c = bkt.shape
    return pl.pallas_call(
        _bias_body,
        grid=(H_A, n),
        in_specs=[pl.BlockSpec(memory_space=pltpu.SMEM), pl.BlockSpec((1, r, c), lambda h, i: (i, 0, 0))],
        out_specs=pl.BlockSpec((1, 1, r, c), lambda h, i: (h, i, 0, 0)),
        out_shape=jax.ShapeDtypeStruct((H_A, n, r, c), F32),
        compiler_params=_cparams(("arbitrary", "arbitrary")),
        name="bias_tiles",
    )(rel_bias, bkt)


def _lam(lq):
    a = jnp.sum(lq[0:1, :] * lq[1:2, :], axis=-1, keepdims=True)
    b = jnp.sum(lq[2:3, :] * lq[3:4, :], axis=-1, keepdims=True)
    return jnp.exp(a) - jnp.exp(b) + LAM_INIT


def _attn_init_t(rows, bq):
    return (jnp.full((1, bq), NEG, F32), jnp.zeros((rows, bq), F32))


def _online_update_t(state, st, vt):
    m, acc = state
    m_new = jnp.maximum(m, jnp.max(st, axis=0, keepdims=True))
    p = jnp.exp2(st - m_new)
    alpha = jnp.exp2(m - m_new)
    return m_new, alpha * acc + jnp.dot(vt, p.astype(BF16), preferred_element_type=F32)


def _normalized_t(state, rows):
    acc = state[1]
    return acc[0:rows] / acc[rows:rows + 1]


def _pipelined_sweep(i, n_sub, n_near, slots, score_fn, value_fn, modify, rows, blk, first=0):
    assert n_sub % 2 == 0 and n_near in (1, 2)

    def run_block(jb, states, near, last, next_start=None):
        states = list(states)
        for s in range(n_sub):
            j = jb * n_sub + s
            cur, nxt = slots[s % 2], slots[(s + 1) % 2]
            ahead = next_start if (next_start is not None and s == n_sub - 1) else j + 1
            for c in range(2):
                if not (last and s == n_sub - 1):
                    nxt[c] = score_fn(c, ahead)
                st = cur[c]
                if near is not None:
                    st = modify(st, near, s)
                states[c] = _online_update_t(states[c], st, value_fn(c, j))
        return tuple(states)

    for c in range(2):
        slots[0][c] = score_fn(c, first * n_sub)
    states = (_attn_init_t(rows, blk), _attn_init_t(rows, blk))
    n_far = jnp.maximum(i + 1 - n_near, 0)

    def run_far(jb, states, n_blocks):
        for d in range(n_blocks):
            states = run_block(jb + d, states, None, False)
        return states

    n_groups = jnp.maximum(n_far - first, 0) // FAR_UNROLL
    states = lax.fori_loop(0, n_groups, lambda g, st: run_far(first + g * FAR_UNROLL, st, FAR_UNROLL), states)
    done = first + n_groups * FAR_UNROLL
    n_pairs = jnp.maximum(n_far - done, 0) // 2
    states = lax.fori_loop(0, n_pairs, lambda g, st: run_far(done + 2 * g, st, 2), states)
    states = lax.fori_loop(done + 2 * n_pairs, n_far, lambda jb, st: run_far(jb, st, 1), states)
    if n_near == 2:
        states = run_block(jnp.maximum(i - 1, 0), states, 1, False, next_start=i * n_sub)
    return run_block(i, states, 0, True)


def _fox_body(first_ref, qt_ref, k_ref, vt_ref, gout_ref, gn_ref, o_ref, s0_scr, s1_scr, *, blk, bk):
    i = pl.program_id(2)
    first = first_ref[(pl.program_id(0) * pl.num_programs(1) + pl.program_id(1)) * pl.num_programs(2) + i]
    krow = lax.broadcasted_iota(jnp.int32, (bk, blk), 0)
    qcol = lax.broadcasted_iota(jnp.int32, (bk, blk), 1)
    qts = (qt_ref[0, 0], qt_ref[0, 1])

    def score_fn(hh, j):
        off = pl.multiple_of(j * bk, bk)
        return jnp.dot(k_ref[0, hh, pl.ds(off, bk), :], qts[hh], preferred_element_type=F32)

    def causal(st, near, s):
        return jnp.where(krow + s * bk <= qcol, st, NEG)

    states = _pipelined_sweep(i, blk // bk, 1, (s0_scr, s1_scr), score_fn, lambda hh, j: vt_ref[0, hh, j],
                              causal, VB_ROWS, blk, first=first)
    o_t = jnp.concatenate([_normalized_t(states[0], HEAD_DIM), _normalized_t(states[1], HEAD_DIM)], axis=0)
    o_ref[0] = (_group_rms(o_t.T, gout_ref[0], gn_ref[...])).astype(BF16)


EXP2_ZERO_BELOW = -150.0
BOUND_SLACK = 1.02


def _fox_score_spread(g_qk_b):
    qk_max = HEAD_DIM * jnp.max(jnp.abs(g_qk_b[0])) * jnp.max(jnp.abs(g_qk_b[1])) * QK_SCALE * LOG2E
    return 2.0 * qk_max * BOUND_SLACK


def _fox_skip_plan(c_edge, g_qk_b):
    b, nq = c_edge.shape[:2]
    c_first = c_edge[:, :, 0, :H_B]
    c_last = c_edge[:, :, 1, :H_B]
    best = (_fox_score_spread(g_qk_b)
            + (c_first[:, :, None, :] - c_last[:, None, :, :]) * (1.0 / BOUND_SLACK))
    dead = best < EXP2_ZERO_BELOW - 1.0
    dead = jnp.logical_and(dead[..., 0::2], dead[..., 1::2])
    j_lt_i = (jnp.arange(nq)[None, :] < jnp.arange(nq)[:, None])[None, :, :, None]
    lead = jnp.cumprod(jnp.logical_and(dead, j_lt_i).astype(jnp.int32), axis=2)
    first = jnp.sum(lead, axis=2)
    return jnp.transpose(first, (0, 2, 1)).reshape(-1).astype(jnp.int32)


def _fox_attention(qb_t, kb_aug, vb_t, c_edge, g_qk_b, g_out_b):
    b, _, _, t = qb_t.shape
    blk = min(ATTN_BLOCK, t)
    bk = vb_t.shape[-1]
    pairs = H_B // 2
    gout = g_out_b.reshape(pairs, 1, LANES)
    first = _fox_skip_plan(c_edge, g_qk_b)
    return pl.pallas_call(
        functools.partial(_fox_body, blk=blk, bk=bk),
        grid_spec=pltpu.PrefetchScalarGridSpec(
            num_scalar_prefetch=1, grid=(b, pairs, t // blk),
            in_specs=[
                pl.BlockSpec((1, 2, LANES, blk), lambda bi, p, i, f: (bi, p, 0, i)),
                pl.BlockSpec((1, 2, t, LANES), lambda bi, p, i, f: (bi, p, 0, 0)),
                pl.BlockSpec((1, 2, t // bk, VB_ROWS, bk), lambda bi, p, i, f: (bi, p, 0, 0, 0)),
                pl.BlockSpec((1, 1, LANES), lambda bi, p, i, f: (p, 0, 0)),
                pl.BlockSpec((LANES, LANES), lambda bi, p, i, f: (0, 0)),
            ],
            out_specs=pl.BlockSpec((1, blk, LANES), lambda bi, p, i, f: (bi, i, p)),
            scratch_shapes=[pltpu.VMEM((2, bk, blk), F32), pltpu.VMEM((2, bk, blk), F32)]),
        out_shape=jax.ShapeDtypeStruct((b, t, H_B * HEAD_DIM), BF16),
        compiler_params=_cparams(("arbitrary", "arbitrary", "arbitrary")),
        name="fox_attention",
    )(first, qb_t, kb_aug, vb_t, gout, _blockdiag_mean(LANES, HEAD_DIM))


def _diff_finish(states, lam, gout):
    (_, l1, a1), (_, l2, a2) = states
    o = a1 / l1 - lam * (a2 / l2)
    ms = jnp.mean(o * o, axis=-1, keepdims=True)
    return (o * lax.rsqrt(ms + RMS_EPS)) * gout * (1.0 - LAM_INIT)


def _diff_body(lam_ref, qt_ref, k_ref, vt_ref, bias_ref, gout_ref, o_ref, s0_scr, s1_scr, *, blk, bk):
    i = pl.program_id(2)
    n_sub = blk // bk
    qts = (qt_ref[0, 0], qt_ref[0, 1])

    def score_fn(mi, j):
        off = pl.multiple_of(j * bk, bk)
        return jnp.dot(k_ref[0, 0, pl.ds(off, bk), :], qts[mi], preferred_element_type=F32)

    def add_bias(st, near, s):
        if near == 0:
            return st + bias_ref[0, n_sub + s]
        return st + bias_ref[0, jnp.where(i == 0, 2 * n_sub + s, s)]

    states = _pipelined_sweep(i, n_sub, 2, (s0_scr, s1_scr), score_fn, lambda mi, j: vt_ref[0, 0, j],
                              add_bias, VA_ROWS, blk)
    lam = _lam(lam_ref[...])
    o = (_normalized_t(states[0], 2 * HEAD_DIM) - lam * _normalized_t(states[1], 2 * HEAD_DIM)).T
    ms = jnp.mean(o * o, axis=-1, keepdims=True)
    o_ref[0] = ((o * lax.rsqrt(ms + RMS_EPS)) * gout_ref[0] * (1.0 - LAM_INIT)).astype(BF16)


def _diff_attention(qa_t, ka_bf, va_t, bias, lambda_qk, g_out_a):
    b, _, _, t = qa_t.shape
    blk = min(ATTN_BLOCK, t)
    bk = va_t.shape[-1]
    gout = g_out_a.reshape(H_A, 1, LANES)
    return pl.pallas_call(
        functools.partial(_diff_body, blk=blk, bk=bk),
        grid=(b, H_A, t // blk),
        in_specs=[
            _const_spec((4, HEAD_DIM)),
            pl.BlockSpec((1, 2, LANES, blk), lambda bi, h, i: (bi, h, 0, i)),
            pl.BlockSpec((1, 1, t, LANES), lambda bi, h, i: (bi, h, 0, 0)),
            pl.BlockSpec((1, 1, t // bk, VA_ROWS, bk), lambda bi, h, i: (bi, h, 0, 0, 0)),
            pl.BlockSpec((1, 3 * (blk // bk), bk, blk), lambda bi, h, i: (h, 0, 0, 0)),
            pl.BlockSpec((1, 1, LANES), lambda bi, h, i: (h, 0, 0)),
        ],
        out_specs=pl.BlockSpec((1, blk, LANES), lambda bi, h, i: (bi, i, h)),
        out_shape=jax.ShapeDtypeStruct((b, t, H_A * 2 * HEAD_DIM), BF16),
        scratch_shapes=[pltpu.VMEM((2, bk, blk), F32), pltpu.VMEM((2, bk, blk), F32)],
        compiler_params=_cparams(("arbitrary", "arbitrary", "arbitrary")),
        name="diff_attention",
    )(lambda_qk, qa_t, ka_bf, va_t, bias, gout)


def _dec_load(m_scr, l_scr, acc_scr, idx):
    return m_scr[idx], l_scr[idx], acc_scr[idx]


def _dec_store(m_scr, l_scr, acc_scr, idx, state):
    m_scr[idx], l_scr[idx], acc_scr[idx] = state


def _dec_init(m_scr, l_scr, acc_scr):
    m_scr[...] = jnp.full(m_scr.shape, NEG, F32)
    l_scr[...] = jnp.zeros(l_scr.shape, F32)
    acc_scr[...] = jnp.zeros(acc_scr.shape, F32)


def _stack_queries(q, n_blocks):
    col = lax.broadcasted_iota(jnp.int32, q.shape, 1)
    zero = jnp.zeros_like(q)
    return jnp.concatenate([jnp.where((col >= c * HEAD_DIM) & (col < (c + 1) * HEAD_DIM), q, zero)
                            for c in range(n_blocks)], axis=0)


def _stacked_update(state, s, pv_fn):
    m, l, acc = state
    m_new = jnp.maximum(m, jnp.max(s, axis=-1, keepdims=True))
    p = jnp.exp2(s - m_new)
    alpha = jnp.exp2(m - m_new)
    return m_new, alpha * l + jnp.sum(p, axis=-1, keepdims=True), alpha * acc + pv_fn(p.astype(BF16))


def _diff_dec_body(lam_ref, q_ref, kn_ref, vn_ref, kt_ref, v_ref, bc_ref, bn_ref, gout_ref, o_ref,
                   qbig_scr, m_scr, l_scr, acc_scr, *, ck, nq):
    kc = pl.program_id(1)
    n_maps = 2 * H_A

    @pl.when(kc == 0)
    def _():
        qbig_scr[...] = _stack_queries(q_ref[0], n_maps)
        _dec_init(m_scr, l_scr, acc_scr)

    qbig = qbig_scr[...]

    def bias_rows(ref):
        return jnp.concatenate([ref[h, 0] for h in range(H_A) for _ in range(2)], axis=0)

    def per_head(p, values_of):
        return jnp.concatenate([jnp.dot(p[2 * h * nq:(2 * h + 2) * nq], values_of(h), preferred_element_type=F32)
                                for h in range(H_A)], axis=0)

    kt_all = kt_ref[0].reshape(n_maps * HEAD_DIM, ck).astype(BF16)
    s = jnp.dot(qbig, kt_all, preferred_element_type=F32) + bias_rows(bc_ref)
    state = _stacked_update((m_scr[...], l_scr[...], acc_scr[...]), s, lambda p: per_head(
        p, lambda h: v_ref[0, pl.ds(h, ck, stride=H_A), :].astype(BF16)))
    m_scr[...], l_scr[...], acc_scr[...] = state

    @pl.when(kc == pl.num_programs(1) - 1)
    def _():
        v_new = vn_ref[0].astype(BF16)
        s_new = _qk(qbig, kn_ref[0].astype(BF16)) + bias_rows(bn_ref)
        m, l, acc = _stacked_update(state, s_new, lambda p: per_head(
            p, lambda h: v_new[:, h * LANES:(h + 1) * LANES]))
        lam = _lam(lam_ref[...])
        for h in range(H_A):
            r = 2 * h * nq
            maps = tuple((m[a:a + nq], l[a:a + nq], acc[a:a + nq]) for a in (r, r + nq))
            o_ref[0, :, h * LANES:(h + 1) * LANES] = _diff_finish(maps, lam, gout_ref[h]).astype(BF16)


def _diff_decode(qa, ka_new, va_new, cache_kt, cache_v, bias_c, bias_n, lambda_qk, g_out_a):
    b, nq, w = qa.shape
    past = cache_kt.shape[-1]
    ck = min(CACHE_CHUNK, past)
    n_kc = past // ck
    new_spec = pl.BlockSpec((1, nq, w), lambda bi, c: (bi, 0, 0))
    return pl.pallas_call(
        functools.partial(_diff_dec_body, ck=ck, nq=nq),
        grid=(b, n_kc),
        in_specs=[
            _const_spec((4, HEAD_DIM)), new_spec, new_spec, new_spec,
            pl.BlockSpec((1, H_A, 2, HEAD_DIM, ck), lambda bi, c: (bi, 0, 0, 0, c)),
            pl.BlockSpec((1, H_A * ck, LANES), lambda bi, c: (bi, c, 0)),
            pl.BlockSpec((H_A, 1, nq, ck), lambda bi, c: (0, c, 0, 0)),
            _const_spec((H_A, 1, nq, nq)),
            _const_spec((H_A, 1, LANES)),
        ],
        out_specs=new_spec,
        out_shape=jax.ShapeDtypeStruct((b, nq, w), BF16),
        scratch_shapes=[pltpu.VMEM((2 * H_A * nq, w), BF16),
                        pltpu.VMEM((2 * H_A * nq, 1), F32), pltpu.VMEM((2 * H_A * nq, 1), F32),
                        pltpu.VMEM((2 * H_A * nq, LANES), F32)],
        compiler_params=_cparams(("arbitrary", "arbitrary")),
        name="diff_decode",
    )(lambda_qk, qa, ka_new, va_new, cache_kt, cache_v, bias_c, bias_n, g_out_a.reshape(H_A, 1, LANES))


def _suffix_sum_lanes(x):
    n = x.shape[-1]
    lane = lax.broadcasted_iota(jnp.int32, x.shape, x.ndim - 1)
    shift = 1
    while shift < n:
        x = x + jnp.where(lane + shift < n, pltpu.roll(x, n - shift, x.ndim - 1), 0.0)
        shift *= 2
    return x


def _fox_dec_body(qk2_ref, q_ref, kn_ref, vn_ref, lfn_col_ref, lfn_row_ref, kt_ref, vt_ref, lfc_ref, gout_ref, gn_ref,
                  o_ref, qbig_scr, cq_scr, cn_scr, after_scr, m_scr, l_scr, acc_scr, *, nq):
    kc = pl.program_id(1)
    n_kc = after_scr.shape[0]
    ck = after_scr.shape[-1]

    def head_rows(x):
        return jnp.concatenate([jnp.broadcast_to(x[h:h + 1, :], (nq, x.shape[-1])) for h in range(H_B)], axis=0)

    @pl.when(kc == 0)
    def _():
        qbig_scr[...] = _stack_queries(q_ref[0], H_B)
        _dec_init(m_scr, l_scr, acc_scr)
        r = lax.broadcasted_iota(jnp.int32, (nq, nq), 0)
        c = lax.broadcasted_iota(jnp.int32, (nq, nq), 1)
        tri_n = jnp.where(c <= r, 1.0, 0.0).astype(BF16)
        cq = _cumsum_rows(tri_n, lfn_col_ref[0] * LOG2E, jnp.zeros((1, LANES), F32))
        cq_scr[...] = jnp.concatenate([jnp.broadcast_to(cq[:, h:h + 1], (nq, LANES)) for h in range(H_B)], axis=0)
        lf_row = lfn_row_ref[0] * LOG2E
        total = jnp.sum(lf_row, axis=-1, keepdims=True)
        cn_scr[...] = total - _suffix_sum_lanes(lf_row) + lf_row
        lf = lfc_ref[0] * LOG2E
        after = _suffix_sum_lanes(lf) - lf
        for c_idx in range(n_kc):
            after_scr[c_idx] = after[:, c_idx * ck:(c_idx + 1) * ck]

    after = after_scr[n_kc - 1 - kc]
    qbig = qbig_scr[...]
    cq_col = cq_scr[...][:, 0:1]
    live = jnp.max(after[:, ck - 1:ck]) + qk2_ref[0] >= EXP2_ZERO_BELOW - 1.0

    @pl.when(live)
    def _():
        kt_all = kt_ref[0].reshape(H_B * HEAD_DIM, ck).astype(BF16)
        vt_all = vt_ref[0].reshape(H_B * HEAD_DIM, ck).astype(BF16)
        s = jnp.dot(qbig, kt_all, preferred_element_type=F32) + cq_col + head_rows(after)
        m_scr[...], l_scr[...], acc_scr[...] = _stacked_update(
            (m_scr[...], l_scr[...], acc_scr[...]), s, lambda p: _qk(p, vt_all))

    @pl.when(kc == pl.num_programs(1) - 1)
    def _():
        row = jnp.concatenate([lax.broadcasted_iota(jnp.int32, (nq, nq), 0)] * H_B, axis=0)
        causal = lax.broadcasted_iota(jnp.int32, (H_B * nq, nq), 1) <= row
        s = _qk(qbig, kn_ref[0].astype(BF16)) + cq_col - head_rows(cn_scr[...][:, 0:nq])
        v_new = vn_ref[0].astype(BF16)
        _, l, acc = _stacked_update((m_scr[...], l_scr[...], acc_scr[...]), jnp.where(causal, s, NEG),
                                    lambda p: jnp.dot(p, v_new, preferred_element_type=F32))
        outs = [acc[h * nq:(h + 1) * nq, h * HEAD_DIM:(h + 1) * HEAD_DIM] / l[h * nq:(h + 1) * nq]
                for h in range(H_B)]
        for p in range(H_B // 2):
            o = jnp.concatenate([outs[2 * p], outs[2 * p + 1]], axis=-1)
            o_ref[0, :, p * LANES:(p + 1) * LANES] = _group_rms(o, gout_ref[p], gn_ref[...]).astype(BF16)


def _fox_decode(qb, kb_new, vb_new, logf_new, cache_kt, cache_vt, cache_logf_t, g_qk_b, g_out_b):
    b, nq, w = qb.shape
    past = cache_kt.shape[-1]
    ck = min(CACHE_CHUNK, past)
    n_kc = past // ck
    assert nq <= LANES
    pad = jnp.zeros((b, nq, LANES - H_B), F32)
    lfn_col = jnp.concatenate([logf_new, pad], axis=-1)
    lfn_row = jnp.concatenate([jnp.transpose(logf_new, (0, 2, 1)), jnp.zeros((b, H_B, LANES - nq), F32)], axis=-1)
    new_spec = pl.BlockSpec((1, nq, w), lambda bi, c: (bi, 0, 0))
    newest_first = lambda bi, c: (bi, 0, 0, n_kc - 1 - c)
    return pl.pallas_call(
        functools.partial(_fox_dec_body, nq=nq),
        grid=(b, n_kc),
        in_specs=[
            pl.BlockSpec(memory_space=pltpu.SMEM),
            new_spec, new_spec, new_spec,
            pl.BlockSpec((1, nq, LANES), lambda bi, c: (bi, 0, 0)),
            pl.BlockSpec((1, H_B, LANES), lambda bi, c: (bi, 0, 0)),
            pl.BlockSpec((1, H_B, HEAD_DIM, ck), newest_first),
            pl.BlockSpec((1, H_B, HEAD_DIM, ck), newest_first),
            pl.BlockSpec((1, H_B, past), lambda bi, c: (bi, 0, 0)),
            _const_spec((H_B // 2, 1, LANES)), _const_spec((LANES, LANES)),
        ],
        out_specs=new_spec,
        out_shape=jax.ShapeDtypeStruct((b, nq, w), BF16),
        scratch_shapes=[pltpu.VMEM((H_B * nq, w), BF16),
                        pltpu.VMEM((H_B * nq, LANES), F32), pltpu.VMEM((H_B, LANES), F32),
                        pltpu.VMEM((n_kc, H_B, ck), F32),
                        pltpu.VMEM((H_B * nq, 1), F32), pltpu.VMEM((H_B * nq, 1), F32),
                        pltpu.VMEM((H_B * nq, w), F32)],
        compiler_params=_cparams(("arbitrary", "arbitrary")),
        name="fox_decode",
    )(_fox_score_spread(g_qk_b).reshape(1), qb, kb_new, vb_new, lfn_col, lfn_row, cache_kt, cache_vt, cache_logf_t,
      g_out_b.reshape(H_B // 2, 1, LANES), _blockdiag_mean(LANES, HEAD_DIM))


def _route(logits):
    lane_i = lax.broadcasted_iota(jnp.int32, logits.shape, 1)
    lane = lane_i.astype(F32)
    big = float(LANES)
    lg = jnp.where(lane_i < N_GROUPS, logits, NEG)
    mx = jnp.max(lg, axis=-1, keepdims=True)
    grp = jnp.min(jnp.where(lg == mx, lane, big), axis=-1, keepdims=True)
    p_grp = 1.0 / jnp.sum(jnp.exp(lg - mx), axis=-1, keepdims=True)
    e = lane_i - ROUTER_LANE0
    e_grp = lax.shift_right_arithmetic(e, 3).astype(F32)
    sel = (e >= 0) & (e < N_EXPERTS) & (e_grp == grp)
    v = jnp.where(sel, logits, NEG)
    v1 = jnp.max(v, axis=-1, keepdims=True)
    i1 = jnp.min(jnp.where(sel & (v == v1), lane, big), axis=-1, keepdims=True)
    sel2 = sel & (lane != i1)
    vv = jnp.where(sel2, logits, NEG)
    v2 = jnp.max(vv, axis=-1, keepdims=True)
    i2 = jnp.min(jnp.where(sel2 & (vv == v2), lane, big), axis=-1, keepdims=True)
    e2 = jnp.exp(v2 - v1)
    w1 = p_grp / (1.0 + e2)
    w2 = p_grp * e2 / (1.0 + e2)
    gates = jnp.where(lane == i1, w1, 0.0) + jnp.where(lane == i2, w2, 0.0)
    return gates, (i1, i2, w1, w2)


R_E1, R_E2, R_RANK1, R_RANK2, R_W1, R_W2 = range(6)


def _mix_body(oa_ref, ob_ref, x_ref, wa_ref, wb_ref, g2_ref, wr1_ref, wr2_ref, br_ref, tri_ref,
              x1_ref, xn_ref, gates_ref, route_ref, counts_ref):
    y = (jnp.dot(oa_ref[...], wa_ref[...], preferred_element_type=F32)
         + jnp.dot(ob_ref[...], wb_ref[...], preferred_element_type=F32))
    x1 = x_ref[...] + y
    x1_ref[...] = x1
    ms = jnp.mean(x1 * x1, axis=-1, keepdims=True)
    xn = (x1 * lax.rsqrt(ms + RMS_EPS)) * g2_ref[...]
    xn_ref[...] = xn
    h1 = xn.astype(BF16)
    h2 = (xn - h1.astype(F32)).astype(BF16)
    logits = (jnp.dot(h1, wr1_ref[...], preferred_element_type=F32)
              + jnp.dot(h1, wr2_ref[...], preferred_element_type=F32)
              + jnp.dot(h2, wr1_ref[...], preferred_element_type=F32)) + br_ref[...]
    gates, (i1, i2, w1, w2) = _route(logits)
    gates_ref[...] = gates

    @pl.when(pl.program_id(0) == 0)
    def _():
        counts_ref[...] = jnp.zeros_like(counts_ref)

    lane_i = lax.broadcasted_iota(jnp.int32, gates.shape, 1)
    lane = lane_i.astype(F32)
    oh1 = jnp.where(lane == i1, 1.0, 0.0)
    oh2 = jnp.where(lane == i2, 1.0, 0.0)
    comb = oh1 + oh2
    running = counts_ref[0:1, :]
    before = jnp.dot(tri_ref[...], comb.astype(BF16), preferred_element_type=F32) + running
    rank1 = jnp.sum(before * oh1, axis=-1, keepdims=True)
    rank2 = jnp.sum(before * oh2, axis=-1, keepdims=True)
    counts_ref[0:1, :] = running + jnp.sum(comb, axis=0, keepdims=True)
    rec = jnp.zeros_like(gates)
    for idx, val in ((R_E1, i1 - ROUTER_LANE0), (R_E2, i2 - ROUTER_LANE0), (R_RANK1, rank1), (R_RANK2, rank2),
                     (R_W1, w1), (R_W2, w2)):
        rec = jnp.where(lane_i == idx, val, rec)
    route_ref[...] = rec


def _mix_and_route(o_a, o_b, x, w_out, g_norm, w_rg, b_rg, w_re, b_re):
    n, d = x.shape
    bt = min(TOKEN_TILE, n)
    r = jnp.arange(bt)
    tri_strict = (r[None, :] < r[:, None]).astype(BF16)
    wa = w_out[:GROUP_W].astype(BF16)
    wb = w_out[GROUP_W:].astype(BF16)
    n_r = N_GROUPS + N_EXPERTS
    wr = jnp.concatenate([w_rg, w_re, jnp.zeros((d, LANES - n_r), F32)], axis=1)
    wr1 = wr.astype(BF16)
    wr2 = (wr - wr1.astype(F32)).astype(BF16)
    br = jnp.concatenate([b_rg, b_re, jnp.zeros((LANES - n_r,), F32)])[None, :]
    row = lambda width: pl.BlockSpec((bt, width), lambda i: (i, 0))
    return pl.pallas_call(
        _mix_body,
        grid=(n // bt,),
        in_specs=[row(GROUP_W), row(GROUP_W), row(d), _const_spec((GROUP_W, d)), _const_spec((GROUP_W, d)),
                  _const_spec((1, d)), _const_spec((d, LANES)), _const_spec((d, LANES)), _const_spec((1, LANES)),
                  _const_spec((bt, bt))],
        out_specs=[row(d), row(d), row(LANES), row(LANES), _const_spec((8, LANES))],
        out_shape=[jax.ShapeDtypeStruct((n, d), F32), jax.ShapeDtypeStruct((n, d), F32),
                   jax.ShapeDtypeStruct((n, LANES), F32), jax.ShapeDtypeStruct((n, LANES), F32),
                   jax.ShapeDtypeStruct((8, LANES), F32)],
        compiler_params=_cparams(("arbitrary",)),
        name="mix_route",
    )(o_a, o_b, x, wa, wb, g_norm[None, :], wr1, wr2, br, tri_strict)


def _swiglu(xn, wg, wu, wd, gate=None):
    x = xn.astype(BF16)
    g = jnp.dot(x, wg.astype(BF16), preferred_element_type=F32)
    u = jnp.dot(x, wu.astype(BF16), preferred_element_type=F32)
    h = (g * jax.nn.sigmoid(g)) * u
    if gate is not None:
        h = h * gate
    return jnp.dot(h.astype(BF16), wd.astype(BF16), preferred_element_type=F32)


def _expert_body(xn_ref, x1_ref, gates_ref, wg_ref, wu_ref, wd_ref, o_ref):
    e = pl.program_id(1)

    @pl.when(e == 0)
    def _():
        o_ref[...] = x1_ref[...]

    gates = gates_ref[...]
    lane = lax.broadcasted_iota(jnp.int32, gates.shape, 1)
    gate = jnp.sum(jnp.where(lane == e + ROUTER_LANE0, gates, 0.0), axis=-1, keepdims=True)
    o_ref[...] += _swiglu(xn_ref[...], wg_ref[0], wu_ref[0], wd_ref[0], gate)


def _experts(xn, x1, gates, w_gate, w_up, w_down):
    n, d = x1.shape
    ff = w_gate.shape[-1]
    bt = min(MOE_TILE, n)
    row = lambda width: pl.BlockSpec((bt, width), lambda i, e: (i, 0))
    return pl.pallas_call(
        _expert_body,
        grid=(n // bt, N_EXPERTS),
        in_specs=[row(d), row(d), row(LANES),
                  pl.BlockSpec((1, d, ff), lambda i, e: (e, 0, 0)),
                  pl.BlockSpec((1, d, ff), lambda i, e: (e, 0, 0)),
                  pl.BlockSpec((1, ff, d), lambda i, e: (e, 0, 0))],
        out_specs=row(d),
        out_shape=jax.ShapeDtypeStruct((n, d), F32),
        compiler_params=_cparams(("arbitrary", "arbitrary")),
        name="experts",
    )(xn, x1, gates, w_gate, w_up, w_down)


def _row_copies(n_rows, make_copy):
    def issue(r, carry):
        for s in range(2):
            make_copy(r, s).start()
        return carry

    lax.fori_loop(0, n_rows, issue, 0, unroll=8)

    def drain(r, carry):
        for s in range(2):
            make_copy(r, s).wait()
        return carry

    lax.fori_loop(0, n_rows, drain, 0, unroll=8)


def _row_position_body(route_ref, base_ref, pos_ref):
    rec = route_ref[...]
    lane_i = lax.broadcasted_iota(jnp.int32, rec.shape, 1)
    lane = lane_i.astype(F32)
    out = jnp.zeros(rec.shape, F32)
    for slot, (e_lane, r_lane) in enumerate(((R_E1, R_RANK1), (R_E2, R_RANK2))):
        onehot = lane == rec[:, e_lane:e_lane + 1] + float(ROUTER_LANE0)
        base = jnp.sum(jnp.where(onehot, base_ref[...], 0.0), axis=-1, keepdims=True)
        out = jnp.where(lane_i == slot, base + rec[:, r_lane:r_lane + 1], out)
    pos_ref[...] = out.astype(jnp.int32)


def _dispatch_body(last_tile_ref, pos_ref, x_ref, xs_ref, zero_scr, sem):
    @pl.when(pl.program_id(0) == 0)
    def _():
        zero_scr[...] = jnp.zeros_like(zero_scr)
        tm = zero_scr.shape[0]
        fills = [pltpu.make_async_copy(zero_scr, xs_ref.at[pl.ds(pl.multiple_of(last_tile_ref[e], tm), tm)], sem)
                 for e in range(N_EXPERTS)]
        for f in fills:
            f.start()
        for f in fills:
            f.wait()

        def fill_unused(t, carry):
            f = pltpu.make_async_copy(zero_scr, xs_ref.at[pl.ds(pl.multiple_of(t * tm, tm), tm)], sem)
            f.start()
            f.wait()
            return carry

        lax.fori_loop(last_tile_ref[N_EXPERTS], xs_ref.shape[0] // tm, fill_unused, 0)

    _row_copies(x_ref.shape[0], lambda r, s: pltpu.make_async_copy(
        x_ref.at[pl.ds(r, 1)], xs_ref.at[pl.ds(pos_ref[0, 0, 2 * r + s], 1)], sem))


def _grouped_body(te_ref, nu_ref, xs_ref, wg_ref, wu_ref, wd_ref, ys_ref, wg_bf, wu_bf, wd_bf):
    t = pl.program_id(0)
    used = t < nu_ref[0]
    new_expert = jnp.logical_or(t == 0, te_ref[t] != te_ref[jnp.maximum(t - 1, 0)])

    @pl.when(jnp.logical_and(used, new_expert))
    def _():
        wg_bf[...] = wg_ref[0].astype(BF16)
        wu_bf[...] = wu_ref[0].astype(BF16)
        wd_bf[...] = wd_ref[0].astype(BF16)

    @pl.when(used)
    def _():
        ys_ref[...] = _swiglu(xs_ref[...], wg_bf[...], wu_bf[...], wd_bf[...])

    @pl.when(jnp.logical_not(used))
    def _():
        ys_ref[...] = jnp.zeros_like(ys_ref)


def _combine_body(pos_ref, route_ref, x1_ref, ys_ref, o_ref, buf_scr, sem):
    _row_copies(x1_ref.shape[0], lambda r, s: pltpu.make_async_copy(
        ys_ref.at[pl.ds(pos_ref[0, 0, 2 * r + s], 1)], buf_scr.at[s, pl.ds(r, 1)], sem))
    rec = route_ref[...]
    o_ref[...] = (x1_ref[...] + rec[:, R_W1:R_W1 + 1] * buf_scr[0] + rec[:, R_W2:R_W2 + 1] * buf_scr[1])


def _routed_experts(xn, x1, route, counts, w_gate, w_up, w_down):
    n, d = x1.shape
    ff = w_gate.shape[-1]
    bt = min(TOKEN_TILE, n)
    nt = n // bt
    tm = MOE_ROW_TILE
    n_tiles = (2 * n) // tm + N_EXPERTS
    cnt = counts[0, ROUTER_LANE0:ROUTER_LANE0 + N_EXPERTS].astype(jnp.int32)
    tiles = (cnt + tm - 1) // tm
    tile_end = jnp.cumsum(tiles)
    base_row = (tile_end - tiles) * tm
    n_used = tile_end[-1:]
    tile_expert = jnp.minimum(jnp.sum(jnp.arange(n_tiles)[:, None] >= tile_end[None, :], axis=1), N_EXPERTS - 1)
    base_lanes = jnp.zeros((1, LANES), F32).at[0, ROUTER_LANE0:ROUTER_LANE0 + N_EXPERTS].set(base_row.astype(F32))
    row = lambda width: pl.BlockSpec((bt, width), lambda i: (i, 0))
    pos = pl.pallas_call(
        _row_position_body,
        grid=(nt,),
        in_specs=[row(LANES), _const_spec((1, LANES))],
        out_specs=row(LANES),
        out_shape=jax.ShapeDtypeStruct((n, LANES), jnp.int32),
        compiler_params=_cparams(("arbitrary",)),
        name="moe_positions",
    )(route, base_lanes)
    pos = pos[:, :2].reshape(nt, 1, 2 * bt)

    pos_spec = pl.BlockSpec((1, 1, 2 * bt), lambda i: (i, 0, 0), memory_space=pltpu.SMEM)
    any_spec = pl.BlockSpec(memory_space=pl.ANY)
    last_tile = jnp.minimum(base_row + jnp.maximum(tiles - 1, 0) * tm, (n_tiles - 1) * tm)
    last_tile = jnp.concatenate([last_tile, n_used]).astype(jnp.int32)
    xs = pl.pallas_call(
        _dispatch_body,
        grid_spec=pltpu.PrefetchScalarGridSpec(
            num_scalar_prefetch=1, grid=(nt,),
            in_specs=[pl.BlockSpec((1, 1, 2 * bt), lambda i, lt: (i, 0, 0), memory_space=pltpu.SMEM),
                      pl.BlockSpec((bt, d), lambda i, lt: (i, 0))],
            out_specs=any_spec,
            scratch_shapes=[pltpu.VMEM((tm, d), F32), pltpu.SemaphoreType.DMA(())]),
        out_shape=jax.ShapeDtypeStruct((n_tiles * tm, d), F32),
        compiler_params=_cparams(("arbitrary",)),
        name="moe_dispatch",
    )(last_tile, pos, xn)

    w_spec = lambda shape: pl.BlockSpec(shape, lambda t, te, nu: (te[t], 0, 0))
    ys = pl.pallas_call(
        _grouped_body,
        grid_spec=pltpu.PrefetchScalarGridSpec(
            num_scalar_prefetch=2, grid=(n_tiles,),
            in_specs=[pl.BlockSpec((tm, d), lambda t, te, nu: (jnp.minimum(t, nu[0] - 1), 0)),
                      w_spec((1, d, ff)), w_spec((1, d, ff)), w_spec((1, ff, d))],
            out_specs=pl.BlockSpec((tm, d), lambda t, te, nu: (t, 0)),
            scratch_shapes=[pltpu.VMEM((d, ff), BF16), pltpu.VMEM((d, ff), BF16), pltpu.VMEM((ff, d), BF16)]),
        out_shape=jax.ShapeDtypeStruct((n_tiles * tm, d), F32),
        compiler_params=_cparams(("arbitrary",)),
        name="moe_experts",
    )(tile_expert.astype(jnp.int32), n_used.astype(jnp.int32), xs, w_gate, w_up, w_down)

    return pl.pallas_call(
        _combine_body,
        grid=(nt,),
        in_specs=[pos_spec, row(LANES), row(d), any_spec],
        out_specs=row(d),
        out_shape=jax.ShapeDtypeStruct((n, d), F32),
        scratch_shapes=[pltpu.VMEM((2, bt, d), F32), pltpu.SemaphoreType.DMA(())],
        compiler_params=_cparams(("arbitrary",)),
        name="moe_combine",
    )(pos, route, x1, ys)


def _ffn(o_a, o_b, x, w_out, g_norm_ffn, w_rg, b_rg, w_re, b_re, wg, wu, wd):
    b, t, d = x.shape
    n = b * t
    x1, xn, gates, route, counts = _mix_and_route(o_a.reshape(n, -1), o_b.reshape(n, -1), x.reshape(n, d), w_out,
                                                  g_norm_ffn, w_rg, b_rg, w_re, b_re)
    if 2 * n >= ROUTED_MIN_ASSIGNMENTS:
        y = _routed_experts(xn, x1, route, counts, wg, wu, wd)
    else:
        y = _experts(xn, x1, gates, wg, wu, wd)
    return y.reshape(b, t, d)


def kernel(x_prompt, x_sample, cache_a_k, cache_a_v, cache_b_k, cache_b_v, cache_b_logf, g_norm_mix, w_in, b_forget, g_qk_a, g_qk_b, lambda_qk, g_out_a, g_out_b, w_out, rel_bias, g_norm_ffn, w_router_group, b_router_group, w_router_expert, b_router_expert, w_exp_gate, w_exp_up, w_exp_down):
    depth = w_in.shape[0]
    assert depth == 1, "single-layer step only"
    bp, tp, d = x_prompt.shape
    bs, ts, _ = x_sample.shape
    past = cache_a_k.shape[2]
    w_in0, w_out0 = w_in[0], w_out[0]
    wg, wu, wd = w_exp_gate[0], w_exp_up[0], w_exp_down[0]
    ffn_w = (w_out0, g_norm_ffn[0], w_router_group[0], b_router_group[0], w_router_expert[0], b_router_expert[0],
             wg, wu, wd)

    (ka_p, va_p, kb_p, vb_p, logf_p, qa_t, ka_bf, va_t, qb_t, kb_aug, vb_t, c_edge) = _projection(
        x_prompt, g_norm_mix[0], w_in0, b_forget[0], g_qk_a[0], g_qk_b[0], with_aug=True)
    blk = min(ATTN_BLOCK, tp)
    bk = va_t.shape[-1]
    assert blk % CHUNK == 0 and blk >= MAX_DISTANCE and blk % bk == 0
    q_pos = blk + jnp.arange(blk, dtype=jnp.int32)
    bkt_p = jnp.stack([_bucket_map(q_pos, s * bk + jnp.arange(bk, dtype=jnp.int32)).T
                       for s in range(2 * blk // bk)] + [jnp.full((bk, blk), -1, jnp.int32)] * (blk // bk))
    bias_p = _bias_tiles(rel_bias, bkt_p)
    o_a = _diff_attention(qa_t, ka_bf, va_t, bias_p, lambda_qk[0], g_out_a[0])
    o_b = _fox_attention(qb_t, kb_aug, vb_t, c_edge, g_qk_b[0], g_out_b[0])
    y_p = _ffn(o_a, o_b, x_prompt, *ffn_w)

    xs = x_sample.reshape(1, bs * ts, d)
    (ka_s, va_s, kb_s, vb_s, logf_s, qa_s, qb_s) = _projection(
        xs, g_norm_mix[0], w_in0, b_forget[0], g_qk_a[0], g_qk_b[0], with_aug=False)
    per_stream = lambda a: a.reshape(bs, ts, a.shape[-1])
    ka_s, va_s, kb_s, vb_s, logf_s, qa_s, qb_s = map(per_stream, (ka_s, va_s, kb_s, vb_s, logf_s, qa_s, qb_s))
    ck = min(CACHE_CHUNK, past)
    q_pos = past + jnp.arange(ts, dtype=jnp.int32)
    bkt_c = _bucket_map(q_pos, jnp.arange(past, dtype=jnp.int32)).reshape(ts, past // ck, ck).transpose(1, 0, 2)
    bias_c = _bias_tiles(rel_bias, bkt_c)
    bias_n = _bias_tiles(rel_bias, _bucket_map(q_pos, q_pos)[None])
    o_a_s = _diff_decode(qa_s, ka_s, va_s, jnp.transpose(cache_a_k[0], (0, 2, 3, 4, 1)),
                         cache_a_v[0].reshape(bs, past * H_A, 2 * HEAD_DIM), bias_c, bias_n, lambda_qk[0], g_out_a[0])
    o_b_s = _fox_decode(qb_s, kb_s, vb_s, logf_s, jnp.transpose(cache_b_k[0], (0, 2, 3, 1)),
                        jnp.transpose(cache_b_v[0], (0, 2, 3, 1)), jnp.transpose(cache_b_logf[0], (0, 2, 1)),
                        g_qk_b[0], g_out_b[0])
    y_s = _ffn(o_a_s, o_b_s, x_sample, *ffn_w)

    def rows(ka, va, kb, vb, logf, b, t):
        return (ka.reshape(1, b, t, H_A, 2, HEAD_DIM), va.reshape(1, b, t, H_A, 2 * HEAD_DIM),
                kb.reshape(1, b, t, H_B, HEAD_DIM), vb.reshape(1, b, t, H_B, HEAD_DIM), logf.reshape(1, b, t, H_B))

    return (y_p, y_s) + rows(ka_p, va_p, kb_p, vb_p, logf_p, bp, tp) + rows(ka_s, va_s, kb_s, vb_s, logf_s, bs, ts)
```

```python
import functools
import math

import jax
import jax.numpy as jnp
from jax import lax
from jax.experimental import pallas as pl
from jax.experimental.pallas import tpu as pltpu

F32 = jnp.float32
BF16 = jnp.bfloat16

LANES = 128
VMEM_LIMIT_BYTES = 56 * 1024 * 1024

HEAD_DIM = 64
H_A = 4
H_B = 8
GROUP_W = 512
MAIN_W = 6 * GROUP_W
CHUNK = 64
N_BUCKETS = 32
MAX_DISTANCE = 128
N_GROUPS = 4
EXPERTS_PER_GROUP = 8
N_EXPERTS = N_GROUPS * EXPERTS_PER_GROUP
ROUTER_LANE0 = N_GROUPS
RMS_EPS = 1e-6
NEG = -1e30
LOG2E = 1.4426950408889634
QK_SCALE = HEAD_DIM ** -0.5
LAM_INIT = 0.8 - 0.6 * math.exp(-0.3 * 0)
N_CPARTS = 3

ONES_ROWS = 16
VA_ROWS = 2 * HEAD_DIM + ONES_ROWS
VB_ROWS = HEAD_DIM + ONES_ROWS

TOKEN_TILE = 512
ATTN_BLOCK = 512
KV_BLOCK = 256
FAR_UNROLL = 4
CACHE_CHUNK = 1024
MOE_TILE = 1024
MOE_ROW_TILE = 256
ROUTED_MIN_ASSIGNMENTS = 4 * N_EXPERTS * MOE_ROW_TILE


def _cparams(sem):
    return pltpu.CompilerParams(dimension_semantics=sem, vmem_limit_bytes=VMEM_LIMIT_BYTES)


def _const_spec(shape):
    nd = len(shape)
    return pl.BlockSpec(shape, lambda *_: (0,) * nd)


def _split3(x):
    p1 = x.astype(BF16).astype(F32)
    r1 = x - p1
    p2 = r1.astype(BF16).astype(F32)
    p3 = (r1 - p2).astype(BF16).astype(F32)
    return p1, p2, p3


def _lane_groups(parts, lane):
    return jnp.where(lane < 8, parts[0], jnp.where(lane < 16, parts[1], parts[2]))


def _cumsum_rows(tri, x, carry):
    c = carry
    for part in _split3(x):
        c = c + jnp.dot(tri, part.astype(BF16), preferred_element_type=F32)
    return c


def _log_sigmoid(x):
    return jnp.minimum(x, 0.0) - jnp.log1p(jnp.exp(-jnp.abs(x)))


def _group_rms(raw, gain_row, gn):
    ms = jnp.dot((raw * raw).astype(BF16), gn, preferred_element_type=F32)
    return raw * lax.rsqrt(ms + RMS_EPS) * gain_row


def _online_update(state, s, v):
    m, l, acc = state
    m_new = jnp.maximum(m, jnp.max(s, axis=-1, keepdims=True))
    p = jnp.exp2(s - m_new)
    alpha = jnp.exp2(m - m_new)
    l_new = alpha * l + jnp.sum(p, axis=-1, keepdims=True)
    acc_new = alpha * acc + jnp.dot(p.astype(BF16), v, preferred_element_type=F32)
    return m_new, l_new, acc_new


def _qk(q, k):
    return lax.dot_general(q, k, (((1,), (1,)), ((), ())), preferred_element_type=F32)


def _proj_body(*refs, with_aug, bt, bk):
    (x_ref, g1_ref, wm_ref, wf_ref, bf_ref, gains_ref, gn_ref, tri_ref, place_ref) = refs[:9]
    ka_ref, va_ref, kb_ref, vb_ref, logf_ref = refs[9:14]
    x = x_ref[0]
    ms = jnp.mean(x * x, axis=-1, keepdims=True)
    xn = (x * lax.rsqrt(ms + RMS_EPS)) * g1_ref[...]
    xb = xn.astype(BF16)
    proj = jnp.dot(xb, wm_ref[...], preferred_element_type=F32)
    gains = gains_ref[...]
    gn = gn_ref[...]
    w = GROUP_W
    qa = _group_rms(proj[:, 0:w], gains[0:1], gn)
    ka = _group_rms(proj[:, w:2 * w], gains[1:2], gn)
    va = proj[:, 2 * w:3 * w]
    qb = _group_rms(proj[:, 3 * w:4 * w], gains[2:3], gn)
    kb = _group_rms(proj[:, 4 * w:5 * w], gains[3:4], gn)
    vb = proj[:, 5 * w:6 * w]
    ka_ref[0] = ka
    if with_aug:
        for h in range(H_A):
            va_ref[0, pl.ds(h, bt, stride=H_A), :] = va[:, h * LANES:(h + 1) * LANES]
    else:
        va_ref[0] = va
    kb_ref[0] = kb
    vb_ref[0] = vb
    fl = jnp.dot(xb, wf_ref[...], preferred_element_type=F32) + bf_ref[...]
    logf = _log_sigmoid(fl)
    logf_ref[0] = logf[:, 0:H_B]
    qscale = QK_SCALE * LOG2E
    if not with_aug:
        qa_ref, qb_ref = refs[14:16]
        qa_ref[0] = (qa * qscale).astype(BF16)
        qb_ref[0] = (qb * qscale).astype(BF16)
        return
    qa_t, ka_bf, va_t, qb_t, kb_aug, vb_t, cedge_ref, carry_ref = refs[14:22]
    n_chunk = bt // bk
    row = lax.broadcasted_iota(jnp.int32, (LANES, bt), 0)
    ones_tail = jnp.where(lax.broadcasted_iota(jnp.int32, (ONES_ROWS, bt), 0) == 0, 1.0, 0.0)

    def put_chunks(ref, idx, vt):
        vt = vt.astype(BF16)
        for c in range(n_chunk):
            ref[0, idx, c] = vt[:, c * bk:(c + 1) * bk]

    for h in range(H_A):
        sl = slice(h * LANES, (h + 1) * LANES)
        q_t = (qa[:, sl] * qscale).T
        qa_t[0, 2 * h] = jnp.where(row < HEAD_DIM, q_t, 0.0).astype(BF16)
        qa_t[0, 2 * h + 1] = jnp.where(row >= HEAD_DIM, q_t, 0.0).astype(BF16)
        ka_bf[0, h] = ka[:, sl].astype(BF16)
        put_chunks(va_t, h, jnp.concatenate([va[:, sl].T, ones_tail], axis=0))
        vb_pair_t = vb[:, sl].T
        for hh in range(2):
            put_chunks(vb_t, 2 * h + hh,
                       jnp.concatenate([vb_pair_t[hh * HEAD_DIM:(hh + 1) * HEAD_DIM], ones_tail], axis=0))

    @pl.when(pl.program_id(1) == 0)
    def _():
        carry_ref[...] = jnp.zeros_like(carry_ref)

    c = _cumsum_rows(tri_ref[...], logf * LOG2E, carry_ref[0:1, :])
    carry_ref[0:1, :] = c[bt - 1:bt, :]
    edge_row = lax.broadcasted_iota(jnp.int32, (8, LANES), 0)
    cedge_ref[0, 0] = jnp.where(edge_row == 0, c[0:1, :], jnp.where(edge_row == 1, c[bt - 1:bt, :], 0.0))
    lane = lax.broadcasted_iota(jnp.int32, (bt, LANES), 1)
    cparts = _lane_groups(_split3(c), lane).astype(BF16)
    extras = jnp.dot(cparts, place_ref[...], preferred_element_type=F32)
    q_parts_end = HEAD_DIM + N_CPARTS
    ones_q = jnp.where((lane >= q_parts_end) & (lane < q_parts_end + N_CPARTS), 1.0, 0.0)
    ones_k = jnp.where((lane >= HEAD_DIM) & (lane < q_parts_end), 1.0, 0.0)
    for h in range(H_B):
        sl = slice((h // 2) * LANES, (h // 2 + 1) * LANES)
        qp = qb[:, sl] * qscale
        kp = kb[:, sl]
        if h % 2:
            qp = pltpu.roll(qp, HEAD_DIM, 1)
            kp = pltpu.roll(kp, HEAD_DIM, 1)
        e = extras[:, h * LANES:(h + 1) * LANES]
        qb_t[0, h] = jnp.where(lane < HEAD_DIM, qp, jnp.where(lane < q_parts_end, e, ones_q)).T.astype(BF16)
        kb_aug[0, h] = jnp.where(lane < HEAD_DIM, kp, jnp.where(lane < q_parts_end, ones_k, e)).astype(BF16)


def _blockdiag_mean(n, group):
    r = jnp.arange(n)
    return jnp.where((r[:, None] // group) == (r[None, :] // group), 1.0 / group, 0.0).astype(BF16)


def _tri(n):
    r = jnp.arange(n)
    return (r[None, :] <= r[:, None]).astype(BF16)


def _place_matrix():
    rows = jnp.arange(LANES)[:, None]
    cols = jnp.arange(H_B * LANES)[None, :]
    p, h = rows // 8, rows % 8
    valid = rows < 8 * N_CPARTS
    qcol = h * LANES + HEAD_DIM + p
    kcol = h * LANES + HEAD_DIM + N_CPARTS + p
    m = jnp.where(valid & (cols == qcol), 1.0, 0.0) - jnp.where(valid & (cols == kcol), 1.0, 0.0)
    return m.astype(BF16)


def _projection(x, g_norm, w_in, b_forget, g_qk_a, g_qk_b, *, with_aug):
    b, t, d = x.shape
    bt = min(TOKEN_TILE, t)
    nt = t // bt
    wm = w_in[:, :MAIN_W].astype(BF16)
    wf_cols = w_in[:, MAIN_W:MAIN_W + H_B]
    wf = jnp.concatenate([wf_cols] * N_CPARTS + [jnp.zeros((d, LANES - H_B * N_CPARTS), F32)], axis=1).astype(BF16)
    bfv = jnp.concatenate([b_forget] * N_CPARTS + [jnp.zeros((LANES - H_B * N_CPARTS,), F32)])[None, :]
    gains = jnp.stack([jnp.tile(g_qk_a[0], 2 * H_A), jnp.tile(g_qk_a[1], 2 * H_A),
                       jnp.tile(g_qk_b[0], H_B), jnp.tile(g_qk_b[1], H_B)])
    gn = _blockdiag_mean(GROUP_W, HEAD_DIM)
    tri = _tri(bt)
    place = _place_matrix()
    in_specs = [
        pl.BlockSpec((1, bt, d), lambda i, j: (i, j, 0)),
        _const_spec((1, d)),
        pl.BlockSpec((d, MAIN_W), lambda i, j: (0, 0), pipeline_mode=pl.Buffered(1)),
        _const_spec((d, LANES)), _const_spec((1, LANES)), _const_spec((4, GROUP_W)),
        _const_spec((GROUP_W, GROUP_W)), _const_spec((bt, bt)), _const_spec((LANES, H_B * LANES)),
    ]
    row_spec = pl.BlockSpec((1, bt, GROUP_W), lambda i, j: (i, j, 0))
    out_shape = [jax.ShapeDtypeStruct((b, t, GROUP_W), F32)] * 4 + [jax.ShapeDtypeStruct((b, t, H_B), F32)]
    out_specs = [row_spec] * 4 + [pl.BlockSpec((1, bt, H_B), lambda i, j: (i, j, 0))]
    scratch = []
    bk = min(KV_BLOCK, bt)
    if with_aug:
        out_shape[1] = jax.ShapeDtypeStruct((b, t * H_A, 2 * HEAD_DIM), F32)
        out_specs[1] = pl.BlockSpec((1, bt * H_A, 2 * HEAD_DIM), lambda i, j: (i, j, 0))

        def add(shape, block, index_map):
            out_shape.append(jax.ShapeDtypeStruct(shape, BF16))
            out_specs.append(pl.BlockSpec(block, index_map))

        rows_major = lambda i, j: (i, 0, j, 0)
        time_minor = lambda i, j: (i, 0, 0, j)
        chunked = lambda i, j: (i, 0, j, 0, 0)
        add((b, 2 * H_A, LANES, t), (1, 2 * H_A, LANES, bt), time_minor)
        add((b, H_A, t, LANES), (1, H_A, bt, LANES), rows_major)
        add((b, H_A, t // bk, VA_ROWS, bk), (1, H_A, bt // bk, VA_ROWS, bk), chunked)
        add((b, H_B, LANES, t), (1, H_B, LANES, bt), time_minor)
        add((b, H_B, t, LANES), (1, H_B, bt, LANES), rows_major)
        add((b, H_B, t // bk, VB_ROWS, bk), (1, H_B, bt // bk, VB_ROWS, bk), chunked)
        out_shape.append(jax.ShapeDtypeStruct((b, nt, 8, LANES), F32))
        out_specs.append(pl.BlockSpec((1, 1, 8, LANES), lambda i, j: (i, j, 0, 0)))
        scratch = [pltpu.VMEM((8, LANES), F32)]
    else:
        out_shape += [jax.ShapeDtypeStruct((b, t, GROUP_W), BF16)] * 2
        out_specs += [row_spec] * 2
    return pl.pallas_call(
        functools.partial(_proj_body, with_aug=with_aug, bt=bt, bk=bk),
        grid=(b, nt), in_specs=in_specs, out_specs=out_specs, out_shape=out_shape, scratch_shapes=scratch,
        compiler_params=_cparams(("arbitrary", "arbitrary")),
        name="proj_aug" if with_aug else "proj_plain",
    )(x, g_norm[None, :], wm, wf, bfv, gains, gn, tri, place)


def _t5_bucket(rel):
    nb = N_BUCKETS // 2
    max_exact = nb // 2
    base = jnp.where(rel > 0, nb, 0)
    n = jnp.abs(rel)
    large = max_exact + (jnp.log(jnp.maximum(n, max_exact).astype(jnp.float32) / max_exact)
                         / math.log(MAX_DISTANCE / max_exact) * (nb - max_exact)).astype(jnp.int32)
    large = jnp.minimum(large, nb - 1)
    return base + jnp.where(n < max_exact, n, large)


def _bucket_map(q_pos, k_pos):
    bkt = _t5_bucket(k_pos[None, :] - q_pos[:, None])
    visible = (k_pos[None, :] // CHUNK) <= (q_pos[:, None] // CHUNK)
    return jnp.where(visible, bkt, -1).astype(jnp.int32)


def _bias_body(rb_ref, bkt_ref, o_ref):
    h = pl.program_id(0)
    bkt = bkt_ref[0]
    far = rb_ref[N_BUCKETS // 2 - 1, h]
    acc = jnp.zeros(bkt.shape, F32)
    for b in range(N_BUCKETS):
        acc = jnp.where(bkt == b, rb_ref[b, h] - far, acc)
    o_ref[0, 0] = jnp.where(bkt < 0, NEG, acc * LOG2E)


def _bias_tiles(rel_bias, bkt):
    n, r, c = bkt.shape
    return pl.pallas_call(
        _bias_body,
        grid=(H_A, n),
        in_specs=[pl.BlockSpec(memory_space=pltpu.SMEM), pl.BlockSpec((1, r, c), lambda h, i: (i, 0, 0))],
        out_specs=pl.BlockSpec((1, 1, r, c), lambda h, i: (h, i, 0, 0)),
        out_shape=jax.ShapeDtypeStruct((H_A, n, r, c), F32),
        compiler_params=_cparams(("arbitrary", "arbitrary")),
        name="bias_tiles",
    )(rel_bias, bkt)


def _lam(lq):
    a = jnp.sum(lq[0:1, :] * lq[1:2, :], axis=-1, keepdims=True)
    b = jnp.sum(lq[2:3, :] * lq[3:4, :], axis=-1, keepdims=True)
    return jnp.exp(a) - jnp.exp(b) + LAM_INIT


def _attn_init_t(rows, bq):
    return (jnp.full((1, bq), NEG, F32), jnp.zeros((rows, bq), F32))


def _online_update_t(state, st, vt):
    m, acc = state
    m_new = jnp.maximum(m, jnp.max(st, axis=0, keepdims=True))
    p = jnp.exp2(st - m_new)
    alpha = jnp.exp2(m - m_new)
    return m_new, alpha * acc + jnp.dot(vt, p.astype(BF16), preferred_element_type=F32)


def _normalized_t(state, rows):
    acc = state[1]
    return acc[0:rows] / acc[rows:rows + 1]


def _pipelined_sweep(i, n_sub, n_near, slots, score_fn, value_fn, modify, rows, blk, first=0):
    assert n_sub % 2 == 0 and n_near in (1, 2)

    def run_block(jb, states, near, last, next_start=None):
        states = list(states)
        for s in range(n_sub):
            j = jb * n_sub + s
            cur, nxt = slots[s % 2], slots[(s + 1) % 2]
            ahead = next_start if (next_start is not None and s == n_sub - 1) else j + 1
            for c in range(2):
                if not (last and s == n_sub - 1):
                    nxt[c] = score_fn(c, ahead)
                st = cur[c]
                if near is not None:
                    st = modify(st, near, s)
                states[c] = _online_update_t(states[c], st, value_fn(c, j))
        return tuple(states)

    for c in range(2):
        slots[0][c] = score_fn(c, first * n_sub)
    states = (_attn_init_t(rows, blk), _attn_init_t(rows, blk))
    n_far = jnp.maximum(i + 1 - n_near, 0)

    def run_far(jb, states, n_blocks):
        for d in range(n_blocks):
            states = run_block(jb + d, states, None, False)
        return states

    n_groups = jnp.maximum(n_far - first, 0) // FAR_UNROLL
    states = lax.fori_loop(0, n_groups, lambda g, st: run_far(first + g * FAR_UNROLL, st, FAR_UNROLL), states)
    done = first + n_groups * FAR_UNROLL
    n_pairs = jnp.maximum(n_far - done, 0) // 2
    states = lax.fori_loop(0, n_pairs, lambda g, st: run_far(done + 2 * g, st, 2), states)
    states = lax.fori_loop(done + 2 * n_pairs, n_far, lambda jb, st: run_far(jb, st, 1), states)
    if n_near == 2:
        states = run_block(jnp.maximum(i - 1, 0), states, 1, False, next_start=i * n_sub)
    return run_block(i, states, 0, True)


def _fox_body(first_ref, qt_ref, k_ref, vt_ref, gout_ref, gn_ref, o_ref, s0_scr, s1_scr, *, blk, bk):
    i = pl.program_id(2)
    first = first_ref[(pl.program_id(0) * pl.num_programs(1) + pl.program_id(1)) * pl.num_programs(2) + i]
    krow = lax.broadcasted_iota(jnp.int32, (bk, blk), 0)
    qcol = lax.broadcasted_iota(jnp.int32, (bk, blk), 1)
    qts = (qt_ref[0, 0], qt_ref[0, 1])

    def score_fn(hh, j):
        off = pl.multiple_of(j * bk, bk)
        return jnp.dot(k_ref[0, hh, pl.ds(off, bk), :], qts[hh], preferred_element_type=F32)

    def causal(st, near, s):
        return jnp.where(krow + s * bk <= qcol, st, NEG)

    states = _pipelined_sweep(i, blk // bk, 1, (s0_scr, s1_scr), score_fn, lambda hh, j: vt_ref[0, hh, j],
                              causal, VB_ROWS, blk, first=first)
    o_t = jnp.concatenate([_normalized_t(states[0], HEAD_DIM), _normalized_t(states[1], HEAD_DIM)], axis=0)
    o_ref[0] = (_group_rms(o_t.T, gout_ref[0], gn_ref[...])).astype(BF16)


EXP2_ZERO_BELOW = -150.0
BOUND_SLACK = 1.02


def _fox_score_spread(g_qk_b):
    qk_max = HEAD_DIM * jnp.max(jnp.abs(g_qk_b[0])) * jnp.max(jnp.abs(g_qk_b[1])) * QK_SCALE * LOG2E
    return 2.0 * qk_max * BOUND_SLACK


def _fox_skip_plan(c_edge, g_qk_b):
    b, nq = c_edge.shape[:2]
    c_first = c_edge[:, :, 0, :H_B]
    c_last = c_edge[:, :, 1, :H_B]
    best = (_fox_score_spread(g_qk_b)
            + (c_first[:, :, None, :] - c_last[:, None, :, :]) * (1.0 / BOUND_SLACK))
    dead = best < EXP2_ZERO_BELOW - 1.0
    dead = jnp.logical_and(dead[..., 0::2], dead[..., 1::2])
    j_lt_i = (jnp.arange(nq)[None, :] < jnp.arange(nq)[:, None])[None, :, :, None]
    lead = jnp.cumprod(jnp.logical_and(dead, j_lt_i).astype(jnp.int32), axis=2)
    first = jnp.sum(lead, axis=2)
    return jnp.transpose(first, (0, 2, 1)).reshape(-1).astype(jnp.int32)


def _fox_attention(qb_t, kb_aug, vb_t, c_edge, g_qk_b, g_out_b):
    b, _, _, t = qb_t.shape
    blk = min(ATTN_BLOCK, t)
    bk = vb_t.shape[-1]
    pairs = H_B // 2
    gout = g_out_b.reshape(pairs, 1, LANES)
    first = _fox_skip_plan(c_edge, g_qk_b)
    return pl.pallas_call(
        functools.partial(_fox_body, blk=blk, bk=bk),
        grid_spec=pltpu.PrefetchScalarGridSpec(
            num_scalar_prefetch=1, grid=(b, pairs, t // blk),
            in_specs=[
                pl.BlockSpec((1, 2, LANES, blk), lambda bi, p, i, f: (bi, p, 0, i)),
                pl.BlockSpec((1, 2, t, LANES), lambda bi, p, i, f: (bi, p, 0, 0)),
                pl.BlockSpec((1, 2, t // bk, VB_ROWS, bk), lambda bi, p, i, f: (bi, p, 0, 0, 0)),
                pl.BlockSpec((1, 1, LANES), lambda bi, p, i, f: (p, 0, 0)),
                pl.BlockSpec((LANES, LANES), lambda bi, p, i, f: (0, 0)),
            ],
            out_specs=pl.BlockSpec((1, blk, LANES), lambda bi, p, i, f: (bi, i, p)),
            scratch_shapes=[pltpu.VMEM((2, bk, blk), F32), pltpu.VMEM((2, bk, blk), F32)]),
        out_shape=jax.ShapeDtypeStruct((b, t, H_B * HEAD_DIM), BF16),
        compiler_params=_cparams(("arbitrary", "arbitrary", "arbitrary")),
        name="fox_attention",
    )(first, qb_t, kb_aug, vb_t, gout, _blockdiag_mean(LANES, HEAD_DIM))


def _diff_finish(states, lam, gout):
    (_, l1, a1), (_, l2, a2) = states
    o = a1 / l1 - lam * (a2 / l2)
    ms = jnp.mean(o * o, axis=-1, keepdims=True)
    return (o * lax.rsqrt(ms + RMS_EPS)) * gout * (1.0 - LAM_INIT)


def _diff_body(lam_ref, qt_ref, k_ref, vt_ref, bias_ref, gout_ref, o_ref, s0_scr, s1_scr, *, blk, bk):
    i = pl.program_id(2)
    n_sub = blk // bk
    qts = (qt_ref[0, 0], qt_ref[0, 1])

    def score_fn(mi, j):
        off = pl.multiple_of(j * bk, bk)
        return jnp.dot(k_ref[0, 0, pl.ds(off, bk), :], qts[mi], preferred_element_type=F32)

    def add_bias(st, near, s):
        if near == 0:
            return st + bias_ref[0, n_sub + s]
        return st + bias_ref[0, jnp.where(i == 0, 2 * n_sub + s, s)]

    states = _pipelined_sweep(i, n_sub, 2, (s0_scr, s1_scr), score_fn, lambda mi, j: vt_ref[0, 0, j],
                              add_bias, VA_ROWS, blk)
    lam = _lam(lam_ref[...])
    o = (_normalized_t(states[0], 2 * HEAD_DIM) - lam * _normalized_t(states[1], 2 * HEAD_DIM)).T
    ms = jnp.mean(o * o, axis=-1, keepdims=True)
    o_ref[0] = ((o * lax.rsqrt(ms + RMS_EPS)) * gout_ref[0] * (1.0 - LAM_INIT)).astype(BF16)


def _diff_attention(qa_t, ka_bf, va_t, bias, lambda_qk, g_out_a):
    b, _, _, t = qa_t.shape
    blk = min(ATTN_BLOCK, t)
    bk = va_t.shape[-1]
    gout = g_out_a.reshape(H_A, 1, LANES)
    return pl.pallas_call(
        functools.partial(_diff_body, blk=blk, bk=bk),
        grid=(b, H_A, t // blk),
        in_specs=[
            _const_spec((4, HEAD_DIM)),
            pl.BlockSpec((1, 2, LANES, blk), lambda bi, h, i: (bi, h, 0, i)),
            pl.BlockSpec((1, 1, t, LANES), lambda bi, h, i: (bi, h, 0, 0)),
            pl.BlockSpec((1, 1, t // bk, VA_ROWS, bk), lambda bi, h, i: (bi, h, 0, 0, 0)),
            pl.BlockSpec((1, 3 * (blk // bk), bk, blk), lambda bi, h, i: (h, 0, 0, 0)),
            pl.BlockSpec((1, 1, LANES), lambda bi, h, i: (h, 0, 0)),
        ],
        out_specs=pl.BlockSpec((1, blk, LANES), lambda bi, h, i: (bi, i, h)),
        out_shape=jax.ShapeDtypeStruct((b, t, H_A * 2 * HEAD_DIM), BF16),
        scratch_shapes=[pltpu.VMEM((2, bk, blk), F32), pltpu.VMEM((2, bk, blk), F32)],
        compiler_params=_cparams(("arbitrary", "arbitrary", "arbitrary")),
        name="diff_attention",
    )(lambda_qk, qa_t, ka_bf, va_t, bias, gout)


def _dec_load(m_scr, l_scr, acc_scr, idx):
    return m_scr[idx], l_scr[idx], acc_scr[idx]


def _dec_store(m_scr, l_scr, acc_scr, idx, state):
    m_scr[idx], l_scr[idx], acc_scr[idx] = state


def _dec_init(m_scr, l_scr, acc_scr):
    m_scr[...] = jnp.full(m_scr.shape, NEG, F32)
    l_scr[...] = jnp.zeros(l_scr.shape, F32)
    acc_scr[...] = jnp.zeros(acc_scr.shape, F32)


def _stack_queries(q, n_blocks):
    col = lax.broadcasted_iota(jnp.int32, q.shape, 1)
    zero = jnp.zeros_like(q)
    return jnp.concatenate([jnp.where((col >= c * HEAD_DIM) & (col < (c + 1) * HEAD_DIM), q, zero)
                            for c in range(n_blocks)], axis=0)


def _stacked_update(state, s, pv_fn):
    m, l, acc = state
    m_new = jnp.maximum(m, jnp.max(s, axis=-1, keepdims=True))
    p = jnp.exp2(s - m_new)
    alpha = jnp.exp2(m - m_new)
    return m_new, alpha * l + jnp.sum(p, axis=-1, keepdims=True), alpha * acc + pv_fn(p.astype(BF16))


def _diff_dec_body(lam_ref, q_ref, kn_ref, vn_ref, kt_ref, v_ref, bc_ref, bn_ref, gout_ref, o_ref,
                   qbig_scr, m_scr, l_scr, acc_scr, *, ck, nq):
    kc = pl.program_id(1)
    n_maps = 2 * H_A

    @pl.when(kc == 0)
    def _():
        qbig_scr[...] = _stack_queries(q_ref[0], n_maps)
        _dec_init(m_scr, l_scr, acc_scr)

    qbig = qbig_scr[...]

    def bias_rows(ref):
        return jnp.concatenate([ref[h, 0] for h in range(H_A) for _ in range(2)], axis=0)

    def per_head(p, values_of):
        return jnp.concatenate([jnp.dot(p[2 * h * nq:(2 * h + 2) * nq], values_of(h), preferred_element_type=F32)
                                for h in range(H_A)], axis=0)

    kt_all = kt_ref[0].reshape(n_maps * HEAD_DIM, ck).astype(BF16)
    s = jnp.dot(qbig, kt_all, preferred_element_type=F32) + bias_rows(bc_ref)
    state = _stacked_update((m_scr[...], l_scr[...], acc_scr[...]), s, lambda p: per_head(
        p, lambda h: v_ref[0, pl.ds(h, ck, stride=H_A), :].astype(BF16)))
    m_scr[...], l_scr[...], acc_scr[...] = state

    @pl.when(kc == pl.num_programs(1) - 1)
    def _():
        v_new = vn_ref[0].astype(BF16)
        s_new = _qk(qbig, kn_ref[0].astype(BF16)) + bias_rows(bn_ref)
        m, l, acc = _stacked_update(state, s_new, lambda p: per_head(
            p, lambda h: v_new[:, h * LANES:(h + 1) * LANES]))
        lam = _lam(lam_ref[...])
        for h in range(H_A):
            r = 2 * h * nq
            maps = tuple((m[a:a + nq], l[a:a + nq], acc[a:a + nq]) for a in (r, r + nq))
            o_ref[0, :, h * LANES:(h + 1) * LANES] = _diff_finish(maps, lam, gout_ref[h]).astype(BF16)


def _diff_decode(qa, ka_new, va_new, cache_kt, cache_v, bias_c, bias_n, lambda_qk, g_out_a):
    b, nq, w = qa.shape
    past = cache_kt.shape[-1]
    ck = min(CACHE_CHUNK, past)
    n_kc = past // ck
    new_spec = pl.BlockSpec((1, nq, w), lambda bi, c: (bi, 0, 0))
    return pl.pallas_call(
        functools.partial(_diff_dec_body, ck=ck, nq=nq),
        grid=(b, n_kc),
        in_specs=[
            _const_spec((4, HEAD_DIM)), new_spec, new_spec, new_spec,
            pl.BlockSpec((1, H_A, 2, HEAD_DIM, ck), lambda bi, c: (bi, 0, 0, 0, c)),
            pl.BlockSpec((1, H_A * ck, LANES), lambda bi, c: (bi, c, 0)),
            pl.BlockSpec((H_A, 1, nq, ck), lambda bi, c: (0, c, 0, 0)),
            _const_spec((H_A, 1, nq, nq)),
            _const_spec((H_A, 1, LANES)),
        ],
        out_specs=new_spec,
        out_shape=jax.ShapeDtypeStruct((b, nq, w), BF16),
        scratch_shapes=[pltpu.VMEM((2 * H_A * nq, w), BF16),
                        pltpu.VMEM((2 * H_A * nq, 1), F32), pltpu.VMEM((2 * H_A * nq, 1), F32),
                        pltpu.VMEM((2 * H_A * nq, LANES), F32)],
        compiler_params=_cparams(("arbitrary", "arbitrary")),
        name="diff_decode",
    )(lambda_qk, qa, ka_new, va_new, cache_kt, cache_v, bias_c, bias_n, g_out_a.reshape(H_A, 1, LANES))


def _suffix_sum_lanes(x):
    n = x.shape[-1]
    lane = lax.broadcasted_iota(jnp.int32, x.shape, x.ndim - 1)
    shift = 1
    while shift < n:
        x = x + jnp.where(lane + shift < n, pltpu.roll(x, n - shift, x.ndim - 1), 0.0)
        shift *= 2
    return x


def _live_chunks_body(spread_ref, lfc_ref, o_ref, *, n_kc, ck):
    lf = lfc_ref[0] * LOG2E
    newer = jnp.zeros((H_B, 1), F32)
    count = jnp.zeros((1, 1), F32)
    alive = jnp.ones((1, 1), F32)
    for c in reversed(range(n_kc)):
        best = jnp.max(newer, axis=0, keepdims=True) + spread_ref[0]
        alive = alive * jnp.where(best >= EXP2_ZERO_BELOW - 1.0, 1.0, 0.0)
        count = count + alive
        newer = newer + jnp.sum(lf[:, c * ck:(c + 1) * ck], axis=-1, keepdims=True)
    o_ref[0] = jnp.broadcast_to(count, o_ref.shape[1:]).astype(jnp.int32)


def _live_chunks(cache_logf_t, g_qk_b, ck):
    b, _, past = cache_logf_t.shape
    counts = pl.pallas_call(
        functools.partial(_live_chunks_body, n_kc=past // ck, ck=ck),
        grid=(b,),
        in_specs=[pl.BlockSpec(memory_space=pltpu.SMEM), pl.BlockSpec((1, H_B, past), lambda bi: (bi, 0, 0))],
        out_specs=pl.BlockSpec((1, 8, LANES), lambda bi: (bi, 0, 0)),
        out_shape=jax.ShapeDtypeStruct((b, 8, LANES), jnp.int32),
        compiler_params=_cparams(("arbitrary",)),
        name="fox_live_chunks",
    )(_fox_score_spread(g_qk_b).reshape(1), cache_logf_t)
    return counts[:, 0, 0]


def _fox_dec_body(live_ref, q_ref, kn_ref, vn_ref, lfn_col_ref, lfn_row_ref, kt_ref, vt_ref, lfc_ref, gout_ref, gn_ref,
                  o_ref, qbig_scr, cq_scr, cn_scr, after_scr, m_scr, l_scr, acc_scr, *, nq):
    kc = pl.program_id(1)
    n_kc = after_scr.shape[0]
    ck = after_scr.shape[-1]

    def head_rows(x):
        return jnp.concatenate([jnp.broadcast_to(x[h:h + 1, :], (nq, x.shape[-1])) for h in range(H_B)], axis=0)

    @pl.when(kc == 0)
    def _():
        qbig_scr[...] = _stack_queries(q_ref[0], H_B)
        _dec_init(m_scr, l_scr, acc_scr)
        r = lax.broadcasted_iota(jnp.int32, (nq, nq), 0)
        c = lax.broadcasted_iota(jnp.int32, (nq, nq), 1)
        tri_n = jnp.where(c <= r, 1.0, 0.0).astype(BF16)
        cq = _cumsum_rows(tri_n, lfn_col_ref[0] * LOG2E, jnp.zeros((1, LANES), F32))
        cq_scr[...] = jnp.concatenate([jnp.broadcast_to(cq[:, h:h + 1], (nq, LANES)) for h in range(H_B)], axis=0)
        lf_row = lfn_row_ref[0] * LOG2E
        total = jnp.sum(lf_row, axis=-1, keepdims=True)
        cn_scr[...] = total - _suffix_sum_lanes(lf_row) + lf_row
        lf = lfc_ref[0] * LOG2E
        after = _suffix_sum_lanes(lf) - lf
        for c_idx in range(n_kc):
            after_scr[c_idx] = after[:, c_idx * ck:(c_idx + 1) * ck]

    after = after_scr[n_kc - 1 - kc]
    qbig = qbig_scr[...]
    cq_col = cq_scr[...][:, 0:1]
    live = kc < live_ref[pl.program_id(0)]

    @pl.when(live)
    def _():
        kt_all = kt_ref[0].reshape(H_B * HEAD_DIM, ck).astype(BF16)
        vt_all = vt_ref[0].reshape(H_B * HEAD_DIM, ck).astype(BF16)
        s = jnp.dot(qbig, kt_all, preferred_element_type=F32) + cq_col + head_rows(after)
        m_scr[...], l_scr[...], acc_scr[...] = _stacked_update(
            (m_scr[...], l_scr[...], acc_scr[...]), s, lambda p: _qk(p, vt_all))

    @pl.when(kc == pl.num_programs(1) - 1)
    def _():
        row = jnp.concatenate([lax.broadcasted_iota(jnp.int32, (nq, nq), 0)] * H_B, axis=0)
        causal = lax.broadcasted_iota(jnp.int32, (H_B * nq, nq), 1) <= row
        s = _qk(qbig, kn_ref[0].astype(BF16)) + cq_col - head_rows(cn_scr[...][:, 0:nq])
        v_new = vn_ref[0].astype(BF16)
        _, l, acc = _stacked_update((m_scr[...], l_scr[...], acc_scr[...]), jnp.where(causal, s, NEG),
                                    lambda p: jnp.dot(p, v_new, preferred_element_type=F32))
        outs = [acc[h * nq:(h + 1) * nq, h * HEAD_DIM:(h + 1) * HEAD_DIM] / l[h * nq:(h + 1) * nq]
                for h in range(H_B)]
        for p in range(H_B // 2):
            o = jnp.concatenate([outs[2 * p], outs[2 * p + 1]], axis=-1)
            o_ref[0, :, p * LANES:(p + 1) * LANES] = _group_rms(o, gout_ref[p], gn_ref[...]).astype(BF16)


def _fox_decode(qb, kb_new, vb_new, logf_new, cache_kt, cache_vt, cache_logf_t, g_qk_b, g_out_b):
    b, nq, w = qb.shape
    past = cache_kt.shape[-1]
    ck = min(CACHE_CHUNK, past)
    n_kc = past // ck
    assert nq <= LANES
    pad = jnp.zeros((b, nq, LANES - H_B), F32)
    lfn_col = jnp.concatenate([logf_new, pad], axis=-1)
    lfn_row = jnp.concatenate([jnp.transpose(logf_new, (0, 2, 1)), jnp.zeros((b, H_B, LANES - nq), F32)], axis=-1)
    n_live = _live_chunks(cache_logf_t, g_qk_b, ck)
    per_stream = lambda width: pl.BlockSpec((1, nq, width), lambda bi, c, nl: (bi, 0, 0))

    def newest_first(bi, c, nl):
        return (bi, 0, 0, n_kc - 1 - jnp.minimum(c, jnp.maximum(nl[bi] - 1, 0)))

    return pl.pallas_call(
        functools.partial(_fox_dec_body, nq=nq),
        grid_spec=pltpu.PrefetchScalarGridSpec(
            num_scalar_prefetch=1, grid=(b, n_kc),
            in_specs=[
                per_stream(w), per_stream(w), per_stream(w), per_stream(LANES),
                pl.BlockSpec((1, H_B, LANES), lambda bi, c, nl: (bi, 0, 0)),
                pl.BlockSpec((1, H_B, HEAD_DIM, ck), newest_first),
                pl.BlockSpec((1, H_B, HEAD_DIM, ck), newest_first),
                pl.BlockSpec((1, H_B, past), lambda bi, c, nl: (bi, 0, 0)),
                pl.BlockSpec((H_B // 2, 1, LANES), lambda bi, c, nl: (0, 0, 0)),
                pl.BlockSpec((LANES, LANES), lambda bi, c, nl: (0, 0)),
            ],
            out_specs=per_stream(w),
            scratch_shapes=[pltpu.VMEM((H_B * nq, w), BF16),
                            pltpu.VMEM((H_B * nq, LANES), F32), pltpu.VMEM((H_B, LANES), F32),
                            pltpu.VMEM((n_kc, H_B, ck), F32),
                            pltpu.VMEM((H_B * nq, 1), F32), pltpu.VMEM((H_B * nq, 1), F32),
                            pltpu.VMEM((H_B * nq, w), F32)]),
        out_shape=jax.ShapeDtypeStruct((b, nq, w), BF16),
        compiler_params=_cparams(("arbitrary", "arbitrary")),
        name="fox_decode",
    )(n_live, qb, kb_new, vb_new, lfn_col, lfn_row, cache_kt, cache_vt, cache_logf_t,
      g_out_b.reshape(H_B // 2, 1, LANES), _blockdiag_mean(LANES, HEAD_DIM))


def _route(logits):
    lane_i = lax.broadcasted_iota(jnp.int32, logits.shape, 1)
    lane = lane_i.astype(F32)
    big = float(LANES)
    lg = jnp.where(lane_i < N_GROUPS, logits, NEG)
    mx = jnp.max(lg, axis=-1, keepdims=True)
    grp = jnp.min(jnp.where(lg == mx, lane, big), axis=-1, keepdims=True)
    p_grp = 1.0 / jnp.sum(jnp.exp(lg - mx), axis=-1, keepdims=True)
    e = lane_i - ROUTER_LANE0
    e_grp = lax.shift_right_arithmetic(e, 3).astype(F32)
    sel = (e >= 0) & (e < N_EXPERTS) & (e_grp == grp)
    v = jnp.where(sel, logits, NEG)
    v1 = jnp.max(v, axis=-1, keepdims=True)
    i1 = jnp.min(jnp.where(sel & (v == v1), lane, big), axis=-1, keepdims=True)
    sel2 = sel & (lane != i1)
    vv = jnp.where(sel2, logits, NEG)
    v2 = jnp.max(vv, axis=-1, keepdims=True)
    i2 = jnp.min(jnp.where(sel2 & (vv == v2), lane, big), axis=-1, keepdims=True)
    e2 = jnp.exp(v2 - v1)
    w1 = p_grp / (1.0 + e2)
    w2 = p_grp * e2 / (1.0 + e2)
    gates = jnp.where(lane == i1, w1, 0.0) + jnp.where(lane == i2, w2, 0.0)
    return gates, (i1, i2, w1, w2)


R_E1, R_E2, R_RANK1, R_RANK2, R_W1, R_W2 = range(6)


def _mix_body(oa_ref, ob_ref, x_ref, wa_ref, wb_ref, g2_ref, wr1_ref, wr2_ref, br_ref, tri_ref,
              x1_ref, xn_ref, gates_ref, route_ref, counts_ref):
    y = (jnp.dot(oa_ref[...], wa_ref[...], preferred_element_type=F32)
         + jnp.dot(ob_ref[...], wb_ref[...], preferred_element_type=F32))
    x1 = x_ref[...] + y
    x1_ref[...] = x1
    ms = jnp.mean(x1 * x1, axis=-1, keepdims=True)
    xn = (x1 * lax.rsqrt(ms + RMS_EPS)) * g2_ref[...]
    xn_ref[...] = xn
    h1 = xn.astype(BF16)
    h2 = (xn - h1.astype(F32)).astype(BF16)
    logits = (jnp.dot(h1, wr1_ref[...], preferred_element_type=F32)
              + jnp.dot(h1, wr2_ref[...], preferred_element_type=F32)
              + jnp.dot(h2, wr1_ref[...], preferred_element_type=F32)) + br_ref[...]
    gates, (i1, i2, w1, w2) = _route(logits)
    gates_ref[...] = gates

    @pl.when(pl.program_id(0) == 0)
    def _():
        counts_ref[...] = jnp.zeros_like(counts_ref)

    lane_i = lax.broadcasted_iota(jnp.int32, gates.shape, 1)
    lane = lane_i.astype(F32)
    oh1 = jnp.where(lane == i1, 1.0, 0.0)
    oh2 = jnp.where(lane == i2, 1.0, 0.0)
    comb = oh1 + oh2
    running = counts_ref[0:1, :]
    before = jnp.dot(tri_ref[...], comb.astype(BF16), preferred_element_type=F32) + running
    rank1 = jnp.sum(before * oh1, axis=-1, keepdims=True)
    rank2 = jnp.sum(before * oh2, axis=-1, keepdims=True)
    counts_ref[0:1, :] = running + jnp.sum(comb, axis=0, keepdims=True)
    rec = jnp.zeros_like(gates)
    for idx, val in ((R_E1, i1 - ROUTER_LANE0), (R_E2, i2 - ROUTER_LANE0), (R_RANK1, rank1), (R_RANK2, rank2),
                     (R_W1, w1), (R_W2, w2)):
        rec = jnp.where(lane_i == idx, val, rec)
    route_ref[...] = rec


def _mix_and_route(o_a, o_b, x, w_out, g_norm, w_rg, b_rg, w_re, b_re):
    n, d = x.shape
    bt = min(TOKEN_TILE, n)
    r = jnp.arange(bt)
    tri_strict = (r[None, :] < r[:, None]).astype(BF16)
    wa = w_out[:GROUP_W].astype(BF16)
    wb = w_out[GROUP_W:].astype(BF16)
    n_r = N_GROUPS + N_EXPERTS
    wr = jnp.concatenate([w_rg, w_re, jnp.zeros((d, LANES - n_r), F32)], axis=1)
    wr1 = wr.astype(BF16)
    wr2 = (wr - wr1.astype(F32)).astype(BF16)
    br = jnp.concatenate([b_rg, b_re, jnp.zeros((LANES - n_r,), F32)])[None, :]
    row = lambda width: pl.BlockSpec((bt, width), lambda i: (i, 0))
    return pl.pallas_call(
        _mix_body,
        grid=(n // bt,),
        in_specs=[row(GROUP_W), row(GROUP_W), row(d), _const_spec((GROUP_W, d)), _const_spec((GROUP_W, d)),
                  _const_spec((1, d)), _const_spec((d, LANES)), _const_spec((d, LANES)), _const_spec((1, LANES)),
                  _const_spec((bt, bt))],
        out_specs=[row(d), row(d), row(LANES), row(LANES), _const_spec((8, LANES))],
        out_shape=[jax.ShapeDtypeStruct((n, d), F32), jax.ShapeDtypeStruct((n, d), F32),
                   jax.ShapeDtypeStruct((n, LANES), F32), jax.ShapeDtypeStruct((n, LANES), F32),
                   jax.ShapeDtypeStruct((8, LANES), F32)],
        compiler_params=_cparams(("arbitrary",)),
        name="mix_route",
    )(o_a, o_b, x, wa, wb, g_norm[None, :], wr1, wr2, br, tri_strict)


def _swiglu(xn, wg, wu, wd, gate=None):
    x = xn.astype(BF16)
    g = jnp.dot(x, wg.astype(BF16), preferred_element_type=F32)
    u = jnp.dot(x, wu.astype(BF16), preferred_element_type=F32)
    h = (g * jax.nn.sigmoid(g)) * u
    if gate is not None:
        h = h * gate
    return jnp.dot(h.astype(BF16), wd.astype(BF16), preferred_element_type=F32)


def _expert_body(xn_ref, x1_ref, gates_ref, wg_ref, wu_ref, wd_ref, o_ref):
    e = pl.program_id(1)

    @pl.when(e == 0)
    def _():
        o_ref[...] = x1_ref[...]

    gates = gates_ref[...]
    lane = lax.broadcasted_iota(jnp.int32, gates.shape, 1)
    gate = jnp.sum(jnp.where(lane == e + ROUTER_LANE0, gates, 0.0), axis=-1, keepdims=True)
    o_ref[...] += _swiglu(xn_ref[...], wg_ref[0], wu_ref[0], wd_ref[0], gate)


def _experts(xn, x1, gates, w_gate, w_up, w_down):
    n, d = x1.shape
    ff = w_gate.shape[-1]
    bt = min(MOE_TILE, n)
    row = lambda width: pl.BlockSpec((bt, width), lambda i, e: (i, 0))
    return pl.pallas_call(
        _expert_body,
        grid=(n // bt, N_EXPERTS),
        in_specs=[row(d), row(d), row(LANES),
                  pl.BlockSpec((1, d, ff), lambda i, e: (e, 0, 0)),
                  pl.BlockSpec((1, d, ff), lambda i, e: (e, 0, 0)),
                  pl.BlockSpec((1, ff, d), lambda i, e: (e, 0, 0))],
        out_specs=row(d),
        out_shape=jax.ShapeDtypeStruct((n, d), F32),
        compiler_params=_cparams(("arbitrary", "arbitrary")),
        name="experts",
    )(xn, x1, gates, w_gate, w_up, w_down)


def _row_copies(n_rows, make_copy):
    def issue(r, carry):
        for s in range(2):
            make_copy(r, s).start()
        return carry

    lax.fori_loop(0, n_rows, issue, 0, unroll=8)

    def drain(r, carry):
        for s in range(2):
            make_copy(r, s).wait()
        return carry

    lax.fori_loop(0, n_rows, drain, 0, unroll=8)


def _row_position_body(route_ref, base_ref, pos_ref):
    rec = route_ref[...]
    lane_i = lax.broadcasted_iota(jnp.int32, rec.shape, 1)
    lane = lane_i.astype(F32)
    out = jnp.zeros(rec.shape, F32)
    for slot, (e_lane, r_lane) in enumerate(((R_E1, R_RANK1), (R_E2, R_RANK2))):
        onehot = lane == rec[:, e_lane:e_lane + 1] + float(ROUTER_LANE0)
        base = jnp.sum(jnp.where(onehot, base_ref[...], 0.0), axis=-1, keepdims=True)
        out = jnp.where(lane_i == slot, base + rec[:, r_lane:r_lane + 1], out)
    pos_ref[...] = out.astype(jnp.int32)


def _dispatch_body(last_tile_ref, pos_ref, x_ref, xs_ref, zero_scr, sem):
    @pl.when(pl.program_id(0) == 0)
    def _():
        zero_scr[...] = jnp.zeros_like(zero_scr)
        tm = zero_scr.shape[0]
        fills = [pltpu.make_async_copy(zero_scr, xs_ref.at[pl.ds(pl.multiple_of(last_tile_ref[e], tm), tm)], sem)
                 for e in range(N_EXPERTS)]
        for f in fills:
            f.start()
        for f in fills:
            f.wait()

        def fill_unused(t, carry):
            f = pltpu.make_async_copy(zero_scr, xs_ref.at[pl.ds(pl.multiple_of(t * tm, tm), tm)], sem)
            f.start()
            f.wait()
            return carry

        lax.fori_loop(last_tile_ref[N_EXPERTS], xs_ref.shape[0] // tm, fill_unused, 0)

    _row_copies(x_ref.shape[0], lambda r, s: pltpu.make_async_copy(
        x_ref.at[pl.ds(r, 1)], xs_ref.at[pl.ds(pos_ref[0, 0, 2 * r + s], 1)], sem))


def _grouped_body(te_ref, nu_ref, xs_ref, wg_ref, wu_ref, wd_ref, ys_ref, wg_bf, wu_bf, wd_bf):
    t = pl.program_id(0)
    used = t < nu_ref[0]
    new_expert = jnp.logical_or(t == 0, te_ref[t] != te_ref[jnp.maximum(t - 1, 0)])

    @pl.when(jnp.logical_and(used, new_expert))
    def _():
        wg_bf[...] = wg_ref[0].astype(BF16)
        wu_bf[...] = wu_ref[0].astype(BF16)
        wd_bf[...] = wd_ref[0].astype(BF16)

    @pl.when(used)
    def _():
        ys_ref[...] = _swiglu(xs_ref[...], wg_bf[...], wu_bf[...], wd_bf[...])

    @pl.when(jnp.logical_not(used))
    def _():
        ys_ref[...] = jnp.zeros_like(ys_ref)


def _combine_body(pos_ref, route_ref, x1_ref, ys_ref, o_ref, buf_scr, sem):
    _row_copies(x1_ref.shape[0], lambda r, s: pltpu.make_async_copy(
        ys_ref.at[pl.ds(pos_ref[0, 0, 2 * r + s], 1)], buf_scr.at[s, pl.ds(r, 1)], sem))
    rec = route_ref[...]
    o_ref[...] = (x1_ref[...] + rec[:, R_W1:R_W1 + 1] * buf_scr[0] + rec[:, R_W2:R_W2 + 1] * buf_scr[1])


def _routed_experts(xn, x1, route, counts, w_gate, w_up, w_down):
    n, d = x1.shape
    ff = w_gate.shape[-1]
    bt = min(TOKEN_TILE, n)
    nt = n // bt
    tm = MOE_ROW_TILE
    n_tiles = (2 * n) // tm + N_EXPERTS
    cnt = counts[0, ROUTER_LANE0:ROUTER_LANE0 + N_EXPERTS].astype(jnp.int32)
    tiles = (cnt + tm - 1) // tm
    tile_end = jnp.cumsum(tiles)
    base_row = (tile_end - tiles) * tm
    n_used = tile_end[-1:]
    tile_expert = jnp.minimum(jnp.sum(jnp.arange(n_tiles)[:, None] >= tile_end[None, :], axis=1), N_EXPERTS - 1)
    base_lanes = jnp.zeros((1, LANES), F32).at[0, ROUTER_LANE0:ROUTER_LANE0 + N_EXPERTS].set(base_row.astype(F32))
    row = lambda width: pl.BlockSpec((bt, width), lambda i: (i, 0))
    pos = pl.pallas_call(
        _row_position_body,
        grid=(nt,),
        in_specs=[row(LANES), _const_spec((1, LANES))],
        out_specs=row(LANES),
        out_shape=jax.ShapeDtypeStruct((n, LANES), jnp.int32),
        compiler_params=_cparams(("arbitrary",)),
        name="moe_positions",
    )(route, base_lanes)
    pos = pos[:, :2].reshape(nt, 1, 2 * bt)

    pos_spec = pl.BlockSpec((1, 1, 2 * bt), lambda i: (i, 0, 0), memory_space=pltpu.SMEM)
    any_spec = pl.BlockSpec(memory_space=pl.ANY)
    last_tile = jnp.minimum(base_row + jnp.maximum(tiles - 1, 0) * tm, (n_tiles - 1) * tm)
    last_tile = jnp.concatenate([last_tile, n_used]).astype(jnp.int32)
    xs = pl.pallas_call(
        _dispatch_body,
        grid_spec=pltpu.PrefetchScalarGridSpec(
            num_scalar_prefetch=1, grid=(nt,),
            in_specs=[pl.BlockSpec((1, 1, 2 * bt), lambda i, lt: (i, 0, 0), memory_space=pltpu.SMEM),
                      pl.BlockSpec((bt, d), lambda i, lt: (i, 0))],
            out_specs=any_spec,
            scratch_shapes=[pltpu.VMEM((tm, d), F32), pltpu.SemaphoreType.DMA(())]),
        out_shape=jax.ShapeDtypeStruct((n_tiles * tm, d), F32),
        compiler_params=_cparams(("arbitrary",)),
        name="moe_dispatch",
    )(last_tile, pos, xn)

    w_spec = lambda shape: pl.BlockSpec(shape, lambda t, te, nu: (te[t], 0, 0))
    ys = pl.pallas_call(
        _grouped_body,
        grid_spec=pltpu.PrefetchScalarGridSpec(
            num_scalar_prefetch=2, grid=(n_tiles,),
            in_specs=[pl.BlockSpec((tm, d), lambda t, te, nu: (jnp.minimum(t, nu[0] - 1), 0)),
                      w_spec((1, d, ff)), w_spec((1, d, ff)), w_spec((1, ff, d))],
            out_specs=pl.BlockSpec((tm, d), lambda t, te, nu: (t, 0)),
            scratch_shapes=[pltpu.VMEM((d, ff), BF16), pltpu.VMEM((d, ff), BF16), pltpu.VMEM((ff, d), BF16)]),
        out_shape=jax.ShapeDtypeStruct((n_tiles * tm, d), F32),
        compiler_params=_cparams(("arbitrary",)),
        name="moe_experts",
    )(tile_expert.astype(jnp.int32), n_used.astype(jnp.int32), xs, w_gate, w_up, w_down)

    return pl.pallas_call(
        _combine_body,
        grid=(nt,),
        in_specs=[pos_spec, row(LANES), row(d), any_spec],
        out_specs=row(d),
        out_shape=jax.ShapeDtypeStruct((n, d), F32),
        scratch_shapes=[pltpu.VMEM((2, bt, d), F32), pltpu.SemaphoreType.DMA(())],
        compiler_params=_cparams(("arbitrary",)),
        name="moe_combine",
    )(pos, route, x1, ys)


def _ffn(o_a, o_b, x, w_out, g_norm_ffn, w_rg, b_rg, w_re, b_re, wg, wu, wd):
    b, t, d = x.shape
    n = b * t
    x1, xn, gates, route, counts = _mix_and_route(o_a.reshape(n, -1), o_b.reshape(n, -1), x.reshape(n, d), w_out,
                                                  g_norm_ffn, w_rg, b_rg, w_re, b_re)
    if 2 * n >= ROUTED_MIN_ASSIGNMENTS:
        y = _routed_experts(xn, x1, route, counts, wg, wu, wd)
    else:
        y = _experts(xn, x1, gates, wg, wu, wd)
    return y.reshape(b, t, d)


def kernel(x_prompt, x_sample, cache_a_k, cache_a_v, cache_b_k, cache_b_v, cache_b_logf, g_norm_mix, w_in, b_forget, g_qk_a, g_qk_b, lambda_qk, g_out_a, g_out_b, w_out, rel_bias, g_norm_ffn, w_router_group, b_router_group, w_router_expert, b_router_expert, w_exp_gate, w_exp_up, w_exp_down):
    depth = w_in.shape[0]
    assert depth == 1, "single-layer step only"
    bp, tp, d = x_prompt.shape
    bs, ts, _ = x_sample.shape
    past = cache_a_k.shape[2]
    w_in0, w_out0 = w_in[0], w_out[0]
    wg, wu, wd = w_exp_gate[0], w_exp_up[0], w_exp_down[0]
    ffn_w = (w_out0, g_norm_ffn[0], w_router_group[0], b_router_group[0], w_router_expert[0], b_router_expert[0],
             wg, wu, wd)

    (ka_p, va_p, kb_p, vb_p, logf_p, qa_t, ka_bf, va_t, qb_t, kb_aug, vb_t, c_edge) = _projection(
        x_prompt, g_norm_mix[0], w_in0, b_forget[0], g_qk_a[0], g_qk_b[0], with_aug=True)
    blk = min(ATTN_BLOCK, tp)
    bk = va_t.shape[-1]
    assert blk % CHUNK == 0 and blk >= MAX_DISTANCE and blk % bk == 0
    q_pos = blk + jnp.arange(blk, dtype=jnp.int32)
    bkt_p = jnp.stack([_bucket_map(q_pos, s * bk + jnp.arange(bk, dtype=jnp.int32)).T
                       for s in range(2 * blk // bk)] + [jnp.full((bk, blk), -1, jnp.int32)] * (blk // bk))
    bias_p = _bias_tiles(rel_bias, bkt_p)
    o_a = _diff_attention(qa_t, ka_bf, va_t, bias_p, lambda_qk[0], g_out_a[0])
    o_b = _fox_attention(qb_t, kb_aug, vb_t, c_edge, g_qk_b[0], g_out_b[0])
    y_p = _ffn(o_a, o_b, x_prompt, *ffn_w)

    xs = x_sample.reshape(1, bs * ts, d)
    (ka_s, va_s, kb_s, vb_s, logf_s, qa_s, qb_s) = _projection(
        xs, g_norm_mix[0], w_in0, b_forget[0], g_qk_a[0], g_qk_b[0], with_aug=False)
    per_stream = lambda a: a.reshape(bs, ts, a.shape[-1])
    ka_s, va_s, kb_s, vb_s, logf_s, qa_s, qb_s = map(per_stream, (ka_s, va_s, kb_s, vb_s, logf_s, qa_s, qb_s))
    ck = min(CACHE_CHUNK, past)
    q_pos = past + jnp.arange(ts, dtype=jnp.int32)
    bkt_c = _bucket_map(q_pos, jnp.arange(past, dtype=jnp.int32)).reshape(ts, past // ck, ck).transpose(1, 0, 2)
    bias_c = _bias_tiles(rel_bias, bkt_c)
    bias_n = _bias_tiles(rel_bias, _bucket_map(q_pos, q_pos)[None])
    o_a_s = _diff_decode(qa_s, ka_s, va_s, jnp.transpose(cache_a_k[0], (0, 2, 3, 4, 1)),
                         cache_a_v[0].reshape(bs, past * H_A, 2 * HEAD_DIM), bias_c, bias_n, lambda_qk[0], g_out_a[0])
    o_b_s = _fox_decode(qb_s, kb_s, vb_s, logf_s, jnp.transpose(cache_b_k[0], (0, 2, 3, 1)),
                        jnp.transpose(cache_b_v[0], (0, 2, 3, 1)), jnp.transpose(cache_b_logf[0], (0, 2, 1)),
                        g_qk_b[0], g_out_b[0])
    y_s = _ffn(o_a_s, o_b_s, x_sample, *ffn_w)

    def rows(ka, va, kb, vb, logf, b, t):
        return (ka.reshape(1, b, t, H_A, 2, HEAD_DIM), va.reshape(1, b, t, H_A, 2 * HEAD_DIM),
                kb.reshape(1, b, t, H_B, HEAD_DIM), vb.reshape(1, b, t, H_B, HEAD_DIM), logf.reshape(1, b, t, H_B))

    return (y_p, y_s) + rows(ka_p, va_p, kb_p, vb_p, logf_p, bp, tp) + rows(ka_s, va_s, kb_s, vb_s, logf_s, bs, ts)
```

```python
import functools
import math

import jax
import jax.numpy as jnp
from jax import lax
from jax.experimental import pallas as pl
from jax.experimental.pallas import tpu as pltpu

F32 = jnp.float32
BF16 = jnp.bfloat16

LANES = 128
VMEM_LIMIT_BYTES = 56 * 1024 * 1024

HEAD_DIM = 64
H_A = 4
H_B = 8
GROUP_W = 512
MAIN_W = 6 * GROUP_W
CHUNK = 64
N_BUCKETS = 32
MAX_DISTANCE = 128
N_GROUPS = 4
EXPERTS_PER_GROUP = 8
N_EXPERTS = N_GROUPS * EXPERTS_PER_GROUP
EXPERT_GROUP_SHIFT = EXPERTS_PER_GROUP.bit_length() - 1
assert 1 << EXPERT_GROUP_SHIFT == EXPERTS_PER_GROUP
ROUTER_LANE0 = N_GROUPS
RMS_EPS = 1e-6
NEG = -1e30
LOG2E = 1.4426950408889634
QK_SCALE = HEAD_DIM ** -0.5
LAM_INIT = 0.8 - 0.6 * math.exp(-0.3 * 0)
N_CPARTS = 3

ONES_ROWS = 16
VA_ROWS = 2 * HEAD_DIM + ONES_ROWS
VB_ROWS = HEAD_DIM + ONES_ROWS

TOKEN_TILE = 512
ATTN_BLOCK = 512
KV_BLOCK = 256
FAR_UNROLL = 4
CACHE_CHUNK = 1024
MOE_TILE = 1024
MOE_ROW_TILE = 256
ROUTED_MIN_ASSIGNMENTS = 4 * N_EXPERTS * MOE_ROW_TILE


def _cparams(sem):
    return pltpu.CompilerParams(dimension_semantics=sem, vmem_limit_bytes=VMEM_LIMIT_BYTES)


def _const_spec(shape):
    nd = len(shape)
    return pl.BlockSpec(shape, lambda *_: (0,) * nd)


def _split3(x):
    p1 = x.astype(BF16).astype(F32)
    r1 = x - p1
    p2 = r1.astype(BF16).astype(F32)
    p3 = (r1 - p2).astype(BF16).astype(F32)
    return p1, p2, p3


def _lane_groups(parts, lane):
    return jnp.where(lane < 8, parts[0], jnp.where(lane < 16, parts[1], parts[2]))


def _cumsum_rows(tri, x, carry):
    c = carry
    for part in _split3(x):
        c = c + jnp.dot(tri, part.astype(BF16), preferred_element_type=F32)
    return c


def _log_sigmoid(x):
    return jnp.minimum(x, 0.0) - jnp.log1p(jnp.exp(-jnp.abs(x)))


def _group_rms(raw, gain_row, gn):
    ms = jnp.dot((raw * raw).astype(BF16), gn, preferred_element_type=F32)
    return raw * lax.rsqrt(ms + RMS_EPS) * gain_row


def _qk(q, k):
    return lax.dot_general(q, k, (((1,), (1,)), ((), ())), preferred_element_type=F32)


def _proj_body(*refs, with_aug, bt, bk):
    (x_ref, g1_ref, wm_ref, wf_ref, bf_ref, gains_ref, gn_ref, tri_ref, place_ref) = refs[:9]
    ka_ref, va_ref, kb_ref, vb_ref, logf_ref = refs[9:14]
    x = x_ref[0]
    ms = jnp.mean(x * x, axis=-1, keepdims=True)
    xn = (x * lax.rsqrt(ms + RMS_EPS)) * g1_ref[...]
    xb = xn.astype(BF16)
    proj = jnp.dot(xb, wm_ref[...], preferred_element_type=F32)
    gains = gains_ref[...]
    gn = gn_ref[...]
    w = GROUP_W
    qa = _group_rms(proj[:, 0:w], gains[0:1], gn)
    ka = _group_rms(proj[:, w:2 * w], gains[1:2], gn)
    va = proj[:, 2 * w:3 * w]
    qb = _group_rms(proj[:, 3 * w:4 * w], gains[2:3], gn)
    kb = _group_rms(proj[:, 4 * w:5 * w], gains[3:4], gn)
    vb = proj[:, 5 * w:6 * w]
    ka_ref[0] = ka
    if with_aug:
        for h in range(H_A):
            va_ref[0, pl.ds(h, bt, stride=H_A), :] = va[:, h * LANES:(h + 1) * LANES]
    else:
        va_ref[0] = va
    kb_ref[0] = kb
    vb_ref[0] = vb
    fl = jnp.dot(xb, wf_ref[...], preferred_element_type=F32) + bf_ref[...]
    logf = _log_sigmoid(fl)
    logf_ref[0] = logf[:, 0:H_B]
    qscale = QK_SCALE * LOG2E
    if not with_aug:
        qa_ref, qb_ref = refs[14:16]
        qa_ref[0] = (qa * qscale).astype(BF16)
        qb_ref[0] = (qb * qscale).astype(BF16)
        return
    qa_t, ka_bf, va_t, qb_t, kb_aug, vb_t, cedge_ref, carry_ref = refs[14:22]
    n_chunk = bt // bk
    row = lax.broadcasted_iota(jnp.int32, (LANES, bt), 0)
    ones_tail = jnp.where(lax.broadcasted_iota(jnp.int32, (ONES_ROWS, bt), 0) == 0, 1.0, 0.0)

    def put_chunks(ref, idx, vt):
        vt = vt.astype(BF16)
        for c in range(n_chunk):
            ref[0, idx, c] = vt[:, c * bk:(c + 1) * bk]

    for h in range(H_A):
        sl = slice(h * LANES, (h + 1) * LANES)
        q_t = (qa[:, sl] * qscale).T
        qa_t[0, 2 * h] = jnp.where(row < HEAD_DIM, q_t, 0.0).astype(BF16)
        qa_t[0, 2 * h + 1] = jnp.where(row >= HEAD_DIM, q_t, 0.0).astype(BF16)
        ka_bf[0, h] = ka[:, sl].astype(BF16)
        put_chunks(va_t, h, jnp.concatenate([va[:, sl].T, ones_tail], axis=0))
        vb_pair_t = vb[:, sl].T
        for hh in range(2):
            put_chunks(vb_t, 2 * h + hh,
                       jnp.concatenate([vb_pair_t[hh * HEAD_DIM:(hh + 1) * HEAD_DIM], ones_tail], axis=0))

    @pl.when(pl.program_id(1) == 0)
    def _():
        carry_ref[...] = jnp.zeros_like(carry_ref)

    c = _cumsum_rows(tri_ref[...], logf * LOG2E, carry_ref[0:1, :])
    carry_ref[0:1, :] = c[bt - 1:bt, :]
    edge_row = lax.broadcasted_iota(jnp.int32, (8, LANES), 0)
    cedge_ref[0, 0] = jnp.where(edge_row == 0, c[0:1, :], jnp.where(edge_row == 1, c[bt - 1:bt, :], 0.0))
    lane = lax.broadcasted_iota(jnp.int32, (bt, LANES), 1)
    cparts = _lane_groups(_split3(c), lane).astype(BF16)
    extras = jnp.dot(cparts, place_ref[...], preferred_element_type=F32)
    q_parts_end = HEAD_DIM + N_CPARTS
    ones_q = jnp.where((lane >= q_parts_end) & (lane < q_parts_end + N_CPARTS), 1.0, 0.0)
    ones_k = jnp.where((lane >= HEAD_DIM) & (lane < q_parts_end), 1.0, 0.0)
    for h in range(H_B):
        sl = slice((h // 2) * LANES, (h // 2 + 1) * LANES)
        qp = qb[:, sl] * qscale
        kp = kb[:, sl]
        if h % 2:
            qp = pltpu.roll(qp, HEAD_DIM, 1)
            kp = pltpu.roll(kp, HEAD_DIM, 1)
        e = extras[:, h * LANES:(h + 1) * LANES]
        qb_t[0, h] = jnp.where(lane < HEAD_DIM, qp, jnp.where(lane < q_parts_end, e, ones_q)).T.astype(BF16)
        kb_aug[0, h] = jnp.where(lane < HEAD_DIM, kp, jnp.where(lane < q_parts_end, ones_k, e)).astype(BF16)


def _blockdiag_mean(n, group):
    r = jnp.arange(n)
    return jnp.where((r[:, None] // group) == (r[None, :] // group), 1.0 / group, 0.0).astype(BF16)


def _tri(n):
    r = jnp.arange(n)
    return (r[None, :] <= r[:, None]).astype(BF16)


def _place_matrix():
    rows = jnp.arange(LANES)[:, None]
    cols = jnp.arange(H_B * LANES)[None, :]
    p, h = rows // 8, rows % 8
    valid = rows < 8 * N_CPARTS
    qcol = h * LANES + HEAD_DIM + p
    kcol = h * LANES + HEAD_DIM + N_CPARTS + p
    m = jnp.where(valid & (cols == qcol), 1.0, 0.0) - jnp.where(valid & (cols == kcol), 1.0, 0.0)
    return m.astype(BF16)


def _projection(x, g_norm, w_in, b_forget, g_qk_a, g_qk_b, *, with_aug):
    b, t, d = x.shape
    bt = min(TOKEN_TILE, t)
    nt = t // bt
    wm = w_in[:, :MAIN_W].astype(BF16)
    wf_cols = w_in[:, MAIN_W:MAIN_W + H_B]
    wf = jnp.concatenate([wf_cols] * N_CPARTS + [jnp.zeros((d, LANES - H_B * N_CPARTS), F32)], axis=1).astype(BF16)
    bfv = jnp.concatenate([b_forget] * N_CPARTS + [jnp.zeros((LANES - H_B * N_CPARTS,), F32)])[None, :]
    gains = jnp.stack([jnp.tile(g_qk_a[0], 2 * H_A), jnp.tile(g_qk_a[1], 2 * H_A),
                       jnp.tile(g_qk_b[0], H_B), jnp.tile(g_qk_b[1], H_B)])
    gn = _blockdiag_mean(GROUP_W, HEAD_DIM)
    tri = _tri(bt)
    place = _place_matrix()
    in_specs = [
        pl.BlockSpec((1, bt, d), lambda i, j: (i, j, 0)),
        _const_spec((1, d)),
        pl.BlockSpec((d, MAIN_W), lambda i, j: (0, 0), pipeline_mode=pl.Buffered(1)),
        _const_spec((d, LANES)), _const_spec((1, LANES)), _const_spec((4, GROUP_W)),
        _const_spec((GROUP_W, GROUP_W)), _const_spec((bt, bt)), _const_spec((LANES, H_B * LANES)),
    ]
    row_spec = pl.BlockSpec((1, bt, GROUP_W), lambda i, j: (i, j, 0))
    out_shape = [jax.ShapeDtypeStruct((b, t, GROUP_W), F32)] * 4 + [jax.ShapeDtypeStruct((b, t, H_B), F32)]
    out_specs = [row_spec] * 4 + [pl.BlockSpec((1, bt, H_B), lambda i, j: (i, j, 0))]
    scratch = []
    bk = min(KV_BLOCK, bt)
    if with_aug:
        out_shape[1] = jax.ShapeDtypeStruct((b, t * H_A, 2 * HEAD_DIM), F32)
        out_specs[1] = pl.BlockSpec((1, bt * H_A, 2 * HEAD_DIM), lambda i, j: (i, j, 0))

        def add(shape, block, index_map):
            out_shape.append(jax.ShapeDtypeStruct(shape, BF16))
            out_specs.append(pl.BlockSpec(block, index_map))

        rows_major = lambda i, j: (i, 0, j, 0)
        time_minor = lambda i, j: (i, 0, 0, j)
        chunked = lambda i, j: (i, 0, j, 0, 0)
        add((b, 2 * H_A, LANES, t), (1, 2 * H_A, LANES, bt), time_minor)
        add((b, H_A, t, LANES), (1, H_A, bt, LANES), rows_major)
        add((b, H_A, t // bk, VA_ROWS, bk), (1, H_A, bt // bk, VA_ROWS, bk), chunked)
        add((b, H_B, LANES, t), (1, H_B, LANES, bt), time_minor)
        add((b, H_B, t, LANES), (1, H_B, bt, LANES), rows_major)
        add((b, H_B, t // bk, VB_ROWS, bk), (1, H_B, bt // bk, VB_ROWS, bk), chunked)
        out_shape.append(jax.ShapeDtypeStruct((b, nt, 8, LANES), F32))
        out_specs.append(pl.BlockSpec((1, 1, 8, LANES), lambda i, j: (i, j, 0, 0)))
        scratch = [pltpu.VMEM((8, LANES), F32)]
    else:
        out_shape += [jax.ShapeDtypeStruct((b, t, GROUP_W), BF16)] * 2
        out_specs += [row_spec] * 2
    return pl.pallas_call(
        functools.partial(_proj_body, with_aug=with_aug, bt=bt, bk=bk),
        grid=(b, nt), in_specs=in_specs, out_specs=out_specs, out_shape=out_shape, scratch_shapes=scratch,
        compiler_params=_cparams(("arbitrary", "arbitrary")),
        name="proj_aug" if with_aug else "proj_plain",
    )(x, g_norm[None, :], wm, wf, bfv, gains, gn, tri, place)


def _t5_bucket(rel):
    nb = N_BUCKETS // 2
    max_exact = nb // 2
    base = jnp.where(rel > 0, nb, 0)
    n = jnp.abs(rel)
    large = max_exact + (jnp.log(jnp.maximum(n, max_exact).astype(jnp.float32) / max_exact)
                         / math.log(MAX_DISTANCE / max_exact) * (nb - max_exact)).astype(jnp.int32)
    large = jnp.minimum(large, nb - 1)
    return base + jnp.where(n < max_exact, n, large)


def _bucket_map(q_pos, k_pos):
    bkt = _t5_bucket(k_pos[None, :] - q_pos[:, None])
    visible = (k_pos[None, :] // CHUNK) <= (q_pos[:, None] // CHUNK)
    return jnp.where(visible, bkt, -1).astype(jnp.int32)


def _bias_body(rb_ref, bkt_ref, o_ref):
    h = pl.program_id(0)
    bkt = bkt_ref[0]
    far = rb_ref[N_BUCKETS // 2 - 1, h]
    acc = jnp.zeros(bkt.shape, F32)
    for b in range(N_BUCKETS):
        acc = jnp.where(bkt == b, rb_ref[b, h] - far, acc)
    o_ref[0, 0] = jnp.where(bkt < 0, NEG, acc * LOG2E)


def _bias_tiles(rel_bias, bkt):
    n, r, c = bkt.shape
    return pl.pallas_call(
        _bias_body,
        grid=(H_A, n),
        in_specs=[pl.BlockSpec(memory_space=pltpu.SMEM), pl.BlockSpec((1, r, c), lambda h, i: (i, 0, 0))],
        out_specs=pl.BlockSpec((1, 1, r, c), lambda h, i: (h, i, 0, 0)),
        out_shape=jax.ShapeDtypeStruct((H_A, n, r, c), F32),
        compiler_params=_cparams(("arbitrary", "arbitrary")),
        name="bias_tiles",
    )(rel_bias, bkt)


def _lam(lq):
    a = jnp.sum(lq[0:1, :] * lq[1:2, :], axis=-1, keepdims=True)
    b = jnp.sum(lq[2:3, :] * lq[3:4, :], axis=-1, keepdims=True)
    return jnp.exp(a) - jnp.exp(b) + LAM_INIT


def _attn_init_t(rows, bq):
    return (jnp.full((1, bq), NEG, F32), jnp.zeros((rows, bq), F32))


def _online_update_t(state, st, vt):
    m, acc = state
    m_new = jnp.maximum(m, jnp.max(st, axis=0, keepdims=True))
    p = jnp.exp2(st - m_new)
    alpha = jnp.exp2(m - m_new)
    return m_new, alpha * acc + jnp.dot(vt, p.astype(BF16), preferred_element_type=F32)


def _normalized_t(state, rows):
    acc = state[1]
    return acc[0:rows] / acc[rows:rows + 1]


def _pipelined_sweep(i, n_sub, n_near, slots, score_fn, value_fn, modify, rows, blk, first=0):
    assert n_sub % 2 == 0 and n_near in (1, 2)

    def run_block(jb, states, near, last, next_start=None):
        states = list(states)
        for s in range(n_sub):
            j = jb * n_sub + s
            cur, nxt = slots[s % 2], slots[(s + 1) % 2]
            ahead = next_start if (next_start is not None and s == n_sub - 1) else j + 1
            for c in range(2):
                if not (last and s == n_sub - 1):
                    nxt[c] = score_fn(c, ahead)
                st = cur[c]
                if near is not None:
                    st = modify(st, near, s)
                states[c] = _online_update_t(states[c], st, value_fn(c, j))
        return tuple(states)

    for c in range(2):
        slots[0][c] = score_fn(c, first * n_sub)
    states = (_attn_init_t(rows, blk), _attn_init_t(rows, blk))
    n_far = jnp.maximum(i + 1 - n_near, 0)

    def run_far(jb, states, n_blocks):
        for d in range(n_blocks):
            states = run_block(jb + d, states, None, False)
        return states

    n_groups = jnp.maximum(n_far - first, 0) // FAR_UNROLL
    states = lax.fori_loop(0, n_groups, lambda g, st: run_far(first + g * FAR_UNROLL, st, FAR_UNROLL), states)
    done = first + n_groups * FAR_UNROLL
    n_pairs = jnp.maximum(n_far - done, 0) // 2
    states = lax.fori_loop(0, n_pairs, lambda g, st: run_far(done + 2 * g, st, 2), states)
    states = lax.fori_loop(done + 2 * n_pairs, n_far, lambda jb, st: run_far(jb, st, 1), states)
    if n_near == 2:
        states = run_block(jnp.maximum(i - 1, 0), states, 1, False, next_start=i * n_sub)
    return run_block(i, states, 0, True)


def _fox_body(first_ref, qt_ref, k_ref, vt_ref, gout_ref, gn_ref, o_ref, s0_scr, s1_scr, *, blk, bk):
    i = pl.program_id(2)
    first = first_ref[(pl.program_id(0) * pl.num_programs(1) + pl.program_id(1)) * pl.num_programs(2) + i]
    krow = lax.broadcasted_iota(jnp.int32, (bk, blk), 0)
    qcol = lax.broadcasted_iota(jnp.int32, (bk, blk), 1)
    qts = (qt_ref[0, 0], qt_ref[0, 1])

    def score_fn(hh, j):
        off = pl.multiple_of(j * bk, bk)
        return jnp.dot(k_ref[0, hh, pl.ds(off, bk), :], qts[hh], preferred_element_type=F32)

    def causal(st, near, s):
        return jnp.where(krow + s * bk <= qcol, st, NEG)

    states = _pipelined_sweep(i, blk // bk, 1, (s0_scr, s1_scr), score_fn, lambda hh, j: vt_ref[0, hh, j],
                              causal, VB_ROWS, blk, first=first)
    o_t = jnp.concatenate([_normalized_t(states[0], HEAD_DIM), _normalized_t(states[1], HEAD_DIM)], axis=0)
    o_ref[0] = (_group_rms(o_t.T, gout_ref[0], gn_ref[...])).astype(BF16)


EXP2_ZERO_BELOW = -150.0
BOUND_SLACK = 1.02


def _fox_score_spread(g_qk_b):
    qk_max = HEAD_DIM * jnp.max(jnp.abs(g_qk_b[0])) * jnp.max(jnp.abs(g_qk_b[1])) * QK_SCALE * LOG2E
    return 2.0 * qk_max * BOUND_SLACK


def _fox_skip_plan(c_edge, g_qk_b):
    b, nq = c_edge.shape[:2]
    c_first = c_edge[:, :, 0, :H_B]
    c_last = c_edge[:, :, 1, :H_B]
    best = (_fox_score_spread(g_qk_b)
            + (c_first[:, :, None, :] - c_last[:, None, :, :]) * (1.0 / BOUND_SLACK))
    dead = best < EXP2_ZERO_BELOW - 1.0
    dead = jnp.logical_and(dead[..., 0::2], dead[..., 1::2])
    j_lt_i = (jnp.arange(nq)[None, :] < jnp.arange(nq)[:, None])[None, :, :, None]
    lead = jnp.cumprod(jnp.logical_and(dead, j_lt_i).astype(jnp.int32), axis=2)
    first = jnp.sum(lead, axis=2)
    return jnp.transpose(first, (0, 2, 1)).reshape(-1).astype(jnp.int32)


def _fox_attention(qb_t, kb_aug, vb_t, c_edge, g_qk_b, g_out_b):
    b, _, _, t = qb_t.shape
    blk = min(ATTN_BLOCK, t)
    bk = vb_t.shape[-1]
    pairs = H_B // 2
    gout = g_out_b.reshape(pairs, 1, LANES)
    first = _fox_skip_plan(c_edge, g_qk_b)
    return pl.pallas_call(
        functools.partial(_fox_body, blk=blk, bk=bk),
        grid_spec=pltpu.PrefetchScalarGridSpec(
            num_scalar_prefetch=1, grid=(b, pairs, t // blk),
            in_specs=[
                pl.BlockSpec((1, 2, LANES, blk), lambda bi, p, i, f: (bi, p, 0, i)),
                pl.BlockSpec((1, 2, t, LANES), lambda bi, p, i, f: (bi, p, 0, 0)),
                pl.BlockSpec((1, 2, t // bk, VB_ROWS, bk), lambda bi, p, i, f: (bi, p, 0, 0, 0)),
                pl.BlockSpec((1, 1, LANES), lambda bi, p, i, f: (p, 0, 0)),
                pl.BlockSpec((LANES, LANES), lambda bi, p, i, f: (0, 0)),
            ],
            out_specs=pl.BlockSpec((1, blk, LANES), lambda bi, p, i, f: (bi, i, p)),
            scratch_shapes=[pltpu.VMEM((2, bk, blk), F32), pltpu.VMEM((2, bk, blk), F32)]),
        out_shape=jax.ShapeDtypeStruct((b, t, H_B * HEAD_DIM), BF16),
        compiler_params=_cparams(("arbitrary", "arbitrary", "arbitrary")),
        name="fox_attention",
    )(first, qb_t, kb_aug, vb_t, gout, _blockdiag_mean(LANES, HEAD_DIM))


def _diff_finish(states, lam, gout):
    (_, l1, a1), (_, l2, a2) = states
    o = a1 / l1 - lam * (a2 / l2)
    ms = jnp.mean(o * o, axis=-1, keepdims=True)
    return (o * lax.rsqrt(ms + RMS_EPS)) * gout * (1.0 - LAM_INIT)


def _diff_body(lam_ref, qt_ref, k_ref, vt_ref, bias_ref, gout_ref, o_ref, s0_scr, s1_scr, *, blk, bk):
    i = pl.program_id(2)
    n_sub = blk // bk
    qts = (qt_ref[0, 0], qt_ref[0, 1])

    def score_fn(mi, j):
        off = pl.multiple_of(j * bk, bk)
        return jnp.dot(k_ref[0, 0, pl.ds(off, bk), :], qts[mi], preferred_element_type=F32)

    def add_bias(st, near, s):
        if near == 0:
            return st + bias_ref[0, n_sub + s]
        return st + bias_ref[0, jnp.where(i == 0, 2 * n_sub + s, s)]

    states = _pipelined_sweep(i, n_sub, 2, (s0_scr, s1_scr), score_fn, lambda mi, j: vt_ref[0, 0, j],
                              add_bias, VA_ROWS, blk)
    lam = _lam(lam_ref[...])
    o = (_normalized_t(states[0], 2 * HEAD_DIM) - lam * _normalized_t(states[1], 2 * HEAD_DIM)).T
    ms = jnp.mean(o * o, axis=-1, keepdims=True)
    o_ref[0] = ((o * lax.rsqrt(ms + RMS_EPS)) * gout_ref[0] * (1.0 - LAM_INIT)).astype(BF16)


def _diff_attention(qa_t, ka_bf, va_t, bias, lambda_qk, g_out_a):
    b, _, _, t = qa_t.shape
    blk = min(ATTN_BLOCK, t)
    bk = va_t.shape[-1]
    gout = g_out_a.reshape(H_A, 1, LANES)
    return pl.pallas_call(
        functools.partial(_diff_body, blk=blk, bk=bk),
        grid=(b, H_A, t // blk),
        in_specs=[
            _const_spec((4, HEAD_DIM)),
            pl.BlockSpec((1, 2, LANES, blk), lambda bi, h, i: (bi, h, 0, i)),
            pl.BlockSpec((1, 1, t, LANES), lambda bi, h, i: (bi, h, 0, 0)),
            pl.BlockSpec((1, 1, t // bk, VA_ROWS, bk), lambda bi, h, i: (bi, h, 0, 0, 0)),
            pl.BlockSpec((1, 3 * (blk // bk), bk, blk), lambda bi, h, i: (h, 0, 0, 0)),
            pl.BlockSpec((1, 1, LANES), lambda bi, h, i: (h, 0, 0)),
        ],
        out_specs=pl.BlockSpec((1, blk, LANES), lambda bi, h, i: (bi, i, h)),
        out_shape=jax.ShapeDtypeStruct((b, t, H_A * 2 * HEAD_DIM), BF16),
        scratch_shapes=[pltpu.VMEM((2, bk, blk), F32), pltpu.VMEM((2, bk, blk), F32)],
        compiler_params=_cparams(("arbitrary", "arbitrary", "arbitrary")),
        name="diff_attention",
    )(lambda_qk, qa_t, ka_bf, va_t, bias, gout)


def _dec_init(m_scr, l_scr, acc_scr):
    m_scr[...] = jnp.full(m_scr.shape, NEG, F32)
    l_scr[...] = jnp.zeros(l_scr.shape, F32)
    acc_scr[...] = jnp.zeros(acc_scr.shape, F32)


def _stack_queries(q, n_blocks):
    col = lax.broadcasted_iota(jnp.int32, q.shape, 1)
    zero = jnp.zeros_like(q)
    return jnp.concatenate([jnp.where((col >= c * HEAD_DIM) & (col < (c + 1) * HEAD_DIM), q, zero)
                            for c in range(n_blocks)], axis=0)


def _stacked_update(state, s, pv_fn):
    m, l, acc = state
    m_new = jnp.maximum(m, jnp.max(s, axis=-1, keepdims=True))
    p = jnp.exp2(s - m_new)
    alpha = jnp.exp2(m - m_new)
    return m_new, alpha * l + jnp.sum(p, axis=-1, keepdims=True), alpha * acc + pv_fn(p.astype(BF16))


def _diff_dec_body(lam_ref, q_ref, kn_ref, vn_ref, kt_ref, v_ref, bc_ref, bn_ref, gout_ref, o_ref,
                   qbig_scr, m_scr, l_scr, acc_scr, *, ck, nq):
    kc = pl.program_id(1)
    n_maps = 2 * H_A

    @pl.when(kc == 0)
    def _():
        qbig_scr[...] = _stack_queries(q_ref[0], n_maps)
        _dec_init(m_scr, l_scr, acc_scr)

    qbig = qbig_scr[...]

    def bias_rows(ref):
        return jnp.concatenate([ref[h, 0] for h in range(H_A) for _ in range(2)], axis=0)

    def per_head(p, values_of):
        return jnp.concatenate([jnp.dot(p[2 * h * nq:(2 * h + 2) * nq], values_of(h), preferred_element_type=F32)
                                for h in range(H_A)], axis=0)

    kt_all = kt_ref[0].reshape(n_maps * HEAD_DIM, ck).astype(BF16)
    s = jnp.dot(qbig, kt_all, preferred_element_type=F32) + bias_rows(bc_ref)
    state = _stacked_update((m_scr[...], l_scr[...], acc_scr[...]), s, lambda p: per_head(
        p, lambda h: v_ref[0, pl.ds(h, ck, stride=H_A), :].astype(BF16)))
    m_scr[...], l_scr[...], acc_scr[...] = state

    @pl.when(kc == pl.num_programs(1) - 1)
    def _():
        v_new = vn_ref[0].astype(BF16)
        s_new = _qk(qbig, kn_ref[0].astype(BF16)) + bias_rows(bn_ref)
        m, l, acc = _stacked_update(state, s_new, lambda p: per_head(
            p, lambda h: v_new[:, h * LANES:(h + 1) * LANES]))
        lam = _lam(lam_ref[...])
        for h in range(H_A):
            r = 2 * h * nq
            maps = tuple((m[a:a + nq], l[a:a + nq], acc[a:a + nq]) for a in (r, r + nq))
            o_ref[0, :, h * LANES:(h + 1) * LANES] = _diff_finish(maps, lam, gout_ref[h]).astype(BF16)


def _diff_decode(qa, ka_new, va_new, cache_kt, cache_v, bias_c, bias_n, lambda_qk, g_out_a):
    b, nq, w = qa.shape
    past = cache_kt.shape[-1]
    ck = min(CACHE_CHUNK, past)
    n_kc = past // ck
    new_spec = pl.BlockSpec((1, nq, w), lambda bi, c: (bi, 0, 0))
    return pl.pallas_call(
        functools.partial(_diff_dec_body, ck=ck, nq=nq),
        grid=(b, n_kc),
        in_specs=[
            _const_spec((4, HEAD_DIM)), new_spec, new_spec, new_spec,
            pl.BlockSpec((1, H_A, 2, HEAD_DIM, ck), lambda bi, c: (bi, 0, 0, 0, c)),
            pl.BlockSpec((1, H_A * ck, LANES), lambda bi, c: (bi, c, 0)),
            pl.BlockSpec((H_A, 1, nq, ck), lambda bi, c: (0, c, 0, 0)),
            _const_spec((H_A, 1, nq, nq)),
            _const_spec((H_A, 1, LANES)),
        ],
        out_specs=new_spec,
        out_shape=jax.ShapeDtypeStruct((b, nq, w), BF16),
        scratch_shapes=[pltpu.VMEM((2 * H_A * nq, w), BF16),
                        pltpu.VMEM((2 * H_A * nq, 1), F32), pltpu.VMEM((2 * H_A * nq, 1), F32),
                        pltpu.VMEM((2 * H_A * nq, LANES), F32)],
        compiler_params=_cparams(("arbitrary", "arbitrary")),
        name="diff_decode",
    )(lambda_qk, qa, ka_new, va_new, cache_kt, cache_v, bias_c, bias_n, g_out_a.reshape(H_A, 1, LANES))


def _suffix_sum_lanes(x):
    n = x.shape[-1]
    lane = lax.broadcasted_iota(jnp.int32, x.shape, x.ndim - 1)
    shift = 1
    while shift < n:
        x = x + jnp.where(lane + shift < n, pltpu.roll(x, n - shift, x.ndim - 1), 0.0)
        shift *= 2
    return x


def _live_chunks_body(spread_ref, lfc_ref, o_ref, *, n_kc, ck):
    lf = lfc_ref[0] * LOG2E
    newer = jnp.zeros((H_B, 1), F32)
    count = jnp.zeros((1, 1), F32)
    alive = jnp.ones((1, 1), F32)
    for c in reversed(range(n_kc)):
        best = jnp.max(newer, axis=0, keepdims=True) + spread_ref[0]
        alive = alive * jnp.where(best >= EXP2_ZERO_BELOW - 1.0, 1.0, 0.0)
        count = count + alive
        newer = newer + jnp.sum(lf[:, c * ck:(c + 1) * ck], axis=-1, keepdims=True)
    o_ref[0] = jnp.broadcast_to(count, o_ref.shape[1:]).astype(jnp.int32)


def _live_chunks(cache_logf_t, g_qk_b, ck):
    b, _, past = cache_logf_t.shape
    counts = pl.pallas_call(
        functools.partial(_live_chunks_body, n_kc=past // ck, ck=ck),
        grid=(b,),
        in_specs=[pl.BlockSpec(memory_space=pltpu.SMEM), pl.BlockSpec((1, H_B, past), lambda bi: (bi, 0, 0))],
        out_specs=pl.BlockSpec((1, 8, LANES), lambda bi: (bi, 0, 0)),
        out_shape=jax.ShapeDtypeStruct((b, 8, LANES), jnp.int32),
        compiler_params=_cparams(("arbitrary",)),
        name="fox_live_chunks",
    )(_fox_score_spread(g_qk_b).reshape(1), cache_logf_t)
    return counts[:, 0, 0]


def _fox_dec_body(live_ref, q_ref, kn_ref, vn_ref, lfn_col_ref, lfn_row_ref, kt_ref, vt_ref, lfc_ref, gout_ref, gn_ref,
                  o_ref, qbig_scr, cq_scr, cn_scr, after_scr, m_scr, l_scr, acc_scr, *, nq):
    kc = pl.program_id(1)
    n_kc = after_scr.shape[0]
    ck = after_scr.shape[-1]

    def head_rows(x):
        return jnp.concatenate([jnp.broadcast_to(x[h:h + 1, :], (nq, x.shape[-1])) for h in range(H_B)], axis=0)

    @pl.when(kc == 0)
    def _():
        qbig_scr[...] = _stack_queries(q_ref[0], H_B)
        _dec_init(m_scr, l_scr, acc_scr)
        r = lax.broadcasted_iota(jnp.int32, (nq, nq), 0)
        c = lax.broadcasted_iota(jnp.int32, (nq, nq), 1)
        tri_n = jnp.where(c <= r, 1.0, 0.0).astype(BF16)
        cq = _cumsum_rows(tri_n, lfn_col_ref[0] * LOG2E, jnp.zeros((1, LANES), F32))
        cq_scr[...] = jnp.concatenate([jnp.broadcast_to(cq[:, h:h + 1], (nq, LANES)) for h in range(H_B)], axis=0)
        lf_row = lfn_row_ref[0] * LOG2E
        total = jnp.sum(lf_row, axis=-1, keepdims=True)
        cn_scr[...] = total - _suffix_sum_lanes(lf_row) + lf_row
        lf = lfc_ref[0] * LOG2E
        after = _suffix_sum_lanes(lf) - lf
        for c_idx in range(n_kc):
            after_scr[c_idx] = after[:, c_idx * ck:(c_idx + 1) * ck]

    after = after_scr[n_kc - 1 - kc]
    qbig = qbig_scr[...]
    cq_col = cq_scr[...][:, 0:1]
    live = kc < live_ref[pl.program_id(0)]

    @pl.when(live)
    def _():
        kt_all = kt_ref[0].reshape(H_B * HEAD_DIM, ck).astype(BF16)
        vt_all = vt_ref[0].reshape(H_B * HEAD_DIM, ck).astype(BF16)
        s = jnp.dot(qbig, kt_all, preferred_element_type=F32) + cq_col + head_rows(after)
        m_scr[...], l_scr[...], acc_scr[...] = _stacked_update(
            (m_scr[...], l_scr[...], acc_scr[...]), s, lambda p: _qk(p, vt_all))

    @pl.when(kc == pl.num_programs(1) - 1)
    def _():
        row = jnp.concatenate([lax.broadcasted_iota(jnp.int32, (nq, nq), 0)] * H_B, axis=0)
        causal = lax.broadcasted_iota(jnp.int32, (H_B * nq, nq), 1) <= row
        s = _qk(qbig, kn_ref[0].astype(BF16)) + cq_col - head_rows(cn_scr[...][:, 0:nq])
        v_new = vn_ref[0].astype(BF16)
        _, l, acc = _stacked_update((m_scr[...], l_scr[...], acc_scr[...]), jnp.where(causal, s, NEG),
                                    lambda p: jnp.dot(p, v_new, preferred_element_type=F32))
        outs = [acc[h * nq:(h + 1) * nq, h * HEAD_DIM:(h + 1) * HEAD_DIM] / l[h * nq:(h + 1) * nq]
                for h in range(H_B)]
        for p in range(H_B // 2):
            o = jnp.concatenate([outs[2 * p], outs[2 * p + 1]], axis=-1)
            o_ref[0, :, p * LANES:(p + 1) * LANES] = _group_rms(o, gout_ref[p], gn_ref[...]).astype(BF16)


def _fox_decode(qb, kb_new, vb_new, logf_new, cache_kt, cache_vt, cache_logf_t, g_qk_b, g_out_b):
    b, nq, w = qb.shape
    past = cache_kt.shape[-1]
    ck = min(CACHE_CHUNK, past)
    n_kc = past // ck
    assert nq <= LANES
    pad = jnp.zeros((b, nq, LANES - H_B), F32)
    lfn_col = jnp.concatenate([logf_new, pad], axis=-1)
    lfn_row = jnp.concatenate([jnp.transpose(logf_new, (0, 2, 1)), jnp.zeros((b, H_B, LANES - nq), F32)], axis=-1)
    n_live = _live_chunks(cache_logf_t, g_qk_b, ck)
    per_stream = lambda width: pl.BlockSpec((1, nq, width), lambda bi, c, nl: (bi, 0, 0))

    def newest_first(bi, c, nl):
        return (bi, 0, 0, n_kc - 1 - jnp.minimum(c, jnp.maximum(nl[bi] - 1, 0)))

    return pl.pallas_call(
        functools.partial(_fox_dec_body, nq=nq),
        grid_spec=pltpu.PrefetchScalarGridSpec(
            num_scalar_prefetch=1, grid=(b, n_kc),
            in_specs=[
                per_stream(w), per_stream(w), per_stream(w), per_stream(LANES),
                pl.BlockSpec((1, H_B, LANES), lambda bi, c, nl: (bi, 0, 0)),
                pl.BlockSpec((1, H_B, HEAD_DIM, ck), newest_first),
                pl.BlockSpec((1, H_B, HEAD_DIM, ck), newest_first),
                pl.BlockSpec((1, H_B, past), lambda bi, c, nl: (bi, 0, 0)),
                pl.BlockSpec((H_B // 2, 1, LANES), lambda bi, c, nl: (0, 0, 0)),
                pl.BlockSpec((LANES, LANES), lambda bi, c, nl: (0, 0)),
            ],
            out_specs=per_stream(w),
            scratch_shapes=[pltpu.VMEM((H_B * nq, w), BF16),
                            pltpu.VMEM((H_B * nq, LANES), F32), pltpu.VMEM((H_B, LANES), F32),
                            pltpu.VMEM((n_kc, H_B, ck), F32),
                            pltpu.VMEM((H_B * nq, 1), F32), pltpu.VMEM((H_B * nq, 1), F32),
                            pltpu.VMEM((H_B * nq, w), F32)]),
        out_shape=jax.ShapeDtypeStruct((b, nq, w), BF16),
        compiler_params=_cparams(("arbitrary", "arbitrary")),
        name="fox_decode",
    )(n_live, qb, kb_new, vb_new, lfn_col, lfn_row, cache_kt, cache_vt, cache_logf_t,
      g_out_b.reshape(H_B // 2, 1, LANES), _blockdiag_mean(LANES, HEAD_DIM))


def _route(logits):
    lane_i = lax.broadcasted_iota(jnp.int32, logits.shape, 1)
    lane = lane_i.astype(F32)
    big = float(LANES)
    lg = jnp.where(lane_i < N_GROUPS, logits, NEG)
    mx = jnp.max(lg, axis=-1, keepdims=True)
    grp = jnp.min(jnp.where(lg == mx, lane, big), axis=-1, keepdims=True)
    p_grp = 1.0 / jnp.sum(jnp.exp(lg - mx), axis=-1, keepdims=True)
    e = lane_i - ROUTER_LANE0
    e_grp = lax.shift_right_arithmetic(e, EXPERT_GROUP_SHIFT).astype(F32)
    sel = (e >= 0) & (e < N_EXPERTS) & (e_grp == grp)
    v = jnp.where(sel, logits, NEG)
    v1 = jnp.max(v, axis=-1, keepdims=True)
    i1 = jnp.min(jnp.where(sel & (v == v1), lane, big), axis=-1, keepdims=True)
    sel2 = sel & (lane != i1)
    vv = jnp.where(sel2, logits, NEG)
    v2 = jnp.max(vv, axis=-1, keepdims=True)
    i2 = jnp.min(jnp.where(sel2 & (vv == v2), lane, big), axis=-1, keepdims=True)
    e2 = jnp.exp(v2 - v1)
    w1 = p_grp / (1.0 + e2)
    w2 = p_grp * e2 / (1.0 + e2)
    gates = jnp.where(lane == i1, w1, 0.0) + jnp.where(lane == i2, w2, 0.0)
    return gates, (i1, i2, w1, w2)


R_E1, R_E2, R_RANK1, R_RANK2, R_W1, R_W2 = range(6)


def _mix_body(oa_ref, ob_ref, x_ref, wa_ref, wb_ref, g2_ref, wr1_ref, wr2_ref, br_ref, tri_ref,
              x1_ref, xn_ref, gates_ref, route_ref, counts_ref):
    y = (jnp.dot(oa_ref[...], wa_ref[...], preferred_element_type=F32)
         + jnp.dot(ob_ref[...], wb_ref[...], preferred_element_type=F32))
    x1 = x_ref[...] + y
    x1_ref[...] = x1
    ms = jnp.mean(x1 * x1, axis=-1, keepdims=True)
    xn = (x1 * lax.rsqrt(ms + RMS_EPS)) * g2_ref[...]
    xn_ref[...] = xn
    h1 = xn.astype(BF16)
    h2 = (xn - h1.astype(F32)).astype(BF16)
    logits = (jnp.dot(h1, wr1_ref[...], preferred_element_type=F32)
              + jnp.dot(h1, wr2_ref[...], preferred_element_type=F32)
              + jnp.dot(h2, wr1_ref[...], preferred_element_type=F32)) + br_ref[...]
    gates, (i1, i2, w1, w2) = _route(logits)
    gates_ref[...] = gates

    @pl.when(pl.program_id(0) == 0)
    def _():
        counts_ref[...] = jnp.zeros_like(counts_ref)

    lane_i = lax.broadcasted_iota(jnp.int32, gates.shape, 1)
    lane = lane_i.astype(F32)
    oh1 = jnp.where(lane == i1, 1.0, 0.0)
    oh2 = jnp.where(lane == i2, 1.0, 0.0)
    comb = oh1 + oh2
    running = counts_ref[0:1, :]
    before = jnp.dot(tri_ref[...], comb.astype(BF16), preferred_element_type=F32) + running
    rank1 = jnp.sum(before * oh1, axis=-1, keepdims=True)
    rank2 = jnp.sum(before * oh2, axis=-1, keepdims=True)
    counts_ref[0:1, :] = running + jnp.sum(comb, axis=0, keepdims=True)
    rec = jnp.zeros_like(gates)
    for idx, val in ((R_E1, i1 - ROUTER_LANE0), (R_E2, i2 - ROUTER_LANE0), (R_RANK1, rank1), (R_RANK2, rank2),
                     (R_W1, w1), (R_W2, w2)):
        rec = jnp.where(lane_i == idx, val, rec)
    route_ref[...] = rec


def _mix_and_route(o_a, o_b, x, w_out, g_norm, w_rg, b_rg, w_re, b_re):
    n, d = x.shape
    bt = min(TOKEN_TILE, n)
    r = jnp.arange(bt)
    tri_strict = (r[None, :] < r[:, None]).astype(BF16)
    wa = w_out[:GROUP_W].astype(BF16)
    wb = w_out[GROUP_W:].astype(BF16)
    n_r = N_GROUPS + N_EXPERTS
    wr = jnp.concatenate([w_rg, w_re, jnp.zeros((d, LANES - n_r), F32)], axis=1)
    wr1 = wr.astype(BF16)
    wr2 = (wr - wr1.astype(F32)).astype(BF16)
    br = jnp.concatenate([b_rg, b_re, jnp.zeros((LANES - n_r,), F32)])[None, :]
    row = lambda width: pl.BlockSpec((bt, width), lambda i: (i, 0))
    return pl.pallas_call(
        _mix_body,
        grid=(n // bt,),
        in_specs=[row(GROUP_W), row(GROUP_W), row(d), _const_spec((GROUP_W, d)), _const_spec((GROUP_W, d)),
                  _const_spec((1, d)), _const_spec((d, LANES)), _const_spec((d, LANES)), _const_spec((1, LANES)),
                  _const_spec((bt, bt))],
        out_specs=[row(d), row(d), row(LANES), row(LANES), _const_spec((8, LANES))],
        out_shape=[jax.ShapeDtypeStruct((n, d), F32), jax.ShapeDtypeStruct((n, d), F32),
                   jax.ShapeDtypeStruct((n, LANES), F32), jax.ShapeDtypeStruct((n, LANES), F32),
                   jax.ShapeDtypeStruct((8, LANES), F32)],
        compiler_params=_cparams(("arbitrary",)),
        name="mix_route",
    )(o_a, o_b, x, wa, wb, g_norm[None, :], wr1, wr2, br, tri_strict)


def _swiglu(xn, wg, wu, wd, gate=None):
    x = xn.astype(BF16)
    g = jnp.dot(x, wg.astype(BF16), preferred_element_type=F32)
    u = jnp.dot(x, wu.astype(BF16), preferred_element_type=F32)
    h = (g * jax.nn.sigmoid(g)) * u
    if gate is not None:
        h = h * gate
    return jnp.dot(h.astype(BF16), wd.astype(BF16), preferred_element_type=F32)


def _expert_body(xn_ref, x1_ref, gates_ref, wg_ref, wu_ref, wd_ref, o_ref):
    e = pl.program_id(1)

    @pl.when(e == 0)
    def _():
        o_ref[...] = x1_ref[...]

    gates = gates_ref[...]
    lane = lax.broadcasted_iota(jnp.int32, gates.shape, 1)
    gate = jnp.sum(jnp.where(lane == e + ROUTER_LANE0, gates, 0.0), axis=-1, keepdims=True)
    o_ref[...] += _swiglu(xn_ref[...], wg_ref[0], wu_ref[0], wd_ref[0], gate)


def _experts(xn, x1, gates, w_gate, w_up, w_down):
    n, d = x1.shape
    ff = w_gate.shape[-1]
    bt = min(MOE_TILE, n)
    row = lambda width: pl.BlockSpec((bt, width), lambda i, e: (i, 0))
    return pl.pallas_call(
        _expert_body,
        grid=(n // bt, N_EXPERTS),
        in_specs=[row(d), row(d), row(LANES),
                  pl.BlockSpec((1, d, ff), lambda i, e: (e, 0, 0)),
                  pl.BlockSpec((1, d, ff), lambda i, e: (e, 0, 0)),
                  pl.BlockSpec((1, ff, d), lambda i, e: (e, 0, 0))],
        out_specs=row(d),
        out_shape=jax.ShapeDtypeStruct((n, d), F32),
        compiler_params=_cparams(("arbitrary", "arbitrary")),
        name="experts",
    )(xn, x1, gates, w_gate, w_up, w_down)


def _row_copies(n_rows, make_copy):
    def issue(r, carry):
        for s in range(2):
            make_copy(r, s).start()
        return carry

    lax.fori_loop(0, n_rows, issue, 0, unroll=8)

    def drain(r, carry):
        for s in range(2):
            make_copy(r, s).wait()
        return carry

    lax.fori_loop(0, n_rows, drain, 0, unroll=8)


def _row_position_body(route_ref, base_ref, pos_ref):
    rec = route_ref[...]
    lane_i = lax.broadcasted_iota(jnp.int32, rec.shape, 1)
    lane = lane_i.astype(F32)
    out = jnp.zeros(rec.shape, F32)
    for slot, (e_lane, r_lane) in enumerate(((R_E1, R_RANK1), (R_E2, R_RANK2))):
        onehot = lane == rec[:, e_lane:e_lane + 1] + float(ROUTER_LANE0)
        base = jnp.sum(jnp.where(onehot, base_ref[...], 0.0), axis=-1, keepdims=True)
        out = jnp.where(lane_i == slot, base + rec[:, r_lane:r_lane + 1], out)
    pos_ref[...] = out.astype(jnp.int32)


def _dispatch_body(last_tile_ref, pos_ref, x_ref, xs_ref, zero_scr, sem):
    @pl.when(pl.program_id(0) == 0)
    def _():
        zero_scr[...] = jnp.zeros_like(zero_scr)
        tm = zero_scr.shape[0]
        fills = [pltpu.make_async_copy(zero_scr, xs_ref.at[pl.ds(pl.multiple_of(last_tile_ref[e], tm), tm)], sem)
                 for e in range(N_EXPERTS)]
        for f in fills:
            f.start()
        for f in fills:
            f.wait()

        def fill_unused(t, carry):
            f = pltpu.make_async_copy(zero_scr, xs_ref.at[pl.ds(pl.multiple_of(t * tm, tm), tm)], sem)
            f.start()
            f.wait()
            return carry

        lax.fori_loop(last_tile_ref[N_EXPERTS], xs_ref.shape[0] // tm, fill_unused, 0)

    _row_copies(x_ref.shape[0], lambda r, s: pltpu.make_async_copy(
        x_ref.at[pl.ds(r, 1)], xs_ref.at[pl.ds(pos_ref[0, 0, 2 * r + s], 1)], sem))


def _grouped_body(te_ref, nu_ref, xs_ref, wg_ref, wu_ref, wd_ref, ys_ref, wg_bf, wu_bf, wd_bf):
    t = pl.program_id(0)
    used = t < nu_ref[0]
    new_expert = jnp.logical_or(t == 0, te_ref[t] != te_ref[jnp.maximum(t - 1, 0)])

    @pl.when(jnp.logical_and(used, new_expert))
    def _():
        wg_bf[...] = wg_ref[0].astype(BF16)
        wu_bf[...] = wu_ref[0].astype(BF16)
        wd_bf[...] = wd_ref[0].astype(BF16)

    @pl.when(used)
    def _():
        ys_ref[...] = _swiglu(xs_ref[...], wg_bf[...], wu_bf[...], wd_bf[...])

    @pl.when(jnp.logical_not(used))
    def _():
        ys_ref[...] = jnp.zeros_like(ys_ref)


def _combine_body(pos_ref, route_ref, x1_ref, ys_ref, o_ref, buf_scr, sem):
    _row_copies(x1_ref.shape[0], lambda r, s: pltpu.make_async_copy(
        ys_ref.at[pl.ds(pos_ref[0, 0, 2 * r + s], 1)], buf_scr.at[s, pl.ds(r, 1)], sem))
    rec = route_ref[...]
    o_ref[...] = (x1_ref[...] + rec[:, R_W1:R_W1 + 1] * buf_scr[0] + rec[:, R_W2:R_W2 + 1] * buf_scr[1])


def _routed_experts(xn, x1, route, counts, w_gate, w_up, w_down):
    n, d = x1.shape
    ff = w_gate.shape[-1]
    bt = min(TOKEN_TILE, n)
    nt = n // bt
    tm = MOE_ROW_TILE
    n_tiles = (2 * n) // tm + N_EXPERTS
    cnt = counts[0, ROUTER_LANE0:ROUTER_LANE0 + N_EXPERTS].astype(jnp.int32)
    tiles = (cnt + tm - 1) // tm
    tile_end = jnp.cumsum(tiles)
    base_row = (tile_end - tiles) * tm
    n_used = tile_end[-1:]
    tile_expert = jnp.minimum(jnp.sum(jnp.arange(n_tiles)[:, None] >= tile_end[None, :], axis=1), N_EXPERTS - 1)
    base_lanes = jnp.zeros((1, LANES), F32).at[0, ROUTER_LANE0:ROUTER_LANE0 + N_EXPERTS].set(base_row.astype(F32))
    row = lambda width: pl.BlockSpec((bt, width), lambda i: (i, 0))
    pos = pl.pallas_call(
        _row_position_body,
        grid=(nt,),
        in_specs=[row(LANES), _const_spec((1, LANES))],
        out_specs=row(LANES),
        out_shape=jax.ShapeDtypeStruct((n, LANES), jnp.int32),
        compiler_params=_cparams(("arbitrary",)),
        name="moe_positions",
    )(route, base_lanes)
    pos = pos[:, :2].reshape(nt, 1, 2 * bt)

    pos_spec = pl.BlockSpec((1, 1, 2 * bt), lambda i: (i, 0, 0), memory_space=pltpu.SMEM)
    any_spec = pl.BlockSpec(memory_space=pl.ANY)
    last_tile = jnp.minimum(base_row + jnp.maximum(tiles - 1, 0) * tm, (n_tiles - 1) * tm)
    last_tile = jnp.concatenate([last_tile, n_used]).astype(jnp.int32)
    xs = pl.pallas_call(
        _dispatch_body,
        grid_spec=pltpu.PrefetchScalarGridSpec(
            num_scalar_prefetch=1, grid=(nt,),
            in_specs=[pl.BlockSpec((1, 1, 2 * bt), lambda i, lt: (i, 0, 0), memory_space=pltpu.SMEM),
                      pl.BlockSpec((bt, d), lambda i, lt: (i, 0))],
            out_specs=any_spec,
            scratch_shapes=[pltpu.VMEM((tm, d), F32), pltpu.SemaphoreType.DMA(())]),
        out_shape=jax.ShapeDtypeStruct((n_tiles * tm, d), F32),
        compiler_params=_cparams(("arbitrary",)),
        name="moe_dispatch",
    )(last_tile, pos, xn)

    w_spec = lambda shape: pl.BlockSpec(shape, lambda t, te, nu: (te[t], 0, 0))
    ys = pl.pallas_call(
        _grouped_body,
        grid_spec=pltpu.PrefetchScalarGridSpec(
            num_scalar_prefetch=2, grid=(n_tiles,),
            in_specs=[pl.BlockSpec((tm, d), lambda t, te, nu: (jnp.minimum(t, nu[0] - 1), 0)),
                      w_spec((1, d, ff)), w_spec((1, d, ff)), w_spec((1, ff, d))],
            out_specs=pl.BlockSpec((tm, d), lambda t, te, nu: (t, 0)),
            scratch_shapes=[pltpu.VMEM((d, ff), BF16), pltpu.VMEM((d, ff), BF16), pltpu.VMEM((ff, d), BF16)]),
        out_shape=jax.ShapeDtypeStruct((n_tiles * tm, d), F32),
        compiler_params=_cparams(("arbitrary",)),
        name="moe_experts",
    )(tile_expert.astype(jnp.int32), n_used.astype(jnp.int32), xs, w_gate, w_up, w_down)

    return pl.pallas_call(
        _combine_body,
        grid=(nt,),
        in_specs=[pos_spec, row(LANES), row(d), any_spec],
        out_specs=row(d),
        out_shape=jax.ShapeDtypeStruct((n, d), F32),
        scratch_shapes=[pltpu.VMEM((2, bt, d), F32), pltpu.SemaphoreType.DMA(())],
        compiler_params=_cparams(("arbitrary",)),
        name="moe_combine",
    )(pos, route, x1, ys)


def _ffn(o_a, o_b, x, w_out, g_norm_ffn, w_rg, b_rg, w_re, b_re, wg, wu, wd):
    b, t, d = x.shape
    n = b * t
    x1, xn, gates, route, counts = _mix_and_route(o_a.reshape(n, -1), o_b.reshape(n, -1), x.reshape(n, d), w_out,
                                                  g_norm_ffn, w_rg, b_rg, w_re, b_re)
    if 2 * n >= ROUTED_MIN_ASSIGNMENTS:
        y = _routed_experts(xn, x1, route, counts, wg, wu, wd)
    else:
        y = _experts(xn, x1, gates, wg, wu, wd)
    return y.reshape(b, t, d)


def kernel(x_prompt, x_sample, cache_a_k, cache_a_v, cache_b_k, cache_b_v, cache_b_logf, g_norm_mix, w_in, b_forget, g_qk_a, g_qk_b, lambda_qk, g_out_a, g_out_b, w_out, rel_bias, g_norm_ffn, w_router_group, b_router_group, w_router_expert, b_router_expert, w_exp_gate, w_exp_up, w_exp_down):
    depth = w_in.shape[0]
    assert depth == 1, "single-layer step only"
    bp, tp, d = x_prompt.shape
    bs, ts, _ = x_sample.shape
    past = cache_a_k.shape[2]
    w_in0, w_out0 = w_in[0], w_out[0]
    wg, wu, wd = w_exp_gate[0], w_exp_up[0], w_exp_down[0]
    ffn_w = (w_out0, g_norm_ffn[0], w_router_group[0], b_router_group[0], w_router_expert[0], b_router_expert[0],
             wg, wu, wd)

    (ka_p, va_p, kb_p, vb_p, logf_p, qa_t, ka_bf, va_t, qb_t, kb_aug, vb_t, c_edge) = _projection(
        x_prompt, g_norm_mix[0], w_in0, b_forget[0], g_qk_a[0], g_qk_b[0], with_aug=True)
    blk = min(ATTN_BLOCK, tp)
    bk = va_t.shape[-1]
    assert blk % CHUNK == 0 and blk >= MAX_DISTANCE and blk % bk == 0
    q_pos = blk + jnp.arange(blk, dtype=jnp.int32)
    bkt_p = jnp.stack([_bucket_map(q_pos, s * bk + jnp.arange(bk, dtype=jnp.int32)).T
                       for s in range(2 * blk // bk)] + [jnp.full((bk, blk), -1, jnp.int32)] * (blk // bk))
    bias_p = _bias_tiles(rel_bias, bkt_p)
    o_a = _diff_attention(qa_t, ka_bf, va_t, bias_p, lambda_qk[0], g_out_a[0])
    o_b = _fox_attention(qb_t, kb_aug, vb_t, c_edge, g_qk_b[0], g_out_b[0])
    y_p = _ffn(o_a, o_b, x_prompt, *ffn_w)

    xs = x_sample.reshape(1, bs * ts, d)
    (ka_s, va_s, kb_s, vb_s, logf_s, qa_s, qb_s) = _projection(
        xs, g_norm_mix[0], w_in0, b_forget[0], g_qk_a[0], g_qk_b[0], with_aug=False)
    per_stream = lambda a: a.reshape(bs, ts, a.shape[-1])
    ka_s, va_s, kb_s, vb_s, logf_s, qa_s, qb_s = map(per_stream, (ka_s, va_s, kb_s, vb_s, logf_s, qa_s, qb_s))
    ck = min(CACHE_CHUNK, past)
    q_pos = past + jnp.arange(ts, dtype=jnp.int32)
    bkt_c = _bucket_map(q_pos, jnp.arange(past, dtype=jnp.int32)).reshape(ts, past // ck, ck).transpose(1, 0, 2)
    bias_c = _bias_tiles(rel_bias, bkt_c)
    bias_n = _bias_tiles(rel_bias, _bucket_map(q_pos, q_pos)[None])
    o_a_s = _diff_decode(qa_s, ka_s, va_s, jnp.transpose(cache_a_k[0], (0, 2, 3, 4, 1)),
                         cache_a_v[0].reshape(bs, past * H_A, 2 * HEAD_DIM), bias_c, bias_n, lambda_qk[0], g_out_a[0])
    o_b_s = _fox_decode(qb_s, kb_s, vb_s, logf_s, jnp.transpose(cache_b_k[0], (0, 2, 3, 1)),
                        jnp.transpose(cache_b_v[0], (0, 2, 3, 1)), jnp.transpose(cache_b_logf[0], (0, 2, 1)),
                        g_qk_b[0], g_out_b[0])
    y_s = _ffn(o_a_s, o_b_s, x_sample, *ffn_w)

    def rows(ka, va, kb, vb, logf, b, t):
        return (ka.reshape(1, b, t, H_A, 2, HEAD_DIM), va.reshape(1, b, t, H_A, 2 * HEAD_DIM),
                kb.reshape(1, b, t, H_B, HEAD_DIM), vb.reshape(1, b, t, H_B, HEAD_DIM), logf.reshape(1, b, t, H_B))

    return (y_p, y_s) + rows(ka_p, va_p, kb_p, vb_p, logf_p, bp, tp) + rows(ka_s, va_s, kb_s, vb_s, logf_s, bs, ts)
```

```python
import functools
import math

import jax
import jax.numpy as jnp
from jax import lax
from jax.experimental import pallas as pl
from jax.experimental.pallas import tpu as pltpu

F32 = jnp.float32
BF16 = jnp.bfloat16

LANES = 128
VMEM_LIMIT_BYTES = 56 * 1024 * 1024

HEAD_DIM = 64
H_A = 4
H_B = 8
GROUP_W = 512
MAIN_W = 6 * GROUP_W
CHUNK = 64
N_BUCKETS = 32
MAX_DISTANCE = 128
N_GROUPS = 4
EXPERTS_PER_GROUP = 8
N_EXPERTS = N_GROUPS * EXPERTS_PER_GROUP
EXPERT_GROUP_SHIFT = EXPERTS_PER_GROUP.bit_length() - 1
assert 1 << EXPERT_GROUP_SHIFT == EXPERTS_PER_GROUP
ROUTER_LANE0 = N_GROUPS
RMS_EPS = 1e-6
NEG = -1e30
LOG2E = 1.4426950408889634
QK_SCALE = HEAD_DIM ** -0.5
LAM_INIT = 0.8 - 0.6 * math.exp(-0.3 * 0)
N_CPARTS = 3

ONES_ROWS = 16
VA_ROWS = 2 * HEAD_DIM + ONES_ROWS
VB_ROWS = HEAD_DIM + ONES_ROWS

TOKEN_TILE = 512
ATTN_BLOCK = 512
KV_BLOCK = 256
FAR_UNROLLS = (8, 4, 2, 1)
CACHE_CHUNK = 1024
MOE_TILE = 1024
MOE_ROW_TILE = 256
ROUTED_MIN_ASSIGNMENTS = 4 * N_EXPERTS * MOE_ROW_TILE


def _cparams(sem):
    return pltpu.CompilerParams(dimension_semantics=sem, vmem_limit_bytes=VMEM_LIMIT_BYTES)


def _const_spec(shape):
    nd = len(shape)
    return pl.BlockSpec(shape, lambda *_: (0,) * nd)


def _split3(x):
    p1 = x.astype(BF16).astype(F32)
    r1 = x - p1
    p2 = r1.astype(BF16).astype(F32)
    p3 = (r1 - p2).astype(BF16).astype(F32)
    return p1, p2, p3


def _lane_groups(parts, lane):
    return jnp.where(lane < 8, parts[0], jnp.where(lane < 16, parts[1], parts[2]))


def _cumsum_rows(tri, x, carry):
    c = carry
    for part in _split3(x):
        c = c + jnp.dot(tri, part.astype(BF16), preferred_element_type=F32)
    return c


def _log_sigmoid(x):
    return jnp.minimum(x, 0.0) - jnp.log1p(jnp.exp(-jnp.abs(x)))


def _group_rms(raw, gain_row, gn):
    ms = jnp.dot((raw * raw).astype(BF16), gn, preferred_element_type=F32)
    return raw * lax.rsqrt(ms + RMS_EPS) * gain_row


def _qk(q, k):
    return lax.dot_general(q, k, (((1,), (1,)), ((), ())), preferred_element_type=F32)


def _proj_body(*refs, with_aug, bt, bk):
    (x_ref, g1_ref, wm_ref, wf_ref, bf_ref, gains_ref, gn_ref, tri_ref, place_ref) = refs[:9]
    ka_ref, va_ref, kb_ref, vb_ref, logf_ref = refs[9:14]
    x = x_ref[0]
    ms = jnp.mean(x * x, axis=-1, keepdims=True)
    xn = (x * lax.rsqrt(ms + RMS_EPS)) * g1_ref[...]
    xb = xn.astype(BF16)
    proj = jnp.dot(xb, wm_ref[...], preferred_element_type=F32)
    gains = gains_ref[...]
    gn = gn_ref[...]
    w = GROUP_W
    qa = _group_rms(proj[:, 0:w], gains[0:1], gn)
    ka = _group_rms(proj[:, w:2 * w], gains[1:2], gn)
    va = proj[:, 2 * w:3 * w]
    qb = _group_rms(proj[:, 3 * w:4 * w], gains[2:3], gn)
    kb = _group_rms(proj[:, 4 * w:5 * w], gains[3:4], gn)
    vb = proj[:, 5 * w:6 * w]
    ka_ref[0] = ka
    if with_aug:
        for h in range(H_A):
            va_ref[0, pl.ds(h, bt, stride=H_A), :] = va[:, h * LANES:(h + 1) * LANES]
    else:
        va_ref[0] = va
    kb_ref[0] = kb
    vb_ref[0] = vb
    fl = jnp.dot(xb, wf_ref[...], preferred_element_type=F32) + bf_ref[...]
    logf = _log_sigmoid(fl)
    logf_ref[0] = logf[:, 0:H_B]
    qscale = QK_SCALE * LOG2E
    if not with_aug:
        qa_ref, qb_ref = refs[14:16]
        qa_ref[0] = (qa * qscale).astype(BF16)
        qb_ref[0] = (qb * qscale).astype(BF16)
        return
    qa_t, ka_bf, va_t, qb_t, kb_aug, vb_t, cedge_ref, carry_ref = refs[14:22]
    n_chunk = bt // bk
    row = lax.broadcasted_iota(jnp.int32, (LANES, bt), 0)
    ones_tail = jnp.where(lax.broadcasted_iota(jnp.int32, (ONES_ROWS, bt), 0) == 0, 1.0, 0.0)

    def put_chunks(ref, idx, vt):
        vt = vt.astype(BF16)
        for c in range(n_chunk):
            ref[0, idx, c] = vt[:, c * bk:(c + 1) * bk]

    for h in range(H_A):
        sl = slice(h * LANES, (h + 1) * LANES)
        q_t = (qa[:, sl] * qscale).T
        qa_t[0, 2 * h] = jnp.where(row < HEAD_DIM, q_t, 0.0).astype(BF16)
        qa_t[0, 2 * h + 1] = jnp.where(row >= HEAD_DIM, q_t, 0.0).astype(BF16)
        ka_bf[0, h] = ka[:, sl].astype(BF16)
        put_chunks(va_t, h, jnp.concatenate([va[:, sl].T, ones_tail], axis=0))
        vb_pair_t = vb[:, sl].T
        for hh in range(2):
            put_chunks(vb_t, 2 * h + hh,
                       jnp.concatenate([vb_pair_t[hh * HEAD_DIM:(hh + 1) * HEAD_DIM], ones_tail], axis=0))

    @pl.when(pl.program_id(1) == 0)
    def _():
        carry_ref[...] = jnp.zeros_like(carry_ref)

    c = _cumsum_rows(tri_ref[...], logf * LOG2E, carry_ref[0:1, :])
    carry_ref[0:1, :] = c[bt - 1:bt, :]
    edge_row = lax.broadcasted_iota(jnp.int32, (8, LANES), 0)
    cedge_ref[0, 0] = jnp.where(edge_row == 0, c[0:1, :], jnp.where(edge_row == 1, c[bt - 1:bt, :], 0.0))
    lane = lax.broadcasted_iota(jnp.int32, (bt, LANES), 1)
    cparts = _lane_groups(_split3(c), lane).astype(BF16)
    extras = jnp.dot(cparts, place_ref[...], preferred_element_type=F32)
    q_parts_end = HEAD_DIM + N_CPARTS
    ones_q = jnp.where((lane >= q_parts_end) & (lane < q_parts_end + N_CPARTS), 1.0, 0.0)
    ones_k = jnp.where((lane >= HEAD_DIM) & (lane < q_parts_end), 1.0, 0.0)
    for h in range(H_B):
        sl = slice((h // 2) * LANES, (h // 2 + 1) * LANES)
        qp = qb[:, sl] * qscale
        kp = kb[:, sl]
        if h % 2:
            qp = pltpu.roll(qp, HEAD_DIM, 1)
            kp = pltpu.roll(kp, HEAD_DIM, 1)
        e = extras[:, h * LANES:(h + 1) * LANES]
        qb_t[0, h] = jnp.where(lane < HEAD_DIM, qp, jnp.where(lane < q_parts_end, e, ones_q)).T.astype(BF16)
        kb_aug[0, h] = jnp.where(lane < HEAD_DIM, kp, jnp.where(lane < q_parts_end, ones_k, e)).astype(BF16)


def _blockdiag_mean(n, group):
    r = jnp.arange(n)
    return jnp.where((r[:, None] // group) == (r[None, :] // group), 1.0 / group, 0.0).astype(BF16)


def _tri(n):
    r = jnp.arange(n)
    return (r[None, :] <= r[:, None]).astype(BF16)


def _place_matrix():
    rows = jnp.arange(LANES)[:, None]
    cols = jnp.arange(H_B * LANES)[None, :]
    p, h = rows // 8, rows % 8
    valid = rows < 8 * N_CPARTS
    qcol = h * LANES + HEAD_DIM + p
    kcol = h * LANES + HEAD_DIM + N_CPARTS + p
    m = jnp.where(valid & (cols == qcol), 1.0, 0.0) - jnp.where(valid & (cols == kcol), 1.0, 0.0)
    return m.astype(BF16)


def _projection(x, g_norm, w_in, b_forget, g_qk_a, g_qk_b, *, with_aug):
    b, t, d = x.shape
    bt = min(TOKEN_TILE, t)
    nt = t // bt
    wm = w_in[:, :MAIN_W].astype(BF16)
    wf_cols = w_in[:, MAIN_W:MAIN_W + H_B]
    wf = jnp.concatenate([wf_cols] * N_CPARTS + [jnp.zeros((d, LANES - H_B * N_CPARTS), F32)], axis=1).astype(BF16)
    bfv = jnp.concatenate([b_forget] * N_CPARTS + [jnp.zeros((LANES - H_B * N_CPARTS,), F32)])[None, :]
    gains = jnp.stack([jnp.tile(g_qk_a[0], 2 * H_A), jnp.tile(g_qk_a[1], 2 * H_A),
                       jnp.tile(g_qk_b[0], H_B), jnp.tile(g_qk_b[1], H_B)])
    gn = _blockdiag_mean(GROUP_W, HEAD_DIM)
    tri = _tri(bt)
    place = _place_matrix()
    in_specs = [
        pl.BlockSpec((1, bt, d), lambda i, j: (i, j, 0)),
        _const_spec((1, d)),
        pl.BlockSpec((d, MAIN_W), lambda i, j: (0, 0), pipeline_mode=pl.Buffered(1)),
        _const_spec((d, LANES)), _const_spec((1, LANES)), _const_spec((4, GROUP_W)),
        _const_spec((GROUP_W, GROUP_W)), _const_spec((bt, bt)), _const_spec((LANES, H_B * LANES)),
    ]
    row_spec = pl.BlockSpec((1, bt, GROUP_W), lambda i, j: (i, j, 0))
    out_shape = [jax.ShapeDtypeStruct((b, t, GROUP_W), F32)] * 4 + [jax.ShapeDtypeStruct((b, t, H_B), F32)]
    out_specs = [row_spec] * 4 + [pl.BlockSpec((1, bt, H_B), lambda i, j: (i, j, 0))]
    scratch = []
    bk = min(KV_BLOCK, bt)
    if with_aug:
        out_shape[1] = jax.ShapeDtypeStruct((b, t * H_A, 2 * HEAD_DIM), F32)
        out_specs[1] = pl.BlockSpec((1, bt * H_A, 2 * HEAD_DIM), lambda i, j: (i, j, 0))

        def add(shape, block, index_map):
            out_shape.append(jax.ShapeDtypeStruct(shape, BF16))
            out_specs.append(pl.BlockSpec(block, index_map))

        rows_major = lambda i, j: (i, 0, j, 0)
        time_minor = lambda i, j: (i, 0, 0, j)
        chunked = lambda i, j: (i, 0, j, 0, 0)
        add((b, 2 * H_A, LANES, t), (1, 2 * H_A, LANES, bt), time_minor)
        add((b, H_A, t, LANES), (1, H_A, bt, LANES), rows_major)
        add((b, H_A, t // bk, VA_ROWS, bk), (1, H_A, bt // bk, VA_ROWS, bk), chunked)
        add((b, H_B, LANES, t), (1, H_B, LANES, bt), time_minor)
        add((b, H_B, t, LANES), (1, H_B, bt, LANES), rows_major)
        add((b, H_B, t // bk, VB_ROWS, bk), (1, H_B, bt // bk, VB_ROWS, bk), chunked)
        out_shape.append(jax.ShapeDtypeStruct((b, nt, 8, LANES), F32))
        out_specs.append(pl.BlockSpec((1, 1, 8, LANES), lambda i, j: (i, j, 0, 0)))
        scratch = [pltpu.VMEM((8, LANES), F32)]
    else:
        out_shape += [jax.ShapeDtypeStruct((b, t, GROUP_W), BF16)] * 2
        out_specs += [row_spec] * 2
    return pl.pallas_call(
        functools.partial(_proj_body, with_aug=with_aug, bt=bt, bk=bk),
        grid=(b, nt), in_specs=in_specs, out_specs=out_specs, out_shape=out_shape, scratch_shapes=scratch,
        compiler_params=_cparams(("arbitrary", "arbitrary")),
        name="proj_aug" if with_aug else "proj_plain",
    )(x, g_norm[None, :], wm, wf, bfv, gains, gn, tri, place)


def _t5_bucket(rel):
    nb = N_BUCKETS // 2
    max_exact = nb // 2
    base = jnp.where(rel > 0, nb, 0)
    n = jnp.abs(rel)
    large = max_exact + (jnp.log(jnp.maximum(n, max_exact).astype(jnp.float32) / max_exact)
                         / math.log(MAX_DISTANCE / max_exact) * (nb - max_exact)).astype(jnp.int32)
    large = jnp.minimum(large, nb - 1)
    return base + jnp.where(n < max_exact, n, large)


def _bucket_map(q_pos, k_pos):
    bkt = _t5_bucket(k_pos[None, :] - q_pos[:, None])
    visible = (k_pos[None, :] // CHUNK) <= (q_pos[:, None] // CHUNK)
    return jnp.where(visible, bkt, -1).astype(jnp.int32)


def _bias_body(rb_ref, bkt_ref, o_ref):
    h = pl.program_id(0)
    bkt = bkt_ref[0]
    far = rb_ref[N_BUCKETS // 2 - 1, h]
    acc = jnp.zeros(bkt.shape, F32)
    for b in range(N_BUCKETS):
        acc = jnp.where(bkt == b, rb_ref[b, h] - far, acc)
    o_ref[0, 0] = jnp.where(bkt < 0, NEG, acc * LOG2E)


def _bias_tiles(rel_bias, bkt):
    n, r, c = bkt.shape
    return pl.pallas_call(
        _bias_body,
        grid=(H_A, n),
        in_specs=[pl.BlockSpec(memory_space=pltpu.SMEM), pl.BlockSpec((1, r, c), lambda h, i: (i, 0, 0))],
        out_specs=pl.BlockSpec((1, 1, r, c), lambda h, i: (h, i, 0, 0)),
        out_shape=jax.ShapeDtypeStruct((H_A, n, r, c), F32),
        compiler_params=_cparams(("arbitrary", "arbitrary")),
        name="bias_tiles",
    )(rel_bias, bkt)


def _lam(lq):
    a = jnp.sum(lq[0:1, :] * lq[1:2, :], axis=-1, keepdims=True)
    b = jnp.sum(lq[2:3, :] * lq[3:4, :], axis=-1, keepdims=True)
    return jnp.exp(a) - jnp.exp(b) + LAM_INIT


def _attn_init_t(rows, bq):
    return (jnp.full((1, bq), NEG, F32), jnp.zeros((rows, bq), F32))


def _online_update_t(state, st, vt):
    m, acc = state
    m_new = jnp.maximum(m, jnp.max(st, axis=0, keepdims=True))
    p = jnp.exp2(st - m_new)
    alpha = jnp.exp2(m - m_new)
    return m_new, alpha * acc + jnp.dot(vt, p.astype(BF16), preferred_element_type=F32)


def _normalized_t(state, rows):
    acc = state[1]
    return acc[0:rows] / acc[rows:rows + 1]


def _pipelined_sweep(i, n_sub, n_near, slots, score_fn, value_fn, modify, rows, blk, first=0):
    assert n_sub % 2 == 0 and n_near in (1, 2)

    def run_block(jb, states, near, last, next_start=None):
        states = list(states)
        for s in range(n_sub):
            j = jb * n_sub + s
            cur, nxt = slots[s % 2], slots[(s + 1) % 2]
            ahead = next_start if (next_start is not None and s == n_sub - 1) else j + 1
            for c in range(2):
                if not (last and s == n_sub - 1):
                    nxt[c] = score_fn(c, ahead)
                st = cur[c]
                if near is not None:
                    st = modify(st, near, s)
                states[c] = _online_update_t(states[c], st, value_fn(c, j))
        return tuple(states)

    for c in range(2):
        slots[0][c] = score_fn(c, first * n_sub)
    states = (_attn_init_t(rows, blk), _attn_init_t(rows, blk))
    n_far = jnp.maximum(i + 1 - n_near, 0)

    def run_far(jb, states, n_blocks):
        for d in range(n_blocks):
            states = run_block(jb + d, states, None, False)
        return states

    done = first
    for unroll in FAR_UNROLLS:
        n_groups = jnp.maximum(n_far - done, 0) // unroll
        states = lax.fori_loop(0, n_groups, lambda g, st, u=unroll, d=done: run_far(d + g * u, st, u), states)
        done = done + n_groups * unroll
    if n_near == 2:
        states = run_block(jnp.maximum(i - 1, 0), states, 1, False, next_start=i * n_sub)
    return run_block(i, states, 0, True)


def _fox_body(first_ref, qt_ref, k_ref, vt_ref, gout_ref, gn_ref, o_ref, s0_scr, s1_scr, *, blk, bk):
    i = pl.program_id(2)
    first = first_ref[(pl.program_id(0) * pl.num_programs(1) + pl.program_id(1)) * pl.num_programs(2) + i]
    krow = lax.broadcasted_iota(jnp.int32, (bk, blk), 0)
    qcol = lax.broadcasted_iota(jnp.int32, (bk, blk), 1)
    qts = (qt_ref[0, 0], qt_ref[0, 1])

    def score_fn(hh, j):
        off = pl.multiple_of(j * bk, bk)
        return jnp.dot(k_ref[0, hh, pl.ds(off, bk), :], qts[hh], preferred_element_type=F32)

    def causal(st, near, s):
        return jnp.where(krow + s * bk <= qcol, st, NEG)

    states = _pipelined_sweep(i, blk // bk, 1, (s0_scr, s1_scr), score_fn, lambda hh, j: vt_ref[0, hh, j],
                              causal, VB_ROWS, blk, first=first)
    o_t = jnp.concatenate([_normalized_t(states[0], HEAD_DIM), _normalized_t(states[1], HEAD_DIM)], axis=0)
    o_ref[0] = (_group_rms(o_t.T, gout_ref[0], gn_ref[...])).astype(BF16)


EXP2_ZERO_BELOW = -150.0
BOUND_SLACK = 1.02


def _fox_score_spread(g_qk_b):
    qk_max = HEAD_DIM * jnp.max(jnp.abs(g_qk_b[0])) * jnp.max(jnp.abs(g_qk_b[1])) * QK_SCALE * LOG2E
    return 2.0 * qk_max * BOUND_SLACK


def _fox_skip_plan(c_edge, g_qk_b):
    b, nq = c_edge.shape[:2]
    c_first = c_edge[:, :, 0, :H_B]
    c_last = c_edge[:, :, 1, :H_B]
    best = (_fox_score_spread(g_qk_b)
            + (c_first[:, :, None, :] - c_last[:, None, :, :]) * (1.0 / BOUND_SLACK))
    dead = best < EXP2_ZERO_BELOW - 1.0
    dead = jnp.logical_and(dead[..., 0::2], dead[..., 1::2])
    j_lt_i = (jnp.arange(nq)[None, :] < jnp.arange(nq)[:, None])[None, :, :, None]
    lead = jnp.cumprod(jnp.logical_and(dead, j_lt_i).astype(jnp.int32), axis=2)
    first = jnp.sum(lead, axis=2)
    return jnp.transpose(first, (0, 2, 1)).reshape(-1).astype(jnp.int32)


def _fox_attention(qb_t, kb_aug, vb_t, c_edge, g_qk_b, g_out_b):
    b, _, _, t = qb_t.shape
    blk = min(ATTN_BLOCK, t)
    bk = vb_t.shape[-1]
    pairs = H_B // 2
    gout = g_out_b.reshape(pairs, 1, LANES)
    first = _fox_skip_plan(c_edge, g_qk_b)
    return pl.pallas_call(
        functools.partial(_fox_body, blk=blk, bk=bk),
        grid_spec=pltpu.PrefetchScalarGridSpec(
            num_scalar_prefetch=1, grid=(b, pairs, t // blk),
            in_specs=[
                pl.BlockSpec((1, 2, LANES, blk), lambda bi, p, i, f: (bi, p, 0, i)),
                pl.BlockSpec((1, 2, t, LANES), lambda bi, p, i, f: (bi, p, 0, 0)),
                pl.BlockSpec((1, 2, t // bk, VB_ROWS, bk), lambda bi, p, i, f: (bi, p, 0, 0, 0)),
                pl.BlockSpec((1, 1, LANES), lambda bi, p, i, f: (p, 0, 0)),
                pl.BlockSpec((LANES, LANES), lambda bi, p, i, f: (0, 0)),
            ],
            out_specs=pl.BlockSpec((1, blk, LANES), lambda bi, p, i, f: (bi, i, p)),
            scratch_shapes=[pltpu.VMEM((2, bk, blk), F32), pltpu.VMEM((2, bk, blk), F32)]),
        out_shape=jax.ShapeDtypeStruct((b, t, H_B * HEAD_DIM), BF16),
        compiler_params=_cparams(("arbitrary", "arbitrary", "arbitrary")),
        name="fox_attention",
    )(first, qb_t, kb_aug, vb_t, gout, _blockdiag_mean(LANES, HEAD_DIM))


def _diff_finish(states, lam, gout):
    (_, l1, a1), (_, l2, a2) = states
    o = a1 / l1 - lam * (a2 / l2)
    ms = jnp.mean(o * o, axis=-1, keepdims=True)
    return (o * lax.rsqrt(ms + RMS_EPS)) * gout * (1.0 - LAM_INIT)


def _diff_body(lam_ref, qt_ref, k_ref, vt_ref, bias_ref, gout_ref, o_ref, s0_scr, s1_scr, *, blk, bk):
    i = pl.program_id(2)
    n_sub = blk // bk
    qts = (qt_ref[0, 0], qt_ref[0, 1])

    def score_fn(mi, j):
        off = pl.multiple_of(j * bk, bk)
        return jnp.dot(k_ref[0, 0, pl.ds(off, bk), :], qts[mi], preferred_element_type=F32)

    def add_bias(st, near, s):
        if near == 0:
            return st + bias_ref[0, n_sub + s]
        return st + bias_ref[0, jnp.where(i == 0, 2 * n_sub + s, s)]

    states = _pipelined_sweep(i, n_sub, 2, (s0_scr, s1_scr), score_fn, lambda mi, j: vt_ref[0, 0, j],
                              add_bias, VA_ROWS, blk)
    lam = _lam(lam_ref[...])
    o = (_normalized_t(states[0], 2 * HEAD_DIM) - lam * _normalized_t(states[1], 2 * HEAD_DIM)).T
    ms = jnp.mean(o * o, axis=-1, keepdims=True)
    o_ref[0] = ((o * lax.rsqrt(ms + RMS_EPS)) * gout_ref[0] * (1.0 - LAM_INIT)).astype(BF16)


def _diff_attention(qa_t, ka_bf, va_t, bias, lambda_qk, g_out_a):
    b, _, _, t = qa_t.shape
    blk = min(ATTN_BLOCK, t)
    bk = va_t.shape[-1]
    gout = g_out_a.reshape(H_A, 1, LANES)
    return pl.pallas_call(
        functools.partial(_diff_body, blk=blk, bk=bk),
        grid=(b, H_A, t // blk),
        in_specs=[
            _const_spec((4, HEAD_DIM)),
            pl.BlockSpec((1, 2, LANES, blk), lambda bi, h, i: (bi, h, 0, i)),
            pl.BlockSpec((1, 1, t, LANES), lambda bi, h, i: (bi, h, 0, 0)),
            pl.BlockSpec((1, 1, t // bk, VA_ROWS, bk), lambda bi, h, i: (bi, h, 0, 0, 0)),
            pl.BlockSpec((1, 3 * (blk // bk), bk, blk), lambda bi, h, i: (h, 0, 0, 0)),
            pl.BlockSpec((1, 1, LANES), lambda bi, h, i: (h, 0, 0)),
        ],
        out_specs=pl.BlockSpec((1, blk, LANES), lambda bi, h, i: (bi, i, h)),
        out_shape=jax.ShapeDtypeStruct((b, t, H_A * 2 * HEAD_DIM), BF16),
        scratch_shapes=[pltpu.VMEM((2, bk, blk), F32), pltpu.VMEM((2, bk, blk), F32)],
        compiler_params=_cparams(("arbitrary", "arbitrary", "arbitrary")),
        name="diff_attention",
    )(lambda_qk, qa_t, ka_bf, va_t, bias, gout)


def _dec_init(m_scr, l_scr, acc_scr):
    m_scr[...] = jnp.full(m_scr.shape, NEG, F32)
    l_scr[...] = jnp.zeros(l_scr.shape, F32)
    acc_scr[...] = jnp.zeros(acc_scr.shape, F32)


def _stack_queries(q, n_blocks):
    col = lax.broadcasted_iota(jnp.int32, q.shape, 1)
    zero = jnp.zeros_like(q)
    return jnp.concatenate([jnp.where((col >= c * HEAD_DIM) & (col < (c + 1) * HEAD_DIM), q, zero)
                            for c in range(n_blocks)], axis=0)


def _stacked_update(state, s, pv_fn):
    m, l, acc = state
    m_new = jnp.maximum(m, jnp.max(s, axis=-1, keepdims=True))
    p = jnp.exp2(s - m_new)
    alpha = jnp.exp2(m - m_new)
    return m_new, alpha * l + jnp.sum(p, axis=-1, keepdims=True), alpha * acc + pv_fn(p.astype(BF16))


def _diff_dec_body(lam_ref, q_ref, kn_ref, vn_ref, kt_ref, v_ref, bc_ref, bn_ref, gout_ref, o_ref,
                   qbig_scr, m_scr, l_scr, acc_scr, *, ck, nq):
    kc = pl.program_id(1)
    n_maps = 2 * H_A

    @pl.when(kc == 0)
    def _():
        qbig_scr[...] = _stack_queries(q_ref[0], n_maps)
        _dec_init(m_scr, l_scr, acc_scr)

    qbig = qbig_scr[...]

    def bias_rows(ref):
        return jnp.concatenate([ref[h, 0] for h in range(H_A) for _ in range(2)], axis=0)

    def per_head(p, values_of):
        return jnp.concatenate([jnp.dot(p[2 * h * nq:(2 * h + 2) * nq], values_of(h), preferred_element_type=F32)
                                for h in range(H_A)], axis=0)

    kt_all = kt_ref[0].reshape(n_maps * HEAD_DIM, ck).astype(BF16)
    s = jnp.dot(qbig, kt_all, preferred_element_type=F32) + bias_rows(bc_ref)
    state = _stacked_update((m_scr[...], l_scr[...], acc_scr[...]), s, lambda p: per_head(
        p, lambda h: v_ref[0, pl.ds(h, ck, stride=H_A), :].astype(BF16)))
    m_scr[...], l_scr[...], acc_scr[...] = state

    @pl.when(kc == pl.num_programs(1) - 1)
    def _():
        v_new = vn_ref[0].astype(BF16)
        s_new = _qk(qbig, kn_ref[0].astype(BF16)) + bias_rows(bn_ref)
        m, l, acc = _stacked_update(state, s_new, lambda p: per_head(
            p, lambda h: v_new[:, h * LANES:(h + 1) * LANES]))
        lam = _lam(lam_ref[...])
        for h in range(H_A):
            r = 2 * h * nq
            maps = tuple((m[a:a + nq], l[a:a + nq], acc[a:a + nq]) for a in (r, r + nq))
            o_ref[0, :, h * LANES:(h + 1) * LANES] = _diff_finish(maps, lam, gout_ref[h]).astype(BF16)


def _diff_decode(qa, ka_new, va_new, cache_kt, cache_v, bias_c, bias_n, lambda_qk, g_out_a):
    b, nq, w = qa.shape
    past = cache_kt.shape[-1]
    ck = min(CACHE_CHUNK, past)
    n_kc = past // ck
    new_spec = pl.BlockSpec((1, nq, w), lambda bi, c: (bi, 0, 0))
    return pl.pallas_call(
        functools.partial(_diff_dec_body, ck=ck, nq=nq),
        grid=(b, n_kc),
        in_specs=[
            _const_spec((4, HEAD_DIM)), new_spec, new_spec, new_spec,
            pl.BlockSpec((1, H_A, 2, HEAD_DIM, ck), lambda bi, c: (bi, 0, 0, 0, c)),
            pl.BlockSpec((1, H_A * ck, LANES), lambda bi, c: (bi, c, 0)),
            pl.BlockSpec((H_A, 1, nq, ck), lambda bi, c: (0, c, 0, 0)),
            _const_spec((H_A, 1, nq, nq)),
            _const_spec((H_A, 1, LANES)),
        ],
        out_specs=new_spec,
        out_shape=jax.ShapeDtypeStruct((b, nq, w), BF16),
        scratch_shapes=[pltpu.VMEM((2 * H_A * nq, w), BF16),
                        pltpu.VMEM((2 * H_A * nq, 1), F32), pltpu.VMEM((2 * H_A * nq, 1), F32),
                        pltpu.VMEM((2 * H_A * nq, LANES), F32)],
        compiler_params=_cparams(("arbitrary", "arbitrary")),
        name="diff_decode",
    )(lambda_qk, qa, ka_new, va_new, cache_kt, cache_v, bias_c, bias_n, g_out_a.reshape(H_A, 1, LANES))


def _suffix_sum_lanes(x):
    n = x.shape[-1]
    lane = lax.broadcasted_iota(jnp.int32, x.shape, x.ndim - 1)
    shift = 1
    while shift < n:
        x = x + jnp.where(lane + shift < n, pltpu.roll(x, n - shift, x.ndim - 1), 0.0)
        shift *= 2
    return x


def _live_chunks_body(spread_ref, lfc_ref, o_ref, *, n_kc, ck):
    lf = lfc_ref[0] * LOG2E
    newer = jnp.zeros((H_B, 1), F32)
    count = jnp.zeros((1, 1), F32)
    alive = jnp.ones((1, 1), F32)
    for c in reversed(range(n_kc)):
        best = jnp.max(newer, axis=0, keepdims=True) + spread_ref[0]
        alive = alive * jnp.where(best >= EXP2_ZERO_BELOW - 1.0, 1.0, 0.0)
        count = count + alive
        newer = newer + jnp.sum(lf[:, c * ck:(c + 1) * ck], axis=-1, keepdims=True)
    o_ref[0] = jnp.broadcast_to(count, o_ref.shape[1:]).astype(jnp.int32)


def _live_chunks(cache_logf_t, g_qk_b, ck):
    b, _, past = cache_logf_t.shape
    counts = pl.pallas_call(
        functools.partial(_live_chunks_body, n_kc=past // ck, ck=ck),
        grid=(b,),
        in_specs=[pl.BlockSpec(memory_space=pltpu.SMEM), pl.BlockSpec((1, H_B, past), lambda bi: (bi, 0, 0))],
        out_specs=pl.BlockSpec((1, 8, LANES), lambda bi: (bi, 0, 0)),
        out_shape=jax.ShapeDtypeStruct((b, 8, LANES), jnp.int32),
        compiler_params=_cparams(("arbitrary",)),
        name="fox_live_chunks",
    )(_fox_score_spread(g_qk_b).reshape(1), cache_logf_t)
    return counts[:, 0, 0]


def _fox_dec_body(live_ref, q_ref, kn_ref, vn_ref, lfn_col_ref, lfn_row_ref, kt_ref, vt_ref, lfc_ref, gout_ref, gn_ref,
                  o_ref, qbig_scr, cq_scr, cn_scr, after_scr, m_scr, l_scr, acc_scr, *, nq):
    kc = pl.program_id(1)
    n_kc = after_scr.shape[0]
    ck = after_scr.shape[-1]

    def head_rows(x):
        return jnp.concatenate([jnp.broadcast_to(x[h:h + 1, :], (nq, x.shape[-1])) for h in range(H_B)], axis=0)

    @pl.when(kc == 0)
    def _():
        qbig_scr[...] = _stack_queries(q_ref[0], H_B)
        _dec_init(m_scr, l_scr, acc_scr)
        r = lax.broadcasted_iota(jnp.int32, (nq, nq), 0)
        c = lax.broadcasted_iota(jnp.int32, (nq, nq), 1)
        tri_n = jnp.where(c <= r, 1.0, 0.0).astype(BF16)
        cq = _cumsum_rows(tri_n, lfn_col_ref[0] * LOG2E, jnp.zeros((1, LANES), F32))
        cq_scr[...] = jnp.concatenate([jnp.broadcast_to(cq[:, h:h + 1], (nq, LANES)) for h in range(H_B)], axis=0)
        lf_row = lfn_row_ref[0] * LOG2E
        total = jnp.sum(lf_row, axis=-1, keepdims=True)
        cn_scr[...] = total - _suffix_sum_lanes(lf_row) + lf_row
        lf = lfc_ref[0] * LOG2E
        after = _suffix_sum_lanes(lf) - lf
        for c_idx in range(n_kc):
            after_scr[c_idx] = after[:, c_idx * ck:(c_idx + 1) * ck]

    after = after_scr[n_kc - 1 - kc]
    qbig = qbig_scr[...]
    cq_col = cq_scr[...][:, 0:1]
    live = kc < live_ref[pl.program_id(0)]

    @pl.when(live)
    def _():
        kt_all = kt_ref[0].reshape(H_B * HEAD_DIM, ck).astype(BF16)
        vt_all = vt_ref[0].reshape(H_B * HEAD_DIM, ck).astype(BF16)
        s = jnp.dot(qbig, kt_all, preferred_element_type=F32) + cq_col + head_rows(after)
        m_scr[...], l_scr[...], acc_scr[...] = _stacked_update(
            (m_scr[...], l_scr[...], acc_scr[...]), s, lambda p: _qk(p, vt_all))

    @pl.when(kc == pl.num_programs(1) - 1)
    def _():
        row = jnp.concatenate([lax.broadcasted_iota(jnp.int32, (nq, nq), 0)] * H_B, axis=0)
        causal = lax.broadcasted_iota(jnp.int32, (H_B * nq, nq), 1) <= row
        s = _qk(qbig, kn_ref[0].astype(BF16)) + cq_col - head_rows(cn_scr[...][:, 0:nq])
        v_new = vn_ref[0].astype(BF16)
        _, l, acc = _stacked_update((m_scr[...], l_scr[...], acc_scr[...]), jnp.where(causal, s, NEG),
                                    lambda p: jnp.dot(p, v_new, preferred_element_type=F32))
        outs = [acc[h * nq:(h + 1) * nq, h * HEAD_DIM:(h + 1) * HEAD_DIM] / l[h * nq:(h + 1) * nq]
                for h in range(H_B)]
        for p in range(H_B // 2):
            o = jnp.concatenate([outs[2 * p], outs[2 * p + 1]], axis=-1)
            o_ref[0, :, p * LANES:(p + 1) * LANES] = _group_rms(o, gout_ref[p], gn_ref[...]).astype(BF16)


def _fox_decode(qb, kb_new, vb_new, logf_new, cache_kt, cache_vt, cache_logf_t, g_qk_b, g_out_b):
    b, nq, w = qb.shape
    past = cache_kt.shape[-1]
    ck = min(CACHE_CHUNK, past)
    n_kc = past // ck
    assert nq <= LANES
    pad = jnp.zeros((b, nq, LANES - H_B), F32)
    lfn_col = jnp.concatenate([logf_new, pad], axis=-1)
    lfn_row = jnp.concatenate([jnp.transpose(logf_new, (0, 2, 1)), jnp.zeros((b, H_B, LANES - nq), F32)], axis=-1)
    n_live = _live_chunks(cache_logf_t, g_qk_b, ck)
    per_stream = lambda width: pl.BlockSpec((1, nq, width), lambda bi, c, nl: (bi, 0, 0))

    def newest_first(bi, c, nl):
        return (bi, 0, 0, n_kc - 1 - jnp.minimum(c, jnp.maximum(nl[bi] - 1, 0)))

    return pl.pallas_call(
        functools.partial(_fox_dec_body, nq=nq),
        grid_spec=pltpu.PrefetchScalarGridSpec(
            num_scalar_prefetch=1, grid=(b, n_kc),
            in_specs=[
                per_stream(w), per_stream(w), per_stream(w), per_stream(LANES),
                pl.BlockSpec((1, H_B, LANES), lambda bi, c, nl: (bi, 0, 0)),
                pl.BlockSpec((1, H_B, HEAD_DIM, ck), newest_first),
                pl.BlockSpec((1, H_B, HEAD_DIM, ck), newest_first),
                pl.BlockSpec((1, H_B, past), lambda bi, c, nl: (bi, 0, 0)),
                pl.BlockSpec((H_B // 2, 1, LANES), lambda bi, c, nl: (0, 0, 0)),
                pl.BlockSpec((LANES, LANES), lambda bi, c, nl: (0, 0)),
            ],
            out_specs=per_stream(w),
            scratch_shapes=[pltpu.VMEM((H_B * nq, w), BF16),
                            pltpu.VMEM((H_B * nq, LANES), F32), pltpu.VMEM((H_B, LANES), F32),
                            pltpu.VMEM((n_kc, H_B, ck), F32),
                            pltpu.VMEM((H_B * nq, 1), F32), pltpu.VMEM((H_B * nq, 1), F32),
                            pltpu.VMEM((H_B * nq, w), F32)]),
        out_shape=jax.ShapeDtypeStruct((b, nq, w), BF16),
        compiler_params=_cparams(("arbitrary", "arbitrary")),
        name="fox_decode",
    )(n_live, qb, kb_new, vb_new, lfn_col, lfn_row, cache_kt, cache_vt, cache_logf_t,
      g_out_b.reshape(H_B // 2, 1, LANES), _blockdiag_mean(LANES, HEAD_DIM))


def _route(logits):
    lane_i = lax.broadcasted_iota(jnp.int32, logits.shape, 1)
    lane = lane_i.astype(F32)
    big = float(LANES)
    lg = jnp.where(lane_i < N_GROUPS, logits, NEG)
    mx = jnp.max(lg, axis=-1, keepdims=True)
    grp = jnp.min(jnp.where(lg == mx, lane, big), axis=-1, keepdims=True)
    p_grp = 1.0 / jnp.sum(jnp.exp(lg - mx), axis=-1, keepdims=True)
    e = lane_i - ROUTER_LANE0
    e_grp = lax.shift_right_arithmetic(e, EXPERT_GROUP_SHIFT).astype(F32)
    sel = (e >= 0) & (e < N_EXPERTS) & (e_grp == grp)
    v = jnp.where(sel, logits, NEG)
    v1 = jnp.max(v, axis=-1, keepdims=True)
    i1 = jnp.min(jnp.where(sel & (v == v1), lane, big), axis=-1, keepdims=True)
    sel2 = sel & (lane != i1)
    vv = jnp.where(sel2, logits, NEG)
    v2 = jnp.max(vv, axis=-1, keepdims=True)
    i2 = jnp.min(jnp.where(sel2 & (vv == v2), lane, big), axis=-1, keepdims=True)
    e2 = jnp.exp(v2 - v1)
    w1 = p_grp / (1.0 + e2)
    w2 = p_grp * e2 / (1.0 + e2)
    gates = jnp.where(lane == i1, w1, 0.0) + jnp.where(lane == i2, w2, 0.0)
    return gates, (i1, i2, w1, w2)


R_E1, R_E2, R_RANK1, R_RANK2, R_W1, R_W2 = range(6)


def _mix_body(oa_ref, ob_ref, x_ref, wa_ref, wb_ref, g2_ref, wr1_ref, wr2_ref, br_ref, tri_ref,
              x1_ref, xn_ref, gates_ref, route_ref, counts_ref):
    y = (jnp.dot(oa_ref[...], wa_ref[...], preferred_element_type=F32)
         + jnp.dot(ob_ref[...], wb_ref[...], preferred_element_type=F32))
    x1 = x_ref[...] + y
    x1_ref[...] = x1
    ms = jnp.mean(x1 * x1, axis=-1, keepdims=True)
    xn = (x1 * lax.rsqrt(ms + RMS_EPS)) * g2_ref[...]
    xn_ref[...] = xn
    h1 = xn.astype(BF16)
    h2 = (xn - h1.astype(F32)).astype(BF16)
    logits = (jnp.dot(h1, wr1_ref[...], preferred_element_type=F32)
              + jnp.dot(h1, wr2_ref[...], preferred_element_type=F32)
              + jnp.dot(h2, wr1_ref[...], preferred_element_type=F32)) + br_ref[...]
    gates, (i1, i2, w1, w2) = _route(logits)
    gates_ref[...] = gates

    @pl.when(pl.program_id(0) == 0)
    def _():
        counts_ref[...] = jnp.zeros_like(counts_ref)

    lane_i = lax.broadcasted_iota(jnp.int32, gates.shape, 1)
    lane = lane_i.astype(F32)
    oh1 = jnp.where(lane == i1, 1.0, 0.0)
    oh2 = jnp.where(lane == i2, 1.0, 0.0)
    comb = oh1 + oh2
    running = counts_ref[0:1, :]
    before = jnp.dot(tri_ref[...], comb.astype(BF16), preferred_element_type=F32) + running
    rank1 = jnp.sum(before * oh1, axis=-1, keepdims=True)
    rank2 = jnp.sum(before * oh2, axis=-1, keepdims=True)
    counts_ref[0:1, :] = running + jnp.sum(comb, axis=0, keepdims=True)
    rec = jnp.zeros_like(gates)
    for idx, val in ((R_E1, i1 - ROUTER_LANE0), (R_E2, i2 - ROUTER_LANE0), (R_RANK1, rank1), (R_RANK2, rank2),
                     (R_W1, w1), (R_W2, w2)):
        rec = jnp.where(lane_i == idx, val, rec)
    route_ref[...] = rec


def _mix_and_route(o_a, o_b, x, w_out, g_norm, w_rg, b_rg, w_re, b_re):
    n, d = x.shape
    bt = min(TOKEN_TILE, n)
    r = jnp.arange(bt)
    tri_strict = (r[None, :] < r[:, None]).astype(BF16)
    wa = w_out[:GROUP_W].astype(BF16)
    wb = w_out[GROUP_W:].astype(BF16)
    n_r = N_GROUPS + N_EXPERTS
    wr = jnp.concatenate([w_rg, w_re, jnp.zeros((d, LANES - n_r), F32)], axis=1)
    wr1 = wr.astype(BF16)
    wr2 = (wr - wr1.astype(F32)).astype(BF16)
    br = jnp.concatenate([b_rg, b_re, jnp.zeros((LANES - n_r,), F32)])[None, :]
    row = lambda width: pl.BlockSpec((bt, width), lambda i: (i, 0))
    return pl.pallas_call(
        _mix_body,
        grid=(n // bt,),
        in_specs=[row(GROUP_W), row(GROUP_W), row(d), _const_spec((GROUP_W, d)), _const_spec((GROUP_W, d)),
                  _const_spec((1, d)), _const_spec((d, LANES)), _const_spec((d, LANES)), _const_spec((1, LANES)),
                  _const_spec((bt, bt))],
        out_specs=[row(d), row(d), row(LANES), row(LANES), _const_spec((8, LANES))],
        out_shape=[jax.ShapeDtypeStruct((n, d), F32), jax.ShapeDtypeStruct((n, d), F32),
                   jax.ShapeDtypeStruct((n, LANES), F32), jax.ShapeDtypeStruct((n, LANES), F32),
                   jax.ShapeDtypeStruct((8, LANES), F32)],
        compiler_params=_cparams(("arbitrary",)),
        name="mix_route",
    )(o_a, o_b, x, wa, wb, g_norm[None, :], wr1, wr2, br, tri_strict)


def _swiglu(xn, wg, wu, wd, gate=None):
    x = xn.astype(BF16)
    g = jnp.dot(x, wg.astype(BF16), preferred_element_type=F32)
    u = jnp.dot(x, wu.astype(BF16), preferred_element_type=F32)
    h = (g * jax.nn.sigmoid(g)) * u
    if gate is not None:
        h = h * gate
    return jnp.dot(h.astype(BF16), wd.astype(BF16), preferred_element_type=F32)


def _expert_body(xn_ref, x1_ref, gates_ref, wg_ref, wu_ref, wd_ref, o_ref):
    e = pl.program_id(1)

    @pl.when(e == 0)
    def _():
        o_ref[...] = x1_ref[...]

    gates = gates_ref[...]
    lane = lax.broadcasted_iota(jnp.int32, gates.shape, 1)
    gate = jnp.sum(jnp.where(lane == e + ROUTER_LANE0, gates, 0.0), axis=-1, keepdims=True)
    o_ref[...] += _swiglu(xn_ref[...], wg_ref[0], wu_ref[0], wd_ref[0], gate)


def _experts(xn, x1, gates, w_gate, w_up, w_down):
    n, d = x1.shape
    ff = w_gate.shape[-1]
    bt = min(MOE_TILE, n)
    row = lambda width: pl.BlockSpec((bt, width), lambda i, e: (i, 0))
    return pl.pallas_call(
        _expert_body,
        grid=(n // bt, N_EXPERTS),
        in_specs=[row(d), row(d), row(LANES),
                  pl.BlockSpec((1, d, ff), lambda i, e: (e, 0, 0)),
                  pl.BlockSpec((1, d, ff), lambda i, e: (e, 0, 0)),
                  pl.BlockSpec((1, ff, d), lambda i, e: (e, 0, 0))],
        out_specs=row(d),
        out_shape=jax.ShapeDtypeStruct((n, d), F32),
        compiler_params=_cparams(("arbitrary", "arbitrary")),
        name="experts",
    )(xn, x1, gates, w_gate, w_up, w_down)


def _row_copies(n_rows, make_copy):
    def issue(r, carry):
        for s in range(2):
            make_copy(r, s).start()
        return carry

    lax.fori_loop(0, n_rows, issue, 0, unroll=8)

    def drain(r, carry):
        for s in range(2):
            make_copy(r, s).wait()
        return carry

    lax.fori_loop(0, n_rows, drain, 0, unroll=8)


def _row_position_body(route_ref, base_ref, pos_ref):
    rec = route_ref[...]
    lane_i = lax.broadcasted_iota(jnp.int32, rec.shape, 1)
    lane = lane_i.astype(F32)
    out = jnp.zeros(rec.shape, F32)
    for slot, (e_lane, r_lane) in enumerate(((R_E1, R_RANK1), (R_E2, R_RANK2))):
        onehot = lane == rec[:, e_lane:e_lane + 1] + float(ROUTER_LANE0)
        base = jnp.sum(jnp.where(onehot, base_ref[...], 0.0), axis=-1, keepdims=True)
        out = jnp.where(lane_i == slot, base + rec[:, r_lane:r_lane + 1], out)
    pos_ref[...] = out.astype(jnp.int32)


def _dispatch_body(last_tile_ref, pos_ref, x_ref, xs_ref, zero_scr, sem):
    @pl.when(pl.program_id(0) == 0)
    def _():
        zero_scr[...] = jnp.zeros_like(zero_scr)
        tm = zero_scr.shape[0]
        fills = [pltpu.make_async_copy(zero_scr, xs_ref.at[pl.ds(pl.multiple_of(last_tile_ref[e], tm), tm)], sem)
                 for e in range(N_EXPERTS)]
        for f in fills:
            f.start()
        for f in fills:
            f.wait()

        def fill_unused(t, carry):
            f = pltpu.make_async_copy(zero_scr, xs_ref.at[pl.ds(pl.multiple_of(t * tm, tm), tm)], sem)
            f.start()
            f.wait()
            return carry

        lax.fori_loop(last_tile_ref[N_EXPERTS], xs_ref.shape[0] // tm, fill_unused, 0)

    _row_copies(x_ref.shape[0], lambda r, s: pltpu.make_async_copy(
        x_ref.at[pl.ds(r, 1)], xs_ref.at[pl.ds(pos_ref[0, 0, 2 * r + s], 1)], sem))


def _grouped_body(te_ref, nu_ref, xs_ref, wg_ref, wu_ref, wd_ref, ys_ref, wg_bf, wu_bf, wd_bf):
    t = pl.program_id(0)
    used = t < nu_ref[0]
    new_expert = jnp.logical_or(t == 0, te_ref[t] != te_ref[jnp.maximum(t - 1, 0)])

    @pl.when(jnp.logical_and(used, new_expert))
    def _():
        wg_bf[...] = wg_ref[0].astype(BF16)
        wu_bf[...] = wu_ref[0].astype(BF16)
        wd_bf[...] = wd_ref[0].astype(BF16)

    @pl.when(used)
    def _():
        ys_ref[...] = _swiglu(xs_ref[...], wg_bf[...], wu_bf[...], wd_bf[...])

    @pl.when(jnp.logical_not(used))
    def _():
        ys_ref[...] = jnp.zeros_like(ys_ref)


def _combine_body(pos_ref, route_ref, x1_ref, ys_ref, o_ref, buf_scr, sem):
    _row_copies(x1_ref.shape[0], lambda r, s: pltpu.make_async_copy(
        ys_ref.at[pl.ds(pos_ref[0, 0, 2 * r + s], 1)], buf_scr.at[s, pl.ds(r, 1)], sem))
    rec = route_ref[...]
    o_ref[...] = (x1_ref[...] + rec[:, R_W1:R_W1 + 1] * buf_scr[0] + rec[:, R_W2:R_W2 + 1] * buf_scr[1])


def _routed_experts(xn, x1, route, counts, w_gate, w_up, w_down):
    n, d = x1.shape
    ff = w_gate.shape[-1]
    bt = min(TOKEN_TILE, n)
    nt = n // bt
    tm = MOE_ROW_TILE
    n_tiles = (2 * n) // tm + N_EXPERTS
    cnt = counts[0, ROUTER_LANE0:ROUTER_LANE0 + N_EXPERTS].astype(jnp.int32)
    tiles = (cnt + tm - 1) // tm
    tile_end = jnp.cumsum(tiles)
    base_row = (tile_end - tiles) * tm
    n_used = tile_end[-1:]
    tile_expert = jnp.minimum(jnp.sum(jnp.arange(n_tiles)[:, None] >= tile_end[None, :], axis=1), N_EXPERTS - 1)
    base_lanes = jnp.zeros((1, LANES), F32).at[0, ROUTER_LANE0:ROUTER_LANE0 + N_EXPERTS].set(base_row.astype(F32))
    row = lambda width: pl.BlockSpec((bt, width), lambda i: (i, 0))
    pos = pl.pallas_call(
        _row_position_body,
        grid=(nt,),
        in_specs=[row(LANES), _const_spec((1, LANES))],
        out_specs=row(LANES),
        out_shape=jax.ShapeDtypeStruct((n, LANES), jnp.int32),
        compiler_params=_cparams(("arbitrary",)),
        name="moe_positions",
    )(route, base_lanes)
    pos = pos[:, :2].reshape(nt, 1, 2 * bt)

    pos_spec = pl.BlockSpec((1, 1, 2 * bt), lambda i: (i, 0, 0), memory_space=pltpu.SMEM)
    any_spec = pl.BlockSpec(memory_space=pl.ANY)
    last_tile = jnp.minimum(base_row + jnp.maximum(tiles - 1, 0) * tm, (n_tiles - 1) * tm)
    last_tile = jnp.concatenate([last_tile, n_used]).astype(jnp.int32)
    xs = pl.pallas_call(
        _dispatch_body,
        grid_spec=pltpu.PrefetchScalarGridSpec(
            num_scalar_prefetch=1, grid=(nt,),
            in_specs=[pl.BlockSpec((1, 1, 2 * bt), lambda i, lt: (i, 0, 0), memory_space=pltpu.SMEM),
                      pl.BlockSpec((bt, d), lambda i, lt: (i, 0))],
            out_specs=any_spec,
            scratch_shapes=[pltpu.VMEM((tm, d), F32), pltpu.SemaphoreType.DMA(())]),
        out_shape=jax.ShapeDtypeStruct((n_tiles * tm, d), F32),
        compiler_params=_cparams(("arbitrary",)),
        name="moe_dispatch",
    )(last_tile, pos, xn)

    w_spec = lambda shape: pl.BlockSpec(shape, lambda t, te, nu: (te[t], 0, 0))
    ys = pl.pallas_call(
        _grouped_body,
        grid_spec=pltpu.PrefetchScalarGridSpec(
            num_scalar_prefetch=2, grid=(n_tiles,),
            in_specs=[pl.BlockSpec((tm, d), lambda t, te, nu: (jnp.minimum(t, nu[0] - 1), 0)),
                      w_spec((1, d, ff)), w_spec((1, d, ff)), w_spec((1, ff, d))],
            out_specs=pl.BlockSpec((tm, d), lambda t, te, nu: (t, 0)),
            scratch_shapes=[pltpu.VMEM((d, ff), BF16), pltpu.VMEM((d, ff), BF16), pltpu.VMEM((ff, d), BF16)]),
        out_shape=jax.ShapeDtypeStruct((n_tiles * tm, d), F32),
        compiler_params=_cparams(("arbitrary",)),
        name="moe_experts",
    )(tile_expert.astype(jnp.int32), n_used.astype(jnp.int32), xs, w_gate, w_up, w_down)

    return pl.pallas_call(
        _combine_body,
        grid=(nt,),
        in_specs=[pos_spec, row(LANES), row(d), any_spec],
        out_specs=row(d),
        out_shape=jax.ShapeDtypeStruct((n, d), F32),
        scratch_shapes=[pltpu.VMEM((2, bt, d), F32), pltpu.SemaphoreType.DMA(())],
        compiler_params=_cparams(("arbitrary",)),
        name="moe_combine",
    )(pos, route, x1, ys)


def _ffn(o_a, o_b, x, w_out, g_norm_ffn, w_rg, b_rg, w_re, b_re, wg, wu, wd):
    b, t, d = x.shape
    n = b * t
    x1, xn, gates, route, counts = _mix_and_route(o_a.reshape(n, -1), o_b.reshape(n, -1), x.reshape(n, d), w_out,
                                                  g_norm_ffn, w_rg, b_rg, w_re, b_re)
    if 2 * n >= ROUTED_MIN_ASSIGNMENTS:
        y = _routed_experts(xn, x1, route, counts, wg, wu, wd)
    else:
        y = _experts(xn, x1, gates, wg, wu, wd)
    return y.reshape(b, t, d)


def kernel(x_prompt, x_sample, cache_a_k, cache_a_v, cache_b_k, cache_b_v, cache_b_logf, g_norm_mix, w_in, b_forget, g_qk_a, g_qk_b, lambda_qk, g_out_a, g_out_b, w_out, rel_bias, g_norm_ffn, w_router_group, b_router_group, w_router_expert, b_router_expert, w_exp_gate, w_exp_up, w_exp_down):
    depth = w_in.shape[0]
    assert depth == 1, "single-layer step only"
    bp, tp, d = x_prompt.shape
    bs, ts, _ = x_sample.shape
    past = cache_a_k.shape[2]
    w_in0, w_out0 = w_in[0], w_out[0]
    wg, wu, wd = w_exp_gate[0], w_exp_up[0], w_exp_down[0]
    ffn_w = (w_out0, g_norm_ffn[0], w_router_group[0], b_router_group[0], w_router_expert[0], b_router_expert[0],
             wg, wu, wd)

    (ka_p, va_p, kb_p, vb_p, logf_p, qa_t, ka_bf, va_t, qb_t, kb_aug, vb_t, c_edge) = _projection(
        x_prompt, g_norm_mix[0], w_in0, b_forget[0], g_qk_a[0], g_qk_b[0], with_aug=True)
    blk = min(ATTN_BLOCK, tp)
    bk = va_t.shape[-1]
    assert blk % CHUNK == 0 and blk >= MAX_DISTANCE and blk % bk == 0
    q_pos = blk + jnp.arange(blk, dtype=jnp.int32)
    bkt_p = jnp.stack([_bucket_map(q_pos, s * bk + jnp.arange(bk, dtype=jnp.int32)).T
                       for s in range(2 * blk // bk)] + [jnp.full((bk, blk), -1, jnp.int32)] * (blk // bk))
    bias_p = _bias_tiles(rel_bias, bkt_p)
    o_a = _diff_attention(qa_t, ka_bf, va_t, bias_p, lambda_qk[0], g_out_a[0])
    o_b = _fox_attention(qb_t, kb_aug, vb_t, c_edge, g_qk_b[0], g_out_b[0])
    y_p = _ffn(o_a, o_b, x_prompt, *ffn_w)

    xs = x_sample.reshape(1, bs * ts, d)
    (ka_s, va_s, kb_s, vb_s, logf_s, qa_s, qb_s) = _projection(
        xs, g_norm_mix[0], w_in0, b_forget[0], g_qk_a[0], g_qk_b[0], with_aug=False)
    per_stream = lambda a: a.reshape(bs, ts, a.shape[-1])
    ka_s, va_s, kb_s, vb_s, logf_s, qa_s, qb_s = map(per_stream, (ka_s, va_s, kb_s, vb_s, logf_s, qa_s, qb_s))
    ck = min(CACHE_CHUNK, past)
    q_pos = past + jnp.arange(ts, dtype=jnp.int32)
    bkt_c = _bucket_map(q_pos, jnp.arange(past, dtype=jnp.int32)).reshape(ts, past // ck, ck).transpose(1, 0, 2)
    bias_c = _bias_tiles(rel_bias, bkt_c)
    bias_n = _bias_tiles(rel_bias, _bucket_map(q_pos, q_pos)[None])
    o_a_s = _diff_decode(qa_s, ka_s, va_s, jnp.transpose(cache_a_k[0], (0, 2, 3, 4, 1)),
                         cache_a_v[0].reshape(bs, past * H_A, 2 * HEAD_DIM), bias_c, bias_n, lambda_qk[0], g_out_a[0])
    o_b_s = _fox_decode(qb_s, kb_s, vb_s, logf_s, jnp.transpose(cache_b_k[0], (0, 2, 3, 1)),
                        jnp.transpose(cache_b_v[0], (0, 2, 3, 1)), jnp.transpose(cache_b_logf[0], (0, 2, 1)),
                        g_qk_b[0], g_out_b[0])
    y_s = _ffn(o_a_s, o_b_s, x_sample, *ffn_w)

    def rows(ka, va, kb, vb, logf, b, t):
        return (ka.reshape(1, b, t, H_A, 2, HEAD_DIM), va.reshape(1, b, t, H_A, 2 * HEAD_DIM),
                kb.reshape(1, b, t, H_B, HEAD_DIM), vb.reshape(1, b, t, H_B, HEAD_DIM), logf.reshape(1, b, t, H_B))

    return (y_p, y_s) + rows(ka_p, va_p, kb_p, vb_p, logf_p, bp, tp) + rows(ka_s, va_s, kb_s, vb_s, logf_s, bs, ts)
```

```python
import functools
import math

import jax
import jax.numpy as jnp
from jax import lax
from jax.experimental import pallas as pl
from jax.experimental.pallas import tpu as pltpu

F32 = jnp.float32
BF16 = jnp.bfloat16

LANES = 128
VMEM_LIMIT_BYTES = 56 * 1024 * 1024

HEAD_DIM = 64
H_A = 4
H_B = 8
GROUP_W = 512
MAIN_W = 6 * GROUP_W
CHUNK = 64
N_BUCKETS = 32
MAX_DISTANCE = 128
N_GROUPS = 4
EXPERTS_PER_GROUP = 8
N_EXPERTS = N_GROUPS * EXPERTS_PER_GROUP
EXPERT_GROUP_SHIFT = EXPERTS_PER_GROUP.bit_length() - 1
assert 1 << EXPERT_GROUP_SHIFT == EXPERTS_PER_GROUP
ROUTER_LANE0 = N_GROUPS
RMS_EPS = 1e-6
NEG = -1e30
LOG2E = 1.4426950408889634
QK_SCALE = HEAD_DIM ** -0.5
LAM_INIT = 0.8 - 0.6 * math.exp(-0.3 * 0)
N_CPARTS = 3

ONES_ROWS = 16
VA_ROWS = 2 * HEAD_DIM + ONES_ROWS
VB_ROWS = HEAD_DIM + ONES_ROWS

TOKEN_TILE = 512
ATTN_BLOCK = 512
KV_BLOCK = 256
FAR_UNROLLS = (8, 4, 2, 1)
CACHE_CHUNK = 1024
MOE_TILE = 1024
MOE_ROW_TILE = 256
ROUTED_MIN_ASSIGNMENTS = 4 * N_EXPERTS * MOE_ROW_TILE


def _cparams(sem):
    return pltpu.CompilerParams(dimension_semantics=sem, vmem_limit_bytes=VMEM_LIMIT_BYTES)


def _const_spec(shape):
    nd = len(shape)
    return pl.BlockSpec(shape, lambda *_: (0,) * nd)


def _split3(x):
    p1 = x.astype(BF16).astype(F32)
    r1 = x - p1
    p2 = r1.astype(BF16).astype(F32)
    p3 = (r1 - p2).astype(BF16).astype(F32)
    return p1, p2, p3


def _lane_groups(parts, lane):
    return jnp.where(lane < 8, parts[0], jnp.where(lane < 16, parts[1], parts[2]))


def _cumsum_rows(tri, x, carry):
    c = carry
    for part in _split3(x):
        c = c + jnp.dot(tri, part.astype(BF16), preferred_element_type=F32)
    return c


def _log_sigmoid(x):
    return jnp.minimum(x, 0.0) - jnp.log1p(jnp.exp(-jnp.abs(x)))


def _group_rms(raw, gain_row, gn):
    ms = jnp.dot((raw * raw).astype(BF16), gn, preferred_element_type=F32)
    return raw * lax.rsqrt(ms + RMS_EPS) * gain_row


def _qk(q, k):
    return lax.dot_general(q, k, (((1,), (1,)), ((), ())), preferred_element_type=F32)


def _proj_body(*refs, with_aug, bt, bk):
    (x_ref, g1_ref, wm_ref, wf_ref, bf_ref, gains_ref, gn_ref, tri_ref, place_ref) = refs[:9]
    ka_ref, va_ref, kb_ref, vb_ref, logf_ref = refs[9:14]
    x = x_ref[0]
    ms = jnp.mean(x * x, axis=-1, keepdims=True)
    xn = (x * lax.rsqrt(ms + RMS_EPS)) * g1_ref[...]
    xb = xn.astype(BF16)
    proj = jnp.dot(xb, wm_ref[...], preferred_element_type=F32)
    gains = gains_ref[...]
    gn = gn_ref[...]
    w = GROUP_W
    qa = _group_rms(proj[:, 0:w], gains[0:1], gn)
    ka = _group_rms(proj[:, w:2 * w], gains[1:2], gn)
    va = proj[:, 2 * w:3 * w]
    qb = _group_rms(proj[:, 3 * w:4 * w], gains[2:3], gn)
    kb = _group_rms(proj[:, 4 * w:5 * w], gains[3:4], gn)
    vb = proj[:, 5 * w:6 * w]
    ka_ref[0] = ka
    if with_aug:
        for h in range(H_A):
            va_ref[0, pl.ds(h, bt, stride=H_A), :] = va[:, h * LANES:(h + 1) * LANES]
    else:
        va_ref[0] = va
    kb_ref[0] = kb
    vb_ref[0] = vb
    fl = jnp.dot(xb, wf_ref[...], preferred_element_type=F32) + bf_ref[...]
    logf = _log_sigmoid(fl)
    logf_ref[0] = logf[:, 0:H_B]
    qscale = QK_SCALE * LOG2E
    if not with_aug:
        qa_ref, qb_ref = refs[14:16]
        qa_ref[0] = (qa * qscale).astype(BF16)
        qb_ref[0] = (qb * qscale).astype(BF16)
        return
    qa_t, ka_bf, va_t, qb_t, kb_aug, vb_t, cedge_ref, carry_ref = refs[14:22]
    n_chunk = bt // bk
    row = lax.broadcasted_iota(jnp.int32, (LANES, bt), 0)
    ones_tail = jnp.where(lax.broadcasted_iota(jnp.int32, (ONES_ROWS, bt), 0) == 0, 1.0, 0.0)

    def put_chunks(ref, idx, vt):
        vt = vt.astype(BF16)
        for c in range(n_chunk):
            ref[0, idx, c] = vt[:, c * bk:(c + 1) * bk]

    for h in range(H_A):
        sl = slice(h * LANES, (h + 1) * LANES)
        q_t = (qa[:, sl] * qscale).T
        qa_t[0, 2 * h] = jnp.where(row < HEAD_DIM, q_t, 0.0).astype(BF16)
        qa_t[0, 2 * h + 1] = jnp.where(row >= HEAD_DIM, q_t, 0.0).astype(BF16)
        ka_bf[0, h] = ka[:, sl].astype(BF16)
        put_chunks(va_t, h, jnp.concatenate([va[:, sl].T, ones_tail], axis=0))
        vb_pair_t = vb[:, sl].T
        for hh in range(2):
            put_chunks(vb_t, 2 * h + hh,
                       jnp.concatenate([vb_pair_t[hh * HEAD_DIM:(hh + 1) * HEAD_DIM], ones_tail], axis=0))

    @pl.when(pl.program_id(1) == 0)
    def _():
        carry_ref[...] = jnp.zeros_like(carry_ref)

    c = _cumsum_rows(tri_ref[...], logf * LOG2E, carry_ref[0:1, :])
    carry_ref[0:1, :] = c[bt - 1:bt, :]
    edge_row = lax.broadcasted_iota(jnp.int32, (8, LANES), 0)
    cedge_ref[0, 0] = jnp.where(edge_row == 0, c[0:1, :], jnp.where(edge_row == 1, c[bt - 1:bt, :], 0.0))
    lane = lax.broadcasted_iota(jnp.int32, (bt, LANES), 1)
    cparts = _lane_groups(_split3(c), lane).astype(BF16)
    extras = jnp.dot(cparts, place_ref[...], preferred_element_type=F32)
    q_parts_end = HEAD_DIM + N_CPARTS
    ones_q = jnp.where((lane >= q_parts_end) & (lane < q_parts_end + N_CPARTS), 1.0, 0.0)
    ones_k = jnp.where((lane >= HEAD_DIM) & (lane < q_parts_end), 1.0, 0.0)
    for h in range(H_B):
        sl = slice((h // 2) * LANES, (h // 2 + 1) * LANES)
        qp = qb[:, sl] * qscale
        kp = kb[:, sl]
        if h % 2:
            qp = pltpu.roll(qp, HEAD_DIM, 1)
            kp = pltpu.roll(kp, HEAD_DIM, 1)
        e = extras[:, h * LANES:(h + 1) * LANES]
        qb_t[0, h] = jnp.where(lane < HEAD_DIM, qp, jnp.where(lane < q_parts_end, e, ones_q)).T.astype(BF16)
        kb_aug[0, h] = jnp.where(lane < HEAD_DIM, kp, jnp.where(lane < q_parts_end, ones_k, e)).astype(BF16)


def _blockdiag_mean(n, group):
    r = jnp.arange(n)
    return jnp.where((r[:, None] // group) == (r[None, :] // group), 1.0 / group, 0.0).astype(BF16)


def _tri(n):
    r = jnp.arange(n)
    return (r[None, :] <= r[:, None]).astype(BF16)


def _place_matrix():
    rows = jnp.arange(LANES)[:, None]
    cols = jnp.arange(H_B * LANES)[None, :]
    p, h = rows // 8, rows % 8
    valid = rows < 8 * N_CPARTS
    qcol = h * LANES + HEAD_DIM + p
    kcol = h * LANES + HEAD_DIM + N_CPARTS + p
    m = jnp.where(valid & (cols == qcol), 1.0, 0.0) - jnp.where(valid & (cols == kcol), 1.0, 0.0)
    return m.astype(BF16)


def _projection(x, g_norm, w_in, b_forget, g_qk_a, g_qk_b, *, with_aug):
    b, t, d = x.shape
    bt = min(TOKEN_TILE, t)
    nt = t // bt
    wm = w_in[:, :MAIN_W].astype(BF16)
    wf_cols = w_in[:, MAIN_W:MAIN_W + H_B]
    wf = jnp.concatenate([wf_cols] * N_CPARTS + [jnp.zeros((d, LANES - H_B * N_CPARTS), F32)], axis=1).astype(BF16)
    bfv = jnp.concatenate([b_forget] * N_CPARTS + [jnp.zeros((LANES - H_B * N_CPARTS,), F32)])[None, :]
    gains = jnp.stack([jnp.tile(g_qk_a[0], 2 * H_A), jnp.tile(g_qk_a[1], 2 * H_A),
                       jnp.tile(g_qk_b[0], H_B), jnp.tile(g_qk_b[1], H_B)])
    gn = _blockdiag_mean(GROUP_W, HEAD_DIM)
    tri = _tri(bt)
    place = _place_matrix()
    in_specs = [
        pl.BlockSpec((1, bt, d), lambda i, j: (i, j, 0)),
        _const_spec((1, d)),
        pl.BlockSpec((d, MAIN_W), lambda i, j: (0, 0), pipeline_mode=pl.Buffered(1)),
        _const_spec((d, LANES)), _const_spec((1, LANES)), _const_spec((4, GROUP_W)),
        _const_spec((GROUP_W, GROUP_W)), _const_spec((bt, bt)), _const_spec((LANES, H_B * LANES)),
    ]
    row_spec = pl.BlockSpec((1, bt, GROUP_W), lambda i, j: (i, j, 0))
    out_shape = [jax.ShapeDtypeStruct((b, t, GROUP_W), F32)] * 4 + [jax.ShapeDtypeStruct((b, t, H_B), F32)]
    out_specs = [row_spec] * 4 + [pl.BlockSpec((1, bt, H_B), lambda i, j: (i, j, 0))]
    scratch = []
    bk = min(KV_BLOCK, bt)
    if with_aug:
        out_shape[1] = jax.ShapeDtypeStruct((b, t * H_A, 2 * HEAD_DIM), F32)
        out_specs[1] = pl.BlockSpec((1, bt * H_A, 2 * HEAD_DIM), lambda i, j: (i, j, 0))

        def add(shape, block, index_map):
            out_shape.append(jax.ShapeDtypeStruct(shape, BF16))
            out_specs.append(pl.BlockSpec(block, index_map))

        rows_major = lambda i, j: (i, 0, j, 0)
        time_minor = lambda i, j: (i, 0, 0, j)
        chunked = lambda i, j: (i, 0, j, 0, 0)
        add((b, 2 * H_A, LANES, t), (1, 2 * H_A, LANES, bt), time_minor)
        add((b, H_A, t, LANES), (1, H_A, bt, LANES), rows_major)
        add((b, H_A, t // bk, VA_ROWS, bk), (1, H_A, bt // bk, VA_ROWS, bk), chunked)
        add((b, H_B, LANES, t), (1, H_B, LANES, bt), time_minor)
        add((b, H_B, t, LANES), (1, H_B, bt, LANES), rows_major)
        add((b, H_B, t // bk, VB_ROWS, bk), (1, H_B, bt // bk, VB_ROWS, bk), chunked)
        out_shape.append(jax.ShapeDtypeStruct((b, nt, 8, LANES), F32))
        out_specs.append(pl.BlockSpec((1, 1, 8, LANES), lambda i, j: (i, j, 0, 0)))
        scratch = [pltpu.VMEM((8, LANES), F32)]
    else:
        out_shape += [jax.ShapeDtypeStruct((b, t, GROUP_W), BF16)] * 2
        out_specs += [row_spec] * 2
    return pl.pallas_call(
        functools.partial(_proj_body, with_aug=with_aug, bt=bt, bk=bk),
        grid=(b, nt), in_specs=in_specs, out_specs=out_specs, out_shape=out_shape, scratch_shapes=scratch,
        compiler_params=_cparams(("arbitrary", "arbitrary")),
        name="proj_aug" if with_aug else "proj_plain",
    )(x, g_norm[None, :], wm, wf, bfv, gains, gn, tri, place)


def _t5_bucket(rel):
    nb = N_BUCKETS // 2
    max_exact = nb // 2
    base = jnp.where(rel > 0, nb, 0)
    n = jnp.abs(rel)
    large = max_exact + (jnp.log(jnp.maximum(n, max_exact).astype(jnp.float32) / max_exact)
                         / math.log(MAX_DISTANCE / max_exact) * (nb - max_exact)).astype(jnp.int32)
    large = jnp.minimum(large, nb - 1)
    return base + jnp.where(n < max_exact, n, large)


def _bucket_map(q_pos, k_pos):
    bkt = _t5_bucket(k_pos[None, :] - q_pos[:, None])
    visible = (k_pos[None, :] // CHUNK) <= (q_pos[:, None] // CHUNK)
    return jnp.where(visible, bkt, -1).astype(jnp.int32)


def _bias_body(rb_ref, bkt_ref, o_ref):
    h = pl.program_id(0)
    bkt = bkt_ref[0]
    far = rb_ref[N_BUCKETS // 2 - 1, h]
    acc = jnp.zeros(bkt.shape, F32)
    for b in range(N_BUCKETS):
        acc = jnp.where(bkt == b, rb_ref[b, h] - far, acc)
    o_ref[0, 0] = jnp.where(bkt < 0, NEG, acc * LOG2E)


def _bias_tiles(rel_bias, bkt):
    n, r, c = bkt.shape
    return pl.pallas_call(
        _bias_body,
        grid=(H_A, n),
        in_specs=[pl.BlockSpec(memory_space=pltpu.SMEM), pl.BlockSpec((1, r, c), lambda h, i: (i, 0, 0))],
        out_specs=pl.BlockSpec((1, 1, r, c), lambda h, i: (h, i, 0, 0)),
        out_shape=jax.ShapeDtypeStruct((H_A, n, r, c), F32),
        compiler_params=_cparams(("arbitrary", "arbitrary")),
        name="bias_tiles",
    )(rel_bias, bkt)


def _lam(lq):
    a = jnp.sum(lq[0:1, :] * lq[1:2, :], axis=-1, keepdims=True)
    b = jnp.sum(lq[2:3, :] * lq[3:4, :], axis=-1, keepdims=True)
    return jnp.exp(a) - jnp.exp(b) + LAM_INIT


def _attn_init_t(rows, bq):
    return (jnp.full((1, bq), NEG, F32), jnp.zeros((rows, bq), F32))


def _online_update_t(state, st, vt):
    m, acc = state
    m_new = jnp.maximum(m, jnp.max(st, axis=0, keepdims=True))
    p = jnp.exp2(st - m_new)
    alpha = jnp.exp2(m - m_new)
    return m_new, alpha * acc + jnp.dot(vt, p.astype(BF16), preferred_element_type=F32)


def _normalized_t(state, rows):
    acc = state[1]
    return acc[0:rows] / acc[rows:rows + 1]


def _pipelined_sweep(i, n_sub, n_near, slots, score_fn, value_fn, modify, rows, blk, first=0):
    assert n_sub % 2 == 0 and n_near in (1, 2)

    def run_block(jb, states, near, last, next_start=None):
        states = list(states)
        for s in range(n_sub):
            j = jb * n_sub + s
            cur, nxt = slots[s % 2], slots[(s + 1) % 2]
            ahead = next_start if (next_start is not None and s == n_sub - 1) else j + 1
            for c in range(2):
                if not (last and s == n_sub - 1):
                    nxt[c] = score_fn(c, ahead)
                st = cur[c]
                if near is not None:
                    st = modify(st, near, s)
                states[c] = _online_update_t(states[c], st, value_fn(c, j))
        return tuple(states)

    for c in range(2):
        slots[0][c] = score_fn(c, first * n_sub)
    states = (_attn_init_t(rows, blk), _attn_init_t(rows, blk))
    n_far = jnp.maximum(i + 1 - n_near, 0)

    def run_far(jb, states, n_blocks):
        for d in range(n_blocks):
            states = run_block(jb + d, states, None, False)
        return states

    done = first
    for unroll in FAR_UNROLLS:
        n_groups = jnp.maximum(n_far - done, 0) // unroll
        states = lax.fori_loop(0, n_groups, lambda g, st, u=unroll, d=done: run_far(d + g * u, st, u), states)
        done = done + n_groups * unroll
    if n_near == 2:
        states = run_block(jnp.maximum(i - 1, 0), states, 1, False, next_start=i * n_sub)
    return run_block(i, states, 0, True)


def _fox_body(first_ref, qt_ref, k_ref, vt_ref, gout_ref, gn_ref, o_ref, s0_scr, s1_scr, *, blk, bk):
    i = pl.program_id(2)
    first = first_ref[(pl.program_id(0) * pl.num_programs(1) + pl.program_id(1)) * pl.num_programs(2) + i]
    krow = lax.broadcasted_iota(jnp.int32, (bk, blk), 0)
    qcol = lax.broadcasted_iota(jnp.int32, (bk, blk), 1)
    qts = (qt_ref[0, 0], qt_ref[0, 1])

    def score_fn(hh, j):
        off = pl.multiple_of(j * bk, bk)
        return jnp.dot(k_ref[0, hh, pl.ds(off, bk), :], qts[hh], preferred_element_type=F32)

    def causal(st, near, s):
        return jnp.where(krow + s * bk <= qcol, st, NEG)

    states = _pipelined_sweep(i, blk // bk, 1, (s0_scr, s1_scr), score_fn, lambda hh, j: vt_ref[0, hh, j],
                              causal, VB_ROWS, blk, first=first)
    o_t = jnp.concatenate([_normalized_t(states[0], HEAD_DIM), _normalized_t(states[1], HEAD_DIM)], axis=0)
    o_ref[0] = (_group_rms(o_t.T, gout_ref[0], gn_ref[...])).astype(BF16)


EXP2_ZERO_BELOW = -150.0
BOUND_SLACK = 1.02


def _fox_score_spread(g_qk_b):
    qk_max = HEAD_DIM * jnp.max(jnp.abs(g_qk_b[0])) * jnp.max(jnp.abs(g_qk_b[1])) * QK_SCALE * LOG2E
    return 2.0 * qk_max * BOUND_SLACK


def _fox_skip_plan(c_edge, g_qk_b):
    b, nq = c_edge.shape[:2]
    c_first = c_edge[:, :, 0, :H_B]
    c_last = c_edge[:, :, 1, :H_B]
    best = (_fox_score_spread(g_qk_b)
            + (c_first[:, :, None, :] - c_last[:, None, :, :]) * (1.0 / BOUND_SLACK))
    dead = best < EXP2_ZERO_BELOW - 1.0
    dead = jnp.logical_and(dead[..., 0::2], dead[..., 1::2])
    j_lt_i = (jnp.arange(nq)[None, :] < jnp.arange(nq)[:, None])[None, :, :, None]
    lead = jnp.cumprod(jnp.logical_and(dead, j_lt_i).astype(jnp.int32), axis=2)
    first = jnp.sum(lead, axis=2)
    return jnp.transpose(first, (0, 2, 1)).reshape(-1).astype(jnp.int32)


def _fox_attention(qb_t, kb_aug, vb_t, c_edge, g_qk_b, g_out_b):
    b, _, _, t = qb_t.shape
    blk = min(ATTN_BLOCK, t)
    bk = vb_t.shape[-1]
    pairs = H_B // 2
    gout = g_out_b.reshape(pairs, 1, LANES)
    first = _fox_skip_plan(c_edge, g_qk_b)
    return pl.pallas_call(
        functools.partial(_fox_body, blk=blk, bk=bk),
        grid_spec=pltpu.PrefetchScalarGridSpec(
            num_scalar_prefetch=1, grid=(b, pairs, t // blk),
            in_specs=[
                pl.BlockSpec((1, 2, LANES, blk), lambda bi, p, i, f: (bi, p, 0, i)),
                pl.BlockSpec((1, 2, t, LANES), lambda bi, p, i, f: (bi, p, 0, 0)),
                pl.BlockSpec((1, 2, t // bk, VB_ROWS, bk), lambda bi, p, i, f: (bi, p, 0, 0, 0)),
                pl.BlockSpec((1, 1, LANES), lambda bi, p, i, f: (p, 0, 0)),
                pl.BlockSpec((LANES, LANES), lambda bi, p, i, f: (0, 0)),
            ],
            out_specs=pl.BlockSpec((1, blk, LANES), lambda bi, p, i, f: (bi, i, p)),
            scratch_shapes=[pltpu.VMEM((2, bk, blk), F32), pltpu.VMEM((2, bk, blk), F32)]),
        out_shape=jax.ShapeDtypeStruct((b, t, H_B * HEAD_DIM), BF16),
        compiler_params=_cparams(("arbitrary", "arbitrary", "arbitrary")),
        name="fox_attention",
    )(first, qb_t, kb_aug, vb_t, gout, _blockdiag_mean(LANES, HEAD_DIM))


def _diff_finish(states, lam, gout):
    (_, l1, a1), (_, l2, a2) = states
    o = a1 / l1 - lam * (a2 / l2)
    ms = jnp.mean(o * o, axis=-1, keepdims=True)
    return (o * lax.rsqrt(ms + RMS_EPS)) * gout * (1.0 - LAM_INIT)


def _diff_body(lam_ref, qt_ref, k_ref, vt_ref, bias_ref, gout_ref, o_ref, s0_scr, s1_scr, *, blk, bk):
    i = pl.program_id(2)
    n_sub = blk // bk
    qts = (qt_ref[0, 0], qt_ref[0, 1])

    def score_fn(mi, j):
        off = pl.multiple_of(j * bk, bk)
        return jnp.dot(k_ref[0, 0, pl.ds(off, bk), :], qts[mi], preferred_element_type=F32)

    def add_bias(st, near, s):
        if near == 0:
            return st + bias_ref[0, n_sub + s]
        return st + bias_ref[0, jnp.where(i == 0, 2 * n_sub + s, s)]

    states = _pipelined_sweep(i, n_sub, 2, (s0_scr, s1_scr), score_fn, lambda mi, j: vt_ref[0, 0, j],
                              add_bias, VA_ROWS, blk)
    lam = _lam(lam_ref[...])
    o = (_normalized_t(states[0], 2 * HEAD_DIM) - lam * _normalized_t(states[1], 2 * HEAD_DIM)).T
    ms = jnp.mean(o * o, axis=-1, keepdims=True)
    o_ref[0] = ((o * lax.rsqrt(ms + RMS_EPS)) * gout_ref[0] * (1.0 - LAM_INIT)).astype(BF16)


def _diff_attention(qa_t, ka_bf, va_t, bias, lambda_qk, g_out_a):
    b, _, _, t = qa_t.shape
    blk = min(ATTN_BLOCK, t)
    bk = va_t.shape[-1]
    gout = g_out_a.reshape(H_A, 1, LANES)
    return pl.pallas_call(
        functools.partial(_diff_body, blk=blk, bk=bk),
        grid=(b, H_A, t // blk),
        in_specs=[
            _const_spec((4, HEAD_DIM)),
            pl.BlockSpec((1, 2, LANES, blk), lambda bi, h, i: (bi, h, 0, i)),
            pl.BlockSpec((1, 1, t, LANES), lambda bi, h, i: (bi, h, 0, 0)),
            pl.BlockSpec((1, 1, t // bk, VA_ROWS, bk), lambda bi, h, i: (bi, h, 0, 0, 0)),
            pl.BlockSpec((1, 3 * (blk // bk), bk, blk), lambda bi, h, i: (h, 0, 0, 0)),
            pl.BlockSpec((1, 1, LANES), lambda bi, h, i: (h, 0, 0)),
        ],
        out_specs=pl.BlockSpec((1, blk, LANES), lambda bi, h, i: (bi, i, h)),
        out_shape=jax.ShapeDtypeStruct((b, t, H_A * 2 * HEAD_DIM), BF16),
        scratch_shapes=[pltpu.VMEM((2, bk, blk), F32), pltpu.VMEM((2, bk, blk), F32)],
        compiler_params=_cparams(("arbitrary", "arbitrary", "arbitrary")),
        name="diff_attention",
    )(lambda_qk, qa_t, ka_bf, va_t, bias, gout)


def _dec_init(m_scr, l_scr, acc_scr):
    m_scr[...] = jnp.full(m_scr.shape, NEG, F32)
    l_scr[...] = jnp.zeros(l_scr.shape, F32)
    acc_scr[...] = jnp.zeros(acc_scr.shape, F32)


def _stack_queries(q, n_blocks):
    col = lax.broadcasted_iota(jnp.int32, q.shape, 1)
    zero = jnp.zeros_like(q)
    return jnp.concatenate([jnp.where((col >= c * HEAD_DIM) & (col < (c + 1) * HEAD_DIM), q, zero)
                            for c in range(n_blocks)], axis=0)


def _stacked_update(state, s, pv_fn):
    m, l, acc = state
    m_new = jnp.maximum(m, jnp.max(s, axis=-1, keepdims=True))
    p = jnp.exp2(s - m_new)
    alpha = jnp.exp2(m - m_new)
    return m_new, alpha * l + jnp.sum(p, axis=-1, keepdims=True), alpha * acc + pv_fn(p.astype(BF16))


def _diff_dec_body(lam_ref, q_ref, kn_ref, vn_ref, kt_ref, v_ref, bc_ref, bn_ref, gout_ref, o_ref,
                   qbig_scr, m_scr, l_scr, acc_scr, *, ck, nq):
    kc = pl.program_id(1)
    n_maps = 2 * H_A

    @pl.when(kc == 0)
    def _():
        qbig_scr[...] = _stack_queries(q_ref[0], n_maps)
        _dec_init(m_scr, l_scr, acc_scr)

    qbig = qbig_scr[...]

    def bias_rows(ref):
        return jnp.concatenate([ref[h, 0] for h in range(H_A) for _ in range(2)], axis=0)

    def per_head(p, values_of):
        return jnp.concatenate([jnp.dot(p[2 * h * nq:(2 * h + 2) * nq], values_of(h), preferred_element_type=F32)
                                for h in range(H_A)], axis=0)

    kt_all = kt_ref[0].reshape(n_maps * HEAD_DIM, ck).astype(BF16)
    s = jnp.dot(qbig, kt_all, preferred_element_type=F32) + bias_rows(bc_ref)
    state = _stacked_update((m_scr[...], l_scr[...], acc_scr[...]), s, lambda p: per_head(
        p, lambda h: v_ref[0, pl.ds(h, ck, stride=H_A), :].astype(BF16)))
    m_scr[...], l_scr[...], acc_scr[...] = state

    @pl.when(kc == pl.num_programs(1) - 1)
    def _():
        v_new = vn_ref[0].astype(BF16)
        s_new = _qk(qbig, kn_ref[0].astype(BF16)) + bias_rows(bn_ref)
        m, l, acc = _stacked_update(state, s_new, lambda p: per_head(
            p, lambda h: v_new[:, h * LANES:(h + 1) * LANES]))
        lam = _lam(lam_ref[...])
        for h in range(H_A):
            r = 2 * h * nq
            maps = tuple((m[a:a + nq], l[a:a + nq], acc[a:a + nq]) for a in (r, r + nq))
            o_ref[0, :, h * LANES:(h + 1) * LANES] = _diff_finish(maps, lam, gout_ref[h]).astype(BF16)


def _diff_decode(qa, ka_new, va_new, cache_kt, cache_v, bias_c, bias_n, lambda_qk, g_out_a):
    b, nq, w = qa.shape
    past = cache_kt.shape[-1]
    ck = min(CACHE_CHUNK, past)
    n_kc = past // ck
    new_spec = pl.BlockSpec((1, nq, w), lambda bi, c: (bi, 0, 0))
    return pl.pallas_call(
        functools.partial(_diff_dec_body, ck=ck, nq=nq),
        grid=(b, n_kc),
        in_specs=[
            _const_spec((4, HEAD_DIM)), new_spec, new_spec, new_spec,
            pl.BlockSpec((1, H_A, 2, HEAD_DIM, ck), lambda bi, c: (bi, 0, 0, 0, c)),
            pl.BlockSpec((1, H_A * ck, LANES), lambda bi, c: (bi, c, 0)),
            pl.BlockSpec((H_A, 1, nq, ck), lambda bi, c: (0, c, 0, 0)),
            _const_spec((H_A, 1, nq, nq)),
            _const_spec((H_A, 1, LANES)),
        ],
        out_specs=new_spec,
        out_shape=jax.ShapeDtypeStruct((b, nq, w), BF16),
        scratch_shapes=[pltpu.VMEM((2 * H_A * nq, w), BF16),
                        pltpu.VMEM((2 * H_A * nq, 1), F32), pltpu.VMEM((2 * H_A * nq, 1), F32),
                        pltpu.VMEM((2 * H_A * nq, LANES), F32)],
        compiler_params=_cparams(("arbitrary", "arbitrary")),
        name="diff_decode",
    )(lambda_qk, qa, ka_new, va_new, cache_kt, cache_v, bias_c, bias_n, g_out_a.reshape(H_A, 1, LANES))


def _suffix_sum_lanes(x):
    n = x.shape[-1]
    lane = lax.broadcasted_iota(jnp.int32, x.shape, x.ndim - 1)
    shift = 1
    while shift < n:
        x = x + jnp.where(lane + shift < n, pltpu.roll(x, n - shift, x.ndim - 1), 0.0)
        shift *= 2
    return x


def _live_chunks_body(spread_ref, lfc_ref, o_ref, *, n_kc, ck):
    lf = lfc_ref[0] * LOG2E
    newer = jnp.zeros((H_B, 1), F32)
    count = jnp.zeros((1, 1), F32)
    alive = jnp.ones((1, 1), F32)
    for c in reversed(range(n_kc)):
        best = jnp.max(newer, axis=0, keepdims=True) + spread_ref[0]
        alive = alive * jnp.where(best >= EXP2_ZERO_BELOW - 1.0, 1.0, 0.0)
        count = count + alive
        newer = newer + jnp.sum(lf[:, c * ck:(c + 1) * ck], axis=-1, keepdims=True)
    o_ref[0] = jnp.broadcast_to(count, o_ref.shape[1:]).astype(jnp.int32)


def _live_chunks(cache_logf_t, g_qk_b, ck):
    b, _, past = cache_logf_t.shape
    counts = pl.pallas_call(
        functools.partial(_live_chunks_body, n_kc=past // ck, ck=ck),
        grid=(b,),
        in_specs=[pl.BlockSpec(memory_space=pltpu.SMEM), pl.BlockSpec((1, H_B, past), lambda bi: (bi, 0, 0))],
        out_specs=pl.BlockSpec((1, 8, LANES), lambda bi: (bi, 0, 0)),
        out_shape=jax.ShapeDtypeStruct((b, 8, LANES), jnp.int32),
        compiler_params=_cparams(("arbitrary",)),
        name="fox_live_chunks",
    )(_fox_score_spread(g_qk_b).reshape(1), cache_logf_t)
    return counts[:, 0, 0]


def _fox_dec_body(live_ref, q_ref, kn_ref, vn_ref, lfn_col_ref, lfn_row_ref, kt_ref, vt_ref, lfc_ref, gout_ref, gn_ref,
                  o_ref, qbig_scr, cq_scr, cn_scr, after_scr, m_scr, l_scr, acc_scr, *, nq):
    kc = pl.program_id(1)
    n_kc = after_scr.shape[0]
    ck = after_scr.shape[-1]

    def head_rows(x):
        return jnp.concatenate([jnp.broadcast_to(x[h:h + 1, :], (nq, x.shape[-1])) for h in range(H_B)], axis=0)

    @pl.when(kc == 0)
    def _():
        qbig_scr[...] = _stack_queries(q_ref[0], H_B)
        _dec_init(m_scr, l_scr, acc_scr)
        r = lax.broadcasted_iota(jnp.int32, (nq, nq), 0)
        c = lax.broadcasted_iota(jnp.int32, (nq, nq), 1)
        tri_n = jnp.where(c <= r, 1.0, 0.0).astype(BF16)
        cq = _cumsum_rows(tri_n, lfn_col_ref[0] * LOG2E, jnp.zeros((1, LANES), F32))
        cq_scr[...] = jnp.concatenate([jnp.broadcast_to(cq[:, h:h + 1], (nq, LANES)) for h in range(H_B)], axis=0)
        lf_row = lfn_row_ref[0] * LOG2E
        total = jnp.sum(lf_row, axis=-1, keepdims=True)
        cn_scr[...] = total - _suffix_sum_lanes(lf_row) + lf_row
        lf = lfc_ref[0] * LOG2E
        after = _suffix_sum_lanes(lf) - lf
        for c_idx in range(n_kc):
            after_scr[c_idx] = after[:, c_idx * ck:(c_idx + 1) * ck]

    after = after_scr[n_kc - 1 - kc]
    qbig = qbig_scr[...]
    cq_col = cq_scr[...][:, 0:1]
    live = kc < live_ref[pl.program_id(0)]

    @pl.when(live)
    def _():
        kt_all = kt_ref[0].reshape(H_B * HEAD_DIM, ck).astype(BF16)
        vt_all = vt_ref[0].reshape(H_B * HEAD_DIM, ck).astype(BF16)
        s = jnp.dot(qbig, kt_all, preferred_element_type=F32) + cq_col + head_rows(after)
        m_scr[...], l_scr[...], acc_scr[...] = _stacked_update(
            (m_scr[...], l_scr[...], acc_scr[...]), s, lambda p: _qk(p, vt_all))

    @pl.when(kc == pl.num_programs(1) - 1)
    def _():
        row = jnp.concatenate([lax.broadcasted_iota(jnp.int32, (nq, nq), 0)] * H_B, axis=0)
        causal = lax.broadcasted_iota(jnp.int32, (H_B * nq, nq), 1) <= row
        s = _qk(qbig, kn_ref[0].astype(BF16)) + cq_col - head_rows(cn_scr[...][:, 0:nq])
        v_new = vn_ref[0].astype(BF16)
        _, l, acc = _stacked_update((m_scr[...], l_scr[...], acc_scr[...]), jnp.where(causal, s, NEG),
                                    lambda p: jnp.dot(p, v_new, preferred_element_type=F32))
        outs = [acc[h * nq:(h + 1) * nq, h * HEAD_DIM:(h + 1) * HEAD_DIM] / l[h * nq:(h + 1) * nq]
                for h in range(H_B)]
        for p in range(H_B // 2):
            o = jnp.concatenate([outs[2 * p], outs[2 * p + 1]], axis=-1)
            o_ref[0, :, p * LANES:(p + 1) * LANES] = _group_rms(o, gout_ref[p], gn_ref[...]).astype(BF16)


def _fox_decode(qb, kb_new, vb_new, logf_new, cache_kt, cache_vt, cache_logf_t, g_qk_b, g_out_b):
    b, nq, w = qb.shape
    past = cache_kt.shape[-1]
    ck = min(CACHE_CHUNK, past)
    n_kc = past // ck
    assert nq <= LANES
    pad = jnp.zeros((b, nq, LANES - H_B), F32)
    lfn_col = jnp.concatenate([logf_new, pad], axis=-1)
    lfn_row = jnp.concatenate([jnp.transpose(logf_new, (0, 2, 1)), jnp.zeros((b, H_B, LANES - nq), F32)], axis=-1)
    n_live = _live_chunks(cache_logf_t, g_qk_b, ck)
    per_stream = lambda width: pl.BlockSpec((1, nq, width), lambda bi, c, nl: (bi, 0, 0))

    def newest_first(bi, c, nl):
        return (bi, 0, 0, n_kc - 1 - jnp.minimum(c, jnp.maximum(nl[bi] - 1, 0)))

    return pl.pallas_call(
        functools.partial(_fox_dec_body, nq=nq),
        grid_spec=pltpu.PrefetchScalarGridSpec(
            num_scalar_prefetch=1, grid=(b, n_kc),
            in_specs=[
                per_stream(w), per_stream(w), per_stream(w), per_stream(LANES),
                pl.BlockSpec((1, H_B, LANES), lambda bi, c, nl: (bi, 0, 0)),
                pl.BlockSpec((1, H_B, HEAD_DIM, ck), newest_first),
                pl.BlockSpec((1, H_B, HEAD_DIM, ck), newest_first),
                pl.BlockSpec((1, H_B, past), lambda bi, c, nl: (bi, 0, 0)),
                pl.BlockSpec((H_B // 2, 1, LANES), lambda bi, c, nl: (0, 0, 0)),
                pl.BlockSpec((LANES, LANES), lambda bi, c, nl: (0, 0)),
            ],
            out_specs=per_stream(w),
            scratch_shapes=[pltpu.VMEM((H_B * nq, w), BF16),
                            pltpu.VMEM((H_B * nq, LANES), F32), pltpu.VMEM((H_B, LANES), F32),
                            pltpu.VMEM((n_kc, H_B, ck), F32),
                            pltpu.VMEM((H_B * nq, 1), F32), pltpu.VMEM((H_B * nq, 1), F32),
                            pltpu.VMEM((H_B * nq, w), F32)]),
        out_shape=jax.ShapeDtypeStruct((b, nq, w), BF16),
        compiler_params=_cparams(("arbitrary", "arbitrary")),
        name="fox_decode",
    )(n_live, qb, kb_new, vb_new, lfn_col, lfn_row, cache_kt, cache_vt, cache_logf_t,
      g_out_b.reshape(H_B // 2, 1, LANES), _blockdiag_mean(LANES, HEAD_DIM))


def _route(logits):
    lane_i = lax.broadcasted_iota(jnp.int32, logits.shape, 1)
    lane = lane_i.astype(F32)
    big = float(LANES)
    lg = jnp.where(lane_i < N_GROUPS, logits, NEG)
    mx = jnp.max(lg, axis=-1, keepdims=True)
    grp = jnp.min(jnp.where(lg == mx, lane, big), axis=-1, keepdims=True)
    p_grp = 1.0 / jnp.sum(jnp.exp(lg - mx), axis=-1, keepdims=True)
    e = lane_i - ROUTER_LANE0
    e_grp = lax.shift_right_arithmetic(e, EXPERT_GROUP_SHIFT).astype(F32)
    sel = (e >= 0) & (e < N_EXPERTS) & (e_grp == grp)
    v = jnp.where(sel, logits, NEG)
    v1 = jnp.max(v, axis=-1, keepdims=True)
    i1 = jnp.min(jnp.where(sel & (v == v1), lane, big), axis=-1, keepdims=True)
    sel2 = sel & (lane != i1)
    vv = jnp.where(sel2, logits, NEG)
    v2 = jnp.max(vv, axis=-1, keepdims=True)
    i2 = jnp.min(jnp.where(sel2 & (vv == v2), lane, big), axis=-1, keepdims=True)
    e2 = jnp.exp(v2 - v1)
    w1 = p_grp / (1.0 + e2)
    w2 = p_grp * e2 / (1.0 + e2)
    gates = jnp.where(lane == i1, w1, 0.0) + jnp.where(lane == i2, w2, 0.0)
    return gates, (i1, i2, w1, w2)


R_E1, R_E2, R_RANK1, R_RANK2, R_W1, R_W2 = range(6)


def _mix_body(oa_ref, ob_ref, x_ref, wa_ref, wb_ref, g2_ref, wr1_ref, wr2_ref, br_ref, tri_ref,
              x1_ref, xn_ref, gates_ref, route_ref, counts_ref):
    y = (jnp.dot(oa_ref[...], wa_ref[...], preferred_element_type=F32)
         + jnp.dot(ob_ref[...], wb_ref[...], preferred_element_type=F32))
    x1 = x_ref[...] + y
    x1_ref[...] = x1
    ms = jnp.mean(x1 * x1, axis=-1, keepdims=True)
    xn = (x1 * lax.rsqrt(ms + RMS_EPS)) * g2_ref[...]
    xn_ref[...] = xn
    h1 = xn.astype(BF16)
    h2 = (xn - h1.astype(F32)).astype(BF16)
    logits = (jnp.dot(h1, wr1_ref[...], preferred_element_type=F32)
              + jnp.dot(h1, wr2_ref[...], preferred_element_type=F32)
              + jnp.dot(h2, wr1_ref[...], preferred_element_type=F32)) + br_ref[...]
    gates, (i1, i2, w1, w2) = _route(logits)
    gates_ref[...] = gates

    @pl.when(pl.program_id(0) == 0)
    def _():
        counts_ref[...] = jnp.zeros_like(counts_ref)

    lane_i = lax.broadcasted_iota(jnp.int32, gates.shape, 1)
    lane = lane_i.astype(F32)
    oh1 = jnp.where(lane == i1, 1.0, 0.0)
    oh2 = jnp.where(lane == i2, 1.0, 0.0)
    comb = oh1 + oh2
    running = counts_ref[0:1, :]
    before = jnp.dot(tri_ref[...], comb.astype(BF16), preferred_element_type=F32) + running
    rank1 = jnp.sum(before * oh1, axis=-1, keepdims=True)
    rank2 = jnp.sum(before * oh2, axis=-1, keepdims=True)
    counts_ref[0:1, :] = running + jnp.sum(comb, axis=0, keepdims=True)
    rec = jnp.zeros_like(gates)
    for idx, val in ((R_E1, i1 - ROUTER_LANE0), (R_E2, i2 - ROUTER_LANE0), (R_RANK1, rank1), (R_RANK2, rank2),
                     (R_W1, w1), (R_W2, w2)):
        rec = jnp.where(lane_i == idx, val, rec)
    route_ref[...] = rec


def _mix_and_route(o_a, o_b, x, w_out, g_norm, w_rg, b_rg, w_re, b_re):
    n, d = x.shape
    bt = min(TOKEN_TILE, n)
    r = jnp.arange(bt)
    tri_strict = (r[None, :] < r[:, None]).astype(BF16)
    wa = w_out[:GROUP_W].astype(BF16)
    wb = w_out[GROUP_W:].astype(BF16)
    n_r = N_GROUPS + N_EXPERTS
    wr = jnp.concatenate([w_rg, w_re, jnp.zeros((d, LANES - n_r), F32)], axis=1)
    wr1 = wr.astype(BF16)
    wr2 = (wr - wr1.astype(F32)).astype(BF16)
    br = jnp.concatenate([b_rg, b_re, jnp.zeros((LANES - n_r,), F32)])[None, :]
    row = lambda width: pl.BlockSpec((bt, width), lambda i: (i, 0))
    return pl.pallas_call(
        _mix_body,
        grid=(n // bt,),
        in_specs=[row(GROUP_W), row(GROUP_W), row(d), _const_spec((GROUP_W, d)), _const_spec((GROUP_W, d)),
                  _const_spec((1, d)), _const_spec((d, LANES)), _const_spec((d, LANES)), _const_spec((1, LANES)),
                  _const_spec((bt, bt))],
        out_specs=[row(d), row(d), row(LANES), row(LANES), _const_spec((8, LANES))],
        out_shape=[jax.ShapeDtypeStruct((n, d), F32), jax.ShapeDtypeStruct((n, d), F32),
                   jax.ShapeDtypeStruct((n, LANES), F32), jax.ShapeDtypeStruct((n, LANES), F32),
                   jax.ShapeDtypeStruct((8, LANES), F32)],
        compiler_params=_cparams(("arbitrary",)),
        name="mix_route",
    )(o_a, o_b, x, wa, wb, g_norm[None, :], wr1, wr2, br, tri_strict)


def _swiglu(xn, wg, wu, wd, gate=None):
    x = xn.astype(BF16)
    g = jnp.dot(x, wg.astype(BF16), preferred_element_type=F32)
    u = jnp.dot(x, wu.astype(BF16), preferred_element_type=F32)
    h = (g * jax.nn.sigmoid(g)) * u
    if gate is not None:
        h = h * gate
    return jnp.dot(h.astype(BF16), wd.astype(BF16), preferred_element_type=F32)


def _expert_body(xn_ref, x1_ref, gates_ref, wg_ref, wu_ref, wd_ref, o_ref):
    e = pl.program_id(1)

    @pl.when(e == 0)
    def _():
        o_ref[...] = x1_ref[...]

    gates = gates_ref[...]
    lane = lax.broadcasted_iota(jnp.int32, gates.shape, 1)
    gate = jnp.sum(jnp.where(lane == e + ROUTER_LANE0, gates, 0.0), axis=-1, keepdims=True)
    o_ref[...] += _swiglu(xn_ref[...], wg_ref[0], wu_ref[0], wd_ref[0], gate)


def _experts(xn, x1, gates, w_gate, w_up, w_down):
    n, d = x1.shape
    ff = w_gate.shape[-1]
    bt = min(MOE_TILE, n)
    row = lambda width: pl.BlockSpec((bt, width), lambda i, e: (i, 0))
    return pl.pallas_call(
        _expert_body,
        grid=(n // bt, N_EXPERTS),
        in_specs=[row(d), row(d), row(LANES),
                  pl.BlockSpec((1, d, ff), lambda i, e: (e, 0, 0)),
                  pl.BlockSpec((1, d, ff), lambda i, e: (e, 0, 0)),
                  pl.BlockSpec((1, ff, d), lambda i, e: (e, 0, 0))],
        out_specs=row(d),
        out_shape=jax.ShapeDtypeStruct((n, d), F32),
        compiler_params=_cparams(("arbitrary", "arbitrary")),
        name="experts",
    )(xn, x1, gates, w_gate, w_up, w_down)


def _row_copies(n_rows, make_copy):
    def issue(r, carry):
        for s in range(2):
            make_copy(r, s).start(priority=s)
        return carry

    lax.fori_loop(0, n_rows, issue, 0, unroll=8)

    def drain(r, carry):
        for s in range(2):
            make_copy(r, s).wait()
        return carry

    lax.fori_loop(0, n_rows, drain, 0, unroll=8)


def _row_position_body(route_ref, base_ref, pos_ref):
    rec = route_ref[...]
    lane_i = lax.broadcasted_iota(jnp.int32, rec.shape, 1)
    lane = lane_i.astype(F32)
    out = jnp.zeros(rec.shape, F32)
    for slot, (e_lane, r_lane) in enumerate(((R_E1, R_RANK1), (R_E2, R_RANK2))):
        onehot = lane == rec[:, e_lane:e_lane + 1] + float(ROUTER_LANE0)
        base = jnp.sum(jnp.where(onehot, base_ref[...], 0.0), axis=-1, keepdims=True)
        out = jnp.where(lane_i == slot, base + rec[:, r_lane:r_lane + 1], out)
    pos_ref[...] = out.astype(jnp.int32)


def _dispatch_body(last_tile_ref, pos_ref, x_ref, xs_ref, zero_scr, sem):
    @pl.when(pl.program_id(0) == 0)
    def _():
        zero_scr[...] = jnp.zeros_like(zero_scr)
        tm = zero_scr.shape[0]
        fills = [pltpu.make_async_copy(zero_scr, xs_ref.at[pl.ds(pl.multiple_of(last_tile_ref[e], tm), tm)], sem)
                 for e in range(N_EXPERTS)]
        for f in fills:
            f.start()
        for f in fills:
            f.wait()

        def fill_unused(t, carry):
            f = pltpu.make_async_copy(zero_scr, xs_ref.at[pl.ds(pl.multiple_of(t * tm, tm), tm)], sem)
            f.start()
            f.wait()
            return carry

        lax.fori_loop(last_tile_ref[N_EXPERTS], xs_ref.shape[0] // tm, fill_unused, 0)

    _row_copies(x_ref.shape[0], lambda r, s: pltpu.make_async_copy(
        x_ref.at[pl.ds(r, 1)], xs_ref.at[pl.ds(pos_ref[0, 0, 2 * r + s], 1)], sem))


def _grouped_body(te_ref, nu_ref, xs_ref, wg_ref, wu_ref, wd_ref, ys_ref, wg_bf, wu_bf, wd_bf):
    t = pl.program_id(0)
    used = t < nu_ref[0]
    new_expert = jnp.logical_or(t == 0, te_ref[t] != te_ref[jnp.maximum(t - 1, 0)])

    @pl.when(jnp.logical_and(used, new_expert))
    def _():
        wg_bf[...] = wg_ref[0].astype(BF16)
        wu_bf[...] = wu_ref[0].astype(BF16)
        wd_bf[...] = wd_ref[0].astype(BF16)

    @pl.when(used)
    def _():
        ys_ref[...] = _swiglu(xs_ref[...], wg_bf[...], wu_bf[...], wd_bf[...])

    @pl.when(jnp.logical_not(used))
    def _():
        ys_ref[...] = jnp.zeros_like(ys_ref)


def _combine_body(pos_ref, route_ref, x1_ref, ys_ref, o_ref, buf_scr, sem):
    _row_copies(x1_ref.shape[0], lambda r, s: pltpu.make_async_copy(
        ys_ref.at[pl.ds(pos_ref[0, 0, 2 * r + s], 1)], buf_scr.at[s, pl.ds(r, 1)], sem))
    rec = route_ref[...]
    o_ref[...] = (x1_ref[...] + rec[:, R_W1:R_W1 + 1] * buf_scr[0] + rec[:, R_W2:R_W2 + 1] * buf_scr[1])


def _routed_experts(xn, x1, route, counts, w_gate, w_up, w_down):
    n, d = x1.shape
    ff = w_gate.shape[-1]
    bt = min(TOKEN_TILE, n)
    nt = n // bt
    tm = MOE_ROW_TILE
    n_tiles = (2 * n) // tm + N_EXPERTS
    cnt = counts[0, ROUTER_LANE0:ROUTER_LANE0 + N_EXPERTS].astype(jnp.int32)
    tiles = (cnt + tm - 1) // tm
    tile_end = jnp.cumsum(tiles)
    base_row = (tile_end - tiles) * tm
    n_used = tile_end[-1:]
    tile_expert = jnp.minimum(jnp.sum(jnp.arange(n_tiles)[:, None] >= tile_end[None, :], axis=1), N_EXPERTS - 1)
    base_lanes = jnp.zeros((1, LANES), F32).at[0, ROUTER_LANE0:ROUTER_LANE0 + N_EXPERTS].set(base_row.astype(F32))
    row = lambda width: pl.BlockSpec((bt, width), lambda i: (i, 0))
    pos = pl.pallas_call(
        _row_position_body,
        grid=(nt,),
        in_specs=[row(LANES), _const_spec((1, LANES))],
        out_specs=row(LANES),
        out_shape=jax.ShapeDtypeStruct((n, LANES), jnp.int32),
        compiler_params=_cparams(("arbitrary",)),
        name="moe_positions",
    )(route, base_lanes)
    pos = pos[:, :2].reshape(nt, 1, 2 * bt)

    pos_spec = pl.BlockSpec((1, 1, 2 * bt), lambda i: (i, 0, 0), memory_space=pltpu.SMEM)
    any_spec = pl.BlockSpec(memory_space=pl.ANY)
    last_tile = jnp.minimum(base_row + jnp.maximum(tiles - 1, 0) * tm, (n_tiles - 1) * tm)
    last_tile = jnp.concatenate([last_tile, n_used]).astype(jnp.int32)
    xs = pl.pallas_call(
        _dispatch_body,
        grid_spec=pltpu.PrefetchScalarGridSpec(
            num_scalar_prefetch=1, grid=(nt,),
            in_specs=[pl.BlockSpec((1, 1, 2 * bt), lambda i, lt: (i, 0, 0), memory_space=pltpu.SMEM),
                      pl.BlockSpec((bt, d), lambda i, lt: (i, 0))],
            out_specs=any_spec,
            scratch_shapes=[pltpu.VMEM((tm, d), F32), pltpu.SemaphoreType.DMA(())]),
        out_shape=jax.ShapeDtypeStruct((n_tiles * tm, d), F32),
        compiler_params=_cparams(("arbitrary",)),
        name="moe_dispatch",
    )(last_tile, pos, xn)

    w_spec = lambda shape: pl.BlockSpec(shape, lambda t, te, nu: (te[t], 0, 0))
    ys = pl.pallas_call(
        _grouped_body,
        grid_spec=pltpu.PrefetchScalarGridSpec(
            num_scalar_prefetch=2, grid=(n_tiles,),
            in_specs=[pl.BlockSpec((tm, d), lambda t, te, nu: (jnp.minimum(t, nu[0] - 1), 0)),
                      w_spec((1, d, ff)), w_spec((1, d, ff)), w_spec((1, ff, d))],
            out_specs=pl.BlockSpec((tm, d), lambda t, te, nu: (t, 0)),
            scratch_shapes=[pltpu.VMEM((d, ff), BF16), pltpu.VMEM((d, ff), BF16), pltpu.VMEM((ff, d), BF16)]),
        out_shape=jax.ShapeDtypeStruct((n_tiles * tm, d), F32),
        compiler_params=_cparams(("arbitrary",)),
        name="moe_experts",
    )(tile_expert.astype(jnp.int32), n_used.astype(jnp.int32), xs, w_gate, w_up, w_down)

    return pl.pallas_call(
        _combine_body,
        grid=(nt,),
        in_specs=[pos_spec, row(LANES), row(d), any_spec],
        out_specs=row(d),
        out_shape=jax.ShapeDtypeStruct((n, d), F32),
        scratch_shapes=[pltpu.VMEM((2, bt, d), F32), pltpu.SemaphoreType.DMA(())],
        compiler_params=_cparams(("arbitrary",)),
        name="moe_combine",
    )(pos, route, x1, ys)


def _ffn(o_a, o_b, x, w_out, g_norm_ffn, w_rg, b_rg, w_re, b_re, wg, wu, wd):
    b, t, d = x.shape
    n = b * t
    x1, xn, gates, route, counts = _mix_and_route(o_a.reshape(n, -1), o_b.reshape(n, -1), x.reshape(n, d), w_out,
                                                  g_norm_ffn, w_rg, b_rg, w_re, b_re)
    if 2 * n >= ROUTED_MIN_ASSIGNMENTS:
        y = _routed_experts(xn, x1, route, counts, wg, wu, wd)
    else:
        y = _experts(xn, x1, gates, wg, wu, wd)
    return y.reshape(b, t, d)


def kernel(x_prompt, x_sample, cache_a_k, cache_a_v, cache_b_k, cache_b_v, cache_b_logf, g_norm_mix, w_in, b_forget, g_qk_a, g_qk_b, lambda_qk, g_out_a, g_out_b, w_out, rel_bias, g_norm_ffn, w_router_group, b_router_group, w_router_expert, b_router_expert, w_exp_gate, w_exp_up, w_exp_down):
    depth = w_in.shape[0]
    assert depth == 1, "single-layer step only"
    bp, tp, d = x_prompt.shape
    bs, ts, _ = x_sample.shape
    past = cache_a_k.shape[2]
    w_in0, w_out0 = w_in[0], w_out[0]
    wg, wu, wd = w_exp_gate[0], w_exp_up[0], w_exp_down[0]
    ffn_w = (w_out0, g_norm_ffn[0], w_router_group[0], b_router_group[0], w_router_expert[0], b_router_expert[0],
             wg, wu, wd)

    (ka_p, va_p, kb_p, vb_p, logf_p, qa_t, ka_bf, va_t, qb_t, kb_aug, vb_t, c_edge) = _projection(
        x_prompt, g_norm_mix[0], w_in0, b_forget[0], g_qk_a[0], g_qk_b[0], with_aug=True)
    blk = min(ATTN_BLOCK, tp)
    bk = va_t.shape[-1]
    assert blk % CHUNK == 0 and blk >= MAX_DISTANCE and blk % bk == 0
    q_pos = blk + jnp.arange(blk, dtype=jnp.int32)
    bkt_p = jnp.stack([_bucket_map(q_pos, s * bk + jnp.arange(bk, dtype=jnp.int32)).T
                       for s in range(2 * blk // bk)] + [jnp.full((bk, blk), -1, jnp.int32)] * (blk // bk))
    bias_p = _bias_tiles(rel_bias, bkt_p)
    o_a = _diff_attention(qa_t, ka_bf, va_t, bias_p, lambda_qk[0], g_out_a[0])
    o_b = _fox_attention(qb_t, kb_aug, vb_t, c_edge, g_qk_b[0], g_out_b[0])
    y_p = _ffn(o_a, o_b, x_prompt, *ffn_w)

    xs = x_sample.reshape(1, bs * ts, d)
    (ka_s, va_s, kb_s, vb_s, logf_s, qa_s, qb_s) = _projection(
        xs, g_norm_mix[0], w_in0, b_forget[0], g_qk_a[0], g_qk_b[0], with_aug=False)
    per_stream = lambda a: a.reshape(bs, ts, a.shape[-1])
    ka_s, va_s, kb_s, vb_s, logf_s, qa_s, qb_s = map(per_stream, (ka_s, va_s, kb_s, vb_s, logf_s, qa_s, qb_s))
    ck = min(CACHE_CHUNK, past)
    q_pos = past + jnp.arange(ts, dtype=jnp.int32)
    bkt_c = _bucket_map(q_pos, jnp.arange(past, dtype=jnp.int32)).reshape(ts, past // ck, ck).transpose(1, 0, 2)
    bias_c = _bias_tiles(rel_bias, bkt_c)
    bias_n = _bias_tiles(rel_bias, _bucket_map(q_pos, q_pos)[None])
    o_a_s = _diff_decode(qa_s, ka_s, va_s, jnp.transpose(cache_a_k[0], (0, 2, 3, 4, 1)),
                         cache_a_v[0].reshape(bs, past * H_A, 2 * HEAD_DIM), bias_c, bias_n, lambda_qk[0], g_out_a[0])
    o_b_s = _fox_decode(qb_s, kb_s, vb_s, logf_s, jnp.transpose(cache_b_k[0], (0, 2, 3, 1)),
                        jnp.transpose(cache_b_v[0], (0, 2, 3, 1)), jnp.transpose(cache_b_logf[0], (0, 2, 1)),
                        g_qk_b[0], g_out_b[0])
    y_s = _ffn(o_a_s, o_b_s, x_sample, *ffn_w)

    def rows(ka, va, kb, vb, logf, b, t):
        return (ka.reshape(1, b, t, H_A, 2, HEAD_DIM), va.reshape(1, b, t, H_A, 2 * HEAD_DIM),
                kb.reshape(1, b, t, H_B, HEAD_DIM), vb.reshape(1, b, t, H_B, HEAD_DIM), logf.reshape(1, b, t, H_B))

    return (y_p, y_s) + rows(ka_p, va_p, kb_p, vb_p, logf_p, bp, tp) + rows(ka_s, va_s, kb_s, vb_s, logf_s, bs, ts)
```
